```python
import jax, jax.numpy as jnp
from jax import lax
import numpy as np

D_MODEL = 1024
BATCH = 8
SEQ = 4096
DEPTH = 2

MOBA_HEADS = 8
MOBA_HEAD_DIM = 64
MOBA_BLOCK = 256
MOBA_TOPK = 3
MOBA_Q_CHUNK = 32
RET_HEADS = 4
RET_DK = 64
RET_DV = 128
RET_CHUNK = 256
SSD_D_INNER = 512
SSD_HEAD_DIM = 64
SSD_HEADS = SSD_D_INNER // SSD_HEAD_DIM
SSD_GROUPS = 2
SSD_STATE = 128
SSD_CONV = 4
SSD_CHUNK = 256
SSD_XBC = SSD_D_INNER + 2 * SSD_GROUPS * SSD_STATE
SB_HEADS = 8
SB_HEAD_DIM = 64
SB_Q_BLOCK = 128

MOBA_W = MOBA_HEADS * MOBA_HEAD_DIM
RET_W = RET_HEADS * RET_DV
SB_W = SB_HEADS * SB_HEAD_DIM
N_BRANCH = 4
BRANCH_W = 512
IN_SIZES = (MOBA_W, MOBA_W, MOBA_W,
            RET_HEADS * RET_DK, RET_HEADS * RET_DK, RET_W, RET_W,
            SSD_D_INNER, SSD_XBC, SSD_HEADS,
            SB_W, SB_W, SB_W,
            N_BRANCH * D_MODEL)
IN_WIDTH = int(sum(IN_SIZES))
SPLIT_POINTS = tuple(int(p) for p in np.cumsum(IN_SIZES)[:-1])

D_FF = 2816
N_EXPERTS = 8
TOP_K = 2
D_FF_EXPERT = 3584
N_DENSE = (DEPTH + 1) // 2
N_MOE = DEPTH // 2

DEEPNORM_ALPHA = (2.0 * DEPTH) ** 0.25
DEEPNORM_BETA = (8.0 * DEPTH) ** -0.25
LN_EPS = 1e-5
NORM_EPS = 1e-6
NEG_INF = -1e30

kernel_name = 'hybrid_moba_retnet_ssd_stickbreak_moe'


def _pad_seq(t, s_pad):
    pad = [(0, 0)] * t.ndim
    pad[1] = (0, s_pad - t.shape[1])
    return jnp.pad(t, pad)


def _layer_norm(x, g, b):
    xf = x.astype(jnp.float32)
    mu = jnp.mean(xf, axis=-1, keepdims=True)
    var = jnp.mean(jnp.square(xf - mu), axis=-1, keepdims=True)
    return ((xf - mu) * lax.rsqrt(var + LN_EPS) * g + b).astype(x.dtype)


def _alibi_slopes(n_heads):
    return jnp.asarray(2.0 ** (-8.0 * np.arange(1, n_heads + 1) / n_heads), dtype=jnp.float32)


def _moba_attention(q, k, v):
    bsz, s, h, dh = q.shape
    s_pad = -(-s // MOBA_BLOCK) * MOBA_BLOCK
    nb = s_pad // MOBA_BLOCK
    k_sel = min(MOBA_TOPK, max(nb - 1, 1))
    q, k, v = (_pad_seq(t, s_pad) for t in (q, k, v))
    qh = q.transpose(0, 2, 1, 3)
    kb = k.transpose(0, 2, 1, 3).reshape(bsz, h, nb, MOBA_BLOCK, dh)
    vb = v.transpose(0, 2, 1, 3).reshape(bsz, h, nb, MOBA_BLOCK, dh)
    kmean = jnp.mean(kb, axis=3)
    slopes = _alibi_slopes(h)[None, :, None, None]
    scale = dh ** -0.5
    bi = jnp.arange(bsz)[:, None, None, None]
    hi = jnp.arange(h)[None, :, None, None]
    offs = jnp.arange(MOBA_BLOCK)
    n_sel = k_sel * MOBA_BLOCK

    def chunk(ci):
        q0 = ci * MOBA_Q_CHUNK
        qc = lax.dynamic_slice_in_dim(qh, q0, MOBA_Q_CHUNK, axis=2)
        qpos = q0 + jnp.arange(MOBA_Q_CHUNK)
        own = q0 // MOBA_BLOCK
        gate = jnp.einsum('bhqd,bhnd->bhqn', qc, kmean).astype(jnp.float32)
        past = jnp.arange(nb) < own
        gate = jnp.where(past, gate, NEG_INF)
        _, sel = lax.top_k(gate, k_sel)
        sel_ok = past[sel]
        ksel = kb[bi, hi, sel]
        vsel = vb[bi, hi, sel]
        s_sel = jnp.einsum('bhqd,bhqjkd->bhqjk', qc, ksel).astype(jnp.float32) * scale
        kpos_sel = sel[..., None] * MOBA_BLOCK + offs
        s_sel = s_sel - slopes[..., None] * (qpos[:, None, None] - kpos_sel)
        s_sel = jnp.where(sel_ok[..., None], s_sel, NEG_INF)
        kown = lax.dynamic_index_in_dim(kb, own, axis=2, keepdims=False)
        vown = lax.dynamic_index_in_dim(vb, own, axis=2, keepdims=False)
        kpos_own = own * MOBA_BLOCK + offs
        s_own = jnp.einsum('bhqd,bhkd->bhqk', qc, kown).astype(jnp.float32) * scale
        s_own = s_own - slopes * (qpos[:, None] - kpos_own[None, :])
        s_own = jnp.where(kpos_own[None, :] <= qpos[:, None], s_own, NEG_INF)
        scores = jnp.concatenate([s_sel.reshape(bsz, h, MOBA_Q_CHUNK, n_sel), s_own], axis=-1)
        p = jax.nn.softmax(scores, axis=-1).astype(v.dtype)
        p_sel = p[..., :n_sel].reshape(bsz, h, MOBA_Q_CHUNK, k_sel, MOBA_BLOCK)
        p_own = p[..., n_sel:]
        return (jnp.einsum('bhqjk,bhqjkd->bhqd', p_sel, vsel)
                + jnp.einsum('bhqk,bhkd->bhqd', p_own, vown))

    outs = lax.map(chunk, jnp.arange(s_pad // MOBA_Q_CHUNK))
    out = outs.transpose(1, 2, 0, 3, 4).reshape(bsz, h, s_pad, dh)[:, :, :s]
    return out.transpose(0, 2, 1, 3)


def _retention(q, k, v, g, gn_g, gn_b):
    bsz, s, h, dk = q.shape
    dv = v.shape[-1]
    s_pad = -(-s // RET_CHUNK) * RET_CHUNK
    n = s_pad // RET_CHUNK
    log_g = jnp.log(1.0 - 2.0 ** (-5.0 - jnp.arange(h, dtype=jnp.float32)))
    q = _pad_seq(q.astype(jnp.float32) * dk ** -0.5, s_pad).reshape(bsz, n, RET_CHUNK, h, dk)
    k = _pad_seq(k.astype(jnp.float32), s_pad).reshape(bsz, n, RET_CHUNK, h, dk)
    v = _pad_seq(v.astype(jnp.float32), s_pad).reshape(bsz, n, RET_CHUNK, h, dv)
    idx = jnp.arange(RET_CHUNK, dtype=jnp.float32)
    diff = idx[:, None] - idx[None, :]
    decay = jnp.where(diff >= 0, jnp.exp(jnp.maximum(diff, 0.0)[None] * log_g[:, None, None]), 0.0)
    qk = jnp.einsum('bnihd,bnjhd->bnhij', q, k) * decay
    o_intra = jnp.einsum('bnhij,bnjhe->bnihe', qk, v)
    k_dec = jnp.exp((RET_CHUNK - 1.0 - idx)[None, :] * log_g[:, None])
    states = jnp.einsum('bnjhd,hj,bnjhe->nbhde', k, k_dec, v)
    chunk_dec = jnp.exp(RET_CHUNK * log_g)[None, :, None, None]

    def step(st, s_n):
        return chunk_dec * st + s_n, st

    _, prev = lax.scan(step, jnp.zeros_like(states[0]), states)
    q_dec = jnp.exp((idx + 1.0)[None, :] * log_g[:, None])
    o_cross = jnp.einsum('bnihd,hi,nbhde->bnihe', q, q_dec, prev)
    o = (o_intra + o_cross).reshape(bsz, s_pad, h, dv)[:, :s]
    mu = jnp.mean(o, axis=-1, keepdims=True)
    var = jnp.mean(jnp.square(o - mu), axis=-1, keepdims=True)
    o = ((o - mu) * lax.rsqrt(var + NORM_EPS)).reshape(bsz, s, h * dv) * gn_g + gn_b
    return (jax.nn.silu(g.astype(jnp.float32)) * o).astype(g.dtype)


def _causal_dwconv(x, w, b):
    kw, ch = w.shape
    out = lax.conv_general_dilated(x, w[:, None, :].astype(x.dtype), window_strides=(1,),
                                   padding=[(kw - 1, 0)],
                                   dimension_numbers=('NWC', 'WIO', 'NWC'),
                                   feature_group_count=ch)
    return out + b


def _ssd_chunked(x, dt, a, bm, cm):
    bsz, s = x.shape[0], x.shape[1]
    g, r, p = x.shape[2], x.shape[3], x.shape[4]
    s_pad = -(-s // SSD_CHUNK) * SSD_CHUNK
    n = s_pad // SSD_CHUNK
    x, dt, bm, cm = (_pad_seq(t, s_pad) for t in (x, dt, bm, cm))
    x = x.reshape(bsz, n, SSD_CHUNK, g, r, p)
    dt = dt.reshape(bsz, n, SSD_CHUNK, g, r)
    bm = bm.reshape(bsz, n, SSD_CHUNK, g, SSD_STATE)
    cm = cm.reshape(bsz, n, SSD_CHUNK, g, SSD_STATE)
    cum = jnp.cumsum(dt * a, axis=2)
    idx = jnp.arange(SSD_CHUNK)
    causal = (idx[:, None] >= idx[None, :])[:, :, None, None]
    seg = cum[:, :, :, None] - cum[:, :, None, :]
    lmat = jnp.exp(jnp.where(causal, seg, NEG_INF))
    cb = jnp.einsum('bntgk,bnsgk->bntsg', cm, bm)
    w = cb[..., None] * lmat * dt[:, :, None]
    y_diag = jnp.einsum('bntsgr,bnsgrp->bntgrp', w, x)
    decay_states = jnp.exp(cum[:, :, -1:] - cum) * dt
    states = jnp.einsum('bnsgk,bnsgr,bnsgrp->nbgrpk', bm, decay_states, x)
    chunk_decay = jnp.exp(cum[:, :, -1]).transpose(1, 0, 2, 3)

    def step(hs, inp):
        st, dec = inp
        return dec[..., None, None] * hs + st, hs

    _, prev = lax.scan(step, jnp.zeros_like(states[0]), (states, chunk_decay))
    y_off = jnp.einsum('bntgk,nbgrpk->bntgrp', cm, prev) * jnp.exp(cum)[..., None]
    y = (y_diag + y_off).reshape(bsz, s_pad, g, r, p)
    return y[:, :s]


def _ssd_mixer(z, xbc, dt_raw, conv_w, conv_b, dt_bias, a_log, d_skip, norm_g):
    bsz, s, _ = xbc.shape
    r = SSD_HEADS // SSD_GROUPS
    xbc = jax.nn.silu(_causal_dwconv(xbc, conv_w, conv_b)).astype(jnp.float32)
    xs, bm, cm = jnp.split(xbc, [SSD_D_INNER, SSD_D_INNER + SSD_GROUPS * SSD_STATE], axis=-1)
    x = xs.reshape(bsz, s, SSD_GROUPS, r, SSD_HEAD_DIM)
    bm = bm.reshape(bsz, s, SSD_GROUPS, SSD_STATE)
    cm = cm.reshape(bsz, s, SSD_GROUPS, SSD_STATE)
    dt = jax.nn.softplus(dt_raw.astype(jnp.float32) + dt_bias).reshape(bsz, s, SSD_GROUPS, r)
    a = -jnp.exp(a_log.astype(jnp.float32)).reshape(SSD_GROUPS, r)
    y = _ssd_chunked(x, dt, a, bm, cm)
    y = y + d_skip.reshape(SSD_GROUPS, r)[:, :, None] * x
    y = y.reshape(bsz, s, SSD_D_INNER) * jax.nn.silu(z.astype(jnp.float32))
    yg = y.reshape(bsz, s, SSD_GROUPS, SSD_D_INNER // SSD_GROUPS)
    yg = yg * lax.rsqrt(jnp.mean(jnp.square(yg), axis=-1, keepdims=True) + LN_EPS)
    return (yg.reshape(bsz, s, SSD_D_INNER) * norm_g).astype(z.dtype)


def _stick_breaking(q, k, v):
    bsz, s, h, dh = q.shape
    scale = dh ** -0.5
    kpos = jnp.arange(s)

    def block(i):
        q0 = i * SB_Q_BLOCK
        qb = lax.dynamic_slice_in_dim(q, q0, SB_Q_BLOCK, axis=1)
        z = jnp.einsum('bqhd,bkhd->bhqk', qb, k).astype(jnp.float32) * scale
        qpos = q0 + jnp.arange(SB_Q_BLOCK)
        causal = kpos[None, :] < qpos[:, None]
        log_beta = jax.nn.log_sigmoid(z)
        log_1mb = jnp.where(causal, jax.nn.log_sigmoid(-z), 0.0)
        rem = lax.cumsum(log_1mb, axis=3, reverse=True) - log_1mb
        w = jnp.where(causal, jnp.exp(log_beta + rem), 0.0).astype(v.dtype)
        return jnp.einsum('bhqk,bkhd->bqhd', w, v)

    outs = lax.map(block, jnp.arange(s // SB_Q_BLOCK))
    return outs.transpose(1, 0, 2, 3, 4).reshape(bsz, s, h, dh)


def _hybrid_mixer(h, w_in, conv_w, conv_b, dt_bias, a_log, d_skip, ssm_norm_g,
                  ret_gn_g, ret_gn_b, w_br, w_out):
    bsz, s, _ = h.shape
    proj = jnp.einsum('bsd,de->bse', h, w_in)
    (mq, mk, mv, rq, rk, rv, rg, sz, sxbc, sdt, bq, bk, bv, gates) = jnp.split(proj, SPLIT_POINTS, axis=-1)
    o_a = _moba_attention(mq.reshape(bsz, s, MOBA_HEADS, MOBA_HEAD_DIM),
                          mk.reshape(bsz, s, MOBA_HEADS, MOBA_HEAD_DIM),
                          mv.reshape(bsz, s, MOBA_HEADS, MOBA_HEAD_DIM)).reshape(bsz, s, MOBA_W)
    o_b = _retention(rq.reshape(bsz, s, RET_HEADS, RET_DK), rk.reshape(bsz, s, RET_HEADS, RET_DK),
                     rv.reshape(bsz, s, RET_HEADS, RET_DV), rg, ret_gn_g, ret_gn_b)
    o_c = _ssd_mixer(sz, sxbc, sdt, conv_w, conv_b, dt_bias, a_log, d_skip, ssm_norm_g)
    o_d = _stick_breaking(bq.reshape(bsz, s, SB_HEADS, SB_HEAD_DIM),
                          bk.reshape(bsz, s, SB_HEADS, SB_HEAD_DIM),
                          bv.reshape(bsz, s, SB_HEADS, SB_HEAD_DIM)).reshape(bsz, s, SB_W)
    o = jnp.stack([o_a.astype(h.dtype), o_b.astype(h.dtype), o_c.astype(h.dtype), o_d.astype(h.dtype)], axis=2)
    y = jnp.einsum('bsnc,ncd->bsnd', o, w_br)
    gate = jax.nn.sigmoid(gates.reshape(bsz, s, N_BRANCH, D_MODEL))
    merged = jnp.einsum('bsnd,bsnd->bsd', gate, y)
    return jnp.einsum('bsd,de->bse', merged, w_out)


def _swiglu(h, w_gu, w_down):
    gu = jnp.einsum('bsd,df->bsf', h, w_gu)
    g, u = jnp.split(gu, 2, axis=-1)
    return jnp.einsum('bsf,fd->bsd', jax.nn.silu(g) * u, w_down)


def _moe_swiglu(h, router_w, router_b, w_gu, w_down):
    logits = (jnp.einsum('bsd,de->bse', h, router_w) + router_b).astype(jnp.float32)
    top_val, top_idx = lax.top_k(logits, TOP_K)
    top_w = jax.nn.softmax(top_val, axis=-1)
    combine = jnp.einsum('bsk,bske->bse', top_w,
                         jax.nn.one_hot(top_idx, N_EXPERTS, dtype=jnp.float32)).astype(h.dtype)
    y = jnp.zeros_like(h)
    for e in range(N_EXPERTS):
        y = y + combine[..., e:e + 1] * _swiglu(h, w_gu[e], w_down[e])
    return y


def setup_inputs(seed: int = 0) -> dict:
    key = jax.random.key(seed)
    ks = jax.random.split(key, 26)
    f32 = jnp.float32

    def nrm(k, shape, scale):
        return jax.random.normal(k, shape, f32) * scale

    dt0 = jnp.exp(jax.random.uniform(ks[7], (DEPTH, SSD_HEADS), f32, np.log(1e-3), np.log(1e-1)))
    return {
        'x': nrm(ks[0], (BATCH, SEQ, D_MODEL), 1.0),
        'c': nrm(ks[1], (BATCH, D_MODEL), 1.0),
        'w_ada': nrm(ks[2], (DEPTH, D_MODEL, 6 * D_MODEL), 0.5 * D_MODEL ** -0.5),
        'b_ada': nrm(ks[3], (DEPTH, 6 * D_MODEL), 0.02),
        'w_in': nrm(ks[4], (DEPTH, D_MODEL, IN_WIDTH), D_MODEL ** -0.5),
        'conv_w': nrm(ks[5], (DEPTH, SSD_CONV, SSD_XBC), SSD_CONV ** -0.5),
        'conv_b': nrm(ks[6], (DEPTH, SSD_XBC), 0.02),
        'dt_bias': dt0 + jnp.log(-jnp.expm1(-dt0)),
        'a_log': jnp.log(jax.random.uniform(ks[8], (DEPTH, SSD_HEADS), f32, 1.0, 16.0)),
        'd_skip': 1.0 + nrm(ks[9], (DEPTH, SSD_HEADS), 0.1),
        'ssm_norm_g': 1.0 + nrm(ks[10], (DEPTH, SSD_D_INNER), 0.05),
        'ret_gn_g': 1.0 + nrm(ks[11], (DEPTH, RET_W), 0.05),
        'ret_gn_b': nrm(ks[12], (DEPTH, RET_W), 0.02),
        'w_br': nrm(ks[13], (DEPTH, N_BRANCH, BRANCH_W, D_MODEL), BRANCH_W ** -0.5),
        'w_out': nrm(ks[14], (DEPTH, D_MODEL, D_MODEL), DEEPNORM_BETA * D_MODEL ** -0.5),
        'ln1_g': 1.0 + nrm(ks[15], (DEPTH, D_MODEL), 0.05),
        'ln1_b': nrm(ks[16], (DEPTH, D_MODEL), 0.02),
        'ln2_g': 1.0 + nrm(ks[17], (DEPTH, D_MODEL), 0.05),
        'ln2_b': nrm(ks[18], (DEPTH, D_MODEL), 0.02),
        'ffn_w_gu': nrm(ks[19], (N_DENSE, D_MODEL, 2 * D_FF), D_MODEL ** -0.5),
        'ffn_w_down': nrm(ks[20], (N_DENSE, D_FF, D_MODEL), DEEPNORM_BETA * D_FF ** -0.5),
        'router_w': nrm(ks[21], (N_MOE, D_MODEL, N_EXPERTS), D_MODEL ** -0.5),
        'router_b': nrm(ks[22], (N_MOE, N_EXPERTS), 0.01),
        'expert_w_gu': nrm(ks[23], (N_MOE, N_EXPERTS, D_MODEL, 2 * D_FF_EXPERT), D_MODEL ** -0.5),
        'expert_w_down': nrm(ks[24], (N_MOE, N_EXPERTS, D_FF_EXPERT, D_MODEL), DEEPNORM_BETA * D_FF_EXPERT ** -0.5),
    }


def reference(x, c, w_ada, b_ada, w_in, conv_w, conv_b, dt_bias, a_log, d_skip, ssm_norm_g,
              ret_gn_g, ret_gn_b, w_br, w_out, ln1_g, ln1_b, ln2_g, ln2_b,
              ffn_w_gu, ffn_w_down, router_w, router_b, expert_w_gu, expert_w_down):
    c_act = jax.nn.silu(c)
    for l in range(DEPTH):
        mod = jnp.einsum('bd,de->be', c_act, w_ada[l]) + b_ada[l]
        sh1, sc1, g1, sh2, sc2, g2 = jnp.split(mod[:, None, :], 6, axis=-1)
        h = x * (1.0 + sc1) + sh1
        mix = _hybrid_mixer(h, w_in[l], conv_w[l], conv_b[l], dt_bias[l], a_log[l], d_skip[l],
                            ssm_norm_g[l], ret_gn_g[l], ret_gn_b[l], w_br[l], w_out[l])
        x = _layer_norm(DEEPNORM_ALPHA * x + g1 * mix, ln1_g[l], ln1_b[l])
        h = x * (1.0 + sc2) + sh2
        if l % 2 == 0:
            f = _swiglu(h, ffn_w_gu[l // 2], ffn_w_down[l // 2])
        else:
            f = _moe_swiglu(h, router_w[l // 2], router_b[l // 2], expert_w_gu[l // 2], expert_w_down[l // 2])
        x = _layer_norm(DEEPNORM_ALPHA * x + g2 * f, ln2_g[l], ln2_b[l])
    return x
```

```python
import functools

import numpy as np
import jax
import jax.numpy as jnp
from jax import lax
from jax.experimental import pallas as pl
from jax.experimental.pallas import tpu as pltpu

F32 = jnp.float32
MXU_DTYPE = jnp.bfloat16

D_MODEL = 1024
DEPTH = 2
MOBA_HEADS = 8
MOBA_BLOCK = 256
MOBA_TOPK = 3
RET_HEADS = 4
RET_DK = 64
RET_DV = 128
RET_CHUNK = 256
SSD_D_INNER = 512
SSD_HEAD_DIM = 64
SSD_HEADS = 8
SSD_GROUPS = 2
SSD_STATE = 128
SSD_CONV = 4
SSD_CHUNK = 256
SSD_XBC = SSD_D_INNER + 2 * SSD_GROUPS * SSD_STATE
SB_HEADS = 8
SB_BLOCK = 256
HEAD_DIM = 64
BRANCH_W = 512
N_BRANCH = 4
D_FF = 2816
N_EXPERTS = 8
D_FF_EXPERT = 3584
DEEPNORM_ALPHA = (2.0 * DEPTH) ** 0.25
LN_EPS = 1e-5
NORM_EPS = 1e-6
NEG_INF = -1e30
LANES = 128
VMEM_LIMIT = 48 * 2 ** 20

A_MOBA_Q, A_MOBA_K, A_MOBA_V = 0, 512, 1024
A_RET_Q, A_RET_K, A_RET_V = 1536, 1792, 2048
A_SB_Q, A_SB_K, A_SB_V = 2560, 3072, 3584
A_WIDTH = 4096
B_GATES, B_RET_G, B_SSD_Z, B_XBC, B_DT = 0, 4096, 4608, 5120, 6144
B_WIDTH = 6272


def _params(sem):
    return pltpu.CompilerParams(dimension_semantics=sem, vmem_limit_bytes=VMEM_LIMIT)


def _dot(a, b):
    return jnp.dot(a, b, preferred_element_type=F32)


def _dot_nt(a, b):
    return lax.dot_general(a, b, (((1,), (1,)), ((), ())), preferred_element_type=F32)


def _dot_tn(a, b):
    return lax.dot_general(a, b, (((0,), (0,)), ((), ())), preferred_element_type=F32)


def _split(x):
    hi = x.astype(MXU_DTYPE)
    lo = (x - hi.astype(F32)).astype(MXU_DTYPE)
    return hi, lo


def _dot_split_lhs(a, b_exact):
    hi, lo = _split(a)
    return _dot(hi, b_exact) + _dot(lo, b_exact)


def _dot3(a, b):
    ah, al = _split(a)
    bh, bl = _split(b)
    return _dot(ah, bh) + _dot(al, bh) + _dot(ah, bl)


def _sigmoid(x):
    return 1.0 / (1.0 + jnp.exp(-x))


def _silu(x):
    return x * _sigmoid(x)


def _resid_ln(x, gate, f, ln_g, ln_b):
    y = DEEPNORM_ALPHA * x + gate * f
    mu = jnp.mean(y, axis=-1, keepdims=True)
    d = y - mu
    var = jnp.mean(d * d, axis=-1, keepdims=True)
    return d * lax.rsqrt(var + LN_EPS) * ln_g + ln_b


def _ada_kernel(c_ref, w_ref, b_ref, o_ref):
    c = c_ref[...]
    o_ref[0] = _dot3(_silu(c), w_ref[0]) + b_ref[0]


def _ada(c, w_ada, b_ada):
    depth, d, n = w_ada.shape
    bsz = c.shape[0]
    tn = 1536
    return pl.pallas_call(
        _ada_kernel,
        grid=(depth, n // tn),
        in_specs=[pl.BlockSpec((bsz, d), lambda l, j: (0, 0)),
                  pl.BlockSpec((1, d, tn), lambda l, j: (l, 0, j)),
                  pl.BlockSpec((1, 1, tn), lambda l, j: (l, 0, j))],
        out_specs=pl.BlockSpec((1, bsz, tn), lambda l, j: (l, 0, j)),
        out_shape=jax.ShapeDtypeStruct((depth, bsz, n), F32),
        compiler_params=_params(("arbitrary", "arbitrary")),
        name="ada_mod",
    )(c, w_ada, b_ada.reshape(depth, 1, n))


def _inproj_kernel(x_ref, sc_ref, sh_ref, w_ref, o_ref, h_ref):
    @pl.when(pl.program_id(2) == 0)
    def _():
        h = x_ref[0] * (1.0 + sc_ref[0]) + sh_ref[0]
        h_ref[...] = h.astype(MXU_DTYPE)

    o_ref[0] = _dot(h_ref[...], w_ref[...]).astype(o_ref.dtype)


def _inproj(x, sc, sh, w, out_dtype, tn, tm=1024):
    bsz, s, d = x.shape
    n = w.shape[1]
    tm = min(tm, s)
    return pl.pallas_call(
        _inproj_kernel,
        grid=(bsz, s // tm, n // tn),
        in_specs=[pl.BlockSpec((1, tm, d), lambda b, i, j: (b, i, 0)),
                  pl.BlockSpec((1, 1, d), lambda b, i, j: (b, 0, 0)),
                  pl.BlockSpec((1, 1, d), lambda b, i, j: (b, 0, 0)),
                  pl.BlockSpec((d, tn), lambda b, i, j: (0, j))],
        out_specs=pl.BlockSpec((1, tm, tn), lambda b, i, j: (b, i, j)),
        out_shape=jax.ShapeDtypeStruct((bsz, s, n), out_dtype),
        scratch_shapes=[pltpu.VMEM((tm, d), MXU_DTYPE)],
        compiler_params=_params(("parallel", "parallel", "arbitrary")),
        name="in_proj",
    )(x, sc, sh, w)


def _moba_kernel(q_ref, k_ref, v_ref, sl_ref, o_ref, km_ref, sel_ref, *, nb):
    blk = MOBA_BLOCK
    qi = pl.program_id(2)

    @pl.when(qi == 0)
    def _():
        kk = k_ref[0].astype(F32).reshape(nb, blk, LANES)
        km_ref[...] = jnp.zeros_like(km_ref)
        km_ref[0:nb, :] = jnp.sum(kk, axis=1) * (1.0 / blk)

    r_iota = lax.broadcasted_iota(jnp.int32, (blk, blk), 0)
    c_iota = lax.broadcasted_iota(jnp.int32, (blk, blk), 1)
    n_iota = lax.broadcasted_iota(jnp.int32, (blk, LANES), 1)
    rel = (c_iota - r_iota).astype(F32)
    past = n_iota < qi
    k_sel = min(MOBA_TOPK, max(nb - 1, 1))

    for h in range(2):
        lo, hi = h * HEAD_DIM, (h + 1) * HEAD_DIM
        q = q_ref[0, :, lo:hi]
        qs = (q.astype(F32) * HEAD_DIM ** -0.5).astype(MXU_DTYPE)
        slope = sl_ref[0, :, lo:lo + 1]

        km_hi, km_lo = _split(km_ref[:, lo:hi])
        gate = _dot_nt(q, km_hi) + _dot_nt(q, km_lo)
        g = jnp.where(past, gate, NEG_INF)
        g = jnp.where(n_iota < nb, g, -jnp.inf)
        sel = jnp.zeros((blk, LANES), F32)
        for _ in range(k_sel):
            mx = jnp.max(g, axis=1, keepdims=True)
            idx = jnp.min(jnp.where(g == mx, n_iota, LANES), axis=1, keepdims=True)
            pick = n_iota == idx
            sel = jnp.where(pick, 1.0, sel)
            g = jnp.where(pick, -jnp.inf, g)
        sel_ref[...] = jnp.where(past, sel, 0.0)

        base = slope * rel

        own0 = pl.multiple_of(qi * blk, blk)
        s = _dot_nt(qs, k_ref[0, pl.ds(own0, blk), lo:hi]) + base
        s = jnp.where(c_iota <= r_iota, s, NEG_INF)
        m0 = jnp.max(s, axis=1, keepdims=True)
        p = jnp.exp(s - m0)
        l0 = jnp.sum(p, axis=1, keepdims=True)
        acc0 = _dot(p.astype(MXU_DTYPE), v_ref[0, pl.ds(own0, blk), lo:hi])

        def body(j, carry):
            m, l, acc = carry
            j0 = pl.multiple_of(j * blk, blk)
            picked = jnp.sum(jnp.where(n_iota == j, sel_ref[...], 0.0), axis=1, keepdims=True)
            shift = slope * ((qi - j) * blk).astype(F32)
            s = _dot_nt(qs, k_ref[0, pl.ds(j0, blk), lo:hi]) + (base - shift)
            s = jnp.where(picked > 0.5, s, NEG_INF)
            m_new = jnp.maximum(m, jnp.max(s, axis=1, keepdims=True))
            a = jnp.exp(m - m_new)
            p = jnp.exp(s - m_new)
            l = a * l + jnp.sum(p, axis=1, keepdims=True)
            acc = a * acc + _dot(p.astype(MXU_DTYPE), v_ref[0, pl.ds(j0, blk), lo:hi])
            return m_new, l, acc

        _, l, acc = lax.fori_loop(0, qi, body, (m0, l0, acc0))
        o_ref[0, :, lo:hi] = (acc / l).astype(o_ref.dtype)


def _moba(proj_a):
    bsz, s, _ = proj_a.shape
    blk = MOBA_BLOCK
    nb = s // blk
    n_pair = MOBA_HEADS // 2
    slopes = 2.0 ** (-8.0 * np.arange(1, MOBA_HEADS + 1) / MOBA_HEADS)
    sl = jnp.asarray(np.repeat(slopes, HEAD_DIM).reshape(n_pair, 1, LANES), F32)
    qb, kb, vb = A_MOBA_Q // LANES, A_MOBA_K // LANES, A_MOBA_V // LANES
    return pl.pallas_call(
        functools.partial(_moba_kernel, nb=nb),
        grid=(bsz, n_pair, nb),
        in_specs=[pl.BlockSpec((1, blk, LANES), lambda b, hp, i: (b, i, qb + hp)),
                  pl.BlockSpec((1, s, LANES), lambda b, hp, i: (b, 0, kb + hp)),
                  pl.BlockSpec((1, s, LANES), lambda b, hp, i: (b, 0, vb + hp)),
                  pl.BlockSpec((1, 1, LANES), lambda b, hp, i: (hp, 0, 0))],
        out_specs=pl.BlockSpec((1, blk, LANES), lambda b, hp, i: (b, i, hp)),
        out_shape=jax.ShapeDtypeStruct((bsz, s, MOBA_HEADS * HEAD_DIM), MXU_DTYPE),
        scratch_shapes=[pltpu.VMEM((LANES, LANES), F32), pltpu.VMEM((blk, LANES), F32)],
        compiler_params=_params(("parallel", "parallel", "arbitrary")),
        name="moba",
    )(proj_a, proj_a, proj_a, sl)


def _sb_kernel(q_ref, k_ref, v_ref, o_ref):
    blk = SB_BLOCK
    qi = pl.program_id(2)
    r_iota = lax.broadcasted_iota(jnp.int32, (blk, blk), 0)
    c_iota = lax.broadcasted_iota(jnp.int32, (blk, blk), 1)
    causal = c_iota < r_iota
    suffix = jnp.where(r_iota > c_iota, 1.0, 0.0).astype(MXU_DTYPE)

    def tile(qs, j0, lo, hi, rest, diag):
        z = _dot_nt(qs, k_ref[0, pl.ds(j0, blk), lo:hi])
        log_beta = jnp.minimum(z, 0.0) - jnp.log1p(jnp.exp(-jnp.abs(z)))
        log_1mb = log_beta - z
        if diag:
            log_1mb = jnp.where(causal, log_1mb, 0.0)
        rem = _dot_split_lhs(log_1mb, suffix)
        w = jnp.exp(log_beta + rem + rest)
        if diag:
            w = jnp.where(causal, w, 0.0)
        out = _dot(w.astype(MXU_DTYPE), v_ref[0, pl.ds(j0, blk), lo:hi])
        return out, rest + rem[:, 0:1] + log_1mb[:, 0:1]

    for h in range(2):
        lo, hi = h * HEAD_DIM, (h + 1) * HEAD_DIM
        q = q_ref[0, :, lo:hi]
        qs = (q.astype(F32) * HEAD_DIM ** -0.5).astype(MXU_DTYPE)
        own0 = pl.multiple_of(qi * blk, blk)
        acc0, rest0 = tile(qs, own0, lo, hi, jnp.zeros((blk, 1), F32), True)

        def body(t, carry):
            acc, rest = carry
            j0 = pl.multiple_of((qi - 1 - t) * blk, blk)
            out, rest = tile(qs, j0, lo, hi, rest, False)
            return acc + out, rest

        acc, _ = lax.fori_loop(0, qi, body, (acc0, rest0))
        o_ref[0, :, lo:hi] = acc.astype(o_ref.dtype)


def _stick_breaking(proj_a):
    bsz, s, _ = proj_a.shape
    blk = SB_BLOCK
    n_pair = SB_HEADS // 2
    qb, kb, vb = A_SB_Q // LANES, A_SB_K // LANES, A_SB_V // LANES
    return pl.pallas_call(
        _sb_kernel,
        grid=(bsz, n_pair, s // blk),
        in_specs=[pl.BlockSpec((1, blk, LANES), lambda b, hp, i: (b, i, qb + hp)),
                  pl.BlockSpec((1, s, LANES), lambda b, hp, i: (b, 0, kb + hp)),
                  pl.BlockSpec((1, s, LANES), lambda b, hp, i: (b, 0, vb + hp))],
        out_specs=pl.BlockSpec((1, blk, LANES), lambda b, hp, i: (b, i, hp)),
        out_shape=jax.ShapeDtypeStruct((bsz, s, SB_HEADS * HEAD_DIM), MXU_DTYPE),
        compiler_params=_params(("parallel", "parallel", "arbitrary")),
        name="stick_breaking",
    )(proj_a, proj_a, proj_a)


def _ret_kernel(q_ref, k_ref, v_ref, g_ref, gn_g_ref, gn_b_ref, o_ref, st_ref):
    ch = RET_CHUNK

    @pl.when(pl.program_id(1) == 0)
    def _():
        st_ref[...] = jnp.zeros_like(st_ref)

    i_col = lax.broadcasted_iota(jnp.int32, (ch, 1), 0).astype(F32)
    diff = (lax.broadcasted_iota(jnp.int32, (ch, ch), 0)
            - lax.broadcasted_iota(jnp.int32, (ch, ch), 1)).astype(F32)
    for h in range(RET_HEADS):
        log_g = float(np.log(1.0 - 2.0 ** (-5.0 - h)))
        q = q_ref[0, :, h * RET_DK:(h + 1) * RET_DK]
        qs = (q.astype(F32) * RET_DK ** -0.5).astype(MXU_DTYPE)
        k = k_ref[0, :, h * RET_DK:(h + 1) * RET_DK]
        v = v_ref[0, :, h * RET_DV:(h + 1) * RET_DV]
        decay = jnp.where(diff >= 0, jnp.exp(jnp.maximum(diff, 0.0) * log_g), 0.0)
        qk = _dot_nt(qs, k) * decay
        o = _dot(qk.astype(MXU_DTYPE), v)
        prev = st_ref[h]
        q_dec = jnp.exp((i_col + 1.0) * log_g)
        o = o + _dot(qs, prev.astype(MXU_DTYPE)) * q_dec
        k_dec = jnp.exp((ch - 1.0 - i_col) * log_g)
        kd = (k.astype(F32) * k_dec).astype(MXU_DTYPE)
        st_ref[h] = float(np.exp(ch * log_g)) * prev + _dot_tn(kd, v)
        mu = jnp.mean(o, axis=-1, keepdims=True)
        d = o - mu
        var = jnp.mean(d * d, axis=-1, keepdims=True)
        cs = slice(h * RET_DV, (h + 1) * RET_DV)
        on = d * lax.rsqrt(var + NORM_EPS) * gn_g_ref[:, cs] + gn_b_ref[:, cs]
        o_ref[0, :, cs] = (_silu(g_ref[0, :, cs]) * on).astype(o_ref.dtype)


def _retention(proj_a, proj_b, gn_g, gn_b):
    bsz, s, _ = proj_a.shape
    ch = RET_CHUNK
    wq, wv = RET_HEADS * RET_DK, RET_HEADS * RET_DV
    return pl.pallas_call(
        _ret_kernel,
        grid=(bsz, s // ch),
        in_specs=[pl.BlockSpec((1, ch, wq), lambda b, i: (b, i, A_RET_Q // wq)),
                  pl.BlockSpec((1, ch, wq), lambda b, i: (b, i, A_RET_K // wq)),
                  pl.BlockSpec((1, ch, wv), lambda b, i: (b, i, A_RET_V // wv)),
                  pl.BlockSpec((1, ch, wv), lambda b, i: (b, i, B_RET_G // wv)),
                  pl.BlockSpec((1, wv), lambda b, i: (0, 0)),
                  pl.BlockSpec((1, wv), lambda b, i: (0, 0))],
        out_specs=pl.BlockSpec((1, ch, wv), lambda b, i: (b, i, 0)),
        out_shape=jax.ShapeDtypeStruct((bsz, s, wv), MXU_DTYPE),
        scratch_shapes=[pltpu.VMEM((RET_HEADS, RET_DK, RET_DV), F32)],
        compiler_params=_params(("parallel", "arbitrary")),
        name="retention",
    )(proj_a, proj_a, proj_a, proj_b, gn_g.reshape(1, wv), gn_b.reshape(1, wv))


def _ssd_kernel(z_ref, xbc_ref, dt_ref, cw_ref, cb_ref, dtb_ref, alog_ref, dsk_ref, ng_ref,
                o_ref, tail_ref, st_ref, y_ref):
    ch = SSD_CHUNK
    heads_per_group = SSD_HEADS // SSD_GROUPS
    halo = 8

    @pl.when(pl.program_id(1) == 0)
    def _():
        tail_ref[...] = jnp.zeros_like(tail_ref)
        st_ref[...] = jnp.zeros_like(st_ref)

    xbc = xbc_ref[0]
    xpad = jnp.concatenate([tail_ref[...], xbc], axis=0)
    tail_ref[...] = xbc[ch - halo:, :]
    conv = cb_ref[...]
    for kk in range(SSD_CONV):
        off = halo - (SSD_CONV - 1) + kk
        conv = conv + cw_ref[kk:kk + 1, :] * xpad[off:off + ch, :]
    xc = _silu(conv)
    xs = xc[:, :SSD_D_INNER]
    bm = xc[:, SSD_D_INNER:SSD_D_INNER + SSD_GROUPS * SSD_STATE]
    cm = xc[:, SSD_D_INNER + SSD_GROUPS * SSD_STATE:]

    dtr = dt_ref[0] + dtb_ref[...]
    dt = jnp.maximum(dtr, 0.0) + jnp.log1p(jnp.exp(-jnp.abs(dtr)))
    a = -jnp.exp(alog_ref[...])
    r_iota = lax.broadcasted_iota(jnp.int32, (ch, ch), 0)
    c_iota = lax.broadcasted_iota(jnp.int32, (ch, ch), 1)
    causal = r_iota >= c_iota
    tri = jnp.where(causal, 1.0, 0.0).astype(MXU_DTYPE)
    da = dt * a
    da_hi, da_lo = _split(da)
    cum = _dot(tri, da_hi) + _dot(tri, da_lo)
    cum_t = cum.T
    dt_t = dt.T
    cum_last = cum[ch - 1:ch, :]
    decay_st = jnp.exp(cum_last - cum) * dt
    e_cum = jnp.exp(cum)
    e_last = jnp.exp(cum_last)

    for g in range(SSD_GROUPS):
        bm_g = bm[:, g * SSD_STATE:(g + 1) * SSD_STATE].astype(MXU_DTYPE)
        cm_g = cm[:, g * SSD_STATE:(g + 1) * SSD_STATE].astype(MXU_DTYPE)
        cb = _dot_nt(cm_g, bm_g)
        for r in range(heads_per_group):
            hd = g * heads_per_group + r
            cs = slice(hd * SSD_HEAD_DIM, (hd + 1) * SSD_HEAD_DIM)
            x_h = xs[:, cs]
            seg = cum[:, hd:hd + 1] - cum_t[hd:hd + 1, :]
            lmat = jnp.exp(jnp.where(causal, seg, NEG_INF))
            w = cb * lmat * dt_t[hd:hd + 1, :]
            y = _dot(w.astype(MXU_DTYPE), x_h.astype(MXU_DTYPE))
            prev = st_ref[hd]
            y = y + _dot(cm_g, prev.astype(MXU_DTYPE)) * e_cum[:, hd:hd + 1]
            xd = (x_h * decay_st[:, hd:hd + 1]).astype(MXU_DTYPE)
            st_ref[hd] = e_last[:, hd:hd + 1] * prev + _dot_tn(bm_g, xd)
            y_ref[:, cs] = y + dsk_ref[:, cs] * x_h

    y = y_ref[...] * _silu(z_ref[0])
    gw = SSD_D_INNER // SSD_GROUPS
    for g in range(SSD_GROUPS):
        cs = slice(g * gw, (g + 1) * gw)
        yg = y[:, cs]
        ms = jnp.mean(yg * yg, axis=-1, keepdims=True)
        o_ref[0, :, cs] = (yg * lax.rsqrt(ms + LN_EPS) * ng_ref[:, cs]).astype(o_ref.dtype)


def _ssd(proj_b, conv_w, conv_b, dt_bias, a_log, d_skip, norm_g):
    bsz, s, _ = proj_b.shape
    ch = SSD_CHUNK
    pad = LANES - SSD_HEADS
    dtb = jnp.pad(dt_bias, (0, pad)).reshape(1, LANES)
    alog = jnp.pad(a_log, (0, pad)).reshape(1, LANES)
    dsk = jnp.repeat(d_skip, SSD_HEAD_DIM).reshape(1, SSD_D_INNER)
    const = lambda b, i: (0, 0)
    return pl.pallas_call(
        _ssd_kernel,
        grid=(bsz, s // ch),
        in_specs=[pl.BlockSpec((1, ch, SSD_D_INNER), lambda b, i: (b, i, B_SSD_Z // SSD_D_INNER)),
                  pl.BlockSpec((1, ch, SSD_XBC), lambda b, i: (b, i, B_XBC // SSD_XBC)),
                  pl.BlockSpec((1, ch, LANES), lambda b, i: (b, i, B_DT // LANES)),
                  pl.BlockSpec((SSD_CONV, SSD_XBC), const),
                  pl.BlockSpec((1, SSD_XBC), const),
                  pl.BlockSpec((1, LANES), const),
                  pl.BlockSpec((1, LANES), const),
                  pl.BlockSpec((1, SSD_D_INNER), const),
                  pl.BlockSpec((1, SSD_D_INNER), const)],
        out_specs=pl.BlockSpec((1, ch, SSD_D_INNER), lambda b, i: (b, i, 0)),
        out_shape=jax.ShapeDtypeStruct((bsz, s, SSD_D_INNER), MXU_DTYPE),
        scratch_shapes=[pltpu.VMEM((8, SSD_XBC), F32),
                        pltpu.VMEM((SSD_HEADS, SSD_STATE, SSD_HEAD_DIM), F32),
                        pltpu.VMEM((ch, SSD_D_INNER), F32)],
        compiler_params=_params(("parallel", "arbitrary")),
        name="ssd",
    )(proj_b, proj_b, proj_b, conv_w, conv_b.reshape(1, SSD_XBC), dtb, alog, dsk,
      norm_g.reshape(1, SSD_D_INNER))


def _merge_kernel(oa_ref, ob_ref, oc_ref, od_ref, gates_ref, x_ref, g1_ref, wbr_ref, wout_ref,
                  lg_ref, lb_ref, o_ref):
    merged = None
    for n, br in enumerate((oa_ref, ob_ref, oc_ref, od_ref)):
        y = _dot(br[0], wbr_ref[n])
        t = _sigmoid(gates_ref[0, :, n * D_MODEL:(n + 1) * D_MODEL]) * y
        merged = t if merged is None else merged + t
    mix = _dot(merged.astype(MXU_DTYPE), wout_ref[...])
    o_ref[0] = _resid_ln(x_ref[0], g1_ref[0], mix, lg_ref[...], lb_ref[...])


def _merge(o_a, o_b, o_c, o_d, proj_b, x, g1, w_br, w_out, ln_g, ln_b, tm=512):
    bsz, s, d = x.shape
    tm = min(tm, s)
    gw = N_BRANCH * d
    row = lambda b, i: (b, i, 0)
    const2 = lambda b, i: (0, 0)
    br_spec = pl.BlockSpec((1, tm, BRANCH_W), row)
    return pl.pallas_call(
        _merge_kernel,
        grid=(bsz, s // tm),
        in_specs=[br_spec, br_spec, br_spec, br_spec,
                  pl.BlockSpec((1, tm, gw), lambda b, i: (b, i, B_GATES // gw)),
                  pl.BlockSpec((1, tm, d), row),
                  pl.BlockSpec((1, 1, d), lambda b, i: (b, 0, 0)),
                  pl.BlockSpec((N_BRANCH, BRANCH_W, d), lambda b, i: (0, 0, 0)),
                  pl.BlockSpec((d, d), const2),
                  pl.BlockSpec((1, d), const2),
                  pl.BlockSpec((1, d), const2)],
        out_specs=pl.BlockSpec((1, tm, d), row),
        out_shape=jax.ShapeDtypeStruct((bsz, s, d), F32),
        compiler_params=_params(("parallel", "parallel")),
        name="merge",
    )(o_a, o_b, o_c, o_d, proj_b, x, g1, w_br, w_out, ln_g.reshape(1, d), ln_b.reshape(1, d))


def _ffn_kernel(x_ref, sc_ref, sh_ref, g2_ref, wg_ref, wu_ref, wd_ref, lg_ref, lb_ref,
                o_ref, h_ref, acc_ref):
    j = pl.program_id(2)

    @pl.when(j == 0)
    def _():
        h = x_ref[0] * (1.0 + sc_ref[0]) + sh_ref[0]
        h_ref[...] = h.astype(MXU_DTYPE)
        acc_ref[...] = jnp.zeros_like(acc_ref)

    h = h_ref[...]
    act = _silu(_dot(h, wg_ref[...])) * _dot(h, wu_ref[...])
    acc_ref[...] += _dot(act.astype(MXU_DTYPE), wd_ref[...])

    @pl.when(j == pl.num_programs(2) - 1)
    def _():
        o_ref[0] = _resid_ln(x_ref[0], g2_ref[0], acc_ref[...], lg_ref[...], lb_ref[...])


def _ffn(x, sc, sh, g2, w_gu, w_down, ln_g, ln_b, tm=512, tf=1408):
    bsz, s, d = x.shape
    tm = min(tm, s)
    nf = D_FF // tf
    row = lambda b, i, j: (b, i, 0)
    vec = lambda b, i, j: (b, 0, 0)
    const2 = lambda b, i, j: (0, 0)
    return pl.pallas_call(
        _ffn_kernel,
        grid=(bsz, s // tm, nf),
        in_specs=[pl.BlockSpec((1, tm, d), row),
                  pl.BlockSpec((1, 1, d), vec), pl.BlockSpec((1, 1, d), vec),
                  pl.BlockSpec((1, 1, d), vec),
                  pl.BlockSpec((d, tf), lambda b, i, j: (0, j)),
                  pl.BlockSpec((d, tf), lambda b, i, j: (0, nf + j)),
                  pl.BlockSpec((tf, d), lambda b, i, j: (j, 0)),
                  pl.BlockSpec((1, d), const2), pl.BlockSpec((1, d), const2)],
        out_specs=pl.BlockSpec((1, tm, d), row),
        out_shape=jax.ShapeDtypeStruct((bsz, s, d), F32),
        scratch_shapes=[pltpu.VMEM((tm, d), MXU_DTYPE), pltpu.VMEM((tm, d), F32)],
        compiler_params=_params(("parallel", "parallel", "arbitrary")),
        name="ffn_dense",
    )(x, sc, sh, g2, w_gu, w_gu, w_down, ln_g.reshape(1, d), ln_b.reshape(1, d))


def _moe_kernel(x_ref, sc_ref, sh_ref, g2_ref, rw_ref, rb_ref, wg_ref, wu_ref, wd_ref,
                lg_ref, lb_ref, o_ref, h_ref, acc_ref, comb_ref):
    e = pl.program_id(2)
    j = pl.program_id(3)
    lane = lax.broadcasted_iota(jnp.int32, comb_ref.shape, 1)

    @pl.when(jnp.logical_and(e == 0, j == 0))
    def _():
        h = x_ref[0] * (1.0 + sc_ref[0]) + sh_ref[0]
        h_ref[...] = h.astype(MXU_DTYPE)
        acc_ref[...] = jnp.zeros_like(acc_ref)
        logits = _dot3(h, rw_ref[...]) + rb_ref[...]
        logits = jnp.where(lane < N_EXPERTS, logits, -jnp.inf)
        v1 = jnp.max(logits, axis=1, keepdims=True)
        i1 = jnp.min(jnp.where(logits == v1, lane, LANES), axis=1, keepdims=True)
        rest = jnp.where(lane == i1, -jnp.inf, logits)
        v2 = jnp.max(rest, axis=1, keepdims=True)
        i2 = jnp.min(jnp.where(rest == v2, lane, LANES), axis=1, keepdims=True)
        e2 = jnp.exp(v2 - v1)
        w1 = 1.0 / (1.0 + e2)
        w2 = e2 / (1.0 + e2)
        comb_ref[...] = jnp.where(lane == i1, w1, 0.0) + jnp.where(lane == i2, w2, 0.0)

    h = h_ref[...]
    ce = jnp.sum(jnp.where(lane == e, comb_ref[...], 0.0), axis=1, keepdims=True)
    act = _silu(_dot(h, wg_ref[0])) * _dot(h, wu_ref[0]) * ce
    acc_ref[...] += _dot(act.astype(MXU_DTYPE), wd_ref[0])

    @pl.when(jnp.logical_and(e == pl.num_programs(2) - 1, j == pl.num_programs(3) - 1))
    def _():
        o_ref[0] = _resid_ln(x_ref[0], g2_ref[0], acc_ref[...], lg_ref[...], lb_ref[...])


def _moe(x, sc, sh, g2, router_w, router_b, w_gu, w_down, ln_g, ln_b, tm=512, tf=896):
    bsz, s, d = x.shape
    tm = min(tm, s)
    nf = D_FF_EXPERT // tf
    rw = jnp.pad(router_w, ((0, 0), (0, LANES - N_EXPERTS)))
    rb = jnp.pad(router_b, (0, LANES - N_EXPERTS)).reshape(1, LANES)
    row = lambda b, i, e, j: (b, i, 0)
    vec = lambda b, i, e, j: (b, 0, 0)
    const2 = lambda b, i, e, j: (0, 0)
    return pl.pallas_call(
        _moe_kernel,
        grid=(bsz, s // tm, N_EXPERTS, nf),
        in_specs=[pl.BlockSpec((1, tm, d), row),
                  pl.BlockSpec((1, 1, d), vec), pl.BlockSpec((1, 1, d), vec),
                  pl.BlockSpec((1, 1, d), vec),
                  pl.BlockSpec((d, LANES), const2), pl.BlockSpec((1, LANES), const2),
                  pl.BlockSpec((1, d, tf), lambda b, i, e, j: (e, 0, j)),
                  pl.BlockSpec((1, d, tf), lambda b, i, e, j: (e, 0, nf + j)),
                  pl.BlockSpec((1, tf, d), lambda b, i, e, j: (e, j, 0)),
                  pl.BlockSpec((1, d), const2), pl.BlockSpec((1, d), const2)],
        out_specs=pl.BlockSpec((1, tm, d), row),
        out_shape=jax.ShapeDtypeStruct((bsz, s, d), F32),
        scratch_shapes=[pltpu.VMEM((tm, d), MXU_DTYPE), pltpu.VMEM((tm, d), F32),
                        pltpu.VMEM((tm, LANES), F32)],
        compiler_params=_params(("parallel", "parallel", "arbitrary", "arbitrary")),
        name="moe",
    )(x, sc, sh, g2, rw, rb, w_gu, w_gu, w_down, ln_g.reshape(1, d), ln_b.reshape(1, d))


def _split_w_in(w):
    p = np.cumsum([0, 512, 512, 512, 256, 256, 512, 512, 512, SSD_XBC, SSD_HEADS, 512, 512, 512,
                   N_BRANCH * D_MODEL])
    mq, mk, mv, rq, rk, rv, rg, sz, sxbc, sdt, bq, bk, bv, gates = (
        w[:, p[i]:p[i + 1]] for i in range(14))
    w_a = jnp.concatenate([mq, mk, mv, rq, rk, rv, bq, bk, bv], axis=1).astype(MXU_DTYPE)
    sdt = jnp.pad(sdt, ((0, 0), (0, LANES - SSD_HEADS)))
    w_b = jnp.concatenate([gates, rg, sz, sxbc, sdt], axis=1).astype(MXU_DTYPE)
    return w_a, w_b


def kernel(x, c, w_ada, b_ada, w_in, conv_w, conv_b, dt_bias, a_log, d_skip, ssm_norm_g, ret_gn_g, ret_gn_b, w_br, w_out, ln1_g, ln1_b, ln2_g, ln2_b, ffn_w_gu, ffn_w_down, router_w, router_b, expert_w_gu, expert_w_down):
    bsz = x.shape[0]
    mod = _ada(c, w_ada, b_ada)
    for l in range(DEPTH):
        sh1, sc1, g1, sh2, sc2, g2 = (
            mod[l, :, i * D_MODEL:(i + 1) * D_MODEL].reshape(bsz, 1, D_MODEL) for i in range(6))
        w_a, w_b = _split_w_in(w_in[l])
        proj_a = _inproj(x, sc1, sh1, w_a, MXU_DTYPE, tn=1024)
        proj_b = _inproj(x, sc1, sh1, w_b, F32, tn=896)
        o_a = _moba(proj_a)
        o_b = _retention(proj_a, proj_b, ret_gn_g[l], ret_gn_b[l])
        o_c = _ssd(proj_b, conv_w[l], conv_b[l], dt_bias[l], a_log[l], d_skip[l], ssm_norm_g[l])
        o_d = _stick_breaking(proj_a)
        x = _merge(o_a, o_b, o_c, o_d, proj_b, x, g1, w_br[l].astype(MXU_DTYPE),
                   w_out[l].astype(MXU_DTYPE), ln1_g[l], ln1_b[l])
        if l % 2 == 0:
            x = _ffn(x, sc2, sh2, g2, ffn_w_gu[l // 2].astype(MXU_DTYPE),
                     ffn_w_down[l // 2].astype(MXU_DTYPE), ln2_g[l], ln2_b[l])
        else:
            x = _moe(x, sc2, sh2, g2, router_w[l // 2], router_b[l // 2],
                     expert_w_gu[l // 2].astype(MXU_DTYPE), expert_w_down[l // 2].astype(MXU_DTYPE),
                     ln2_g[l], ln2_b[l])
    return x
```

```python
import functools

import numpy as np
import jax
import jax.numpy as jnp
from jax import lax
from jax.experimental import pallas as pl
from jax.experimental.pallas import tpu as pltpu

F32 = jnp.float32
MXU_DTYPE = jnp.bfloat16

D_MODEL = 1024
DEPTH = 2
MOBA_HEADS = 8
MOBA_BLOCK = 256
MOBA_TOPK = 3
RET_HEADS = 4
RET_DK = 64
RET_DV = 128
RET_CHUNK = 256
SSD_D_INNER = 512
SSD_HEAD_DIM = 64
SSD_HEADS = 8
SSD_GROUPS = 2
SSD_STATE = 128
SSD_CONV = 4
SSD_CHUNK = 256
SSD_XBC = SSD_D_INNER + 2 * SSD_GROUPS * SSD_STATE
SB_HEADS = 8
SB_BLOCK = 256
HEAD_DIM = 64
ATT_HEADS_PER_STEP = 4
ATT_W = ATT_HEADS_PER_STEP * HEAD_DIM
BRANCH_W = 512
N_BRANCH = 4
D_FF = 2816
N_EXPERTS = 8
D_FF_EXPERT = 3584
DEEPNORM_ALPHA = (2.0 * DEPTH) ** 0.25
LN_EPS = 1e-5
NORM_EPS = 1e-6
NEG_INF = -1e30
LANES = 128
VMEM_LIMIT = 48 * 2 ** 20

A_MOBA_Q, A_MOBA_K, A_MOBA_V = 0, 512, 1024
A_RET_Q, A_RET_K, A_RET_V = 1536, 1792, 2048
A_SB_Q, A_SB_K, A_SB_V = 2560, 3072, 3584
A_WIDTH = 4096
B_GATES, B_RET_G, B_SSD_Z, B_XBC, B_DT = 0, 4096, 4608, 5120, 6144
B_WIDTH = 6272


def _params(sem):
    return pltpu.CompilerParams(dimension_semantics=sem, vmem_limit_bytes=VMEM_LIMIT)


def _dot(a, b):
    return jnp.dot(a, b, preferred_element_type=F32)


def _dot_nt(a, b):
    return lax.dot_general(a, b, (((1,), (1,)), ((), ())), preferred_element_type=F32)


def _dot_tn(a, b):
    return lax.dot_general(a, b, (((0,), (0,)), ((), ())), preferred_element_type=F32)


def _split(x):
    hi = x.astype(MXU_DTYPE)
    lo = (x - hi.astype(F32)).astype(MXU_DTYPE)
    return hi, lo


def _dot_split_lhs(a, b_exact):
    hi, lo = _split(a)
    return _dot(hi, b_exact) + _dot(lo, b_exact)


def _dot3(a, b):
    ah, al = _split(a)
    bh, bl = _split(b)
    return _dot(ah, bh) + _dot(al, bh) + _dot(ah, bl)


def _sigmoid(x):
    return 1.0 / (1.0 + jnp.exp(-x))


def _silu(x):
    return x * _sigmoid(x)


def _resid_ln(x, gate, f, ln_g, ln_b):
    y = DEEPNORM_ALPHA * x + gate * f
    mu = jnp.mean(y, axis=-1, keepdims=True)
    d = y - mu
    var = jnp.mean(d * d, axis=-1, keepdims=True)
    return d * lax.rsqrt(var + LN_EPS) * ln_g + ln_b


def _ada_kernel(c_ref, w_ref, b_ref, o_ref):
    c = c_ref[...]
    o_ref[0] = _dot3(_silu(c), w_ref[0]) + b_ref[0]


def _ada(c, w_ada, b_ada):
    depth, d, n = w_ada.shape
    bsz = c.shape[0]
    tn = 1536
    return pl.pallas_call(
        _ada_kernel,
        grid=(depth, n // tn),
        in_specs=[pl.BlockSpec((bsz, d), lambda l, j: (0, 0)),
                  pl.BlockSpec((1, d, tn), lambda l, j: (l, 0, j)),
                  pl.BlockSpec((1, 1, tn), lambda l, j: (l, 0, j))],
        out_specs=pl.BlockSpec((1, bsz, tn), lambda l, j: (l, 0, j)),
        out_shape=jax.ShapeDtypeStruct((depth, bsz, n), F32),
        compiler_params=_params(("arbitrary", "arbitrary")),
        name="ada_mod",
    )(c, w_ada, b_ada.reshape(depth, 1, n))


def _inproj_kernel(x_ref, sc_ref, sh_ref, w_ref, o_ref, h_ref):
    @pl.when(pl.program_id(2) == 0)
    def _():
        h = x_ref[0] * (1.0 + sc_ref[0]) + sh_ref[0]
        h_ref[...] = h.astype(MXU_DTYPE)

    o_ref[0] = _dot(h_ref[...], w_ref[...]).astype(o_ref.dtype)


def _inproj(x, sc, sh, w, out_dtype, tn, tm=1024):
    bsz, s, d = x.shape
    n = w.shape[1]
    tm = min(tm, s)
    return pl.pallas_call(
        _inproj_kernel,
        grid=(bsz, s // tm, n // tn),
        in_specs=[pl.BlockSpec((1, tm, d), lambda b, i, j: (b, i, 0)),
                  pl.BlockSpec((1, 1, d), lambda b, i, j: (b, 0, 0)),
                  pl.BlockSpec((1, 1, d), lambda b, i, j: (b, 0, 0)),
                  pl.BlockSpec((d, tn), lambda b, i, j: (0, j))],
        out_specs=pl.BlockSpec((1, tm, tn), lambda b, i, j: (b, i, j)),
        out_shape=jax.ShapeDtypeStruct((bsz, s, n), out_dtype),
        scratch_shapes=[pltpu.VMEM((tm, d), MXU_DTYPE)],
        compiler_params=_params(("parallel", "parallel", "arbitrary")),
        name="in_proj",
    )(x, sc, sh, w)


def _moba_kernel(q_ref, k_ref, v_ref, sl_ref, o_ref, km_ref, ka_ref, va_ref, *, nb):
    blk = MOBA_BLOCK
    nh = ATT_HEADS_PER_STEP
    qi = pl.program_id(2)
    s_len = nb * blk
    nb_pad = -(-nb // 8) * 8
    assert nb_pad <= HEAD_DIM
    cols = [slice(h * HEAD_DIM, (h + 1) * HEAD_DIM) for h in range(nh)]

    @pl.when(qi == 0)
    def _():
        k_all = k_ref[0]
        v_all = v_ref[0]
        kk = k_all.astype(F32).reshape(nb, blk, ATT_W)
        km_ref[...] = jnp.zeros_like(km_ref)
        km_ref[0:nb, :] = jnp.sum(kk, axis=1) * (1.0 / blk)
        key_block = lax.broadcasted_iota(jnp.int32, (s_len, HEAD_DIM), 0) // blk
        lane = lax.broadcasted_iota(jnp.int32, (s_len, HEAD_DIM), 1)
        one_hot = jnp.where(lane == key_block, 1.0, 0.0).astype(MXU_DTYPE)
        ones_col = jnp.where(lane == 0, 1.0, 0.0).astype(MXU_DTYPE)
        for h in range(nh):
            ka_ref[h] = jnp.concatenate([k_all[:, cols[h]], one_hot], axis=1)
            va_ref[h] = jnp.concatenate([v_all[:, cols[h]], ones_col], axis=1)

    r_iota = lax.broadcasted_iota(jnp.int32, (blk, blk), 0)
    c_iota = lax.broadcasted_iota(jnp.int32, (blk, blk), 1)
    c_row = lax.broadcasted_iota(jnp.int32, (1, blk), 1).astype(F32)
    n_iota = lax.broadcasted_iota(jnp.int32, (nb_pad, blk), 0)
    past = n_iota < qi
    k_sel = min(MOBA_TOPK, max(nb - 1, 1))

    qa, slope = [], []
    for h in range(nh):
        q = q_ref[0, :, cols[h]]
        qs = (q.astype(F32) * HEAD_DIM ** -0.5).astype(MXU_DTYPE)
        slope.append(sl_ref[0, :, h * HEAD_DIM:h * HEAD_DIM + 1])
        km_hi, km_lo = _split(km_ref[:, cols[h]])
        gate = (_dot_nt(km_hi, q) + _dot_nt(km_lo, q))[0:nb_pad, :]
        g = jnp.where(past, gate, NEG_INF)
        g = jnp.where(n_iota < nb, g, -jnp.inf)
        sel = jnp.zeros((nb_pad, blk), F32)
        for _ in range(k_sel):
            mx = jnp.max(g, axis=0, keepdims=True)
            idx = jnp.min(jnp.where(g == mx, n_iota, nb_pad), axis=0, keepdims=True)
            pick = n_iota == idx
            sel = jnp.where(pick, 1.0, sel)
            g = jnp.where(pick, -jnp.inf, g)
        keep = jnp.logical_or(jnp.logical_and(past, sel > 0.5), n_iota == qi)
        bias_t = jnp.where(keep, 0.0, NEG_INF)
        bias_t = jnp.concatenate([bias_t, jnp.zeros((LANES - nb_pad, blk), F32)], axis=0)
        bias = bias_t.T[:, 0:HEAD_DIM]
        qa.append(jnp.concatenate([qs, bias.astype(MXU_DTYPE)], axis=1))

    def tiles(j0, key_off, carry, diag):
        heads = range(nh)
        s = [_dot_nt(qa[h], ka_ref[h, pl.ds(j0, blk), :]) + slope[h] * (c_row - key_off)
             for h in heads]
        m_new, p, scale = [], [], []
        for h in heads:
            if diag:
                sh = jnp.where(c_iota <= r_iota, s[h], NEG_INF)
                m_new.append(jnp.max(sh, axis=1, keepdims=True))
            else:
                sh = s[h]
                m_new.append(jnp.maximum(carry[h], jnp.max(sh, axis=1, keepdims=True)))
                scale.append(jnp.exp(carry[h] - m_new[h]))
            p.append(jnp.exp(sh - m_new[h]).astype(MXU_DTYPE))
        pv = [_dot(p[h], va_ref[h, pl.ds(j0, blk), :]) for h in heads]
        acc = pv if diag else [scale[h] * carry[nh + h] + pv[h] for h in heads]
        return tuple(m_new) + tuple(acc)

    own0 = pl.multiple_of(qi * blk, blk)
    carry0 = tiles(own0, 0.0, None, True)

    def body(j, carry):
        j0 = pl.multiple_of(j * blk, blk)
        return tiles(j0, ((qi - j) * blk).astype(F32), carry, False)

    res = lax.fori_loop(0, qi, body, carry0)
    for h in range(nh):
        acc = res[nh + h]
        o_ref[0, :, cols[h]] = (acc[:, 0:HEAD_DIM] / acc[:, HEAD_DIM:HEAD_DIM + 1]).astype(o_ref.dtype)


def _moba(proj_a):
    bsz, s, _ = proj_a.shape
    blk = MOBA_BLOCK
    nb = s // blk
    n_grp = MOBA_HEADS // ATT_HEADS_PER_STEP
    slopes = 2.0 ** (-8.0 * np.arange(1, MOBA_HEADS + 1) / MOBA_HEADS)
    sl = jnp.asarray(np.repeat(slopes, HEAD_DIM).reshape(n_grp, 1, ATT_W), F32)
    qb, kb, vb = A_MOBA_Q // ATT_W, A_MOBA_K // ATT_W, A_MOBA_V // ATT_W
    return pl.pallas_call(
        functools.partial(_moba_kernel, nb=nb),
        grid=(bsz, n_grp, nb),
        in_specs=[pl.BlockSpec((1, blk, ATT_W), lambda b, hg, i: (b, i, qb + hg)),
                  pl.BlockSpec((1, s, ATT_W), lambda b, hg, i: (b, 0, kb + hg)),
                  pl.BlockSpec((1, s, ATT_W), lambda b, hg, i: (b, 0, vb + hg)),
                  pl.BlockSpec((1, 1, ATT_W), lambda b, hg, i: (hg, 0, 0))],
        out_specs=pl.BlockSpec((1, blk, ATT_W), lambda b, hg, i: (b, i, hg)),
        out_shape=jax.ShapeDtypeStruct((bsz, s, MOBA_HEADS * HEAD_DIM), MXU_DTYPE),
        scratch_shapes=[pltpu.VMEM((LANES, ATT_W), F32),
                        pltpu.VMEM((ATT_HEADS_PER_STEP, s, 2 * HEAD_DIM), MXU_DTYPE),
                        pltpu.VMEM((ATT_HEADS_PER_STEP, s, 2 * HEAD_DIM), MXU_DTYPE)],
        compiler_params=_params(("parallel", "parallel", "arbitrary")),
        name="moba",
    )(proj_a, proj_a, proj_a, sl)


def _sb_kernel(q_ref, k_ref, v_ref, o_ref):
    blk = SB_BLOCK
    nh = ATT_HEADS_PER_STEP
    qi = pl.program_id(2)
    r_iota = lax.broadcasted_iota(jnp.int32, (blk, blk), 0)
    c_iota = lax.broadcasted_iota(jnp.int32, (blk, blk), 1)
    causal = c_iota < r_iota
    suffix = jnp.where(r_iota > c_iota, 1.0, 0.0).astype(MXU_DTYPE)
    cols = [slice(h * HEAD_DIM, (h + 1) * HEAD_DIM) for h in range(nh)]
    qs = [(q_ref[0, :, cols[h]].astype(F32) * HEAD_DIM ** -0.5).astype(MXU_DTYPE)
          for h in range(nh)]

    def tiles(j0, accs, rests, diag):
        heads = range(nh)
        z = [_dot_nt(qs[h], k_ref[0, pl.ds(j0, blk), cols[h]]) for h in heads]
        log_beta, log_1mb, parts = [], [], []
        for h in heads:
            lb = jnp.minimum(z[h], 0.0) - jnp.log(1.0 + jnp.exp(-jnp.abs(z[h])))
            l1 = lb - z[h]
            if diag:
                l1 = jnp.where(causal, l1, 0.0)
            log_beta.append(lb)
            log_1mb.append(l1)
            parts.append(_split(l1))
        rem = [_dot(parts[h][0], suffix) + _dot(parts[h][1], suffix) for h in heads]
        w = []
        for h in heads:
            wh = jnp.exp(log_beta[h] + rem[h] + rests[h])
            if diag:
                wh = jnp.where(causal, wh, 0.0)
            w.append(wh.astype(MXU_DTYPE))
        outs = [_dot(w[h], v_ref[0, pl.ds(j0, blk), cols[h]]) for h in heads]
        accs = [outs[h] if accs is None else accs[h] + outs[h] for h in heads]
        rests = [rests[h] + rem[h][:, 0:1] + log_1mb[h][:, 0:1] for h in heads]
        return accs, rests

    own0 = pl.multiple_of(qi * blk, blk)
    accs, rests = tiles(own0, None, [jnp.zeros((blk, 1), F32)] * nh, True)

    def body(t, carry):
        j0 = pl.multiple_of((qi - 1 - t) * blk, blk)
        accs, rests = tiles(j0, list(carry[:nh]), list(carry[nh:]), False)
        return tuple(accs + rests)

    res = lax.fori_loop(0, qi, body, tuple(accs + rests))
    for h in range(nh):
        o_ref[0, :, cols[h]] = res[h].astype(o_ref.dtype)


def _stick_breaking(proj_a):
    bsz, s, _ = proj_a.shape
    blk = SB_BLOCK
    n_grp = SB_HEADS // ATT_HEADS_PER_STEP
    qb, kb, vb = A_SB_Q // ATT_W, A_SB_K // ATT_W, A_SB_V // ATT_W
    return pl.pallas_call(
        _sb_kernel,
        grid=(bsz, n_grp, s // blk),
        in_specs=[pl.BlockSpec((1, blk, ATT_W), lambda b, hg, i: (b, i, qb + hg)),
                  pl.BlockSpec((1, s, ATT_W), lambda b, hg, i: (b, 0, kb + hg)),
                  pl.BlockSpec((1, s, ATT_W), lambda b, hg, i: (b, 0, vb + hg))],
        out_specs=pl.BlockSpec((1, blk, ATT_W), lambda b, hg, i: (b, i, hg)),
        out_shape=jax.ShapeDtypeStruct((bsz, s, SB_HEADS * HEAD_DIM), MXU_DTYPE),
        compiler_params=_params(("parallel", "parallel", "arbitrary")),
        name="stick_breaking",
    )(proj_a, proj_a, proj_a)


def _ret_kernel(q_ref, k_ref, v_ref, g_ref, gn_g_ref, gn_b_ref, o_ref, st_ref):
    ch = RET_CHUNK

    @pl.when(pl.program_id(1) == 0)
    def _():
        st_ref[...] = jnp.zeros_like(st_ref)

    i_col = lax.broadcasted_iota(jnp.int32, (ch, 1), 0).astype(F32)
    diff = (lax.broadcasted_iota(jnp.int32, (ch, ch), 0)
            - lax.broadcasted_iota(jnp.int32, (ch, ch), 1)).astype(F32)
    for h in range(RET_HEADS):
        log_g = float(np.log(1.0 - 2.0 ** (-5.0 - h)))
        q = q_ref[0, :, h * RET_DK:(h + 1) * RET_DK]
        qs = (q.astype(F32) * RET_DK ** -0.5).astype(MXU_DTYPE)
        k = k_ref[0, :, h * RET_DK:(h + 1) * RET_DK]
        v = v_ref[0, :, h * RET_DV:(h + 1) * RET_DV]
        decay = jnp.where(diff >= 0, jnp.exp(jnp.maximum(diff, 0.0) * log_g), 0.0)
        qk = _dot_nt(qs, k) * decay
        o = _dot(qk.astype(MXU_DTYPE), v)
        prev = st_ref[h]
        q_dec = jnp.exp((i_col + 1.0) * log_g)
        o = o + _dot(qs, prev.astype(MXU_DTYPE)) * q_dec
        k_dec = jnp.exp((ch - 1.0 - i_col) * log_g)
        kd = (k.astype(F32) * k_dec).astype(MXU_DTYPE)
        st_ref[h] = float(np.exp(ch * log_g)) * prev + _dot_tn(kd, v)
        mu = jnp.mean(o, axis=-1, keepdims=True)
        d = o - mu
        var = jnp.mean(d * d, axis=-1, keepdims=True)
        cs = slice(h * RET_DV, (h + 1) * RET_DV)
        on = d * lax.rsqrt(var + NORM_EPS) * gn_g_ref[:, cs] + gn_b_ref[:, cs]
        o_ref[0, :, cs] = (_silu(g_ref[0, :, cs]) * on).astype(o_ref.dtype)


def _retention(proj_a, proj_b, gn_g, gn_b):
    bsz, s, _ = proj_a.shape
    ch = RET_CHUNK
    wq, wv = RET_HEADS * RET_DK, RET_HEADS * RET_DV
    return pl.pallas_call(
        _ret_kernel,
        grid=(bsz, s // ch),
        in_specs=[pl.BlockSpec((1, ch, wq), lambda b, i: (b, i, A_RET_Q // wq)),
                  pl.BlockSpec((1, ch, wq), lambda b, i: (b, i, A_RET_K // wq)),
                  pl.BlockSpec((1, ch, wv), lambda b, i: (b, i, A_RET_V // wv)),
                  pl.BlockSpec((1, ch, wv), lambda b, i: (b, i, B_RET_G // wv)),
                  pl.BlockSpec((1, wv), lambda b, i: (0, 0)),
                  pl.BlockSpec((1, wv), lambda b, i: (0, 0))],
        out_specs=pl.BlockSpec((1, ch, wv), lambda b, i: (b, i, 0)),
        out_shape=jax.ShapeDtypeStruct((bsz, s, wv), MXU_DTYPE),
        scratch_shapes=[pltpu.VMEM((RET_HEADS, RET_DK, RET_DV), F32)],
        compiler_params=_params(("parallel", "arbitrary")),
        name="retention",
    )(proj_a, proj_a, proj_a, proj_b, gn_g.reshape(1, wv), gn_b.reshape(1, wv))


def _ssd_kernel(z_ref, xbc_ref, dt_ref, cw_ref, cb_ref, dtb_ref, alog_ref, dsk_ref, ng_ref,
                o_ref, tail_ref, st_ref, y_ref):
    ch = SSD_CHUNK
    heads_per_group = SSD_HEADS // SSD_GROUPS
    halo = 8

    @pl.when(pl.program_id(1) == 0)
    def _():
        tail_ref[...] = jnp.zeros_like(tail_ref)
        st_ref[...] = jnp.zeros_like(st_ref)

    xbc = xbc_ref[0]
    xpad = jnp.concatenate([tail_ref[...], xbc], axis=0)
    tail_ref[...] = xbc[ch - halo:, :]
    conv = cb_ref[...]
    for kk in range(SSD_CONV):
        off = halo - (SSD_CONV - 1) + kk
        conv = conv + cw_ref[kk:kk + 1, :] * xpad[off:off + ch, :]
    xc = _silu(conv)
    xs = xc[:, :SSD_D_INNER]
    bm = xc[:, SSD_D_INNER:SSD_D_INNER + SSD_GROUPS * SSD_STATE]
    cm = xc[:, SSD_D_INNER + SSD_GROUPS * SSD_STATE:]

    dtr = dt_ref[0] + dtb_ref[...]
    dt = jnp.maximum(dtr, 0.0) + jnp.log1p(jnp.exp(-jnp.abs(dtr)))
    a = -jnp.exp(alog_ref[...])
    r_iota = lax.broadcasted_iota(jnp.int32, (ch, ch), 0)
    c_iota = lax.broadcasted_iota(jnp.int32, (ch, ch), 1)
    causal = r_iota >= c_iota
    tri = jnp.where(causal, 1.0, 0.0).astype(MXU_DTYPE)
    da = dt * a
    da_hi, da_lo = _split(da)
    cum = _dot(tri, da_hi) + _dot(tri, da_lo)
    cum_t = cum.T
    dt_t = dt.T
    cum_last = cum[ch - 1:ch, :]
    decay_st = jnp.exp(cum_last - cum) * dt
    e_cum = jnp.exp(cum)
    e_last = jnp.exp(cum_last)

    for g in range(SSD_GROUPS):
        bm_g = bm[:, g * SSD_STATE:(g + 1) * SSD_STATE].astype(MXU_DTYPE)
        cm_g = cm[:, g * SSD_STATE:(g + 1) * SSD_STATE].astype(MXU_DTYPE)
        cb = _dot_nt(cm_g, bm_g)
        for r in range(heads_per_group):
            hd = g * heads_per_group + r
            cs = slice(hd * SSD_HEAD_DIM, (hd + 1) * SSD_HEAD_DIM)
            x_h = xs[:, cs]
            seg = cum[:, hd:hd + 1] - cum_t[hd:hd + 1, :]
            lmat = jnp.exp(jnp.where(causal, seg, NEG_INF))
            w = cb * lmat * dt_t[hd:hd + 1, :]
            y = _dot(w.astype(MXU_DTYPE), x_h.astype(MXU_DTYPE))
            prev = st_ref[hd]
            y = y + _dot(cm_g, prev.astype(MXU_DTYPE)) * e_cum[:, hd:hd + 1]
            xd = (x_h * decay_st[:, hd:hd + 1]).astype(MXU_DTYPE)
            st_ref[hd] = e_last[:, hd:hd + 1] * prev + _dot_tn(bm_g, xd)
            y_ref[:, cs] = y + dsk_ref[:, cs] * x_h

    y = y_ref[...] * _silu(z_ref[0])
    gw = SSD_D_INNER // SSD_GROUPS
    for g in range(SSD_GROUPS):
        cs = slice(g * gw, (g + 1) * gw)
        yg = y[:, cs]
        ms = jnp.mean(yg * yg, axis=-1, keepdims=True)
        o_ref[0, :, cs] = (yg * lax.rsqrt(ms + LN_EPS) * ng_ref[:, cs]).astype(o_ref.dtype)


def _ssd(proj_b, conv_w, conv_b, dt_bias, a_log, d_skip, norm_g):
    bsz, s, _ = proj_b.shape
    ch = SSD_CHUNK
    pad = LANES - SSD_HEADS
    dtb = jnp.pad(dt_bias, (0, pad)).reshape(1, LANES)
    alog = jnp.pad(a_log, (0, pad)).reshape(1, LANES)
    dsk = jnp.repeat(d_skip, SSD_HEAD_DIM).reshape(1, SSD_D_INNER)
    const = lambda b, i: (0, 0)
    return pl.pallas_call(
        _ssd_kernel,
        grid=(bsz, s // ch),
        in_specs=[pl.BlockSpec((1, ch, SSD_D_INNER), lambda b, i: (b, i, B_SSD_Z // SSD_D_INNER)),
                  pl.BlockSpec((1, ch, SSD_XBC), lambda b, i: (b, i, B_XBC // SSD_XBC)),
                  pl.BlockSpec((1, ch, LANES), lambda b, i: (b, i, B_DT // LANES)),
                  pl.BlockSpec((SSD_CONV, SSD_XBC), const),
                  pl.BlockSpec((1, SSD_XBC), const),
                  pl.BlockSpec((1, LANES), const),
                  pl.BlockSpec((1, LANES), const),
                  pl.BlockSpec((1, SSD_D_INNER), const),
                  pl.BlockSpec((1, SSD_D_INNER), const)],
        out_specs=pl.BlockSpec((1, ch, SSD_D_INNER), lambda b, i: (b, i, 0)),
        out_shape=jax.ShapeDtypeStruct((bsz, s, SSD_D_INNER), MXU_DTYPE),
        scratch_shapes=[pltpu.VMEM((8, SSD_XBC), F32),
                        pltpu.VMEM((SSD_HEADS, SSD_STATE, SSD_HEAD_DIM), F32),
                        pltpu.VMEM((ch, SSD_D_INNER), F32)],
        compiler_params=_params(("parallel", "arbitrary")),
        name="ssd",
    )(proj_b, proj_b, proj_b, conv_w, conv_b.reshape(1, SSD_XBC), dtb, alog, dsk,
      norm_g.reshape(1, SSD_D_INNER))


def _merge_kernel(oa_ref, ob_ref, oc_ref, od_ref, gates_ref, x_ref, g1_ref, wbr_ref, wout_ref,
                  lg_ref, lb_ref, o_ref):
    merged = None
    for n, br in enumerate((oa_ref, ob_ref, oc_ref, od_ref)):
        y = _dot(br[0], wbr_ref[n])
        t = _sigmoid(gates_ref[0, :, n * D_MODEL:(n + 1) * D_MODEL]) * y
        merged = t if merged is None else merged + t
    mix = _dot(merged.astype(MXU_DTYPE), wout_ref[...])
    o_ref[0] = _resid_ln(x_ref[0], g1_ref[0], mix, lg_ref[...], lb_ref[...])


def _merge(o_a, o_b, o_c, o_d, proj_b, x, g1, w_br, w_out, ln_g, ln_b, tm=512):
    bsz, s, d = x.shape
    tm = min(tm, s)
    gw = N_BRANCH * d
    row = lambda b, i: (b, i, 0)
    const2 = lambda b, i: (0, 0)
    br_spec = pl.BlockSpec((1, tm, BRANCH_W), row)
    return pl.pallas_call(
        _merge_kernel,
        grid=(bsz, s // tm),
        in_specs=[br_spec, br_spec, br_spec, br_spec,
                  pl.BlockSpec((1, tm, gw), lambda b, i: (b, i, B_GATES // gw)),
                  pl.BlockSpec((1, tm, d), row),
                  pl.BlockSpec((1, 1, d), lambda b, i: (b, 0, 0)),
                  pl.BlockSpec((N_BRANCH, BRANCH_W, d), lambda b, i: (0, 0, 0)),
                  pl.BlockSpec((d, d), const2),
                  pl.BlockSpec((1, d), const2),
                  pl.BlockSpec((1, d), const2)],
        out_specs=pl.BlockSpec((1, tm, d), row),
        out_shape=jax.ShapeDtypeStruct((bsz, s, d), F32),
        compiler_params=_params(("parallel", "parallel")),
        name="merge",
    )(o_a, o_b, o_c, o_d, proj_b, x, g1, w_br, w_out, ln_g.reshape(1, d), ln_b.reshape(1, d))


def _ffn_kernel(x_ref, sc_ref, sh_ref, g2_ref, wg_ref, wu_ref, wd_ref, lg_ref, lb_ref,
                o_ref, h_ref, acc_ref):
    j = pl.program_id(2)

    @pl.when(j == 0)
    def _():
        h = x_ref[0] * (1.0 + sc_ref[0]) + sh_ref[0]
        h_ref[...] = h.astype(MXU_DTYPE)
        acc_ref[...] = jnp.zeros_like(acc_ref)

    h = h_ref[...]
    act = _silu(_dot(h, wg_ref[...])) * _dot(h, wu_ref[...])
    acc_ref[...] += _dot(act.astype(MXU_DTYPE), wd_ref[...])

    @pl.when(j == pl.num_programs(2) - 1)
    def _():
        o_ref[0] = _resid_ln(x_ref[0], g2_ref[0], acc_ref[...], lg_ref[...], lb_ref[...])


def _ffn(x, sc, sh, g2, w_gu, w_down, ln_g, ln_b, tm=512, tf=1408):
    bsz, s, d = x.shape
    tm = min(tm, s)
    nf = D_FF // tf
    row = lambda b, i, j: (b, i, 0)
    vec = lambda b, i, j: (b, 0, 0)
    const2 = lambda b, i, j: (0, 0)
    return pl.pallas_call(
        _ffn_kernel,
        grid=(bsz, s // tm, nf),
        in_specs=[pl.BlockSpec((1, tm, d), row),
                  pl.BlockSpec((1, 1, d), vec), pl.BlockSpec((1, 1, d), vec),
                  pl.BlockSpec((1, 1, d), vec),
                  pl.BlockSpec((d, tf), lambda b, i, j: (0, j)),
                  pl.BlockSpec((d, tf), lambda b, i, j: (0, nf + j)),
                  pl.BlockSpec((tf, d), lambda b, i, j: (j, 0)),
                  pl.BlockSpec((1, d), const2), pl.BlockSpec((1, d), const2)],
        out_specs=pl.BlockSpec((1, tm, d), row),
        out_shape=jax.ShapeDtypeStruct((bsz, s, d), F32),
        scratch_shapes=[pltpu.VMEM((tm, d), MXU_DTYPE), pltpu.VMEM((tm, d), F32)],
        compiler_params=_params(("parallel", "parallel", "arbitrary")),
        name="ffn_dense",
    )(x, sc, sh, g2, w_gu, w_gu, w_down, ln_g.reshape(1, d), ln_b.reshape(1, d))


def _moe_kernel(x_ref, sc_ref, sh_ref, g2_ref, rw_ref, rb_ref, wg_ref, wu_ref, wd_ref,
                lg_ref, lb_ref, o_ref, h_ref, acc_ref, comb_ref):
    e = pl.program_id(2)
    j = pl.program_id(3)
    lane = lax.broadcasted_iota(jnp.int32, comb_ref.shape, 1)

    @pl.when(jnp.logical_and(e == 0, j == 0))
    def _():
        h = x_ref[0] * (1.0 + sc_ref[0]) + sh_ref[0]
        h_ref[...] = h.astype(MXU_DTYPE)
        acc_ref[...] = jnp.zeros_like(acc_ref)
        logits = _dot3(h, rw_ref[...]) + rb_ref[...]
        logits = jnp.where(lane < N_EXPERTS, logits, -jnp.inf)
        v1 = jnp.max(logits, axis=1, keepdims=True)
        i1 = jnp.min(jnp.where(logits == v1, lane, LANES), axis=1, keepdims=True)
        rest = jnp.where(lane == i1, -jnp.inf, logits)
        v2 = jnp.max(rest, axis=1, keepdims=True)
        i2 = jnp.min(jnp.where(rest == v2, lane, LANES), axis=1, keepdims=True)
        e2 = jnp.exp(v2 - v1)
        w1 = 1.0 / (1.0 + e2)
        w2 = e2 / (1.0 + e2)
        comb_ref[...] = jnp.where(lane == i1, w1, 0.0) + jnp.where(lane == i2, w2, 0.0)

    h = h_ref[...]
    ce = jnp.sum(jnp.where(lane == e, comb_ref[...], 0.0), axis=1, keepdims=True)
    act = _silu(_dot(h, wg_ref[0])) * _dot(h, wu_ref[0]) * ce
    acc_ref[...] += _dot(act.astype(MXU_DTYPE), wd_ref[0])

    @pl.when(jnp.logical_and(e == pl.num_programs(2) - 1, j == pl.num_programs(3) - 1))
    def _():
        o_ref[0] = _resid_ln(x_ref[0], g2_ref[0], acc_ref[...], lg_ref[...], lb_ref[...])


def _moe(x, sc, sh, g2, router_w, router_b, w_gu, w_down, ln_g, ln_b, tm=512, tf=896):
    bsz, s, d = x.shape
    tm = min(tm, s)
    nf = D_FF_EXPERT // tf
    rw = jnp.pad(router_w, ((0, 0), (0, LANES - N_EXPERTS)))
    rb = jnp.pad(router_b, (0, LANES - N_EXPERTS)).reshape(1, LANES)
    row = lambda b, i, e, j: (b, i, 0)
    vec = lambda b, i, e, j: (b, 0, 0)
    const2 = lambda b, i, e, j: (0, 0)
    return pl.pallas_call(
        _moe_kernel,
        grid=(bsz, s // tm, N_EXPERTS, nf),
        in_specs=[pl.BlockSpec((1, tm, d), row),
                  pl.BlockSpec((1, 1, d), vec), pl.BlockSpec((1, 1, d), vec),
                  pl.BlockSpec((1, 1, d), vec),
                  pl.BlockSpec((d, LANES), const2), pl.BlockSpec((1, LANES), const2),
                  pl.BlockSpec((1, d, tf), lambda b, i, e, j: (e, 0, j)),
                  pl.BlockSpec((1, d, tf), lambda b, i, e, j: (e, 0, nf + j)),
                  pl.BlockSpec((1, tf, d), lambda b, i, e, j: (e, j, 0)),
                  pl.BlockSpec((1, d), const2), pl.BlockSpec((1, d), const2)],
        out_specs=pl.BlockSpec((1, tm, d), row),
        out_shape=jax.ShapeDtypeStruct((bsz, s, d), F32),
        scratch_shapes=[pltpu.VMEM((tm, d), MXU_DTYPE), pltpu.VMEM((tm, d), F32),
                        pltpu.VMEM((tm, LANES), F32)],
        compiler_params=_params(("parallel", "parallel", "arbitrary", "arbitrary")),
        name="moe",
    )(x, sc, sh, g2, rw, rb, w_gu, w_gu, w_down, ln_g.reshape(1, d), ln_b.reshape(1, d))


def _split_w_in(w):
    p = np.cumsum([0, 512, 512, 512, 256, 256, 512, 512, 512, SSD_XBC, SSD_HEADS, 512, 512, 512,
                   N_BRANCH * D_MODEL])
    mq, mk, mv, rq, rk, rv, rg, sz, sxbc, sdt, bq, bk, bv, gates = (
        w[:, p[i]:p[i + 1]] for i in range(14))
    w_a = jnp.concatenate([mq, mk, mv, rq, rk, rv, bq, bk, bv], axis=1).astype(MXU_DTYPE)
    sdt = jnp.pad(sdt, ((0, 0), (0, LANES - SSD_HEADS)))
    w_b = jnp.concatenate([gates, rg, sz, sxbc, sdt], axis=1).astype(MXU_DTYPE)
    return w_a, w_b


def kernel(x, c, w_ada, b_ada, w_in, conv_w, conv_b, dt_bias, a_log, d_skip, ssm_norm_g, ret_gn_g, ret_gn_b, w_br, w_out, ln1_g, ln1_b, ln2_g, ln2_b, ffn_w_gu, ffn_w_down, router_w, router_b, expert_w_gu, expert_w_down):
    bsz = x.shape[0]
    mod = _ada(c, w_ada, b_ada)
    for l in range(DEPTH):
        sh1, sc1, g1, sh2, sc2, g2 = (
            mod[l, :, i * D_MODEL:(i + 1) * D_MODEL].reshape(bsz, 1, D_MODEL) for i in range(6))
        w_a, w_b = _split_w_in(w_in[l])
        proj_a = _inproj(x, sc1, sh1, w_a, MXU_DTYPE, tn=1024)
        proj_b = _inproj(x, sc1, sh1, w_b, F32, tn=896)
        o_a = _moba(proj_a)
        o_b = _retention(proj_a, proj_b, ret_gn_g[l], ret_gn_b[l])
        o_c = _ssd(proj_b, conv_w[l], conv_b[l], dt_bias[l], a_log[l], d_skip[l], ssm_norm_g[l])
        o_d = _stick_breaking(proj_a)
        x = _merge(o_a, o_b, o_c, o_d, proj_b, x, g1, w_br[l].astype(MXU_DTYPE),
                   w_out[l].astype(MXU_DTYPE), ln1_g[l], ln1_b[l])
        if l % 2 == 0:
            x = _ffn(x, sc2, sh2, g2, ffn_w_gu[l // 2].astype(MXU_DTYPE),
                     ffn_w_down[l // 2].astype(MXU_DTYPE), ln2_g[l], ln2_b[l])
        else:
            x = _moe(x, sc2, sh2, g2, router_w[l // 2], router_b[l // 2],
                     expert_w_gu[l // 2].astype(MXU_DTYPE), expert_w_down[l // 2].astype(MXU_DTYPE),
                     ln2_g[l], ln2_b[l])
    return x
```

```python
import functools

import numpy as np
import jax
import jax.numpy as jnp
from jax import lax
from jax.experimental import pallas as pl
from jax.experimental.pallas import tpu as pltpu

F32 = jnp.float32
MXU_DTYPE = jnp.bfloat16

D_MODEL = 1024
DEPTH = 2
MOBA_HEADS = 8
MOBA_BLOCK = 256
MOBA_TOPK = 3
RET_HEADS = 4
RET_DK = 64
RET_DV = 128
RET_CHUNK = 256
SSD_D_INNER = 512
SSD_HEAD_DIM = 64
SSD_HEADS = 8
SSD_GROUPS = 2
SSD_STATE = 128
SSD_CONV = 4
SSD_CHUNK = 256
SSD_XBC = SSD_D_INNER + 2 * SSD_GROUPS * SSD_STATE
SB_HEADS = 8
SB_BLOCK = 256
HEAD_DIM = 64
ATT_HEADS_PER_STEP = 4
ATT_W = ATT_HEADS_PER_STEP * HEAD_DIM
BRANCH_W = 512
N_BRANCH = 4
D_FF = 2816
N_EXPERTS = 8
D_FF_EXPERT = 3584
DEEPNORM_ALPHA = (2.0 * DEPTH) ** 0.25
LN_EPS = 1e-5
NORM_EPS = 1e-6
NEG_INF = -1e30
LANES = 128
VMEM_LIMIT = 48 * 2 ** 20
MOE_VMEM_LIMIT = 56 * 2 ** 20
MOE_CHUNK = 256

A_MOBA_Q, A_MOBA_K, A_MOBA_V = 0, 512, 1024
A_RET_Q, A_RET_K, A_RET_V = 1536, 1792, 2048
A_SB_Q, A_SB_K, A_SB_V = 2560, 3072, 3584
A_WIDTH = 4096
B_GATES, B_RET_G, B_SSD_Z, B_XBC, B_DT = 0, 4096, 4608, 5120, 6144
B_WIDTH = 6272


def _params(sem):
    return pltpu.CompilerParams(dimension_semantics=sem, vmem_limit_bytes=VMEM_LIMIT)


def _dot(a, b):
    return jnp.dot(a, b, preferred_element_type=F32)


def _dot_nt(a, b):
    return lax.dot_general(a, b, (((1,), (1,)), ((), ())), preferred_element_type=F32)


def _dot_tn(a, b):
    return lax.dot_general(a, b, (((0,), (0,)), ((), ())), preferred_element_type=F32)


def _split(x):
    hi = x.astype(MXU_DTYPE)
    lo = (x - hi.astype(F32)).astype(MXU_DTYPE)
    return hi, lo


def _dot_split_lhs(a, b_exact):
    hi, lo = _split(a)
    return _dot(hi, b_exact) + _dot(lo, b_exact)


def _dot3(a, b):
    ah, al = _split(a)
    bh, bl = _split(b)
    return _dot(ah, bh) + _dot(al, bh) + _dot(ah, bl)


def _sigmoid(x):
    return 1.0 / (1.0 + jnp.exp(-x))


def _silu(x):
    return x * _sigmoid(x)


def _resid_ln(x, gate, f, ln_g, ln_b):
    y = DEEPNORM_ALPHA * x + gate * f
    mu = jnp.mean(y, axis=-1, keepdims=True)
    d = y - mu
    var = jnp.mean(d * d, axis=-1, keepdims=True)
    return d * lax.rsqrt(var + LN_EPS) * ln_g + ln_b


def _ada_kernel(c_ref, w_ref, b_ref, o_ref):
    c = c_ref[...]
    o_ref[0] = _dot3(_silu(c), w_ref[0]) + b_ref[0]


def _ada(c, w_ada, b_ada):
    depth, d, n = w_ada.shape
    bsz = c.shape[0]
    tn = 1536
    return pl.pallas_call(
        _ada_kernel,
        grid=(depth, n // tn),
        in_specs=[pl.BlockSpec((bsz, d), lambda l, j: (0, 0)),
                  pl.BlockSpec((1, d, tn), lambda l, j: (l, 0, j)),
                  pl.BlockSpec((1, 1, tn), lambda l, j: (l, 0, j))],
        out_specs=pl.BlockSpec((1, bsz, tn), lambda l, j: (l, 0, j)),
        out_shape=jax.ShapeDtypeStruct((depth, bsz, n), F32),
        compiler_params=_params(("arbitrary", "arbitrary")),
        name="ada_mod",
    )(c, w_ada, b_ada.reshape(depth, 1, n))


def _inproj_kernel(x_ref, sc_ref, sh_ref, w_ref, o_ref, h_ref):
    @pl.when(pl.program_id(2) == 0)
    def _():
        h = x_ref[0] * (1.0 + sc_ref[0]) + sh_ref[0]
        h_ref[...] = h.astype(MXU_DTYPE)

    o_ref[0] = _dot(h_ref[...], w_ref[...]).astype(o_ref.dtype)


def _inproj(x, sc, sh, w, out_dtype, tn, tm=1024):
    bsz, s, d = x.shape
    n = w.shape[1]
    tm = min(tm, s)
    return pl.pallas_call(
        _inproj_kernel,
        grid=(bsz, s // tm, n // tn),
        in_specs=[pl.BlockSpec((1, tm, d), lambda b, i, j: (b, i, 0)),
                  pl.BlockSpec((1, 1, d), lambda b, i, j: (b, 0, 0)),
                  pl.BlockSpec((1, 1, d), lambda b, i, j: (b, 0, 0)),
                  pl.BlockSpec((d, tn), lambda b, i, j: (0, j))],
        out_specs=pl.BlockSpec((1, tm, tn), lambda b, i, j: (b, i, j)),
        out_shape=jax.ShapeDtypeStruct((bsz, s, n), out_dtype),
        scratch_shapes=[pltpu.VMEM((tm, d), MXU_DTYPE)],
        compiler_params=_params(("parallel", "parallel", "arbitrary")),
        name="in_proj",
    )(x, sc, sh, w)


def _moba_kernel(q_ref, k_ref, v_ref, sl_ref, o_ref, km_ref, ka_ref, va_ref, *, nb):
    blk = MOBA_BLOCK
    nh = ATT_HEADS_PER_STEP
    qi = pl.program_id(2)
    s_len = nb * blk
    nb_pad = -(-nb // 8) * 8
    assert nb_pad <= HEAD_DIM
    cols = [slice(h * HEAD_DIM, (h + 1) * HEAD_DIM) for h in range(nh)]

    @pl.when(qi == 0)
    def _():
        k_all = k_ref[0]
        v_all = v_ref[0]
        kk = k_all.astype(F32).reshape(nb, blk, ATT_W)
        km_ref[...] = jnp.zeros_like(km_ref)
        km_ref[0:nb, :] = jnp.sum(kk, axis=1) * (1.0 / blk)
        key_block = lax.broadcasted_iota(jnp.int32, (s_len, HEAD_DIM), 0) // blk
        lane = lax.broadcasted_iota(jnp.int32, (s_len, HEAD_DIM), 1)
        one_hot = jnp.where(lane == key_block, 1.0, 0.0).astype(MXU_DTYPE)
        ones_col = jnp.where(lane == 0, 1.0, 0.0).astype(MXU_DTYPE)
        for h in range(nh):
            ka_ref[h] = jnp.concatenate([k_all[:, cols[h]], one_hot], axis=1)
            va_ref[h] = jnp.concatenate([v_all[:, cols[h]], ones_col], axis=1)

    r_iota = lax.broadcasted_iota(jnp.int32, (blk, blk), 0)
    c_iota = lax.broadcasted_iota(jnp.int32, (blk, blk), 1)
    c_row = lax.broadcasted_iota(jnp.int32, (1, blk), 1).astype(F32)
    n_iota = lax.broadcasted_iota(jnp.int32, (nb_pad, blk), 0)
    past = n_iota < qi
    k_sel = min(MOBA_TOPK, max(nb - 1, 1))

    qa, slope = [], []
    for h in range(nh):
        q = q_ref[0, :, cols[h]]
        qs = (q.astype(F32) * HEAD_DIM ** -0.5).astype(MXU_DTYPE)
        slope.append(sl_ref[0, :, h * HEAD_DIM:h * HEAD_DIM + 1])
        km_hi, km_lo = _split(km_ref[:, cols[h]])
        gate = (_dot_nt(km_hi, q) + _dot_nt(km_lo, q))[0:nb_pad, :]
        g = jnp.where(past, gate, NEG_INF)
        g = jnp.where(n_iota < nb, g, -jnp.inf)
        sel = jnp.zeros((nb_pad, blk), F32)
        for _ in range(k_sel):
            mx = jnp.max(g, axis=0, keepdims=True)
            idx = jnp.min(jnp.where(g == mx, n_iota, nb_pad), axis=0, keepdims=True)
            pick = n_iota == idx
            sel = jnp.where(pick, 1.0, sel)
            g = jnp.where(pick, -jnp.inf, g)
        keep = jnp.logical_or(jnp.logical_and(past, sel > 0.5), n_iota == qi)
        bias_t = jnp.where(keep, 0.0, NEG_INF)
        bias_t = jnp.concatenate([bias_t, jnp.zeros((LANES - nb_pad, blk), F32)], axis=0)
        bias = bias_t.T[:, 0:HEAD_DIM]
        qa.append(jnp.concatenate([qs, bias.astype(MXU_DTYPE)], axis=1))

    def tiles(j0, key_off, carry, diag):
        heads = range(nh)
        s = [_dot_nt(qa[h], ka_ref[h, pl.ds(j0, blk), :]) + slope[h] * (c_row - key_off)
             for h in heads]
        m_new, p, scale = [], [], []
        for h in heads:
            if diag:
                sh = jnp.where(c_iota <= r_iota, s[h], NEG_INF)
                m_new.append(jnp.max(sh, axis=1, keepdims=True))
            else:
                sh = s[h]
                m_new.append(jnp.maximum(carry[h], jnp.max(sh, axis=1, keepdims=True)))
                scale.append(jnp.exp(carry[h] - m_new[h]))
            p.append(jnp.exp(sh - m_new[h]).astype(MXU_DTYPE))
        pv = [_dot(p[h], va_ref[h, pl.ds(j0, blk), :]) for h in heads]
        acc = pv if diag else [scale[h] * carry[nh + h] + pv[h] for h in heads]
        return tuple(m_new) + tuple(acc)

    own0 = pl.multiple_of(qi * blk, blk)
    carry0 = tiles(own0, 0.0, None, True)

    def body(j, carry):
        j0 = pl.multiple_of(j * blk, blk)
        return tiles(j0, ((qi - j) * blk).astype(F32), carry, False)

    res = lax.fori_loop(0, qi, body, carry0)
    for h in range(nh):
        acc = res[nh + h]
        o_ref[0, :, cols[h]] = (acc[:, 0:HEAD_DIM] / acc[:, HEAD_DIM:HEAD_DIM + 1]).astype(o_ref.dtype)


def _moba(proj_a):
    bsz, s, _ = proj_a.shape
    blk = MOBA_BLOCK
    nb = s // blk
    n_grp = MOBA_HEADS // ATT_HEADS_PER_STEP
    slopes = 2.0 ** (-8.0 * np.arange(1, MOBA_HEADS + 1) / MOBA_HEADS)
    sl = jnp.asarray(np.repeat(slopes, HEAD_DIM).reshape(n_grp, 1, ATT_W), F32)
    qb, kb, vb = A_MOBA_Q // ATT_W, A_MOBA_K // ATT_W, A_MOBA_V // ATT_W
    return pl.pallas_call(
        functools.partial(_moba_kernel, nb=nb),
        grid=(bsz, n_grp, nb),
        in_specs=[pl.BlockSpec((1, blk, ATT_W), lambda b, hg, i: (b, i, qb + hg)),
                  pl.BlockSpec((1, s, ATT_W), lambda b, hg, i: (b, 0, kb + hg)),
                  pl.BlockSpec((1, s, ATT_W), lambda b, hg, i: (b, 0, vb + hg)),
                  pl.BlockSpec((1, 1, ATT_W), lambda b, hg, i: (hg, 0, 0))],
        out_specs=pl.BlockSpec((1, blk, ATT_W), lambda b, hg, i: (b, i, hg)),
        out_shape=jax.ShapeDtypeStruct((bsz, s, MOBA_HEADS * HEAD_DIM), MXU_DTYPE),
        scratch_shapes=[pltpu.VMEM((LANES, ATT_W), F32),
                        pltpu.VMEM((ATT_HEADS_PER_STEP, s, 2 * HEAD_DIM), MXU_DTYPE),
                        pltpu.VMEM((ATT_HEADS_PER_STEP, s, 2 * HEAD_DIM), MXU_DTYPE)],
        compiler_params=_params(("parallel", "parallel", "arbitrary")),
        name="moba",
    )(proj_a, proj_a, proj_a, sl)


def _sb_kernel(q_ref, k_ref, v_ref, o_ref):
    blk = SB_BLOCK
    nh = ATT_HEADS_PER_STEP
    qi = pl.program_id(2)
    r_iota = lax.broadcasted_iota(jnp.int32, (blk, blk), 0)
    c_iota = lax.broadcasted_iota(jnp.int32, (blk, blk), 1)
    causal = c_iota < r_iota
    suffix = jnp.where(r_iota > c_iota, 1.0, 0.0).astype(MXU_DTYPE)
    cols = [slice(h * HEAD_DIM, (h + 1) * HEAD_DIM) for h in range(nh)]
    qs = [(q_ref[0, :, cols[h]].astype(F32) * HEAD_DIM ** -0.5).astype(MXU_DTYPE)
          for h in range(nh)]

    def tiles(j0, accs, rests, diag):
        heads = range(nh)
        z = [_dot_nt(qs[h], k_ref[0, pl.ds(j0, blk), cols[h]]) for h in heads]
        log_beta, log_1mb, parts = [], [], []
        for h in heads:
            lb = jnp.minimum(z[h], 0.0) - jnp.log(1.0 + jnp.exp(-jnp.abs(z[h])))
            l1 = lb - z[h]
            if diag:
                l1 = jnp.where(causal, l1, 0.0)
            log_beta.append(lb)
            log_1mb.append(l1)
            parts.append(_split(l1))
        rem = [_dot(parts[h][0], suffix) + _dot(parts[h][1], suffix) for h in heads]
        w = []
        for h in heads:
            wh = jnp.exp(log_beta[h] + rem[h] + rests[h])
            if diag:
                wh = jnp.where(causal, wh, 0.0)
            w.append(wh.astype(MXU_DTYPE))
        outs = [_dot(w[h], v_ref[0, pl.ds(j0, blk), cols[h]]) for h in heads]
        accs = [outs[h] if accs is None else accs[h] + outs[h] for h in heads]
        rests = [rests[h] + rem[h][:, 0:1] + log_1mb[h][:, 0:1] for h in heads]
        return accs, rests

    own0 = pl.multiple_of(qi * blk, blk)
    accs, rests = tiles(own0, None, [jnp.zeros((blk, 1), F32)] * nh, True)

    def body(t, carry):
        j0 = pl.multiple_of((qi - 1 - t) * blk, blk)
        accs, rests = tiles(j0, list(carry[:nh]), list(carry[nh:]), False)
        return tuple(accs + rests)

    res = lax.fori_loop(0, qi, body, tuple(accs + rests))
    for h in range(nh):
        o_ref[0, :, cols[h]] = res[h].astype(o_ref.dtype)


def _stick_breaking(proj_a):
    bsz, s, _ = proj_a.shape
    blk = SB_BLOCK
    n_grp = SB_HEADS // ATT_HEADS_PER_STEP
    qb, kb, vb = A_SB_Q // ATT_W, A_SB_K // ATT_W, A_SB_V // ATT_W
    return pl.pallas_call(
        _sb_kernel,
        grid=(bsz, n_grp, s // blk),
        in_specs=[pl.BlockSpec((1, blk, ATT_W), lambda b, hg, i: (b, i, qb + hg)),
                  pl.BlockSpec((1, s, ATT_W), lambda b, hg, i: (b, 0, kb + hg)),
                  pl.BlockSpec((1, s, ATT_W), lambda b, hg, i: (b, 0, vb + hg))],
        out_specs=pl.BlockSpec((1, blk, ATT_W), lambda b, hg, i: (b, i, hg)),
        out_shape=jax.ShapeDtypeStruct((bsz, s, SB_HEADS * HEAD_DIM), MXU_DTYPE),
        compiler_params=_params(("parallel", "parallel", "arbitrary")),
        name="stick_breaking",
    )(proj_a, proj_a, proj_a)


def _ret_kernel(q_ref, k_ref, v_ref, g_ref, gn_g_ref, gn_b_ref, o_ref, st_ref):
    ch = RET_CHUNK

    @pl.when(pl.program_id(1) == 0)
    def _():
        st_ref[...] = jnp.zeros_like(st_ref)

    i_col = lax.broadcasted_iota(jnp.int32, (ch, 1), 0).astype(F32)
    diff = (lax.broadcasted_iota(jnp.int32, (ch, ch), 0)
            - lax.broadcasted_iota(jnp.int32, (ch, ch), 1)).astype(F32)
    for h in range(RET_HEADS):
        log_g = float(np.log(1.0 - 2.0 ** (-5.0 - h)))
        q = q_ref[0, :, h * RET_DK:(h + 1) * RET_DK]
        qs = (q.astype(F32) * RET_DK ** -0.5).astype(MXU_DTYPE)
        k = k_ref[0, :, h * RET_DK:(h + 1) * RET_DK]
        v = v_ref[0, :, h * RET_DV:(h + 1) * RET_DV]
        decay = jnp.where(diff >= 0, jnp.exp(jnp.maximum(diff, 0.0) * log_g), 0.0)
        qk = _dot_nt(qs, k) * decay
        o = _dot(qk.astype(MXU_DTYPE), v)
        prev = st_ref[h]
        q_dec = jnp.exp((i_col + 1.0) * log_g)
        o = o + _dot(qs, prev.astype(MXU_DTYPE)) * q_dec
        k_dec = jnp.exp((ch - 1.0 - i_col) * log_g)
        kd = (k.astype(F32) * k_dec).astype(MXU_DTYPE)
        st_ref[h] = float(np.exp(ch * log_g)) * prev + _dot_tn(kd, v)
        mu = jnp.mean(o, axis=-1, keepdims=True)
        d = o - mu
        var = jnp.mean(d * d, axis=-1, keepdims=True)
        cs = slice(h * RET_DV, (h + 1) * RET_DV)
        on = d * lax.rsqrt(var + NORM_EPS) * gn_g_ref[:, cs] + gn_b_ref[:, cs]
        o_ref[0, :, cs] = (_silu(g_ref[0, :, cs]) * on).astype(o_ref.dtype)


def _retention(proj_a, proj_b, gn_g, gn_b):
    bsz, s, _ = proj_a.shape
    ch = RET_CHUNK
    wq, wv = RET_HEADS * RET_DK, RET_HEADS * RET_DV
    return pl.pallas_call(
        _ret_kernel,
        grid=(bsz, s // ch),
        in_specs=[pl.BlockSpec((1, ch, wq), lambda b, i: (b, i, A_RET_Q // wq)),
                  pl.BlockSpec((1, ch, wq), lambda b, i: (b, i, A_RET_K // wq)),
                  pl.BlockSpec((1, ch, wv), lambda b, i: (b, i, A_RET_V // wv)),
                  pl.BlockSpec((1, ch, wv), lambda b, i: (b, i, B_RET_G // wv)),
                  pl.BlockSpec((1, wv), lambda b, i: (0, 0)),
                  pl.BlockSpec((1, wv), lambda b, i: (0, 0))],
        out_specs=pl.BlockSpec((1, ch, wv), lambda b, i: (b, i, 0)),
        out_shape=jax.ShapeDtypeStruct((bsz, s, wv), MXU_DTYPE),
        scratch_shapes=[pltpu.VMEM((RET_HEADS, RET_DK, RET_DV), F32)],
        compiler_params=_params(("parallel", "arbitrary")),
        name="retention",
    )(proj_a, proj_a, proj_a, proj_b, gn_g.reshape(1, wv), gn_b.reshape(1, wv))


def _ssd_kernel(z_ref, xbc_ref, dt_ref, cw_ref, cb_ref, dtb_ref, alog_ref, dsk_ref, ng_ref,
                o_ref, tail_ref, st_ref, y_ref):
    ch = SSD_CHUNK
    heads_per_group = SSD_HEADS // SSD_GROUPS
    halo = 8

    @pl.when(pl.program_id(1) == 0)
    def _():
        tail_ref[...] = jnp.zeros_like(tail_ref)
        st_ref[...] = jnp.zeros_like(st_ref)

    xbc = xbc_ref[0]
    xpad = jnp.concatenate([tail_ref[...], xbc], axis=0)
    tail_ref[...] = xbc[ch - halo:, :]
    conv = cb_ref[...]
    for kk in range(SSD_CONV):
        off = halo - (SSD_CONV - 1) + kk
        conv = conv + cw_ref[kk:kk + 1, :] * xpad[off:off + ch, :]
    xc = _silu(conv)
    xs = xc[:, :SSD_D_INNER]
    bm = xc[:, SSD_D_INNER:SSD_D_INNER + SSD_GROUPS * SSD_STATE]
    cm = xc[:, SSD_D_INNER + SSD_GROUPS * SSD_STATE:]

    dtr = dt_ref[0] + dtb_ref[...]
    dt = jnp.maximum(dtr, 0.0) + jnp.log1p(jnp.exp(-jnp.abs(dtr)))
    a = -jnp.exp(alog_ref[...])
    r_iota = lax.broadcasted_iota(jnp.int32, (ch, ch), 0)
    c_iota = lax.broadcasted_iota(jnp.int32, (ch, ch), 1)
    causal = r_iota >= c_iota
    tri = jnp.where(causal, 1.0, 0.0).astype(MXU_DTYPE)
    da = dt * a
    da_hi, da_lo = _split(da)
    cum = _dot(tri, da_hi) + _dot(tri, da_lo)
    cum_t = cum.T
    dt_t = dt.T
    cum_last = cum[ch - 1:ch, :]
    decay_st = jnp.exp(cum_last - cum) * dt
    e_cum = jnp.exp(cum)
    e_last = jnp.exp(cum_last)

    for g in range(SSD_GROUPS):
        bm_g = bm[:, g * SSD_STATE:(g + 1) * SSD_STATE].astype(MXU_DTYPE)
        cm_g = cm[:, g * SSD_STATE:(g + 1) * SSD_STATE].astype(MXU_DTYPE)
        cb = _dot_nt(cm_g, bm_g)
        for r in range(heads_per_group):
            hd = g * heads_per_group + r
            cs = slice(hd * SSD_HEAD_DIM, (hd + 1) * SSD_HEAD_DIM)
            x_h = xs[:, cs]
            seg = cum[:, hd:hd + 1] - cum_t[hd:hd + 1, :]
            lmat = jnp.exp(jnp.where(causal, seg, NEG_INF))
            w = cb * lmat * dt_t[hd:hd + 1, :]
            y = _dot(w.astype(MXU_DTYPE), x_h.astype(MXU_DTYPE))
            prev = st_ref[hd]
            y = y + _dot(cm_g, prev.astype(MXU_DTYPE)) * e_cum[:, hd:hd + 1]
            xd = (x_h * decay_st[:, hd:hd + 1]).astype(MXU_DTYPE)
            st_ref[hd] = e_last[:, hd:hd + 1] * prev + _dot_tn(bm_g, xd)
            y_ref[:, cs] = y + dsk_ref[:, cs] * x_h

    y = y_ref[...] * _silu(z_ref[0])
    gw = SSD_D_INNER // SSD_GROUPS
    for g in range(SSD_GROUPS):
        cs = slice(g * gw, (g + 1) * gw)
        yg = y[:, cs]
        ms = jnp.mean(yg * yg, axis=-1, keepdims=True)
        o_ref[0, :, cs] = (yg * lax.rsqrt(ms + LN_EPS) * ng_ref[:, cs]).astype(o_ref.dtype)


def _ssd(proj_b, conv_w, conv_b, dt_bias, a_log, d_skip, norm_g):
    bsz, s, _ = proj_b.shape
    ch = SSD_CHUNK
    pad = LANES - SSD_HEADS
    dtb = jnp.pad(dt_bias, (0, pad)).reshape(1, LANES)
    alog = jnp.pad(a_log, (0, pad)).reshape(1, LANES)
    dsk = jnp.repeat(d_skip, SSD_HEAD_DIM).reshape(1, SSD_D_INNER)
    const = lambda b, i: (0, 0)
    return pl.pallas_call(
        _ssd_kernel,
        grid=(bsz, s // ch),
        in_specs=[pl.BlockSpec((1, ch, SSD_D_INNER), lambda b, i: (b, i, B_SSD_Z // SSD_D_INNER)),
                  pl.BlockSpec((1, ch, SSD_XBC), lambda b, i: (b, i, B_XBC // SSD_XBC)),
                  pl.BlockSpec((1, ch, LANES), lambda b, i: (b, i, B_DT // LANES)),
                  pl.BlockSpec((SSD_CONV, SSD_XBC), const),
                  pl.BlockSpec((1, SSD_XBC), const),
                  pl.BlockSpec((1, LANES), const),
                  pl.BlockSpec((1, LANES), const),
                  pl.BlockSpec((1, SSD_D_INNER), const),
                  pl.BlockSpec((1, SSD_D_INNER), const)],
        out_specs=pl.BlockSpec((1, ch, SSD_D_INNER), lambda b, i: (b, i, 0)),
        out_shape=jax.ShapeDtypeStruct((bsz, s, SSD_D_INNER), MXU_DTYPE),
        scratch_shapes=[pltpu.VMEM((8, SSD_XBC), F32),
                        pltpu.VMEM((SSD_HEADS, SSD_STATE, SSD_HEAD_DIM), F32),
                        pltpu.VMEM((ch, SSD_D_INNER), F32)],
        compiler_params=_params(("parallel", "arbitrary")),
        name="ssd",
    )(proj_b, proj_b, proj_b, conv_w, conv_b.reshape(1, SSD_XBC), dtb, alog, dsk,
      norm_g.reshape(1, SSD_D_INNER))


def _merge_kernel(oa_ref, ob_ref, oc_ref, od_ref, gates_ref, x_ref, g1_ref, wbr_ref, wout_ref,
                  lg_ref, lb_ref, o_ref):
    merged = None
    for n, br in enumerate((oa_ref, ob_ref, oc_ref, od_ref)):
        y = _dot(br[0], wbr_ref[n])
        t = _sigmoid(gates_ref[0, :, n * D_MODEL:(n + 1) * D_MODEL]) * y
        merged = t if merged is None else merged + t
    mix = _dot(merged.astype(MXU_DTYPE), wout_ref[...])
    o_ref[0] = _resid_ln(x_ref[0], g1_ref[0], mix, lg_ref[...], lb_ref[...])


def _merge(o_a, o_b, o_c, o_d, proj_b, x, g1, w_br, w_out, ln_g, ln_b, tm=512):
    bsz, s, d = x.shape
    tm = min(tm, s)
    gw = N_BRANCH * d
    row = lambda b, i: (b, i, 0)
    const2 = lambda b, i: (0, 0)
    br_spec = pl.BlockSpec((1, tm, BRANCH_W), row)
    return pl.pallas_call(
        _merge_kernel,
        grid=(bsz, s // tm),
        in_specs=[br_spec, br_spec, br_spec, br_spec,
                  pl.BlockSpec((1, tm, gw), lambda b, i: (b, i, B_GATES // gw)),
                  pl.BlockSpec((1, tm, d), row),
                  pl.BlockSpec((1, 1, d), lambda b, i: (b, 0, 0)),
                  pl.BlockSpec((N_BRANCH, BRANCH_W, d), lambda b, i: (0, 0, 0)),
                  pl.BlockSpec((d, d), const2),
                  pl.BlockSpec((1, d), const2),
                  pl.BlockSpec((1, d), const2)],
        out_specs=pl.BlockSpec((1, tm, d), row),
        out_shape=jax.ShapeDtypeStruct((bsz, s, d), F32),
        compiler_params=_params(("parallel", "parallel")),
        name="merge",
    )(o_a, o_b, o_c, o_d, proj_b, x, g1, w_br, w_out, ln_g.reshape(1, d), ln_b.reshape(1, d))


def _ffn_kernel(x_ref, sc_ref, sh_ref, g2_ref, wg_ref, wu_ref, wd_ref, lg_ref, lb_ref,
                o_ref, h_ref, acc_ref):
    j = pl.program_id(2)

    @pl.when(j == 0)
    def _():
        h = x_ref[0] * (1.0 + sc_ref[0]) + sh_ref[0]
        h_ref[...] = h.astype(MXU_DTYPE)
        acc_ref[...] = jnp.zeros_like(acc_ref)

    h = h_ref[...]
    act = _silu(_dot(h, wg_ref[...])) * _dot(h, wu_ref[...])
    acc_ref[...] += _dot(act.astype(MXU_DTYPE), wd_ref[...])

    @pl.when(j == pl.num_programs(2) - 1)
    def _():
        o_ref[0] = _resid_ln(x_ref[0], g2_ref[0], acc_ref[...], lg_ref[...], lb_ref[...])


def _ffn(x, sc, sh, g2, w_gu, w_down, ln_g, ln_b, tm=512, tf=1408):
    bsz, s, d = x.shape
    tm = min(tm, s)
    nf = D_FF // tf
    row = lambda b, i, j: (b, i, 0)
    vec = lambda b, i, j: (b, 0, 0)
    const2 = lambda b, i, j: (0, 0)
    return pl.pallas_call(
        _ffn_kernel,
        grid=(bsz, s // tm, nf),
        in_specs=[pl.BlockSpec((1, tm, d), row),
                  pl.BlockSpec((1, 1, d), vec), pl.BlockSpec((1, 1, d), vec),
                  pl.BlockSpec((1, 1, d), vec),
                  pl.BlockSpec((d, tf), lambda b, i, j: (0, j)),
                  pl.BlockSpec((d, tf), lambda b, i, j: (0, nf + j)),
                  pl.BlockSpec((tf, d), lambda b, i, j: (j, 0)),
                  pl.BlockSpec((1, d), const2), pl.BlockSpec((1, d), const2)],
        out_specs=pl.BlockSpec((1, tm, d), row),
        out_shape=jax.ShapeDtypeStruct((bsz, s, d), F32),
        scratch_shapes=[pltpu.VMEM((tm, d), MXU_DTYPE), pltpu.VMEM((tm, d), F32)],
        compiler_params=_params(("parallel", "parallel", "arbitrary")),
        name="ffn_dense",
    )(x, sc, sh, g2, w_gu, w_gu, w_down, ln_g.reshape(1, d), ln_b.reshape(1, d))


def _router_kernel(x_ref, sc_ref, sh_ref, rw_ref, rb_ref,
                   h_ref, comb_ref, slot_ref, slot_t_ref, cnt_ref):
    tm = x_ref.shape[1]
    lane = lax.broadcasted_iota(jnp.int32, (tm, LANES), 1)
    h = x_ref[0] * (1.0 + sc_ref[0]) + sh_ref[0]
    h_ref[0] = h.astype(MXU_DTYPE)
    logits = _dot3(h, rw_ref[...]) + rb_ref[...]
    logits = jnp.where(lane < N_EXPERTS, logits, -jnp.inf)
    v1 = jnp.max(logits, axis=1, keepdims=True)
    i1 = jnp.min(jnp.where(logits == v1, lane, LANES), axis=1, keepdims=True)
    rest = jnp.where(lane == i1, -jnp.inf, logits)
    v2 = jnp.max(rest, axis=1, keepdims=True)
    i2 = jnp.min(jnp.where(rest == v2, lane, LANES), axis=1, keepdims=True)
    e2 = jnp.exp(v2 - v1)
    w1 = 1.0 / (1.0 + e2)
    w2 = e2 / (1.0 + e2)
    comb_ref[0] = jnp.where(lane == i1, w1, 0.0) + jnp.where(lane == i2, w2, 0.0)
    routed = jnp.logical_or(lane == i1, lane == i2)
    earlier = (lax.broadcasted_iota(jnp.int32, (tm, tm), 1)
               < lax.broadcasted_iota(jnp.int32, (tm, tm), 0))
    before = _dot(jnp.where(earlier, 1.0, 0.0).astype(MXU_DTYPE),
                  jnp.where(routed, 1.0, 0.0).astype(MXU_DTYPE))
    slot = jnp.where(routed, before, -1.0)
    slot_ref[0] = slot
    slot_t_ref[0] = slot.T[0:N_EXPERTS, :]
    cnt_ref[0] = jnp.sum(jnp.where(routed, 1, 0), axis=0, keepdims=True)


def _moe_kernel(cnt_ref, h_ref, x_ref, g2_ref, comb_ref, slot_ref, slot_t_ref,
                wg_ref, wu_ref, wd_ref, lg_ref, lb_ref, o_ref, xe_ref, ye_ref, *, n_tiles):
    tm = x_ref.shape[1]
    ch = MOE_CHUNK
    e = pl.program_id(2)
    j = pl.program_id(3)
    last_j = pl.num_programs(3) - 1
    count = cnt_ref[(pl.program_id(0) * n_tiles + pl.program_id(1)) * N_EXPERTS + e]
    n_chunks = (count + ch - 1) // ch

    @pl.when(jnp.logical_and(e == 0, j == 0))
    def _():
        o_ref[0] = jnp.zeros((tm, D_MODEL), F32)

    @pl.when(j == 0)
    def _():
        slot_row = slot_t_ref[0, pl.ds(e, 1), :]
        row = lax.broadcasted_iota(jnp.int32, (ch, tm), 0).astype(F32)

        def gather(c, carry):
            r0 = pl.multiple_of(c * ch, ch)
            pick = jnp.where(slot_row == row + (c * ch).astype(F32), 1.0, 0.0).astype(MXU_DTYPE)
            xe_ref[pl.ds(r0, ch), :] = _dot(pick, h_ref[0]).astype(MXU_DTYPE)
            ye_ref[pl.ds(r0, ch), :] = jnp.zeros((ch, D_MODEL), F32)
            return carry

        lax.fori_loop(0, n_chunks, gather, 0)

    def expert(c, carry):
        r0 = pl.multiple_of(c * ch, ch)
        xc = xe_ref[pl.ds(r0, ch), :]
        act = _silu(_dot(xc, wg_ref[0])) * _dot(xc, wu_ref[0])
        ye_ref[pl.ds(r0, ch), :] += _dot(act.astype(MXU_DTYPE), wd_ref[0])
        return carry

    lax.fori_loop(0, n_chunks, expert, 0)

    @pl.when(j == last_j)
    def _():
        lane = lax.broadcasted_iota(jnp.int32, (tm, LANES), 1)
        mine = lane == e
        slot_col = jnp.sum(jnp.where(mine, slot_ref[0], 0.0), axis=1, keepdims=True)
        comb_col = jnp.sum(jnp.where(mine, comb_ref[0], 0.0), axis=1, keepdims=True)
        col = lax.broadcasted_iota(jnp.int32, (tm, ch), 1).astype(F32)

        def scatter(c, carry):
            r0 = pl.multiple_of(c * ch, ch)
            place = jnp.where(slot_col == col + (c * ch).astype(F32), 1.0, 0.0).astype(MXU_DTYPE)
            o_ref[0] += comb_col * _dot(place, ye_ref[pl.ds(r0, ch), :].astype(MXU_DTYPE))
            return carry

        lax.fori_loop(0, n_chunks, scatter, 0)

    @pl.when(jnp.logical_and(e == pl.num_programs(2) - 1, j == last_j))
    def _():
        o_ref[0] = _resid_ln(x_ref[0], g2_ref[0], o_ref[0], lg_ref[...], lb_ref[...])


def _moe(x, sc, sh, g2, router_w, router_b, w_gu, w_down, ln_g, ln_b, tm=1024, tf=896):
    bsz, s, d = x.shape
    tm = min(tm, s)
    nt = s // tm
    nf = D_FF_EXPERT // tf
    rw = jnp.pad(router_w, ((0, 0), (0, LANES - N_EXPERTS)))
    rb = jnp.pad(router_b, (0, LANES - N_EXPERTS)).reshape(1, LANES)

    row2 = lambda b, i: (b, i, 0)
    vec2 = lambda b, i: (b, 0, 0)
    h, comb, slot, slot_t, cnt = pl.pallas_call(
        _router_kernel,
        grid=(bsz, nt),
        in_specs=[pl.BlockSpec((1, tm, d), row2),
                  pl.BlockSpec((1, 1, d), vec2), pl.BlockSpec((1, 1, d), vec2),
                  pl.BlockSpec((d, LANES), lambda b, i: (0, 0)),
                  pl.BlockSpec((1, LANES), lambda b, i: (0, 0))],
        out_specs=[pl.BlockSpec((1, tm, d), row2),
                   pl.BlockSpec((1, tm, LANES), row2),
                   pl.BlockSpec((1, tm, LANES), row2),
                   pl.BlockSpec((1, N_EXPERTS, tm), lambda b, i: (b * nt + i, 0, 0)),
                   pl.BlockSpec((1, 1, LANES), lambda b, i: (b * nt + i, 0, 0))],
        out_shape=[jax.ShapeDtypeStruct((bsz, s, d), MXU_DTYPE),
                   jax.ShapeDtypeStruct((bsz, s, LANES), F32),
                   jax.ShapeDtypeStruct((bsz, s, LANES), F32),
                   jax.ShapeDtypeStruct((bsz * nt, N_EXPERTS, tm), F32),
                   jax.ShapeDtypeStruct((bsz * nt, 1, LANES), jnp.int32)],
        compiler_params=_params(("parallel", "parallel")),
        name="moe_router",
    )(x, sc, sh, rw, rb)
    counts = cnt[:, 0, :N_EXPERTS].reshape(bsz * nt * N_EXPERTS)

    row = lambda b, i, e, j, cnt: (b, i, 0)
    vec = lambda b, i, e, j, cnt: (b, 0, 0)
    const2 = lambda b, i, e, j, cnt: (0, 0)
    grid_spec = pltpu.PrefetchScalarGridSpec(
        num_scalar_prefetch=1,
        grid=(bsz, nt, N_EXPERTS, nf),
        in_specs=[pl.BlockSpec((1, tm, d), row),
                  pl.BlockSpec((1, tm, d), row),
                  pl.BlockSpec((1, 1, d), vec),
                  pl.BlockSpec((1, tm, LANES), row),
                  pl.BlockSpec((1, tm, LANES), row),
                  pl.BlockSpec((1, N_EXPERTS, tm), lambda b, i, e, j, cnt: (b * nt + i, 0, 0)),
                  pl.BlockSpec((1, d, tf), lambda b, i, e, j, cnt: (e, 0, j)),
                  pl.BlockSpec((1, d, tf), lambda b, i, e, j, cnt: (e, 0, nf + j)),
                  pl.BlockSpec((1, tf, d), lambda b, i, e, j, cnt: (e, j, 0)),
                  pl.BlockSpec((1, d), const2), pl.BlockSpec((1, d), const2)],
        out_specs=pl.BlockSpec((1, tm, d), row),
        scratch_shapes=[pltpu.VMEM((tm, d), MXU_DTYPE), pltpu.VMEM((tm, d), F32)])
    return pl.pallas_call(
        functools.partial(_moe_kernel, n_tiles=nt),
        grid_spec=grid_spec,
        out_shape=jax.ShapeDtypeStruct((bsz, s, d), F32),
        compiler_params=pltpu.CompilerParams(
            dimension_semantics=("parallel", "parallel", "arbitrary", "arbitrary"),
            vmem_limit_bytes=MOE_VMEM_LIMIT),
        name="moe_experts",
    )(counts, h, x, g2, comb, slot, slot_t, w_gu, w_gu, w_down,
      ln_g.reshape(1, d), ln_b.reshape(1, d))


def _split_w_in(w):
    p = np.cumsum([0, 512, 512, 512, 256, 256, 512, 512, 512, SSD_XBC, SSD_HEADS, 512, 512, 512,
                   N_BRANCH * D_MODEL])
    mq, mk, mv, rq, rk, rv, rg, sz, sxbc, sdt, bq, bk, bv, gates = (
        w[:, p[i]:p[i + 1]] for i in range(14))
    w_a = jnp.concatenate([mq, mk, mv, rq, rk, rv, bq, bk, bv], axis=1).astype(MXU_DTYPE)
    sdt = jnp.pad(sdt, ((0, 0), (0, LANES - SSD_HEADS)))
    w_b = jnp.concatenate([gates, rg, sz, sxbc, sdt], axis=1).astype(MXU_DTYPE)
    return w_a, w_b


def kernel(x, c, w_ada, b_ada, w_in, conv_w, conv_b, dt_bias, a_log, d_skip, ssm_norm_g, ret_gn_g, ret_gn_b, w_br, w_out, ln1_g, ln1_b, ln2_g, ln2_b, ffn_w_gu, ffn_w_down, router_w, router_b, expert_w_gu, expert_w_down):
    bsz = x.shape[0]
    mod = _ada(c, w_ada, b_ada)
    for l in range(DEPTH):
        sh1, sc1, g1, sh2, sc2, g2 = (
            mod[l, :, i * D_MODEL:(i + 1) * D_MODEL].reshape(bsz, 1, D_MODEL) for i in range(6))
        w_a, w_b = _split_w_in(w_in[l])
        proj_a = _inproj(x, sc1, sh1, w_a, MXU_DTYPE, tn=1024)
        proj_b = _inproj(x, sc1, sh1, w_b, F32, tn=896)
        o_a = _moba(proj_a)
        o_b = _retention(proj_a, proj_b, ret_gn_g[l], ret_gn_b[l])
        o_c = _ssd(proj_b, conv_w[l], conv_b[l], dt_bias[l], a_log[l], d_skip[l], ssm_norm_g[l])
        o_d = _stick_breaking(proj_a)
        x = _merge(o_a, o_b, o_c, o_d, proj_b, x, g1, w_br[l].astype(MXU_DTYPE),
                   w_out[l].astype(MXU_DTYPE), ln1_g[l], ln1_b[l])
        if l % 2 == 0:
            x = _ffn(x, sc2, sh2, g2, ffn_w_gu[l // 2].astype(MXU_DTYPE),
                     ffn_w_down[l // 2].astype(MXU_DTYPE), ln2_g[l], ln2_b[l])
        else:
            x = _moe(x, sc2, sh2, g2, router_w[l // 2], router_b[l // 2],
                     expert_w_gu[l // 2].astype(MXU_DTYPE), expert_w_down[l // 2].astype(MXU_DTYPE),
                     ln2_g[l], ln2_b[l])
    return x
```

```python
import functools

import numpy as np
import jax
import jax.numpy as jnp
from jax import lax
from jax.experimental import pallas as pl
from jax.experimental.pallas import tpu as pltpu

F32 = jnp.float32
MXU_DTYPE = jnp.bfloat16

D_MODEL = 1024
DEPTH = 2
MOBA_HEADS = 8
MOBA_BLOCK = 256
MOBA_TOPK = 3
RET_HEADS = 4
RET_DK = 64
RET_DV = 128
RET_CHUNK = 256
SSD_D_INNER = 512
SSD_HEAD_DIM = 64
SSD_HEADS = 8
SSD_GROUPS = 2
SSD_STATE = 128
SSD_CONV = 4
SSD_CHUNK = 256
SSD_XBC = SSD_D_INNER + 2 * SSD_GROUPS * SSD_STATE
SB_HEADS = 8
SB_BLOCK = 256
HEAD_DIM = 64
ATT_HEADS_PER_STEP = 4
ATT_W = ATT_HEADS_PER_STEP * HEAD_DIM
BRANCH_W = 512
N_BRANCH = 4
D_FF = 2816
N_EXPERTS = 8
D_FF_EXPERT = 3584
DEEPNORM_ALPHA = (2.0 * DEPTH) ** 0.25
LN_EPS = 1e-5
NORM_EPS = 1e-6
NEG_INF = -1e30
LANES = 128
VMEM_LIMIT = 48 * 2 ** 20
MOE_VMEM_LIMIT = 56 * 2 ** 20
MOE_CHUNK = 256

A_MOBA_Q, A_MOBA_K, A_MOBA_V = 0, 512, 1024
A_RET_Q, A_RET_K, A_RET_V = 1536, 1792, 2048
A_SB_Q, A_SB_K, A_SB_V = 2560, 3072, 3584
A_WIDTH = 4096
B_GATES, B_RET_G, B_SSD_Z, B_XBC, B_DT = 0, 4096, 4608, 5120, 6144
B_WIDTH = 6272


def _params(sem):
    return pltpu.CompilerParams(dimension_semantics=sem, vmem_limit_bytes=VMEM_LIMIT)


def _dot(a, b):
    return jnp.dot(a, b, preferred_element_type=F32)


def _dot_nt(a, b):
    return lax.dot_general(a, b, (((1,), (1,)), ((), ())), preferred_element_type=F32)


def _dot_tn(a, b):
    return lax.dot_general(a, b, (((0,), (0,)), ((), ())), preferred_element_type=F32)


def _split(x):
    hi = x.astype(MXU_DTYPE)
    lo = (x - hi.astype(F32)).astype(MXU_DTYPE)
    return hi, lo


def _dot_split_lhs(a, b_exact):
    hi, lo = _split(a)
    return _dot(hi, b_exact) + _dot(lo, b_exact)


def _dot3(a, b):
    ah, al = _split(a)
    bh, bl = _split(b)
    return _dot(ah, bh) + _dot(al, bh) + _dot(ah, bl)


def _sigmoid(x):
    return 1.0 / (1.0 + jnp.exp(-x))


def _silu(x):
    return x * _sigmoid(x)


def _resid_ln(x, gate, f, ln_g, ln_b):
    y = DEEPNORM_ALPHA * x + gate * f
    mu = jnp.mean(y, axis=-1, keepdims=True)
    d = y - mu
    var = jnp.mean(d * d, axis=-1, keepdims=True)
    return d * lax.rsqrt(var + LN_EPS) * ln_g + ln_b


def _ada_kernel(c_ref, w_ref, b_ref, o_ref):
    c = c_ref[...]
    o_ref[0] = _dot3(_silu(c), w_ref[0]) + b_ref[0]


def _ada(c, w_ada, b_ada):
    depth, d, n = w_ada.shape
    bsz = c.shape[0]
    tn = 1536
    return pl.pallas_call(
        _ada_kernel,
        grid=(depth, n // tn),
        in_specs=[pl.BlockSpec((bsz, d), lambda l, j: (0, 0)),
                  pl.BlockSpec((1, d, tn), lambda l, j: (l, 0, j)),
                  pl.BlockSpec((1, 1, tn), lambda l, j: (l, 0, j))],
        out_specs=pl.BlockSpec((1, bsz, tn), lambda l, j: (l, 0, j)),
        out_shape=jax.ShapeDtypeStruct((depth, bsz, n), F32),
        compiler_params=_params(("arbitrary", "arbitrary")),
        name="ada_mod",
    )(c, w_ada, b_ada.reshape(depth, 1, n))


def _inproj_kernel(x_ref, sc_ref, sh_ref, w_ref, o_ref, h_ref):
    @pl.when(pl.program_id(2) == 0)
    def _():
        h = x_ref[0] * (1.0 + sc_ref[0]) + sh_ref[0]
        h_ref[...] = h.astype(MXU_DTYPE)

    o_ref[0] = _dot(h_ref[...], w_ref[...]).astype(o_ref.dtype)


def _inproj(x, sc, sh, w, out_dtype, tn, tm=1024):
    bsz, s, d = x.shape
    n = w.shape[1]
    tm = min(tm, s)
    return pl.pallas_call(
        _inproj_kernel,
        grid=(bsz, s // tm, n // tn),
        in_specs=[pl.BlockSpec((1, tm, d), lambda b, i, j: (b, i, 0)),
                  pl.BlockSpec((1, 1, d), lambda b, i, j: (b, 0, 0)),
                  pl.BlockSpec((1, 1, d), lambda b, i, j: (b, 0, 0)),
                  pl.BlockSpec((d, tn), lambda b, i, j: (0, j))],
        out_specs=pl.BlockSpec((1, tm, tn), lambda b, i, j: (b, i, j)),
        out_shape=jax.ShapeDtypeStruct((bsz, s, n), out_dtype),
        scratch_shapes=[pltpu.VMEM((tm, d), MXU_DTYPE)],
        compiler_params=_params(("parallel", "parallel", "arbitrary")),
        name="in_proj",
    )(x, sc, sh, w)


def _moba_kernel(q_ref, k_ref, v_ref, sl_ref, o_ref, km_ref, ka_ref, vt_ref, *, nb):
    blk = MOBA_BLOCK
    nh = ATT_HEADS_PER_STEP
    heads = range(nh)
    qi = pl.program_id(2)
    s_len = nb * blk
    nb_pad = -(-nb // 8) * 8
    extra = 2 * HEAD_DIM - HEAD_DIM
    assert nb_pad + 8 <= extra
    cols = [slice(h * HEAD_DIM, (h + 1) * HEAD_DIM) for h in heads]

    @pl.when(qi == 0)
    def _():
        k_all = k_ref[0]
        kk = k_all.astype(F32).reshape(nb, blk, ATT_W)
        km_ref[...] = jnp.zeros_like(km_ref)
        km_ref[0:nb, :] = jnp.sum(kk, axis=1) * (1.0 / blk)
        pos = lax.broadcasted_iota(jnp.int32, (s_len, extra), 0)
        lane = lax.broadcasted_iota(jnp.int32, (s_len, extra), 1)
        one_hot = jnp.where(lane == pos // blk, 1.0, 0.0)
        pos_hi = jnp.where(lane == nb_pad, (pos // blk * blk).astype(F32), 0.0)
        pos_lo = jnp.where(lane == nb_pad + 1, (pos % blk).astype(F32), 0.0)
        tail = (one_hot + pos_hi + pos_lo).astype(MXU_DTYPE)
        for h in heads:
            ka_ref[h] = jnp.concatenate([k_all[:, cols[h]], tail], axis=1)

        ones_rows = jnp.where(lax.broadcasted_iota(jnp.int32, (extra, blk), 0) == 0, 1.0, 0.0)

        def transpose_block(j, carry):
            j0 = pl.multiple_of(j * blk, blk)
            vt = v_ref[0, pl.ds(j0, blk), :].astype(F32).T
            for h in heads:
                vt_ref[j, h] = jnp.concatenate([vt[cols[h], :], ones_rows], axis=0).astype(MXU_DTYPE)
            return carry

        lax.fori_loop(0, nb, transpose_block, 0)

    key_i = lax.broadcasted_iota(jnp.int32, (blk, blk), 0)
    qry_i = lax.broadcasted_iota(jnp.int32, (blk, blk), 1)
    n_iota = lax.broadcasted_iota(jnp.int32, (nb_pad, blk), 0)
    past = n_iota < qi
    k_sel = min(MOBA_TOPK, max(nb - 1, 1))
    q_t = q_ref[0].astype(F32).T
    slope_rows = lax.broadcasted_iota(jnp.int32, (8, blk), 0) < 2

    qa = []
    for h in heads:
        qh_t = q_t[cols[h], :]
        slope = sl_ref[0, :, h * HEAD_DIM:h * HEAD_DIM + 1]
        km_hi, km_lo = _split(km_ref[:, cols[h]])
        qh_mxu = qh_t.astype(MXU_DTYPE)
        gate = (_dot(km_hi, qh_mxu) + _dot(km_lo, qh_mxu))[0:nb_pad, :]
        g = jnp.where(past, gate, NEG_INF)
        g = jnp.where(n_iota < nb, g, -jnp.inf)
        sel = jnp.zeros((nb_pad, blk), F32)
        for _ in range(k_sel):
            mx = jnp.max(g, axis=0, keepdims=True)
            idx = jnp.min(jnp.where(g == mx, n_iota, nb_pad), axis=0, keepdims=True)
            pick = n_iota == idx
            sel = jnp.where(pick, 1.0, sel)
            g = jnp.where(pick, -jnp.inf, g)
        keep = jnp.logical_or(jnp.logical_and(past, sel > 0.5), n_iota == qi)
        bias = jnp.where(keep, 0.0, NEG_INF)
        alibi = jnp.where(slope_rows, slope, 0.0)
        pad = jnp.zeros((extra - nb_pad - 8, blk), F32)
        qa.append(jnp.concatenate([qh_t * HEAD_DIM ** -0.5, bias, alibi, pad],
                                  axis=0).astype(MXU_DTYPE))

    def tiles(js, carry, diag):
        starts = [pl.multiple_of(j * blk, blk) for j in js]
        s = [[_dot(ka_ref[h, pl.ds(j0, blk), :], qa[h]) for h in heads] for j0 in starts]
        m = list(carry[:nh]) if carry is not None else [None] * nh
        acc = list(carry[nh:]) if carry is not None else [None] * nh
        p = {}
        for h in heads:
            sh = [jnp.where(key_i <= qry_i, s[t][h], NEG_INF) if diag else s[t][h]
                  for t in range(len(js))]
            top = jnp.max(sh[0], axis=0, keepdims=True)
            for t in range(1, len(js)):
                top = jnp.maximum(top, jnp.max(sh[t], axis=0, keepdims=True))
            if m[h] is None:
                m[h] = top
            else:
                m_new = jnp.maximum(m[h], top)
                acc[h] = jnp.exp(m[h] - m_new) * acc[h]
                m[h] = m_new
            for t in range(len(js)):
                p[t, h] = jnp.exp(sh[t] - m[h]).astype(MXU_DTYPE)
        for t, j in enumerate(js):
            for h in heads:
                pv = _dot(vt_ref[j, h], p[t, h])
                acc[h] = pv if acc[h] is None else acc[h] + pv
        return tuple(m) + tuple(acc)

    carry0 = tiles([qi], None, True)

    def pair(t, carry):
        return tiles([2 * t, 2 * t + 1], carry, False)

    def single(t, carry):
        return tiles([qi - 1], carry, False)

    res = lax.fori_loop(0, qi // 2, pair, carry0)
    res = lax.fori_loop(0, qi % 2, single, res)
    out_t = [res[nh + h][0:HEAD_DIM, :] / res[nh + h][HEAD_DIM:HEAD_DIM + 1, :] for h in heads]
    o_ref[0] = jnp.concatenate(out_t, axis=0).T.astype(o_ref.dtype)


def _moba(proj_a):
    bsz, s, _ = proj_a.shape
    blk = MOBA_BLOCK
    nb = s // blk
    n_grp = MOBA_HEADS // ATT_HEADS_PER_STEP
    slopes = 2.0 ** (-8.0 * np.arange(1, MOBA_HEADS + 1) / MOBA_HEADS)
    sl = jnp.asarray(np.repeat(slopes, HEAD_DIM).reshape(n_grp, 1, ATT_W), F32)
    qb, kb, vb = A_MOBA_Q // ATT_W, A_MOBA_K // ATT_W, A_MOBA_V // ATT_W
    return pl.pallas_call(
        functools.partial(_moba_kernel, nb=nb),
        grid=(bsz, n_grp, nb),
        in_specs=[pl.BlockSpec((1, blk, ATT_W), lambda b, hg, i: (b, i, qb + hg)),
                  pl.BlockSpec((1, s, ATT_W), lambda b, hg, i: (b, 0, kb + hg)),
                  pl.BlockSpec((1, s, ATT_W), lambda b, hg, i: (b, 0, vb + hg)),
                  pl.BlockSpec((1, 1, ATT_W), lambda b, hg, i: (hg, 0, 0))],
        out_specs=pl.BlockSpec((1, blk, ATT_W), lambda b, hg, i: (b, i, hg)),
        out_shape=jax.ShapeDtypeStruct((bsz, s, MOBA_HEADS * HEAD_DIM), MXU_DTYPE),
        scratch_shapes=[pltpu.VMEM((LANES, ATT_W), F32),
                        pltpu.VMEM((ATT_HEADS_PER_STEP, s, 2 * HEAD_DIM), MXU_DTYPE),
                        pltpu.VMEM((nb, ATT_HEADS_PER_STEP, 2 * HEAD_DIM, blk), MXU_DTYPE)],
        compiler_params=_params(("parallel", "parallel", "arbitrary")),
        name="moba",
    )(proj_a, proj_a, proj_a, sl)


def _sb_kernel(q_ref, k_ref, v_ref, o_ref, vt_ref, *, nb):
    blk = SB_BLOCK
    nh = ATT_HEADS_PER_STEP
    heads = range(nh)
    qi = pl.program_id(2)
    cols = [slice(h * HEAD_DIM, (h + 1) * HEAD_DIM) for h in heads]

    @pl.when(qi == 0)
    def _():
        def transpose_block(j, carry):
            j0 = pl.multiple_of(j * blk, blk)
            vt = v_ref[0, pl.ds(j0, blk), :].astype(F32).T
            for h in heads:
                vt_ref[j, h] = vt[cols[h], :].astype(MXU_DTYPE)
            return carry

        lax.fori_loop(0, nb, transpose_block, 0)

    key_i = lax.broadcasted_iota(jnp.int32, (blk, blk), 0)
    qry_i = lax.broadcasted_iota(jnp.int32, (blk, blk), 1)
    causal = key_i < qry_i
    later = jnp.where(qry_i > key_i, 1.0, 0.0).astype(MXU_DTYPE)
    later2 = jnp.concatenate([later, later], axis=1)
    qs = [(q_ref[0, :, cols[h]].astype(F32) * HEAD_DIM ** -0.5).astype(MXU_DTYPE)
          for h in heads]

    def tiles(js, accs, rests, diag):
        starts = [pl.multiple_of(j * blk, blk) for j in js]
        z = [[_dot_nt(k_ref[0, pl.ds(j0, blk), cols[h]], qs[h]) for h in heads] for j0 in starts]
        log_beta, log_1mb, parts = [], [], []
        for t in range(len(js)):
            for h in heads:
                zz = z[t][h]
                lb = jnp.minimum(zz, 0.0) - jnp.log(1.0 + jnp.exp(-jnp.abs(zz)))
                l1 = lb - zz
                if diag:
                    l1 = jnp.where(causal, l1, 0.0)
                log_beta.append(lb)
                log_1mb.append(l1)
                parts.append(jnp.concatenate(_split(l1), axis=0))
        rem = [_dot(later2, p) for p in parts]
        w = []
        rests = list(rests)
        for t in range(len(js)):
            for h in heads:
                i = t * nh + h
                wh = jnp.exp(log_beta[i] + rem[i] + rests[h])
                if diag:
                    wh = jnp.where(causal, wh, 0.0)
                w.append(wh.astype(MXU_DTYPE))
                rests[h] = rests[h] + rem[i][0:1, :] + log_1mb[i][0:1, :]
        accs = list(accs)
        for t, j in enumerate(js):
            for h in heads:
                accs[h] = accs[h] + _dot(vt_ref[j, h], w[t * nh + h])
        return accs, rests

    zero_acc = [jnp.zeros((HEAD_DIM, blk), F32)] * nh
    accs, rests = tiles([qi], zero_acc, [jnp.zeros((1, blk), F32)] * nh, True)

    def pair(t, carry):
        j = qi - 1 - 2 * t
        accs, rests = tiles([j, j - 1], carry[:nh], carry[nh:], False)
        return tuple(accs + rests)

    def single(t, carry):
        accs, rests = tiles([0], carry[:nh], carry[nh:], False)
        return tuple(accs + rests)

    res = lax.fori_loop(0, qi // 2, pair, tuple(accs + rests))
    res = lax.fori_loop(0, qi % 2, single, res)
    o_ref[0] = jnp.concatenate(res[:nh], axis=0).T.astype(o_ref.dtype)


def _stick_breaking(proj_a):
    bsz, s, _ = proj_a.shape
    blk = SB_BLOCK
    n_grp = SB_HEADS // ATT_HEADS_PER_STEP
    qb, kb, vb = A_SB_Q // ATT_W, A_SB_K // ATT_W, A_SB_V // ATT_W
    return pl.pallas_call(
        functools.partial(_sb_kernel, nb=s // blk),
        grid=(bsz, n_grp, s // blk),
        in_specs=[pl.BlockSpec((1, blk, ATT_W), lambda b, hg, i: (b, i, qb + hg)),
                  pl.BlockSpec((1, s, ATT_W), lambda b, hg, i: (b, 0, kb + hg)),
                  pl.BlockSpec((1, s, ATT_W), lambda b, hg, i: (b, 0, vb + hg))],
        out_specs=pl.BlockSpec((1, blk, ATT_W), lambda b, hg, i: (b, i, hg)),
        out_shape=jax.ShapeDtypeStruct((bsz, s, SB_HEADS * HEAD_DIM), MXU_DTYPE),
        scratch_shapes=[pltpu.VMEM((s // blk, ATT_HEADS_PER_STEP, HEAD_DIM, blk), MXU_DTYPE)],
        compiler_params=_params(("parallel", "parallel", "arbitrary")),
        name="stick_breaking",
    )(proj_a, proj_a, proj_a)


def _ret_kernel(q_ref, k_ref, v_ref, g_ref, gn_g_ref, gn_b_ref, o_ref, st_ref):
    ch = RET_CHUNK

    @pl.when(pl.program_id(1) == 0)
    def _():
        st_ref[...] = jnp.zeros_like(st_ref)

    i_col = lax.broadcasted_iota(jnp.int32, (ch, 1), 0).astype(F32)
    diff = (lax.broadcasted_iota(jnp.int32, (ch, ch), 0)
            - lax.broadcasted_iota(jnp.int32, (ch, ch), 1)).astype(F32)
    for h in range(RET_HEADS):
        log_g = float(np.log(1.0 - 2.0 ** (-5.0 - h)))
        q = q_ref[0, :, h * RET_DK:(h + 1) * RET_DK]
        qs = (q.astype(F32) * RET_DK ** -0.5).astype(MXU_DTYPE)
        k = k_ref[0, :, h * RET_DK:(h + 1) * RET_DK]
        v = v_ref[0, :, h * RET_DV:(h + 1) * RET_DV]
        decay = jnp.where(diff >= 0, jnp.exp(jnp.maximum(diff, 0.0) * log_g), 0.0)
        qk = _dot_nt(qs, k) * decay
        o = _dot(qk.astype(MXU_DTYPE), v)
        prev = st_ref[h]
        q_dec = jnp.exp((i_col + 1.0) * log_g)
        o = o + _dot(qs, prev.astype(MXU_DTYPE)) * q_dec
        k_dec = jnp.exp((ch - 1.0 - i_col) * log_g)
        kd = (k.astype(F32) * k_dec).astype(MXU_DTYPE)
        st_ref[h] = float(np.exp(ch * log_g)) * prev + _dot_tn(kd, v)
        mu = jnp.mean(o, axis=-1, keepdims=True)
        d = o - mu
        var = jnp.mean(d * d, axis=-1, keepdims=True)
        cs = slice(h * RET_DV, (h + 1) * RET_DV)
        on = d * lax.rsqrt(var + NORM_EPS) * gn_g_ref[:, cs] + gn_b_ref[:, cs]
        o_ref[0, :, cs] = (_silu(g_ref[0, :, cs]) * on).astype(o_ref.dtype)


def _retention(proj_a, proj_b, gn_g, gn_b):
    bsz, s, _ = proj_a.shape
    ch = RET_CHUNK
    wq, wv = RET_HEADS * RET_DK, RET_HEADS * RET_DV
    return pl.pallas_call(
        _ret_kernel,
        grid=(bsz, s // ch),
        in_specs=[pl.BlockSpec((1, ch, wq), lambda b, i: (b, i, A_RET_Q // wq)),
                  pl.BlockSpec((1, ch, wq), lambda b, i: (b, i, A_RET_K // wq)),
                  pl.BlockSpec((1, ch, wv), lambda b, i: (b, i, A_RET_V // wv)),
                  pl.BlockSpec((1, ch, wv), lambda b, i: (b, i, B_RET_G // wv)),
                  pl.BlockSpec((1, wv), lambda b, i: (0, 0)),
                  pl.BlockSpec((1, wv), lambda b, i: (0, 0))],
        out_specs=pl.BlockSpec((1, ch, wv), lambda b, i: (b, i, 0)),
        out_shape=jax.ShapeDtypeStruct((bsz, s, wv), MXU_DTYPE),
        scratch_shapes=[pltpu.VMEM((RET_HEADS, RET_DK, RET_DV), F32)],
        compiler_params=_params(("parallel", "arbitrary")),
        name="retention",
    )(proj_a, proj_a, proj_a, proj_b, gn_g.reshape(1, wv), gn_b.reshape(1, wv))


def _ssd_kernel(z_ref, xbc_ref, dt_ref, cw_ref, cb_ref, dtb_ref, alog_ref, dsk_ref, ng_ref,
                o_ref, tail_ref, st_ref, y_ref):
    ch = SSD_CHUNK
    heads_per_group = SSD_HEADS // SSD_GROUPS
    halo = 8

    @pl.when(pl.program_id(1) == 0)
    def _():
        tail_ref[...] = jnp.zeros_like(tail_ref)
        st_ref[...] = jnp.zeros_like(st_ref)

    xbc = xbc_ref[0]
    xpad = jnp.concatenate([tail_ref[...], xbc], axis=0)
    tail_ref[...] = xbc[ch - halo:, :]
    conv = cb_ref[...]
    for kk in range(SSD_CONV):
        off = halo - (SSD_CONV - 1) + kk
        conv = conv + cw_ref[kk:kk + 1, :] * xpad[off:off + ch, :]
    xc = _silu(conv)
    xs = xc[:, :SSD_D_INNER]
    bm = xc[:, SSD_D_INNER:SSD_D_INNER + SSD_GROUPS * SSD_STATE]
    cm = xc[:, SSD_D_INNER + SSD_GROUPS * SSD_STATE:]

    dtr = dt_ref[0] + dtb_ref[...]
    dt = jnp.maximum(dtr, 0.0) + jnp.log1p(jnp.exp(-jnp.abs(dtr)))
    a = -jnp.exp(alog_ref[...])
    r_iota = lax.broadcasted_iota(jnp.int32, (ch, ch), 0)
    c_iota = lax.broadcasted_iota(jnp.int32, (ch, ch), 1)
    causal = r_iota >= c_iota
    tri = jnp.where(causal, 1.0, 0.0).astype(MXU_DTYPE)
    da = dt * a
    da_hi, da_lo = _split(da)
    cum = _dot(tri, da_hi) + _dot(tri, da_lo)
    cum_t = cum.T
    dt_t = dt.T
    cum_last = cum[ch - 1:ch, :]
    decay_st = jnp.exp(cum_last - cum) * dt
    e_cum = jnp.exp(cum)
    e_last = jnp.exp(cum_last)

    for g in range(SSD_GROUPS):
        bm_g = bm[:, g * SSD_STATE:(g + 1) * SSD_STATE].astype(MXU_DTYPE)
        cm_g = cm[:, g * SSD_STATE:(g + 1) * SSD_STATE].astype(MXU_DTYPE)
        cb = _dot_nt(cm_g, bm_g)
        for r in range(heads_per_group):
            hd = g * heads_per_group + r
            cs = slice(hd * SSD_HEAD_DIM, (hd + 1) * SSD_HEAD_DIM)
            x_h = xs[:, cs]
            seg = cum[:, hd:hd + 1] - cum_t[hd:hd + 1, :]
            lmat = jnp.exp(jnp.where(causal, seg, NEG_INF))
            w = cb * lmat * dt_t[hd:hd + 1, :]
            y = _dot(w.astype(MXU_DTYPE), x_h.astype(MXU_DTYPE))
            prev = st_ref[hd]
            y = y + _dot(cm_g, prev.astype(MXU_DTYPE)) * e_cum[:, hd:hd + 1]
            xd = (x_h * decay_st[:, hd:hd + 1]).astype(MXU_DTYPE)
            st_ref[hd] = e_last[:, hd:hd + 1] * prev + _dot_tn(bm_g, xd)
            y_ref[:, cs] = y + dsk_ref[:, cs] * x_h

    y = y_ref[...] * _silu(z_ref[0])
    gw = SSD_D_INNER // SSD_GROUPS
    for g in range(SSD_GROUPS):
        cs = slice(g * gw, (g + 1) * gw)
        yg = y[:, cs]
        ms = jnp.mean(yg * yg, axis=-1, keepdims=True)
        o_ref[0, :, cs] = (yg * lax.rsqrt(ms + LN_EPS) * ng_ref[:, cs]).astype(o_ref.dtype)


def _ssd(proj_b, conv_w, conv_b, dt_bias, a_log, d_skip, norm_g):
    bsz, s, _ = proj_b.shape
    ch = SSD_CHUNK
    pad = LANES - SSD_HEADS
    dtb = jnp.pad(dt_bias, (0, pad)).reshape(1, LANES)
    alog = jnp.pad(a_log, (0, pad)).reshape(1, LANES)
    dsk = jnp.repeat(d_skip, SSD_HEAD_DIM).reshape(1, SSD_D_INNER)
    const = lambda b, i: (0, 0)
    return pl.pallas_call(
        _ssd_kernel,
        grid=(bsz, s // ch),
        in_specs=[pl.BlockSpec((1, ch, SSD_D_INNER), lambda b, i: (b, i, B_SSD_Z // SSD_D_INNER)),
                  pl.BlockSpec((1, ch, SSD_XBC), lambda b, i: (b, i, B_XBC // SSD_XBC)),
                  pl.BlockSpec((1, ch, LANES), lambda b, i: (b, i, B_DT // LANES)),
                  pl.BlockSpec((SSD_CONV, SSD_XBC), const),
                  pl.BlockSpec((1, SSD_XBC), const),
                  pl.BlockSpec((1, LANES), const),
                  pl.BlockSpec((1, LANES), const),
                  pl.BlockSpec((1, SSD_D_INNER), const),
                  pl.BlockSpec((1, SSD_D_INNER), const)],
        out_specs=pl.BlockSpec((1, ch, SSD_D_INNER), lambda b, i: (b, i, 0)),
        out_shape=jax.ShapeDtypeStruct((bsz, s, SSD_D_INNER), MXU_DTYPE),
        scratch_shapes=[pltpu.VMEM((8, SSD_XBC), F32),
                        pltpu.VMEM((SSD_HEADS, SSD_STATE, SSD_HEAD_DIM), F32),
                        pltpu.VMEM((ch, SSD_D_INNER), F32)],
        compiler_params=_params(("parallel", "arbitrary")),
        name="ssd",
    )(proj_b, proj_b, proj_b, conv_w, conv_b.reshape(1, SSD_XBC), dtb, alog, dsk,
      norm_g.reshape(1, SSD_D_INNER))


def _merge_kernel(oa_ref, ob_ref, oc_ref, od_ref, gates_ref, x_ref, g1_ref, wbr_ref, wout_ref,
                  lg_ref, lb_ref, o_ref):
    merged = None
    for n, br in enumerate((oa_ref, ob_ref, oc_ref, od_ref)):
        y = _dot(br[0], wbr_ref[n])
        t = _sigmoid(gates_ref[0, :, n * D_MODEL:(n + 1) * D_MODEL]) * y
        merged = t if merged is None else merged + t
    mix = _dot(merged.astype(MXU_DTYPE), wout_ref[...])
    o_ref[0] = _resid_ln(x_ref[0], g1_ref[0], mix, lg_ref[...], lb_ref[...])


def _merge(o_a, o_b, o_c, o_d, proj_b, x, g1, w_br, w_out, ln_g, ln_b, tm=512):
    bsz, s, d = x.shape
    tm = min(tm, s)
    gw = N_BRANCH * d
    row = lambda b, i: (b, i, 0)
    const2 = lambda b, i: (0, 0)
    br_spec = pl.BlockSpec((1, tm, BRANCH_W), row)
    return pl.pallas_call(
        _merge_kernel,
        grid=(bsz, s // tm),
        in_specs=[br_spec, br_spec, br_spec, br_spec,
                  pl.BlockSpec((1, tm, gw), lambda b, i: (b, i, B_GATES // gw)),
                  pl.BlockSpec((1, tm, d), row),
                  pl.BlockSpec((1, 1, d), lambda b, i: (b, 0, 0)),
                  pl.BlockSpec((N_BRANCH, BRANCH_W, d), lambda b, i: (0, 0, 0)),
                  pl.BlockSpec((d, d), const2),
                  pl.BlockSpec((1, d), const2),
                  pl.BlockSpec((1, d), const2)],
        out_specs=pl.BlockSpec((1, tm, d), row),
        out_shape=jax.ShapeDtypeStruct((bsz, s, d), F32),
        compiler_params=_params(("parallel", "parallel")),
        name="merge",
    )(o_a, o_b, o_c, o_d, proj_b, x, g1, w_br, w_out, ln_g.reshape(1, d), ln_b.reshape(1, d))


def _ffn_kernel(x_ref, sc_ref, sh_ref, g2_ref, wg_ref, wu_ref, wd_ref, lg_ref, lb_ref,
                o_ref, h_ref, acc_ref):
    j = pl.program_id(2)

    @pl.when(j == 0)
    def _():
        h = x_ref[0] * (1.0 + sc_ref[0]) + sh_ref[0]
        h_ref[...] = h.astype(MXU_DTYPE)
        acc_ref[...] = jnp.zeros_like(acc_ref)

    h = h_ref[...]
    act = _silu(_dot(h, wg_ref[...])) * _dot(h, wu_ref[...])
    acc_ref[...] += _dot(act.astype(MXU_DTYPE), wd_ref[...])

    @pl.when(j == pl.num_programs(2) - 1)
    def _():
        o_ref[0] = _resid_ln(x_ref[0], g2_ref[0], acc_ref[...], lg_ref[...], lb_ref[...])


def _ffn(x, sc, sh, g2, w_gu, w_down, ln_g, ln_b, tm=512, tf=1408):
    bsz, s, d = x.shape
    tm = min(tm, s)
    nf = D_FF // tf
    row = lambda b, i, j: (b, i, 0)
    vec = lambda b, i, j: (b, 0, 0)
    const2 = lambda b, i, j: (0, 0)
    return pl.pallas_call(
        _ffn_kernel,
        grid=(bsz, s // tm, nf),
        in_specs=[pl.BlockSpec((1, tm, d), row),
                  pl.BlockSpec((1, 1, d), vec), pl.BlockSpec((1, 1, d), vec),
                  pl.BlockSpec((1, 1, d), vec),
                  pl.BlockSpec((d, tf), lambda b, i, j: (0, j)),
                  pl.BlockSpec((d, tf), lambda b, i, j: (0, nf + j)),
                  pl.BlockSpec((tf, d), lambda b, i, j: (j, 0)),
                  pl.BlockSpec((1, d), const2), pl.BlockSpec((1, d), const2)],
        out_specs=pl.BlockSpec((1, tm, d), row),
        out_shape=jax.ShapeDtypeStruct((bsz, s, d), F32),
        scratch_shapes=[pltpu.VMEM((tm, d), MXU_DTYPE), pltpu.VMEM((tm, d), F32)],
        compiler_params=_params(("parallel", "parallel", "arbitrary")),
        name="ffn_dense",
    )(x, sc, sh, g2, w_gu, w_gu, w_down, ln_g.reshape(1, d), ln_b.reshape(1, d))


def _router_kernel(x_ref, sc_ref, sh_ref, rw_ref, rb_ref,
                   h_ref, comb_ref, slot_ref, slot_t_ref, cnt_ref):
    tm = x_ref.shape[1]
    lane = lax.broadcasted_iota(jnp.int32, (tm, LANES), 1)
    h = x_ref[0] * (1.0 + sc_ref[0]) + sh_ref[0]
    h_ref[0] = h.astype(MXU_DTYPE)
    logits = _dot3(h, rw_ref[...]) + rb_ref[...]
    logits = jnp.where(lane < N_EXPERTS, logits, -jnp.inf)
    v1 = jnp.max(logits, axis=1, keepdims=True)
    i1 = jnp.min(jnp.where(logits == v1, lane, LANES), axis=1, keepdims=True)
    rest = jnp.where(lane == i1, -jnp.inf, logits)
    v2 = jnp.max(rest, axis=1, keepdims=True)
    i2 = jnp.min(jnp.where(rest == v2, lane, LANES), axis=1, keepdims=True)
    e2 = jnp.exp(v2 - v1)
    w1 = 1.0 / (1.0 + e2)
    w2 = e2 / (1.0 + e2)
    comb_ref[0] = jnp.where(lane == i1, w1, 0.0) + jnp.where(lane == i2, w2, 0.0)
    routed = jnp.logical_or(lane == i1, lane == i2)
    earlier = (lax.broadcasted_iota(jnp.int32, (tm, tm), 1)
               < lax.broadcasted_iota(jnp.int32, (tm, tm), 0))
    before = _dot(jnp.where(earlier, 1.0, 0.0).astype(MXU_DTYPE),
                  jnp.where(routed, 1.0, 0.0).astype(MXU_DTYPE))
    slot = jnp.where(routed, before, -1.0)
    slot_ref[0] = slot
    slot_t_ref[0] = slot.T[0:N_EXPERTS, :]
    cnt_ref[0] = jnp.sum(jnp.where(routed, 1, 0), axis=0, keepdims=True)


def _moe_kernel(cnt_ref, h_ref, x_ref, g2_ref, comb_ref, slot_ref, slot_t_ref,
                wg_ref, wu_ref, wd_ref, lg_ref, lb_ref, o_ref, xe_ref, ye_ref, *, n_tiles):
    tm = x_ref.shape[1]
    ch = MOE_CHUNK
    e = pl.program_id(2)
    j = pl.program_id(3)
    last_j = pl.num_programs(3) - 1
    count = cnt_ref[(pl.program_id(0) * n_tiles + pl.program_id(1)) * N_EXPERTS + e]
    n_chunks = (count + ch - 1) // ch

    @pl.when(jnp.logical_and(e == 0, j == 0))
    def _():
        o_ref[0] = jnp.zeros((tm, D_MODEL), F32)

    @pl.when(j == 0)
    def _():
        slot_row = slot_t_ref[0, pl.ds(e, 1), :]
        row = lax.broadcasted_iota(jnp.int32, (ch, tm), 0).astype(F32)

        def gather(c, carry):
            r0 = pl.multiple_of(c * ch, ch)
            pick = jnp.where(slot_row == row + (c * ch).astype(F32), 1.0, 0.0).astype(MXU_DTYPE)
            xe_ref[pl.ds(r0, ch), :] = _dot(pick, h_ref[0]).astype(MXU_DTYPE)
            ye_ref[pl.ds(r0, ch), :] = jnp.zeros((ch, D_MODEL), F32)
            return carry

        lax.fori_loop(0, n_chunks, gather, 0)

    def expert(c, carry):
        r0 = pl.multiple_of(c * ch, ch)
        xc = xe_ref[pl.ds(r0, ch), :]
        act = _silu(_dot(xc, wg_ref[0])) * _dot(xc, wu_ref[0])
        ye_ref[pl.ds(r0, ch), :] += _dot(act.astype(MXU_DTYPE), wd_ref[0])
        return carry

    lax.fori_loop(0, n_chunks, expert, 0)

    @pl.when(j == last_j)
    def _():
        lane = lax.broadcasted_iota(jnp.int32, (tm, LANES), 1)
        mine = lane == e
        slot_col = jnp.sum(jnp.where(mine, slot_ref[0], 0.0), axis=1, keepdims=True)
        comb_col = jnp.sum(jnp.where(mine, comb_ref[0], 0.0), axis=1, keepdims=True)
        col = lax.broadcasted_iota(jnp.int32, (tm, ch), 1).astype(F32)

        def scatter(c, carry):
            r0 = pl.multiple_of(c * ch, ch)
            place = jnp.where(slot_col == col + (c * ch).astype(F32), 1.0, 0.0).astype(MXU_DTYPE)
            o_ref[0] += comb_col * _dot(place, ye_ref[pl.ds(r0, ch), :].astype(MXU_DTYPE))
            return carry

        lax.fori_loop(0, n_chunks, scatter, 0)

    @pl.when(jnp.logical_and(e == pl.num_programs(2) - 1, j == last_j))
    def _():
        o_ref[0] = _resid_ln(x_ref[0], g2_ref[0], o_ref[0], lg_ref[...], lb_ref[...])


def _moe(x, sc, sh, g2, router_w, router_b, w_gu, w_down, ln_g, ln_b, tm=1024, tf=896):
    bsz, s, d = x.shape
    tm = min(tm, s)
    nt = s // tm
    nf = D_FF_EXPERT // tf
    rw = jnp.pad(router_w, ((0, 0), (0, LANES - N_EXPERTS)))
    rb = jnp.pad(router_b, (0, LANES - N_EXPERTS)).reshape(1, LANES)

    row2 = lambda b, i: (b, i, 0)
    vec2 = lambda b, i: (b, 0, 0)
    h, comb, slot, slot_t, cnt = pl.pallas_call(
        _router_kernel,
        grid=(bsz, nt),
        in_specs=[pl.BlockSpec((1, tm, d), row2),
                  pl.BlockSpec((1, 1, d), vec2), pl.BlockSpec((1, 1, d), vec2),
                  pl.BlockSpec((d, LANES), lambda b, i: (0, 0)),
                  pl.BlockSpec((1, LANES), lambda b, i: (0, 0))],
        out_specs=[pl.BlockSpec((1, tm, d), row2),
                   pl.BlockSpec((1, tm, LANES), row2),
                   pl.BlockSpec((1, tm, LANES), row2),
                   pl.BlockSpec((1, N_EXPERTS, tm), lambda b, i: (b * nt + i, 0, 0)),
                   pl.BlockSpec((1, 1, LANES), lambda b, i: (b * nt + i, 0, 0))],
        out_shape=[jax.ShapeDtypeStruct((bsz, s, d), MXU_DTYPE),
                   jax.ShapeDtypeStruct((bsz, s, LANES), F32),
                   jax.ShapeDtypeStruct((bsz, s, LANES), F32),
                   jax.ShapeDtypeStruct((bsz * nt, N_EXPERTS, tm), F32),
                   jax.ShapeDtypeStruct((bsz * nt, 1, LANES), jnp.int32)],
        compiler_params=_params(("parallel", "parallel")),
        name="moe_router",
    )(x, sc, sh, rw, rb)
    counts = cnt[:, 0, :N_EXPERTS].reshape(bsz * nt * N_EXPERTS)

    row = lambda b, i, e, j, cnt: (b, i, 0)
    vec = lambda b, i, e, j, cnt: (b, 0, 0)
    const2 = lambda b, i, e, j, cnt: (0, 0)
    grid_spec = pltpu.PrefetchScalarGridSpec(
        num_scalar_prefetch=1,
        grid=(bsz, nt, N_EXPERTS, nf),
        in_specs=[pl.BlockSpec((1, tm, d), row),
                  pl.BlockSpec((1, tm, d), row),
                  pl.BlockSpec((1, 1, d), vec),
                  pl.BlockSpec((1, tm, LANES), row),
                  pl.BlockSpec((1, tm, LANES), row),
                  pl.BlockSpec((1, N_EXPERTS, tm), lambda b, i, e, j, cnt: (b * nt + i, 0, 0)),
                  pl.BlockSpec((1, d, tf), lambda b, i, e, j, cnt: (e, 0, j)),
                  pl.BlockSpec((1, d, tf), lambda b, i, e, j, cnt: (e, 0, nf + j)),
                  pl.BlockSpec((1, tf, d), lambda b, i, e, j, cnt: (e, j, 0)),
                  pl.BlockSpec((1, d), const2), pl.BlockSpec((1, d), const2)],
        out_specs=pl.BlockSpec((1, tm, d), row),
        scratch_shapes=[pltpu.VMEM((tm, d), MXU_DTYPE), pltpu.VMEM((tm, d), F32)])
    return pl.pallas_call(
        functools.partial(_moe_kernel, n_tiles=nt),
        grid_spec=grid_spec,
        out_shape=jax.ShapeDtypeStruct((bsz, s, d), F32),
        compiler_params=pltpu.CompilerParams(
            dimension_semantics=("parallel", "parallel", "arbitrary", "arbitrary"),
            vmem_limit_bytes=MOE_VMEM_LIMIT),
        name="moe_experts",
    )(counts, h, x, g2, comb, slot, slot_t, w_gu, w_gu, w_down,
      ln_g.reshape(1, d), ln_b.reshape(1, d))


def _split_w_in(w):
    p = np.cumsum([0, 512, 512, 512, 256, 256, 512, 512, 512, SSD_XBC, SSD_HEADS, 512, 512, 512,
                   N_BRANCH * D_MODEL])
    mq, mk, mv, rq, rk, rv, rg, sz, sxbc, sdt, bq, bk, bv, gates = (
        w[:, p[i]:p[i + 1]] for i in range(14))
    w_a = jnp.concatenate([mq, mk, mv, rq, rk, rv, bq, bk, bv], axis=1).astype(MXU_DTYPE)
    sdt = jnp.pad(sdt, ((0, 0), (0, LANES - SSD_HEADS)))
    w_b = jnp.concatenate([gates, rg, sz, sxbc, sdt], axis=1).astype(MXU_DTYPE)
    return w_a, w_b


def kernel(x, c, w_ada, b_ada, w_in, conv_w, conv_b, dt_bias, a_log, d_skip, ssm_norm_g, ret_gn_g, ret_gn_b, w_br, w_out, ln1_g, ln1_b, ln2_g, ln2_b, ffn_w_gu, ffn_w_down, router_w, router_b, expert_w_gu, expert_w_down):
    bsz = x.shape[0]
    mod = _ada(c, w_ada, b_ada)
    for l in range(DEPTH):
        sh1, sc1, g1, sh2, sc2, g2 = (
            mod[l, :, i * D_MODEL:(i + 1) * D_MODEL].reshape(bsz, 1, D_MODEL) for i in range(6))
        w_a, w_b = _split_w_in(w_in[l])
        proj_a = _inproj(x, sc1, sh1, w_a, MXU_DTYPE, tn=1024)
        proj_b = _inproj(x, sc1, sh1, w_b, F32, tn=896)
        o_a = _moba(proj_a)
        o_b = _retention(proj_a, proj_b, ret_gn_g[l], ret_gn_b[l])
        o_c = _ssd(proj_b, conv_w[l], conv_b[l], dt_bias[l], a_log[l], d_skip[l], ssm_norm_g[l])
        o_d = _stick_breaking(proj_a)
        x = _merge(o_a, o_b, o_c, o_d, proj_b, x, g1, w_br[l].astype(MXU_DTYPE),
                   w_out[l].astype(MXU_DTYPE), ln1_g[l], ln1_b[l])
        if l % 2 == 0:
            x = _ffn(x, sc2, sh2, g2, ffn_w_gu[l // 2].astype(MXU_DTYPE),
                     ffn_w_down[l // 2].astype(MXU_DTYPE), ln2_g[l], ln2_b[l])
        else:
            x = _moe(x, sc2, sh2, g2, router_w[l // 2], router_b[l // 2],
                     expert_w_gu[l // 2].astype(MXU_DTYPE), expert_w_down[l // 2].astype(MXU_DTYPE),
                     ln2_g[l], ln2_b[l])
    return x
```

```python
import functools

import numpy as np
import jax
import jax.numpy as jnp
from jax import lax
from jax.experimental import pallas as pl
from jax.experimental.pallas import tpu as pltpu

F32 = jnp.float32
MXU_DTYPE = jnp.bfloat16

D_MODEL = 1024
DEPTH = 2
MOBA_HEADS = 8
MOBA_BLOCK = 256
MOBA_TOPK = 3
RET_HEADS = 4
RET_DK = 64
RET_DV = 128
RET_CHUNK = 256
SSD_D_INNER = 512
SSD_HEAD_DIM = 64
SSD_HEADS = 8
SSD_GROUPS = 2
SSD_STATE = 128
SSD_CONV = 4
SSD_CHUNK = 256
SSD_XBC = SSD_D_INNER + 2 * SSD_GROUPS * SSD_STATE
SB_HEADS = 8
SB_BLOCK = 256
HEAD_DIM = 64
ATT_HEADS_PER_STEP = 8
ATT_W = ATT_HEADS_PER_STEP * HEAD_DIM
BRANCH_W = 512
N_BRANCH = 4
D_FF = 2816
N_EXPERTS = 8
D_FF_EXPERT = 3584
DEEPNORM_ALPHA = (2.0 * DEPTH) ** 0.25
LN_EPS = 1e-5
NORM_EPS = 1e-6
NEG_INF = -1e30
LOG2E = 1.4426950408889634
LANES = 128
VMEM_LIMIT = 48 * 2 ** 20
MOE_VMEM_LIMIT = 56 * 2 ** 20
MOE_CHUNK = 256
MOE_ROWS = 512
MOE_ROW_STEP = 128

A_MOBA_Q, A_MOBA_K, A_MOBA_V = 0, 512, 1024
A_RET_Q, A_RET_K, A_RET_V = 1536, 1792, 2048
A_SB_Q, A_SB_K, A_SB_V = 2560, 3072, 3584
A_WIDTH = 4096
B_GATES, B_RET_G, B_SSD_Z, B_XBC, B_DT = 0, 4096, 4608, 5120, 6144
B_WIDTH = 6272


def _params(sem):
    return pltpu.CompilerParams(dimension_semantics=sem, vmem_limit_bytes=VMEM_LIMIT)


def _dot(a, b):
    return jnp.dot(a, b, preferred_element_type=F32)


def _dot_nt(a, b):
    return lax.dot_general(a, b, (((1,), (1,)), ((), ())), preferred_element_type=F32)


def _dot_tn(a, b):
    return lax.dot_general(a, b, (((0,), (0,)), ((), ())), preferred_element_type=F32)


def _split(x):
    hi = x.astype(MXU_DTYPE)
    lo = (x - hi.astype(F32)).astype(MXU_DTYPE)
    return hi, lo


def _dot_split_lhs(a, b_exact):
    hi, lo = _split(a)
    return _dot(hi, b_exact) + _dot(lo, b_exact)


def _dot3(a, b):
    ah, al = _split(a)
    bh, bl = _split(b)
    return _dot(ah, bh) + _dot(al, bh) + _dot(ah, bl)


def _sigmoid(x):
    return 1.0 / (1.0 + jnp.exp(-x))


def _silu(x):
    return x * _sigmoid(x)


def _resid_ln(x, gate, f, ln_g, ln_b):
    y = DEEPNORM_ALPHA * x + gate * f
    mu = jnp.mean(y, axis=-1, keepdims=True)
    d = y - mu
    var = jnp.mean(d * d, axis=-1, keepdims=True)
    return d * lax.rsqrt(var + LN_EPS) * ln_g + ln_b


def _ada_kernel(c_ref, w_ref, b_ref, o_ref):
    c = c_ref[...]
    o_ref[0] = _dot3(_silu(c), w_ref[0]) + b_ref[0]


def _ada(c, w_ada, b_ada):
    depth, d, n = w_ada.shape
    bsz = c.shape[0]
    tn = 1536
    return pl.pallas_call(
        _ada_kernel,
        grid=(depth, n // tn),
        in_specs=[pl.BlockSpec((bsz, d), lambda l, j: (0, 0)),
                  pl.BlockSpec((1, d, tn), lambda l, j: (l, 0, j)),
                  pl.BlockSpec((1, 1, tn), lambda l, j: (l, 0, j))],
        out_specs=pl.BlockSpec((1, bsz, tn), lambda l, j: (l, 0, j)),
        out_shape=jax.ShapeDtypeStruct((depth, bsz, n), F32),
        compiler_params=_params(("arbitrary", "arbitrary")),
        name="ada_mod",
    )(c, w_ada, b_ada.reshape(depth, 1, n))


def _inproj_kernel(x_ref, sc_ref, sh_ref, w_ref, o_ref, h_ref):
    @pl.when(pl.program_id(2) == 0)
    def _():
        h = x_ref[0] * (1.0 + sc_ref[0]) + sh_ref[0]
        h_ref[...] = h.astype(MXU_DTYPE)

    o_ref[0] = _dot(h_ref[...], w_ref[...]).astype(o_ref.dtype)


def _inproj(x, sc, sh, w, out_dtype, tn, tm=1024):
    bsz, s, d = x.shape
    n = w.shape[1]
    tm = min(tm, s)
    return pl.pallas_call(
        _inproj_kernel,
        grid=(bsz, s // tm, n // tn),
        in_specs=[pl.BlockSpec((1, tm, d), lambda b, i, j: (b, i, 0)),
                  pl.BlockSpec((1, 1, d), lambda b, i, j: (b, 0, 0)),
                  pl.BlockSpec((1, 1, d), lambda b, i, j: (b, 0, 0)),
                  pl.BlockSpec((d, tn), lambda b, i, j: (0, j))],
        out_specs=pl.BlockSpec((1, tm, tn), lambda b, i, j: (b, i, j)),
        out_shape=jax.ShapeDtypeStruct((bsz, s, n), out_dtype),
        scratch_shapes=[pltpu.VMEM((tm, d), MXU_DTYPE)],
        compiler_params=_params(("parallel", "parallel", "arbitrary")),
        name="in_proj",
    )(x, sc, sh, w)


def _moba_kernel(q_ref, k_ref, v_ref, sl_ref, o_ref, km_ref, ka_ref, vt_ref, *, nb):
    blk = MOBA_BLOCK
    nh = ATT_HEADS_PER_STEP
    heads = range(nh)
    qi = pl.program_id(2)
    s_len = nb * blk
    nb_pad = -(-nb // 8) * 8
    extra = 2 * HEAD_DIM - HEAD_DIM
    assert nb_pad + 8 <= extra
    cols = [slice(h * HEAD_DIM, (h + 1) * HEAD_DIM) for h in heads]

    @pl.when(qi == 0)
    def _():
        k_all = k_ref[0]
        kk = k_all.astype(F32).reshape(nb, blk, ATT_W)
        km_ref[...] = jnp.zeros_like(km_ref)
        km_ref[0:nb, :] = jnp.sum(kk, axis=1) * (1.0 / blk)
        pos = lax.broadcasted_iota(jnp.int32, (s_len, extra), 0)
        lane = lax.broadcasted_iota(jnp.int32, (s_len, extra), 1)
        one_hot = jnp.where(lane == pos // blk, 1.0, 0.0)
        pos_hi = jnp.where(lane == nb_pad, (pos // blk * blk).astype(F32), 0.0)
        pos_lo = jnp.where(lane == nb_pad + 1, (pos % blk).astype(F32), 0.0)
        tail = (one_hot + pos_hi + pos_lo).astype(MXU_DTYPE)
        for h in heads:
            ka_ref[h] = jnp.concatenate([k_all[:, cols[h]], tail], axis=1)

        ones_rows = jnp.where(lax.broadcasted_iota(jnp.int32, (extra, blk), 0) == 0, 1.0, 0.0)

        def transpose_block(j, carry):
            j0 = pl.multiple_of(j * blk, blk)
            vt = v_ref[0, pl.ds(j0, blk), :].astype(F32).T
            for h in heads:
                vt_ref[j, h] = jnp.concatenate([vt[cols[h], :], ones_rows], axis=0).astype(MXU_DTYPE)
            return carry

        lax.fori_loop(0, nb, transpose_block, 0)

    key_i = lax.broadcasted_iota(jnp.int32, (blk, blk), 0)
    qry_i = lax.broadcasted_iota(jnp.int32, (blk, blk), 1)
    n_iota = lax.broadcasted_iota(jnp.int32, (nb_pad, blk), 0)
    past = n_iota < qi
    k_sel = min(MOBA_TOPK, max(nb - 1, 1))
    q_t = q_ref[0].astype(F32).T
    slope_rows = lax.broadcasted_iota(jnp.int32, (8, blk), 0) < 2

    qa = []
    for h in heads:
        qh_t = q_t[cols[h], :]
        slope = sl_ref[0, :, h * HEAD_DIM:h * HEAD_DIM + 1]
        km_hi, km_lo = _split(km_ref[:, cols[h]])
        qh_mxu = qh_t.astype(MXU_DTYPE)
        gate = (_dot(km_hi, qh_mxu) + _dot(km_lo, qh_mxu))[0:nb_pad, :]
        g = jnp.where(past, gate, NEG_INF)
        g = jnp.where(n_iota < nb, g, -jnp.inf)
        sel = jnp.zeros((nb_pad, blk), F32)
        for _ in range(k_sel):
            mx = jnp.max(g, axis=0, keepdims=True)
            idx = jnp.min(jnp.where(g == mx, n_iota, nb_pad), axis=0, keepdims=True)
            pick = n_iota == idx
            sel = jnp.where(pick, 1.0, sel)
            g = jnp.where(pick, -jnp.inf, g)
        keep = jnp.logical_or(jnp.logical_and(past, sel > 0.5), n_iota == qi)
        bias = jnp.where(keep, 0.0, NEG_INF)
        alibi = jnp.where(slope_rows, slope, 0.0)
        pad = jnp.zeros((extra - nb_pad - 8, blk), F32)
        qa.append(jnp.concatenate([qh_t * HEAD_DIM ** -0.5, bias, alibi, pad],
                                  axis=0).astype(MXU_DTYPE))

    def tiles(js, carry, diag):
        starts = [pl.multiple_of(j * blk, blk) for j in js]
        s = [[_dot(ka_ref[h, pl.ds(j0, blk), :], qa[h]) for h in heads] for j0 in starts]
        m = list(carry[:nh]) if carry is not None else [None] * nh
        acc = list(carry[nh:]) if carry is not None else [None] * nh
        p = {}
        for h in heads:
            sh = [jnp.where(key_i <= qry_i, s[t][h], NEG_INF) if diag else s[t][h]
                  for t in range(len(js))]
            top = jnp.max(sh[0], axis=0, keepdims=True)
            for t in range(1, len(js)):
                top = jnp.maximum(top, jnp.max(sh[t], axis=0, keepdims=True))
            if m[h] is None:
                m[h] = top
            else:
                m_new = jnp.maximum(m[h], top)
                acc[h] = jnp.exp(m[h] - m_new) * acc[h]
                m[h] = m_new
            for t in range(len(js)):
                p[t, h] = jnp.exp(sh[t] - m[h]).astype(MXU_DTYPE)
        for t, j in enumerate(js):
            for h in heads:
                pv = _dot(vt_ref[j, h], p[t, h])
                acc[h] = pv if acc[h] is None else acc[h] + pv
        return tuple(m) + tuple(acc)

    carry0 = tiles([qi], None, True)

    def pair(t, carry):
        return tiles([2 * t, 2 * t + 1], carry, False)

    def single(t, carry):
        return tiles([qi - 1], carry, False)

    res = lax.fori_loop(0, qi // 2, pair, carry0)
    res = lax.fori_loop(0, qi % 2, single, res)
    out_t = [res[nh + h][0:HEAD_DIM, :] / res[nh + h][HEAD_DIM:HEAD_DIM + 1, :] for h in heads]
    o_ref[0] = jnp.concatenate(out_t, axis=0).T.astype(o_ref.dtype)


def _moba(proj_a):
    bsz, s, _ = proj_a.shape
    blk = MOBA_BLOCK
    nb = s // blk
    n_grp = MOBA_HEADS // ATT_HEADS_PER_STEP
    slopes = 2.0 ** (-8.0 * np.arange(1, MOBA_HEADS + 1) / MOBA_HEADS)
    sl = jnp.asarray(np.repeat(slopes, HEAD_DIM).reshape(n_grp, 1, ATT_W), F32)
    qb, kb, vb = A_MOBA_Q // ATT_W, A_MOBA_K // ATT_W, A_MOBA_V // ATT_W
    return pl.pallas_call(
        functools.partial(_moba_kernel, nb=nb),
        grid=(bsz, n_grp, nb),
        in_specs=[pl.BlockSpec((1, blk, ATT_W), lambda b, hg, i: (b, i, qb + hg)),
                  pl.BlockSpec((1, s, ATT_W), lambda b, hg, i: (b, 0, kb + hg)),
                  pl.BlockSpec((1, s, ATT_W), lambda b, hg, i: (b, 0, vb + hg)),
                  pl.BlockSpec((1, 1, ATT_W), lambda b, hg, i: (hg, 0, 0))],
        out_specs=pl.BlockSpec((1, blk, ATT_W), lambda b, hg, i: (b, i, hg)),
        out_shape=jax.ShapeDtypeStruct((bsz, s, MOBA_HEADS * HEAD_DIM), MXU_DTYPE),
        scratch_shapes=[pltpu.VMEM((LANES, ATT_W), F32),
                        pltpu.VMEM((ATT_HEADS_PER_STEP, s, 2 * HEAD_DIM), MXU_DTYPE),
                        pltpu.VMEM((nb, ATT_HEADS_PER_STEP, 2 * HEAD_DIM, blk), MXU_DTYPE)],
        compiler_params=_params(("parallel", "parallel", "arbitrary")),
        name="moba",
    )(proj_a, proj_a, proj_a, sl)


def _sb_kernel(q_ref, k_ref, v_ref, o_ref, vt_ref, *, nb):
    blk = SB_BLOCK
    nh = ATT_HEADS_PER_STEP
    heads = range(nh)
    qi = pl.program_id(2)
    cols = [slice(h * HEAD_DIM, (h + 1) * HEAD_DIM) for h in heads]

    @pl.when(qi == 0)
    def _():
        def transpose_block(j, carry):
            j0 = pl.multiple_of(j * blk, blk)
            vt = v_ref[0, pl.ds(j0, blk), :].astype(F32).T
            for h in heads:
                vt_ref[j, h] = vt[cols[h], :].astype(MXU_DTYPE)
            return carry

        lax.fori_loop(0, nb, transpose_block, 0)

    key_i = lax.broadcasted_iota(jnp.int32, (blk, blk), 0)
    qry_i = lax.broadcasted_iota(jnp.int32, (blk, blk), 1)
    causal = key_i < qry_i
    later = jnp.where(qry_i > key_i, 1.0, 0.0).astype(MXU_DTYPE)
    later2 = jnp.concatenate([later, later], axis=1)
    qs = [(q_ref[0, :, cols[h]].astype(F32) * HEAD_DIM ** -0.5).astype(MXU_DTYPE)
          for h in heads]

    def tiles(js, accs, rests, diag):
        starts = [pl.multiple_of(j * blk, blk) for j in js]
        z = [[_dot_nt(k_ref[0, pl.ds(j0, blk), cols[h]], qs[h]) for h in heads] for j0 in starts]
        log_beta, log_1mb, parts = [], [], []
        for t in range(len(js)):
            for h in heads:
                zz = z[t][h]
                lb = jnp.minimum(zz, 0.0) - jnp.log(1.0 + jnp.exp2(jnp.abs(zz) * (-LOG2E)))
                l1 = lb - zz
                if diag:
                    l1 = jnp.where(causal, l1, 0.0)
                log_beta.append(lb)
                log_1mb.append(l1)
                parts.append(jnp.concatenate(_split(l1), axis=0))
        rem = [_dot(later2, p) for p in parts]
        w = []
        rests = list(rests)
        for t in range(len(js)):
            for h in heads:
                i = t * nh + h
                wh = jnp.exp(log_beta[i] + rem[i] + rests[h])
                if diag:
                    wh = jnp.where(causal, wh, 0.0)
                w.append(wh.astype(MXU_DTYPE))
                rests[h] = rests[h] + rem[i][0:1, :] + log_1mb[i][0:1, :]
        accs = list(accs)
        for t, j in enumerate(js):
            for h in heads:
                accs[h] = accs[h] + _dot(vt_ref[j, h], w[t * nh + h])
        return accs, rests

    zero_acc = [jnp.zeros((HEAD_DIM, blk), F32)] * nh
    accs, rests = tiles([qi], zero_acc, [jnp.zeros((1, blk), F32)] * nh, True)

    def pair(t, carry):
        j = qi - 1 - 2 * t
        accs, rests = tiles([j, j - 1], carry[:nh], carry[nh:], False)
        return tuple(accs + rests)

    def single(t, carry):
        accs, rests = tiles([0], carry[:nh], carry[nh:], False)
        return tuple(accs + rests)

    res = lax.fori_loop(0, qi // 2, pair, tuple(accs + rests))
    res = lax.fori_loop(0, qi % 2, single, res)
    o_ref[0] = jnp.concatenate(res[:nh], axis=0).T.astype(o_ref.dtype)


def _stick_breaking(proj_a):
    bsz, s, _ = proj_a.shape
    blk = SB_BLOCK
    n_grp = SB_HEADS // ATT_HEADS_PER_STEP
    qb, kb, vb = A_SB_Q // ATT_W, A_SB_K // ATT_W, A_SB_V // ATT_W
    return pl.pallas_call(
        functools.partial(_sb_kernel, nb=s // blk),
        grid=(bsz, n_grp, s // blk),
        in_specs=[pl.BlockSpec((1, blk, ATT_W), lambda b, hg, i: (b, i, qb + hg)),
                  pl.BlockSpec((1, s, ATT_W), lambda b, hg, i: (b, 0, kb + hg)),
                  pl.BlockSpec((1, s, ATT_W), lambda b, hg, i: (b, 0, vb + hg))],
        out_specs=pl.BlockSpec((1, blk, ATT_W), lambda b, hg, i: (b, i, hg)),
        out_shape=jax.ShapeDtypeStruct((bsz, s, SB_HEADS * HEAD_DIM), MXU_DTYPE),
        scratch_shapes=[pltpu.VMEM((s // blk, ATT_HEADS_PER_STEP, HEAD_DIM, blk), MXU_DTYPE)],
        compiler_params=_params(("parallel", "parallel", "arbitrary")),
        name="stick_breaking",
    )(proj_a, proj_a, proj_a)


def _ret_kernel(q_ref, k_ref, v_ref, g_ref, gn_g_ref, gn_b_ref, o_ref, st_ref):
    ch = RET_CHUNK

    @pl.when(pl.program_id(1) == 0)
    def _():
        st_ref[...] = jnp.zeros_like(st_ref)

    i_col = lax.broadcasted_iota(jnp.int32, (ch, 1), 0).astype(F32)
    diff = (lax.broadcasted_iota(jnp.int32, (ch, ch), 0)
            - lax.broadcasted_iota(jnp.int32, (ch, ch), 1)).astype(F32)
    heads = range(RET_HEADS)
    log_g = [float(np.log(1.0 - 2.0 ** (-5.0 - h))) for h in heads]
    qs = [(q_ref[0, :, h * RET_DK:(h + 1) * RET_DK].astype(F32) * RET_DK ** -0.5).astype(MXU_DTYPE)
          for h in heads]
    k = [k_ref[0, :, h * RET_DK:(h + 1) * RET_DK] for h in heads]
    v = [v_ref[0, :, h * RET_DV:(h + 1) * RET_DV] for h in heads]
    prev = [st_ref[h] for h in heads]
    qk = [_dot_nt(qs[h], k[h]) for h in heads]
    o_cross = [_dot(qs[h], prev[h].astype(MXU_DTYPE)) for h in heads]
    kd = [(k[h].astype(F32) * jnp.exp((ch - 1.0 - i_col) * log_g[h])).astype(MXU_DTYPE)
          for h in heads]
    st_new = [_dot_tn(kd[h], v[h]) for h in heads]
    qkd = []
    for h in heads:
        decay = jnp.where(diff >= 0, jnp.exp(jnp.maximum(diff, 0.0) * log_g[h]), 0.0)
        qkd.append((qk[h] * decay).astype(MXU_DTYPE))
    o_intra = [_dot(qkd[h], v[h]) for h in heads]
    for h in heads:
        st_ref[h] = float(np.exp(ch * log_g[h])) * prev[h] + st_new[h]
        o = o_intra[h] + o_cross[h] * jnp.exp((i_col + 1.0) * log_g[h])
        mu = jnp.mean(o, axis=-1, keepdims=True)
        d = o - mu
        var = jnp.mean(d * d, axis=-1, keepdims=True)
        cs = slice(h * RET_DV, (h + 1) * RET_DV)
        on = d * lax.rsqrt(var + NORM_EPS) * gn_g_ref[:, cs] + gn_b_ref[:, cs]
        o_ref[0, :, cs] = (_silu(g_ref[0, :, cs]) * on).astype(o_ref.dtype)


def _retention(proj_a, proj_b, gn_g, gn_b):
    bsz, s, _ = proj_a.shape
    ch = RET_CHUNK
    wq, wv = RET_HEADS * RET_DK, RET_HEADS * RET_DV
    return pl.pallas_call(
        _ret_kernel,
        grid=(bsz, s // ch),
        in_specs=[pl.BlockSpec((1, ch, wq), lambda b, i: (b, i, A_RET_Q // wq)),
                  pl.BlockSpec((1, ch, wq), lambda b, i: (b, i, A_RET_K // wq)),
                  pl.BlockSpec((1, ch, wv), lambda b, i: (b, i, A_RET_V // wv)),
                  pl.BlockSpec((1, ch, wv), lambda b, i: (b, i, B_RET_G // wv)),
                  pl.BlockSpec((1, wv), lambda b, i: (0, 0)),
                  pl.BlockSpec((1, wv), lambda b, i: (0, 0))],
        out_specs=pl.BlockSpec((1, ch, wv), lambda b, i: (b, i, 0)),
        out_shape=jax.ShapeDtypeStruct((bsz, s, wv), MXU_DTYPE),
        scratch_shapes=[pltpu.VMEM((RET_HEADS, RET_DK, RET_DV), F32)],
        compiler_params=_params(("parallel", "arbitrary")),
        name="retention",
    )(proj_a, proj_a, proj_a, proj_b, gn_g.reshape(1, wv), gn_b.reshape(1, wv))


def _ssd_kernel(z_ref, xbc_ref, dt_ref, cw_ref, cb_ref, dtb_ref, alog_ref, dsk_ref, ng_ref,
                o_ref, tail_ref, st_ref, y_ref):
    ch = SSD_CHUNK
    heads_per_group = SSD_HEADS // SSD_GROUPS
    halo = 8

    @pl.when(pl.program_id(1) == 0)
    def _():
        tail_ref[...] = jnp.zeros_like(tail_ref)
        st_ref[...] = jnp.zeros_like(st_ref)

    xbc = xbc_ref[0]
    xpad = jnp.concatenate([tail_ref[...], xbc], axis=0)
    tail_ref[...] = xbc[ch - halo:, :]
    conv = cb_ref[...]
    for kk in range(SSD_CONV):
        off = halo - (SSD_CONV - 1) + kk
        conv = conv + cw_ref[kk:kk + 1, :] * xpad[off:off + ch, :]
    xc = _silu(conv)
    xs = xc[:, :SSD_D_INNER]
    bm = xc[:, SSD_D_INNER:SSD_D_INNER + SSD_GROUPS * SSD_STATE]
    cm = xc[:, SSD_D_INNER + SSD_GROUPS * SSD_STATE:]

    dtr = dt_ref[0] + dtb_ref[...]
    dt = jnp.maximum(dtr, 0.0) + jnp.log1p(jnp.exp(-jnp.abs(dtr)))
    a = -jnp.exp(alog_ref[...])
    r_iota = lax.broadcasted_iota(jnp.int32, (ch, ch), 0)
    c_iota = lax.broadcasted_iota(jnp.int32, (ch, ch), 1)
    causal = r_iota >= c_iota
    tri = jnp.where(causal, 1.0, 0.0).astype(MXU_DTYPE)
    da = dt * a
    da_hi, da_lo = _split(da)
    cum = _dot(tri, da_hi) + _dot(tri, da_lo)
    cum_t = cum.T
    dt_t = dt.T
    cum_last = cum[ch - 1:ch, :]
    decay_st = jnp.exp(cum_last - cum) * dt
    e_cum = jnp.exp(cum)
    e_last = jnp.exp(cum_last)

    for g in range(SSD_GROUPS):
        bm_g = bm[:, g * SSD_STATE:(g + 1) * SSD_STATE].astype(MXU_DTYPE)
        cm_g = cm[:, g * SSD_STATE:(g + 1) * SSD_STATE].astype(MXU_DTYPE)
        cb = _dot_nt(cm_g, bm_g)
        for r in range(heads_per_group):
            hd = g * heads_per_group + r
            cs = slice(hd * SSD_HEAD_DIM, (hd + 1) * SSD_HEAD_DIM)
            x_h = xs[:, cs]
            seg = cum[:, hd:hd + 1] - cum_t[hd:hd + 1, :]
            lmat = jnp.exp(jnp.where(causal, seg, NEG_INF))
            w = cb * lmat * dt_t[hd:hd + 1, :]
            y = _dot(w.astype(MXU_DTYPE), x_h.astype(MXU_DTYPE))
            prev = st_ref[hd]
            y = y + _dot(cm_g, prev.astype(MXU_DTYPE)) * e_cum[:, hd:hd + 1]
            xd = (x_h * decay_st[:, hd:hd + 1]).astype(MXU_DTYPE)
            st_ref[hd] = e_last[:, hd:hd + 1] * prev + _dot_tn(bm_g, xd)
            y_ref[:, cs] = y + dsk_ref[:, cs] * x_h

    y = y_ref[...] * _silu(z_ref[0])
    gw = SSD_D_INNER // SSD_GROUPS
    for g in range(SSD_GROUPS):
        cs = slice(g * gw, (g + 1) * gw)
        yg = y[:, cs]
        ms = jnp.mean(yg * yg, axis=-1, keepdims=True)
        o_ref[0, :, cs] = (yg * lax.rsqrt(ms + LN_EPS) * ng_ref[:, cs]).astype(o_ref.dtype)


def _ssd(proj_b, conv_w, conv_b, dt_bias, a_log, d_skip, norm_g):
    bsz, s, _ = proj_b.shape
    ch = SSD_CHUNK
    pad = LANES - SSD_HEADS
    dtb = jnp.pad(dt_bias, (0, pad)).reshape(1, LANES)
    alog = jnp.pad(a_log, (0, pad)).reshape(1, LANES)
    dsk = jnp.repeat(d_skip, SSD_HEAD_DIM).reshape(1, SSD_D_INNER)
    const = lambda b, i: (0, 0)
    return pl.pallas_call(
        _ssd_kernel,
        grid=(bsz, s // ch),
        in_specs=[pl.BlockSpec((1, ch, SSD_D_INNER), lambda b, i: (b, i, B_SSD_Z // SSD_D_INNER)),
                  pl.BlockSpec((1, ch, SSD_XBC), lambda b, i: (b, i, B_XBC // SSD_XBC)),
                  pl.BlockSpec((1, ch, LANES), lambda b, i: (b, i, B_DT // LANES)),
                  pl.BlockSpec((SSD_CONV, SSD_XBC), const),
                  pl.BlockSpec((1, SSD_XBC), const),
                  pl.BlockSpec((1, LANES), const),
                  pl.BlockSpec((1, LANES), const),
                  pl.BlockSpec((1, SSD_D_INNER), const),
                  pl.BlockSpec((1, SSD_D_INNER), const)],
        out_specs=pl.BlockSpec((1, ch, SSD_D_INNER), lambda b, i: (b, i, 0)),
        out_shape=jax.ShapeDtypeStruct((bsz, s, SSD_D_INNER), MXU_DTYPE),
        scratch_shapes=[pltpu.VMEM((8, SSD_XBC), F32),
                        pltpu.VMEM((SSD_HEADS, SSD_STATE, SSD_HEAD_DIM), F32),
                        pltpu.VMEM((ch, SSD_D_INNER), F32)],
        compiler_params=_params(("parallel", "arbitrary")),
        name="ssd",
    )(proj_b, proj_b, proj_b, conv_w, conv_b.reshape(1, SSD_XBC), dtb, alog, dsk,
      norm_g.reshape(1, SSD_D_INNER))


def _merge_kernel(oa_ref, ob_ref, oc_ref, od_ref, gates_ref, x_ref, g1_ref, wbr_ref, wout_ref,
                  lg_ref, lb_ref, o_ref):
    merged = None
    for n, br in enumerate((oa_ref, ob_ref, oc_ref, od_ref)):
        y = _dot(br[0], wbr_ref[n])
        t = _sigmoid(gates_ref[0, :, n * D_MODEL:(n + 1) * D_MODEL]) * y
        merged = t if merged is None else merged + t
    mix = _dot(merged.astype(MXU_DTYPE), wout_ref[...])
    o_ref[0] = _resid_ln(x_ref[0], g1_ref[0], mix, lg_ref[...], lb_ref[...])


def _merge(o_a, o_b, o_c, o_d, proj_b, x, g1, w_br, w_out, ln_g, ln_b, tm=512):
    bsz, s, d = x.shape
    tm = min(tm, s)
    gw = N_BRANCH * d
    row = lambda b, i: (b, i, 0)
    const2 = lambda b, i: (0, 0)
    br_spec = pl.BlockSpec((1, tm, BRANCH_W), row)
    return pl.pallas_call(
        _merge_kernel,
        grid=(bsz, s // tm),
        in_specs=[br_spec, br_spec, br_spec, br_spec,
                  pl.BlockSpec((1, tm, gw), lambda b, i: (b, i, B_GATES // gw)),
                  pl.BlockSpec((1, tm, d), row),
                  pl.BlockSpec((1, 1, d), lambda b, i: (b, 0, 0)),
                  pl.BlockSpec((N_BRANCH, BRANCH_W, d), lambda b, i: (0, 0, 0)),
                  pl.BlockSpec((d, d), const2),
                  pl.BlockSpec((1, d), const2),
                  pl.BlockSpec((1, d), const2)],
        out_specs=pl.BlockSpec((1, tm, d), row),
        out_shape=jax.ShapeDtypeStruct((bsz, s, d), F32),
        compiler_params=_params(("parallel", "parallel")),
        name="merge",
    )(o_a, o_b, o_c, o_d, proj_b, x, g1, w_br, w_out, ln_g.reshape(1, d), ln_b.reshape(1, d))


def _ffn_kernel(x_ref, sc_ref, sh_ref, g2_ref, wg_ref, wu_ref, wd_ref, lg_ref, lb_ref,
                o_ref, h_ref, acc_ref):
    j = pl.program_id(2)

    @pl.when(j == 0)
    def _():
        h = x_ref[0] * (1.0 + sc_ref[0]) + sh_ref[0]
        h_ref[...] = h.astype(MXU_DTYPE)
        acc_ref[...] = jnp.zeros_like(acc_ref)

    h = h_ref[...]
    act = _silu(_dot(h, wg_ref[...])) * _dot(h, wu_ref[...])
    acc_ref[...] += _dot(act.astype(MXU_DTYPE), wd_ref[...])

    @pl.when(j == pl.num_programs(2) - 1)
    def _():
        o_ref[0] = _resid_ln(x_ref[0], g2_ref[0], acc_ref[...], lg_ref[...], lb_ref[...])


def _ffn(x, sc, sh, g2, w_gu, w_down, ln_g, ln_b, tm=512, tf=1408):
    bsz, s, d = x.shape
    tm = min(tm, s)
    nf = D_FF // tf
    row = lambda b, i, j: (b, i, 0)
    vec = lambda b, i, j: (b, 0, 0)
    const2 = lambda b, i, j: (0, 0)
    return pl.pallas_call(
        _ffn_kernel,
        grid=(bsz, s // tm, nf),
        in_specs=[pl.BlockSpec((1, tm, d), row),
                  pl.BlockSpec((1, 1, d), vec), pl.BlockSpec((1, 1, d), vec),
                  pl.BlockSpec((1, 1, d), vec),
                  pl.BlockSpec((d, tf), lambda b, i, j: (0, j)),
                  pl.BlockSpec((d, tf), lambda b, i, j: (0, nf + j)),
                  pl.BlockSpec((tf, d), lambda b, i, j: (j, 0)),
                  pl.BlockSpec((1, d), const2), pl.BlockSpec((1, d), const2)],
        out_specs=pl.BlockSpec((1, tm, d), row),
        out_shape=jax.ShapeDtypeStruct((bsz, s, d), F32),
        scratch_shapes=[pltpu.VMEM((tm, d), MXU_DTYPE), pltpu.VMEM((tm, d), F32)],
        compiler_params=_params(("parallel", "parallel", "arbitrary")),
        name="ffn_dense",
    )(x, sc, sh, g2, w_gu, w_gu, w_down, ln_g.reshape(1, d), ln_b.reshape(1, d))


def _router_kernel(x_ref, sc_ref, sh_ref, rw_ref, rb_ref,
                   h_ref, comb_ref, slot_ref, slot_t_ref, cnt_ref):
    tm = x_ref.shape[1]
    lane = lax.broadcasted_iota(jnp.int32, (tm, LANES), 1)
    h = x_ref[0] * (1.0 + sc_ref[0]) + sh_ref[0]
    h_ref[0] = h.astype(MXU_DTYPE)
    logits = _dot3(h, rw_ref[...]) + rb_ref[...]
    logits = jnp.where(lane < N_EXPERTS, logits, -jnp.inf)
    v1 = jnp.max(logits, axis=1, keepdims=True)
    i1 = jnp.min(jnp.where(logits == v1, lane, LANES), axis=1, keepdims=True)
    rest = jnp.where(lane == i1, -jnp.inf, logits)
    v2 = jnp.max(rest, axis=1, keepdims=True)
    i2 = jnp.min(jnp.where(rest == v2, lane, LANES), axis=1, keepdims=True)
    e2 = jnp.exp(v2 - v1)
    w1 = 1.0 / (1.0 + e2)
    w2 = e2 / (1.0 + e2)
    comb_ref[0] = jnp.where(lane == i1, w1, 0.0) + jnp.where(lane == i2, w2, 0.0)
    routed = jnp.logical_or(lane == i1, lane == i2)
    earlier = (lax.broadcasted_iota(jnp.int32, (tm, tm), 1)
               < lax.broadcasted_iota(jnp.int32, (tm, tm), 0))
    before = _dot(jnp.where(earlier, 1.0, 0.0).astype(MXU_DTYPE),
                  jnp.where(routed, 1.0, 0.0).astype(MXU_DTYPE))
    slot = jnp.where(routed, before, -1.0)
    slot_ref[0] = slot
    slot_t_ref[0] = slot.T[0:N_EXPERTS, :]
    cnt_ref[0] = jnp.sum(jnp.where(routed, 1, 0), axis=0, keepdims=True)


def _moe_kernel(cnt_ref, h_ref, x_ref, g2_ref, comb_ref, slot_ref, slot_t_ref,
                wg_ref, wu_ref, wd_ref, lg_ref, lb_ref, o_ref, xe_ref, ye_ref, *, n_tiles):
    tm = x_ref.shape[1]
    ch = MOE_CHUNK
    e = pl.program_id(2)
    j = pl.program_id(3)
    last_j = pl.num_programs(3) - 1
    count = cnt_ref[(pl.program_id(0) * n_tiles + pl.program_id(1)) * N_EXPERTS + e]
    n_chunks = (count + ch - 1) // ch

    @pl.when(jnp.logical_and(e == 0, j == 0))
    def _():
        o_ref[0] = jnp.zeros((tm, D_MODEL), F32)

    @pl.when(j == 0)
    def _():
        slot_row = slot_t_ref[0, pl.ds(e, 1), :]
        row = lax.broadcasted_iota(jnp.int32, (ch, tm), 0).astype(F32)

        def gather(c, carry):
            r0 = pl.multiple_of(c * ch, ch)
            pick = jnp.where(slot_row == row + (c * ch).astype(F32), 1.0, 0.0).astype(MXU_DTYPE)
            xe_ref[pl.ds(r0, ch), :] = _dot(pick, h_ref[0]).astype(MXU_DTYPE)
            ye_ref[pl.ds(r0, ch), :] = jnp.zeros((ch, D_MODEL), F32)
            return carry

        lax.fori_loop(0, n_chunks, gather, 0)

    def expert_rows(r0, rows):
        xc = xe_ref[pl.ds(r0, rows), :]
        act = _silu(_dot(xc, wg_ref[0])) * _dot(xc, wu_ref[0])
        ye_ref[pl.ds(r0, rows), :] += _dot(act.astype(MXU_DTYPE), wd_ref[0])

    n_full = count // MOE_ROWS

    def full_piece(c, carry):
        expert_rows(pl.multiple_of(c * MOE_ROWS, MOE_ROWS), MOE_ROWS)
        return carry

    lax.fori_loop(0, n_full, full_piece, 0)
    tail0 = pl.multiple_of(n_full * MOE_ROWS, MOE_ROWS)
    tail_steps = (count - n_full * MOE_ROWS + MOE_ROW_STEP - 1) // MOE_ROW_STEP
    for steps in range(1, MOE_ROWS // MOE_ROW_STEP + 1):
        @pl.when(tail_steps == steps)
        def _(steps=steps):
            expert_rows(tail0, steps * MOE_ROW_STEP)

    @pl.when(j == last_j)
    def _():
        lane = lax.broadcasted_iota(jnp.int32, (tm, LANES), 1)
        mine = lane == e
        slot_col = jnp.sum(jnp.where(mine, slot_ref[0], 0.0), axis=1, keepdims=True)
        comb_col = jnp.sum(jnp.where(mine, comb_ref[0], 0.0), axis=1, keepdims=True)
        col = lax.broadcasted_iota(jnp.int32, (tm, ch), 1).astype(F32)

        def scatter(c, carry):
            r0 = pl.multiple_of(c * ch, ch)
            place = jnp.where(slot_col == col + (c * ch).astype(F32), 1.0, 0.0).astype(MXU_DTYPE)
            o_ref[0] += comb_col * _dot(place, ye_ref[pl.ds(r0, ch), :].astype(MXU_DTYPE))
            return carry

        lax.fori_loop(0, n_chunks, scatter, 0)

    @pl.when(jnp.logical_and(e == pl.num_programs(2) - 1, j == last_j))
    def _():
        o_ref[0] = _resid_ln(x_ref[0], g2_ref[0], o_ref[0], lg_ref[...], lb_ref[...])


def _moe(x, sc, sh, g2, router_w, router_b, w_gu, w_down, ln_g, ln_b, tm=1024, tf=896):
    bsz, s, d = x.shape
    tm = min(tm, s)
    nt = s // tm
    nf = D_FF_EXPERT // tf
    rw = jnp.pad(router_w, ((0, 0), (0, LANES - N_EXPERTS)))
    rb = jnp.pad(router_b, (0, LANES - N_EXPERTS)).reshape(1, LANES)

    row2 = lambda b, i: (b, i, 0)
    vec2 = lambda b, i: (b, 0, 0)
    h, comb, slot, slot_t, cnt = pl.pallas_call(
        _router_kernel,
        grid=(bsz, nt),
        in_specs=[pl.BlockSpec((1, tm, d), row2),
                  pl.BlockSpec((1, 1, d), vec2), pl.BlockSpec((1, 1, d), vec2),
                  pl.BlockSpec((d, LANES), lambda b, i: (0, 0)),
                  pl.BlockSpec((1, LANES), lambda b, i: (0, 0))],
        out_specs=[pl.BlockSpec((1, tm, d), row2),
                   pl.BlockSpec((1, tm, LANES), row2),
                   pl.BlockSpec((1, tm, LANES), row2),
                   pl.BlockSpec((1, N_EXPERTS, tm), lambda b, i: (b * nt + i, 0, 0)),
                   pl.BlockSpec((1, 1, LANES), lambda b, i: (b * nt + i, 0, 0))],
        out_shape=[jax.ShapeDtypeStruct((bsz, s, d), MXU_DTYPE),
                   jax.ShapeDtypeStruct((bsz, s, LANES), F32),
                   jax.ShapeDtypeStruct((bsz, s, LANES), F32),
                   jax.ShapeDtypeStruct((bsz * nt, N_EXPERTS, tm), F32),
                   jax.ShapeDtypeStruct((bsz * nt, 1, LANES), jnp.int32)],
        compiler_params=_params(("parallel", "parallel")),
        name="moe_router",
    )(x, sc, sh, rw, rb)
    counts = cnt[:, 0, :N_EXPERTS].reshape(bsz * nt * N_EXPERTS)

    row = lambda b, i, e, j, cnt: (b, i, 0)
    vec = lambda b, i, e, j, cnt: (b, 0, 0)
    const2 = lambda b, i, e, j, cnt: (0, 0)
    grid_spec = pltpu.PrefetchScalarGridSpec(
        num_scalar_prefetch=1,
        grid=(bsz, nt, N_EXPERTS, nf),
        in_specs=[pl.BlockSpec((1, tm, d), row),
                  pl.BlockSpec((1, tm, d), row),
                  pl.BlockSpec((1, 1, d), vec),
                  pl.BlockSpec((1, tm, LANES), row),
                  pl.BlockSpec((1, tm, LANES), row),
                  pl.BlockSpec((1, N_EXPERTS, tm), lambda b, i, e, j, cnt: (b * nt + i, 0, 0)),
                  pl.BlockSpec((1, d, tf), lambda b, i, e, j, cnt: (e, 0, j)),
                  pl.BlockSpec((1, d, tf), lambda b, i, e, j, cnt: (e, 0, nf + j)),
                  pl.BlockSpec((1, tf, d), lambda b, i, e, j, cnt: (e, j, 0)),
                  pl.BlockSpec((1, d), const2), pl.BlockSpec((1, d), const2)],
        out_specs=pl.BlockSpec((1, tm, d), row),
        scratch_shapes=[pltpu.VMEM((tm, d), MXU_DTYPE), pltpu.VMEM((tm, d), F32)])
    return pl.pallas_call(
        functools.partial(_moe_kernel, n_tiles=nt),
        grid_spec=grid_spec,
        out_shape=jax.ShapeDtypeStruct((bsz, s, d), F32),
        compiler_params=pltpu.CompilerParams(
            dimension_semantics=("parallel", "parallel", "arbitrary", "arbitrary"),
            vmem_limit_bytes=MOE_VMEM_LIMIT),
        name="moe_experts",
    )(counts, h, x, g2, comb, slot, slot_t, w_gu, w_gu, w_down,
      ln_g.reshape(1, d), ln_b.reshape(1, d))


def _split_w_in(w):
    p = np.cumsum([0, 512, 512, 512, 256, 256, 512, 512, 512, SSD_XBC, SSD_HEADS, 512, 512, 512,
                   N_BRANCH * D_MODEL])
    mq, mk, mv, rq, rk, rv, rg, sz, sxbc, sdt, bq, bk, bv, gates = (
        w[:, p[i]:p[i + 1]] for i in range(14))
    w_a = jnp.concatenate([mq, mk, mv, rq, rk, rv, bq, bk, bv], axis=1).astype(MXU_DTYPE)
    sdt = jnp.pad(sdt, ((0, 0), (0, LANES - SSD_HEADS)))
    w_b = jnp.concatenate([gates, rg, sz, sxbc, sdt], axis=1).astype(MXU_DTYPE)
    return w_a, w_b


def kernel(x, c, w_ada, b_ada, w_in, conv_w, conv_b, dt_bias, a_log, d_skip, ssm_norm_g, ret_gn_g, ret_gn_b, w_br, w_out, ln1_g, ln1_b, ln2_g, ln2_b, ffn_w_gu, ffn_w_down, router_w, router_b, expert_w_gu, expert_w_down):
    bsz = x.shape[0]
    mod = _ada(c, w_ada, b_ada)
    for l in range(DEPTH):
        sh1, sc1, g1, sh2, sc2, g2 = (
            mod[l, :, i * D_MODEL:(i + 1) * D_MODEL].reshape(bsz, 1, D_MODEL) for i in range(6))
        w_a, w_b = _split_w_in(w_in[l])
        proj_a = _inproj(x, sc1, sh1, w_a, MXU_DTYPE, tn=1024)
        proj_b = _inproj(x, sc1, sh1, w_b, F32, tn=896)
        o_a = _moba(proj_a)
        o_b = _retention(proj_a, proj_b, ret_gn_g[l], ret_gn_b[l])
        o_c = _ssd(proj_b, conv_w[l], conv_b[l], dt_bias[l], a_log[l], d_skip[l], ssm_norm_g[l])
        o_d = _stick_breaking(proj_a)
        x = _merge(o_a, o_b, o_c, o_d, proj_b, x, g1, w_br[l].astype(MXU_DTYPE),
                   w_out[l].astype(MXU_DTYPE), ln1_g[l], ln1_b[l])
        if l % 2 == 0:
            x = _ffn(x, sc2, sh2, g2, ffn_w_gu[l // 2].astype(MXU_DTYPE),
                     ffn_w_down[l // 2].astype(MXU_DTYPE), ln2_g[l], ln2_b[l])
        else:
            x = _moe(x, sc2, sh2, g2, router_w[l // 2], router_b[l // 2],
                     expert_w_gu[l // 2].astype(MXU_DTYPE), expert_w_down[l // 2].astype(MXU_DTYPE),
                     ln2_g[l], ln2_b[l])
    return x
```

```python
import functools

import numpy as np
import jax
import jax.numpy as jnp
from jax import lax
from jax.experimental import pallas as pl
from jax.experimental.pallas import tpu as pltpu

F32 = jnp.float32
MXU_DTYPE = jnp.bfloat16

D_MODEL = 1024
DEPTH = 2
MOBA_HEADS = 8
MOBA_BLOCK = 256
MOBA_TOPK = 3
RET_HEADS = 4
RET_DK = 64
RET_DV = 128
RET_CHUNK = 256
SSD_D_INNER = 512
SSD_HEAD_DIM = 64
SSD_HEADS = 8
SSD_GROUPS = 2
SSD_STATE = 128
SSD_CONV = 4
SSD_CHUNK = 256
SSD_XBC = SSD_D_INNER + 2 * SSD_GROUPS * SSD_STATE
SB_HEADS = 8
SB_BLOCK = 256
HEAD_DIM = 64
ATT_HEADS_PER_STEP = 8
ATT_W = ATT_HEADS_PER_STEP * HEAD_DIM
BRANCH_W = 512
N_BRANCH = 4
D_FF = 2816
N_EXPERTS = 8
D_FF_EXPERT = 3584
DEEPNORM_ALPHA = (2.0 * DEPTH) ** 0.25
LN_EPS = 1e-5
NORM_EPS = 1e-6
NEG_INF = -1e30
LOG2E = 1.4426950408889634
LANES = 128
VMEM_LIMIT = 48 * 2 ** 20
MOE_VMEM_LIMIT = 56 * 2 ** 20
MOE_CHUNK = 256
MOE_ROWS = 256
MOE_ROW_STEP = 64

A_MOBA_Q, A_MOBA_K, A_MOBA_V = 0, 512, 1024
A_RET_Q, A_RET_K, A_RET_V = 1536, 1792, 2048
A_SB_Q, A_SB_K, A_SB_V = 2560, 3072, 3584
A_WIDTH = 4096
B_GATES, B_RET_G, B_SSD_Z, B_XBC, B_DT = 0, 4096, 4608, 5120, 6144
B_WIDTH = 6272


def _params(sem):
    return pltpu.CompilerParams(dimension_semantics=sem, vmem_limit_bytes=VMEM_LIMIT)


def _dot(a, b):
    return jnp.dot(a, b, preferred_element_type=F32)


def _dot_nt(a, b):
    return lax.dot_general(a, b, (((1,), (1,)), ((), ())), preferred_element_type=F32)


def _dot_tn(a, b):
    return lax.dot_general(a, b, (((0,), (0,)), ((), ())), preferred_element_type=F32)


def _split(x):
    hi = x.astype(MXU_DTYPE)
    lo = (x - hi.astype(F32)).astype(MXU_DTYPE)
    return hi, lo


def _dot3(a, b):
    ah, al = _split(a)
    bh, bl = _split(b)
    return _dot(ah, bh) + _dot(al, bh) + _dot(ah, bl)


def _sigmoid(x):
    return 1.0 / (1.0 + jnp.exp(-x))


def _silu(x):
    return x * _sigmoid(x)


def _resid_ln(x, gate, f, ln_g, ln_b):
    y = DEEPNORM_ALPHA * x + gate * f
    mu = jnp.mean(y, axis=-1, keepdims=True)
    d = y - mu
    var = jnp.mean(d * d, axis=-1, keepdims=True)
    return d * lax.rsqrt(var + LN_EPS) * ln_g + ln_b


def _ada_kernel(c_ref, w_ref, b_ref, o_ref):
    c = c_ref[...]
    o_ref[0] = _dot3(_silu(c), w_ref[0]) + b_ref[0]


def _ada(c, w_ada, b_ada):
    depth, d, n = w_ada.shape
    bsz = c.shape[0]
    tn = 1536
    return pl.pallas_call(
        _ada_kernel,
        grid=(depth, n // tn),
        in_specs=[pl.BlockSpec((bsz, d), lambda l, j: (0, 0)),
                  pl.BlockSpec((1, d, tn), lambda l, j: (l, 0, j)),
                  pl.BlockSpec((1, 1, tn), lambda l, j: (l, 0, j))],
        out_specs=pl.BlockSpec((1, bsz, tn), lambda l, j: (l, 0, j)),
        out_shape=jax.ShapeDtypeStruct((depth, bsz, n), F32),
        compiler_params=_params(("arbitrary", "arbitrary")),
        name="ada_mod",
    )(c, w_ada, b_ada.reshape(depth, 1, n))


def _inproj_kernel(x_ref, sc_ref, sh_ref, w_ref, o_ref, h_ref):
    @pl.when(pl.program_id(2) == 0)
    def _():
        h = x_ref[0] * (1.0 + sc_ref[0]) + sh_ref[0]
        h_ref[...] = h.astype(MXU_DTYPE)

    o_ref[0] = _dot(h_ref[...], w_ref[...]).astype(o_ref.dtype)


def _inproj(x, sc, sh, w, out_dtype, tn, tm=1024):
    bsz, s, d = x.shape
    n = w.shape[1]
    tm = min(tm, s)
    return pl.pallas_call(
        _inproj_kernel,
        grid=(bsz, s // tm, n // tn),
        in_specs=[pl.BlockSpec((1, tm, d), lambda b, i, j: (b, i, 0)),
                  pl.BlockSpec((1, 1, d), lambda b, i, j: (b, 0, 0)),
                  pl.BlockSpec((1, 1, d), lambda b, i, j: (b, 0, 0)),
                  pl.BlockSpec((d, tn), lambda b, i, j: (0, j))],
        out_specs=pl.BlockSpec((1, tm, tn), lambda b, i, j: (b, i, j)),
        out_shape=jax.ShapeDtypeStruct((bsz, s, n), out_dtype),
        scratch_shapes=[pltpu.VMEM((tm, d), MXU_DTYPE)],
        compiler_params=_params(("parallel", "parallel", "arbitrary")),
        name="in_proj",
    )(x, sc, sh, w)


def _moba_kernel(q_ref, k_ref, v_ref, sl_ref, o_ref, km_ref, ka_ref, vt_ref, *, nb):
    blk = MOBA_BLOCK
    nh = ATT_HEADS_PER_STEP
    heads = range(nh)
    qi = pl.program_id(2)
    s_len = nb * blk
    nb_pad = -(-nb // 8) * 8
    extra = 2 * HEAD_DIM - HEAD_DIM
    assert nb_pad + 8 <= extra
    cols = [slice(h * HEAD_DIM, (h + 1) * HEAD_DIM) for h in heads]

    @pl.when(qi == 0)
    def _():
        k_all = k_ref[0]
        kk = k_all.astype(F32).reshape(nb, blk, ATT_W)
        km_ref[...] = jnp.zeros_like(km_ref)
        km_ref[0:nb, :] = jnp.sum(kk, axis=1) * (1.0 / blk)
        pos = lax.broadcasted_iota(jnp.int32, (s_len, extra), 0)
        lane = lax.broadcasted_iota(jnp.int32, (s_len, extra), 1)
        one_hot = jnp.where(lane == pos // blk, 1.0, 0.0)
        pos_hi = jnp.where(lane == nb_pad, (pos // blk * blk).astype(F32), 0.0)
        pos_lo = jnp.where(lane == nb_pad + 1, (pos % blk).astype(F32), 0.0)
        tail = (one_hot + pos_hi + pos_lo).astype(MXU_DTYPE)
        for h in heads:
            ka_ref[h] = jnp.concatenate([k_all[:, cols[h]], tail], axis=1)

        ones_rows = jnp.where(lax.broadcasted_iota(jnp.int32, (extra, blk), 0) == 0, 1.0, 0.0)

        def transpose_block(j, carry):
            j0 = pl.multiple_of(j * blk, blk)
            vt = v_ref[0, pl.ds(j0, blk), :].astype(F32).T
            for h in heads:
                vt_ref[j, h] = jnp.concatenate([vt[cols[h], :], ones_rows], axis=0).astype(MXU_DTYPE)
            return carry

        lax.fori_loop(0, nb, transpose_block, 0)

    key_i = lax.broadcasted_iota(jnp.int32, (blk, blk), 0)
    qry_i = lax.broadcasted_iota(jnp.int32, (blk, blk), 1)
    n_iota = lax.broadcasted_iota(jnp.int32, (nb_pad, blk), 0)
    past = n_iota < qi
    k_sel = min(MOBA_TOPK, max(nb - 1, 1))
    q_t = q_ref[0].astype(F32).T
    slope_rows = lax.broadcasted_iota(jnp.int32, (8, blk), 0) < 2

    qa = []
    for h in heads:
        qh_t = q_t[cols[h], :]
        slope = sl_ref[0, :, h * HEAD_DIM:h * HEAD_DIM + 1]
        km_hi, km_lo = _split(km_ref[:, cols[h]])
        qh_mxu = qh_t.astype(MXU_DTYPE)
        gate = (_dot(km_hi, qh_mxu) + _dot(km_lo, qh_mxu))[0:nb_pad, :]
        g = jnp.where(past, gate, NEG_INF)
        g = jnp.where(n_iota < nb, g, -jnp.inf)
        sel = jnp.zeros((nb_pad, blk), F32)
        for _ in range(k_sel):
            mx = jnp.max(g, axis=0, keepdims=True)
            idx = jnp.min(jnp.where(g == mx, n_iota, nb_pad), axis=0, keepdims=True)
            pick = n_iota == idx
            sel = jnp.where(pick, 1.0, sel)
            g = jnp.where(pick, -jnp.inf, g)
        keep = jnp.logical_or(jnp.logical_and(past, sel > 0.5), n_iota == qi)
        bias = jnp.where(keep, 0.0, NEG_INF)
        alibi = jnp.where(slope_rows, slope, 0.0)
        pad = jnp.zeros((extra - nb_pad - 8, blk), F32)
        qa.append(jnp.concatenate([qh_t * HEAD_DIM ** -0.5, bias, alibi, pad],
                                  axis=0).astype(MXU_DTYPE))

    def tiles(js, carry, diag):
        starts = [pl.multiple_of(j * blk, blk) for j in js]
        s = [[_dot(ka_ref[h, pl.ds(j0, blk), :], qa[h]) for h in heads] for j0 in starts]
        m = list(carry[:nh]) if carry is not None else [None] * nh
        acc = list(carry[nh:]) if carry is not None else [None] * nh
        p = {}
        for h in heads:
            sh = [jnp.where(key_i <= qry_i, s[t][h], NEG_INF) if diag else s[t][h]
                  for t in range(len(js))]
            top = jnp.max(sh[0], axis=0, keepdims=True)
            for t in range(1, len(js)):
                top = jnp.maximum(top, jnp.max(sh[t], axis=0, keepdims=True))
            if m[h] is None:
                m[h] = top
            else:
                m_new = jnp.maximum(m[h], top)
                acc[h] = jnp.exp(m[h] - m_new) * acc[h]
                m[h] = m_new
            for t in range(len(js)):
                p[t, h] = jnp.exp(sh[t] - m[h]).astype(MXU_DTYPE)
        for t, j in enumerate(js):
            for h in heads:
                pv = _dot(vt_ref[j, h], p[t, h])
                acc[h] = pv if acc[h] is None else acc[h] + pv
        return tuple(m) + tuple(acc)

    carry0 = tiles([qi], None, True)

    def pair(t, carry):
        return tiles([2 * t, 2 * t + 1], carry, False)

    def single(t, carry):
        return tiles([qi - 1], carry, False)

    res = lax.fori_loop(0, qi // 2, pair, carry0)
    res = lax.fori_loop(0, qi % 2, single, res)
    out_t = [res[nh + h][0:HEAD_DIM, :] / res[nh + h][HEAD_DIM:HEAD_DIM + 1, :] for h in heads]
    o_ref[0] = jnp.concatenate(out_t, axis=0).T.astype(o_ref.dtype)


def _moba(proj_a):
    bsz, s, _ = proj_a.shape
    blk = MOBA_BLOCK
    nb = s // blk
    n_grp = MOBA_HEADS // ATT_HEADS_PER_STEP
    slopes = 2.0 ** (-8.0 * np.arange(1, MOBA_HEADS + 1) / MOBA_HEADS)
    sl = jnp.asarray(np.repeat(slopes, HEAD_DIM).reshape(n_grp, 1, ATT_W), F32)
    qb, kb, vb = A_MOBA_Q // ATT_W, A_MOBA_K // ATT_W, A_MOBA_V // ATT_W
    return pl.pallas_call(
        functools.partial(_moba_kernel, nb=nb),
        grid=(bsz, n_grp, nb),
        in_specs=[pl.BlockSpec((1, blk, ATT_W), lambda b, hg, i: (b, i, qb + hg)),
                  pl.BlockSpec((1, s, ATT_W), lambda b, hg, i: (b, 0, kb + hg)),
                  pl.BlockSpec((1, s, ATT_W), lambda b, hg, i: (b, 0, vb + hg)),
                  pl.BlockSpec((1, 1, ATT_W), lambda b, hg, i: (hg, 0, 0))],
        out_specs=pl.BlockSpec((1, blk, ATT_W), lambda b, hg, i: (b, i, hg)),
        out_shape=jax.ShapeDtypeStruct((bsz, s, MOBA_HEADS * HEAD_DIM), MXU_DTYPE),
        scratch_shapes=[pltpu.VMEM((LANES, ATT_W), F32),
                        pltpu.VMEM((ATT_HEADS_PER_STEP, s, 2 * HEAD_DIM), MXU_DTYPE),
                        pltpu.VMEM((nb, ATT_HEADS_PER_STEP, 2 * HEAD_DIM, blk), MXU_DTYPE)],
        compiler_params=_params(("parallel", "parallel", "arbitrary")),
        name="moba",
    )(proj_a, proj_a, proj_a, sl)


def _sb_kernel(q_ref, k_ref, v_ref, o_ref, vt_ref, *, nb):
    blk = SB_BLOCK
    nh = ATT_HEADS_PER_STEP
    heads = range(nh)
    qi = pl.program_id(2)
    cols = [slice(h * HEAD_DIM, (h + 1) * HEAD_DIM) for h in heads]

    @pl.when(qi == 0)
    def _():
        def transpose_block(j, carry):
            j0 = pl.multiple_of(j * blk, blk)
            vt = v_ref[0, pl.ds(j0, blk), :].astype(F32).T
            for h in heads:
                vt_ref[j, h] = vt[cols[h], :].astype(MXU_DTYPE)
            return carry

        lax.fori_loop(0, nb, transpose_block, 0)

    key_i = lax.broadcasted_iota(jnp.int32, (blk, blk), 0)
    qry_i = lax.broadcasted_iota(jnp.int32, (blk, blk), 1)
    causal = key_i < qry_i
    later = jnp.where(qry_i > key_i, 1.0, 0.0).astype(MXU_DTYPE)
    qs = [(q_ref[0, :, cols[h]].astype(F32) * HEAD_DIM ** -0.5).astype(MXU_DTYPE)
          for h in heads]

    def tiles(js, accs, rests, diag):
        starts = [pl.multiple_of(j * blk, blk) for j in js]
        z = [[_dot_nt(k_ref[0, pl.ds(j0, blk), cols[h]], qs[h]) for h in heads] for j0 in starts]
        log_beta, log_1mb, parts = [], [], []
        for t in range(len(js)):
            for h in heads:
                zz = z[t][h]
                lb = jnp.minimum(zz, 0.0) - jnp.log(1.0 + jnp.exp2(jnp.abs(zz) * (-LOG2E)))
                l1 = lb - zz
                if diag:
                    l1 = jnp.where(causal, l1, 0.0)
                log_beta.append(lb)
                log_1mb.append(l1)
                parts.append(l1.astype(MXU_DTYPE))
        rem = [_dot(later, p) for p in parts]
        w = []
        rests = list(rests)
        for t in range(len(js)):
            for h in heads:
                i = t * nh + h
                wh = jnp.exp(log_beta[i] + rem[i] + rests[h])
                if diag:
                    wh = jnp.where(causal, wh, 0.0)
                w.append(wh.astype(MXU_DTYPE))
                rests[h] = rests[h] + rem[i][0:1, :] + log_1mb[i][0:1, :]
        accs = list(accs)
        for t, j in enumerate(js):
            for h in heads:
                accs[h] = accs[h] + _dot(vt_ref[j, h], w[t * nh + h])
        return accs, rests

    zero_acc = [jnp.zeros((HEAD_DIM, blk), F32)] * nh
    accs, rests = tiles([qi], zero_acc, [jnp.zeros((1, blk), F32)] * nh, True)

    def pair(t, carry):
        j = qi - 1 - 2 * t
        accs, rests = tiles([j, j - 1], carry[:nh], carry[nh:], False)
        return tuple(accs + rests)

    def single(t, carry):
        accs, rests = tiles([0], carry[:nh], carry[nh:], False)
        return tuple(accs + rests)

    res = lax.fori_loop(0, qi // 2, pair, tuple(accs + rests))
    res = lax.fori_loop(0, qi % 2, single, res)
    o_ref[0] = jnp.concatenate(res[:nh], axis=0).T.astype(o_ref.dtype)


def _stick_breaking(proj_a):
    bsz, s, _ = proj_a.shape
    blk = SB_BLOCK
    n_grp = SB_HEADS // ATT_HEADS_PER_STEP
    qb, kb, vb = A_SB_Q // ATT_W, A_SB_K // ATT_W, A_SB_V // ATT_W
    return pl.pallas_call(
        functools.partial(_sb_kernel, nb=s // blk),
        grid=(bsz, n_grp, s // blk),
        in_specs=[pl.BlockSpec((1, blk, ATT_W), lambda b, hg, i: (b, i, qb + hg)),
                  pl.BlockSpec((1, s, ATT_W), lambda b, hg, i: (b, 0, kb + hg)),
                  pl.BlockSpec((1, s, ATT_W), lambda b, hg, i: (b, 0, vb + hg))],
        out_specs=pl.BlockSpec((1, blk, ATT_W), lambda b, hg, i: (b, i, hg)),
        out_shape=jax.ShapeDtypeStruct((bsz, s, SB_HEADS * HEAD_DIM), MXU_DTYPE),
        scratch_shapes=[pltpu.VMEM((s // blk, ATT_HEADS_PER_STEP, HEAD_DIM, blk), MXU_DTYPE)],
        compiler_params=_params(("parallel", "parallel", "arbitrary")),
        name="stick_breaking",
    )(proj_a, proj_a, proj_a)


def _ret_kernel(q_ref, k_ref, v_ref, g_ref, gn_g_ref, gn_b_ref, o_ref, st_ref):
    ch = RET_CHUNK

    @pl.when(pl.program_id(1) == 0)
    def _():
        st_ref[...] = jnp.zeros_like(st_ref)

    i_col = lax.broadcasted_iota(jnp.int32, (ch, 1), 0).astype(F32)
    diff = (lax.broadcasted_iota(jnp.int32, (ch, ch), 0)
            - lax.broadcasted_iota(jnp.int32, (ch, ch), 1)).astype(F32)
    heads = range(RET_HEADS)
    log_g = [float(np.log(1.0 - 2.0 ** (-5.0 - h))) for h in heads]
    qs = [(q_ref[0, :, h * RET_DK:(h + 1) * RET_DK].astype(F32) * RET_DK ** -0.5).astype(MXU_DTYPE)
          for h in heads]
    k = [k_ref[0, :, h * RET_DK:(h + 1) * RET_DK] for h in heads]
    v = [v_ref[0, :, h * RET_DV:(h + 1) * RET_DV] for h in heads]
    prev = [st_ref[h] for h in heads]
    qk = [_dot_nt(qs[h], k[h]) for h in heads]
    o_cross = [_dot(qs[h], prev[h].astype(MXU_DTYPE)) for h in heads]
    kd = [(k[h].astype(F32) * jnp.exp((ch - 1.0 - i_col) * log_g[h])).astype(MXU_DTYPE)
          for h in heads]
    st_new = [_dot_tn(kd[h], v[h]) for h in heads]
    qkd = []
    for h in heads:
        decay = jnp.where(diff >= 0, jnp.exp(jnp.maximum(diff, 0.0) * log_g[h]), 0.0)
        qkd.append((qk[h] * decay).astype(MXU_DTYPE))
    o_intra = [_dot(qkd[h], v[h]) for h in heads]
    for h in heads:
        st_ref[h] = float(np.exp(ch * log_g[h])) * prev[h] + st_new[h]
        o = o_intra[h] + o_cross[h] * jnp.exp((i_col + 1.0) * log_g[h])
        mu = jnp.mean(o, axis=-1, keepdims=True)
        d = o - mu
        var = jnp.mean(d * d, axis=-1, keepdims=True)
        cs = slice(h * RET_DV, (h + 1) * RET_DV)
        on = d * lax.rsqrt(var + NORM_EPS) * gn_g_ref[:, cs] + gn_b_ref[:, cs]
        o_ref[0, :, cs] = (_silu(g_ref[0, :, cs]) * on).astype(o_ref.dtype)


def _retention(proj_a, proj_b, gn_g, gn_b):
    bsz, s, _ = proj_a.shape
    ch = RET_CHUNK
    wq, wv = RET_HEADS * RET_DK, RET_HEADS * RET_DV
    return pl.pallas_call(
        _ret_kernel,
        grid=(bsz, s // ch),
        in_specs=[pl.BlockSpec((1, ch, wq), lambda b, i: (b, i, A_RET_Q // wq)),
                  pl.BlockSpec((1, ch, wq), lambda b, i: (b, i, A_RET_K // wq)),
                  pl.BlockSpec((1, ch, wv), lambda b, i: (b, i, A_RET_V // wv)),
                  pl.BlockSpec((1, ch, wv), lambda b, i: (b, i, B_RET_G // wv)),
                  pl.BlockSpec((1, wv), lambda b, i: (0, 0)),
                  pl.BlockSpec((1, wv), lambda b, i: (0, 0))],
        out_specs=pl.BlockSpec((1, ch, wv), lambda b, i: (b, i, 0)),
        out_shape=jax.ShapeDtypeStruct((bsz, s, wv), MXU_DTYPE),
        scratch_shapes=[pltpu.VMEM((RET_HEADS, RET_DK, RET_DV), F32)],
        compiler_params=_params(("parallel", "arbitrary")),
        name="retention",
    )(proj_a, proj_a, proj_a, proj_b, gn_g.reshape(1, wv), gn_b.reshape(1, wv))


def _ssd_kernel(z_ref, xbc_ref, dt_ref, cw_ref, cb_ref, dtb_ref, alog_ref, dsk_ref, ng_ref,
                o_ref, tail_ref, st_ref, y_ref):
    ch = SSD_CHUNK
    heads_per_group = SSD_HEADS // SSD_GROUPS
    halo = 8

    @pl.when(pl.program_id(1) == 0)
    def _():
        tail_ref[...] = jnp.zeros_like(tail_ref)
        st_ref[...] = jnp.zeros_like(st_ref)

    xbc = xbc_ref[0]
    xpad = jnp.concatenate([tail_ref[...], xbc], axis=0)
    tail_ref[...] = xbc[ch - halo:, :]
    conv = cb_ref[...]
    for kk in range(SSD_CONV):
        off = halo - (SSD_CONV - 1) + kk
        conv = conv + cw_ref[kk:kk + 1, :] * xpad[off:off + ch, :]
    xc = _silu(conv)
    xs = xc[:, :SSD_D_INNER]
    bm = xc[:, SSD_D_INNER:SSD_D_INNER + SSD_GROUPS * SSD_STATE]
    cm = xc[:, SSD_D_INNER + SSD_GROUPS * SSD_STATE:]

    dtr = dt_ref[0] + dtb_ref[...]
    dt = jnp.maximum(dtr, 0.0) + jnp.log1p(jnp.exp(-jnp.abs(dtr)))
    a = -jnp.exp(alog_ref[...])
    r_iota = lax.broadcasted_iota(jnp.int32, (ch, ch), 0)
    c_iota = lax.broadcasted_iota(jnp.int32, (ch, ch), 1)
    causal = r_iota >= c_iota
    tri = jnp.where(causal, 1.0, 0.0).astype(MXU_DTYPE)
    da = dt * a
    da_hi, da_lo = _split(da)
    cum = _dot(tri, da_hi) + _dot(tri, da_lo)
    cum_t = cum.T
    dt_t = dt.T
    cum_last = cum[ch - 1:ch, :]
    decay_st = jnp.exp(cum_last - cum) * dt
    e_cum = jnp.exp(cum)
    e_last = jnp.exp(cum_last)

    for g in range(SSD_GROUPS):
        bm_g = bm[:, g * SSD_STATE:(g + 1) * SSD_STATE].astype(MXU_DTYPE)
        cm_g = cm[:, g * SSD_STATE:(g + 1) * SSD_STATE].astype(MXU_DTYPE)
        cb = _dot_nt(cm_g, bm_g)
        for r in range(heads_per_group):
            hd = g * heads_per_group + r
            cs = slice(hd * SSD_HEAD_DIM, (hd + 1) * SSD_HEAD_DIM)
            x_h = xs[:, cs]
            seg = cum[:, hd:hd + 1] - cum_t[hd:hd + 1, :]
            lmat = jnp.exp(jnp.where(causal, seg, NEG_INF))
            w = cb * lmat * dt_t[hd:hd + 1, :]
            y = _dot(w.astype(MXU_DTYPE), x_h.astype(MXU_DTYPE))
            prev = st_ref[hd]
            y = y + _dot(cm_g, prev.astype(MXU_DTYPE)) * e_cum[:, hd:hd + 1]
            xd = (x_h * decay_st[:, hd:hd + 1]).astype(MXU_DTYPE)
            st_ref[hd] = e_last[:, hd:hd + 1] * prev + _dot_tn(bm_g, xd)
            y_ref[:, cs] = y + dsk_ref[:, cs] * x_h

    y = y_ref[...] * _silu(z_ref[0])
    gw = SSD_D_INNER // SSD_GROUPS
    for g in range(SSD_GROUPS):
        cs = slice(g * gw, (g + 1) * gw)
        yg = y[:, cs]
        ms = jnp.mean(yg * yg, axis=-1, keepdims=True)
        o_ref[0, :, cs] = (yg * lax.rsqrt(ms + LN_EPS) * ng_ref[:, cs]).astype(o_ref.dtype)


def _ssd(proj_b, conv_w, conv_b, dt_bias, a_log, d_skip, norm_g):
    bsz, s, _ = proj_b.shape
    ch = SSD_CHUNK
    pad = LANES - SSD_HEADS
    dtb = jnp.pad(dt_bias, (0, pad)).reshape(1, LANES)
    alog = jnp.pad(a_log, (0, pad)).reshape(1, LANES)
    dsk = jnp.repeat(d_skip, SSD_HEAD_DIM).reshape(1, SSD_D_INNER)
    const = lambda b, i: (0, 0)
    return pl.pallas_call(
        _ssd_kernel,
        grid=(bsz, s // ch),
        in_specs=[pl.BlockSpec((1, ch, SSD_D_INNER), lambda b, i: (b, i, B_SSD_Z // SSD_D_INNER)),
                  pl.BlockSpec((1, ch, SSD_XBC), lambda b, i: (b, i, B_XBC // SSD_XBC)),
                  pl.BlockSpec((1, ch, LANES), lambda b, i: (b, i, B_DT // LANES)),
                  pl.BlockSpec((SSD_CONV, SSD_XBC), const),
                  pl.BlockSpec((1, SSD_XBC), const),
                  pl.BlockSpec((1, LANES), const),
                  pl.BlockSpec((1, LANES), const),
                  pl.BlockSpec((1, SSD_D_INNER), const),
                  pl.BlockSpec((1, SSD_D_INNER), const)],
        out_specs=pl.BlockSpec((1, ch, SSD_D_INNER), lambda b, i: (b, i, 0)),
        out_shape=jax.ShapeDtypeStruct((bsz, s, SSD_D_INNER), MXU_DTYPE),
        scratch_shapes=[pltpu.VMEM((8, SSD_XBC), F32),
                        pltpu.VMEM((SSD_HEADS, SSD_STATE, SSD_HEAD_DIM), F32),
                        pltpu.VMEM((ch, SSD_D_INNER), F32)],
        compiler_params=_params(("parallel", "arbitrary")),
        name="ssd",
    )(proj_b, proj_b, proj_b, conv_w, conv_b.reshape(1, SSD_XBC), dtb, alog, dsk,
      norm_g.reshape(1, SSD_D_INNER))


def _merge_kernel(oa_ref, ob_ref, oc_ref, od_ref, gates_ref, x_ref, g1_ref, wbr_ref, wout_ref,
                  lg_ref, lb_ref, o_ref):
    merged = None
    for n, br in enumerate((oa_ref, ob_ref, oc_ref, od_ref)):
        y = _dot(br[0], wbr_ref[n])
        t = _sigmoid(gates_ref[0, :, n * D_MODEL:(n + 1) * D_MODEL]) * y
        merged = t if merged is None else merged + t
    mix = _dot(merged.astype(MXU_DTYPE), wout_ref[...])
    o_ref[0] = _resid_ln(x_ref[0], g1_ref[0], mix, lg_ref[...], lb_ref[...])


def _merge(o_a, o_b, o_c, o_d, proj_b, x, g1, w_br, w_out, ln_g, ln_b, tm=512):
    bsz, s, d = x.shape
    tm = min(tm, s)
    gw = N_BRANCH * d
    row = lambda b, i: (b, i, 0)
    const2 = lambda b, i: (0, 0)
    br_spec = pl.BlockSpec((1, tm, BRANCH_W), row)
    return pl.pallas_call(
        _merge_kernel,
        grid=(bsz, s // tm),
        in_specs=[br_spec, br_spec, br_spec, br_spec,
                  pl.BlockSpec((1, tm, gw), lambda b, i: (b, i, B_GATES // gw)),
                  pl.BlockSpec((1, tm, d), row),
                  pl.BlockSpec((1, 1, d), lambda b, i: (b, 0, 0)),
                  pl.BlockSpec((N_BRANCH, BRANCH_W, d), lambda b, i: (0, 0, 0)),
                  pl.BlockSpec((d, d), const2),
                  pl.BlockSpec((1, d), const2),
                  pl.BlockSpec((1, d), const2)],
        out_specs=pl.BlockSpec((1, tm, d), row),
        out_shape=jax.ShapeDtypeStruct((bsz, s, d), F32),
        compiler_params=_params(("parallel", "parallel")),
        name="merge",
    )(o_a, o_b, o_c, o_d, proj_b, x, g1, w_br, w_out, ln_g.reshape(1, d), ln_b.reshape(1, d))


def _ffn_kernel(x_ref, sc_ref, sh_ref, g2_ref, wg_ref, wu_ref, wd_ref, lg_ref, lb_ref,
                o_ref, h_ref, acc_ref):
    j = pl.program_id(2)

    @pl.when(j == 0)
    def _():
        h = x_ref[0] * (1.0 + sc_ref[0]) + sh_ref[0]
        h_ref[...] = h.astype(MXU_DTYPE)
        acc_ref[...] = jnp.zeros_like(acc_ref)

    h = h_ref[...]
    act = _silu(_dot(h, wg_ref[...])) * _dot(h, wu_ref[...])
    acc_ref[...] += _dot(act.astype(MXU_DTYPE), wd_ref[...])

    @pl.when(j == pl.num_programs(2) - 1)
    def _():
        o_ref[0] = _resid_ln(x_ref[0], g2_ref[0], acc_ref[...], lg_ref[...], lb_ref[...])


def _ffn(x, sc, sh, g2, w_gu, w_down, ln_g, ln_b, tm=512, tf=1408):
    bsz, s, d = x.shape
    tm = min(tm, s)
    nf = D_FF // tf
    row = lambda b, i, j: (b, i, 0)
    vec = lambda b, i, j: (b, 0, 0)
    const2 = lambda b, i, j: (0, 0)
    return pl.pallas_call(
        _ffn_kernel,
        grid=(bsz, s // tm, nf),
        in_specs=[pl.BlockSpec((1, tm, d), row),
                  pl.BlockSpec((1, 1, d), vec), pl.BlockSpec((1, 1, d), vec),
                  pl.BlockSpec((1, 1, d), vec),
                  pl.BlockSpec((d, tf), lambda b, i, j: (0, j)),
                  pl.BlockSpec((d, tf), lambda b, i, j: (0, nf + j)),
                  pl.BlockSpec((tf, d), lambda b, i, j: (j, 0)),
                  pl.BlockSpec((1, d), const2), pl.BlockSpec((1, d), const2)],
        out_specs=pl.BlockSpec((1, tm, d), row),
        out_shape=jax.ShapeDtypeStruct((bsz, s, d), F32),
        scratch_shapes=[pltpu.VMEM((tm, d), MXU_DTYPE), pltpu.VMEM((tm, d), F32)],
        compiler_params=_params(("parallel", "parallel", "arbitrary")),
        name="ffn_dense",
    )(x, sc, sh, g2, w_gu, w_gu, w_down, ln_g.reshape(1, d), ln_b.reshape(1, d))


def _router_kernel(x_ref, sc_ref, sh_ref, rw_ref, rb_ref,
                   h_ref, comb_ref, slot_ref, slot_t_ref, cnt_ref):
    tm = x_ref.shape[1]
    lane = lax.broadcasted_iota(jnp.int32, (tm, LANES), 1)
    h = x_ref[0] * (1.0 + sc_ref[0]) + sh_ref[0]
    h_ref[0] = h.astype(MXU_DTYPE)
    logits = _dot3(h, rw_ref[...]) + rb_ref[...]
    logits = jnp.where(lane < N_EXPERTS, logits, -jnp.inf)
    v1 = jnp.max(logits, axis=1, keepdims=True)
    i1 = jnp.min(jnp.where(logits == v1, lane, LANES), axis=1, keepdims=True)
    rest = jnp.where(lane == i1, -jnp.inf, logits)
    v2 = jnp.max(rest, axis=1, keepdims=True)
    i2 = jnp.min(jnp.where(rest == v2, lane, LANES), axis=1, keepdims=True)
    e2 = jnp.exp(v2 - v1)
    w1 = 1.0 / (1.0 + e2)
    w2 = e2 / (1.0 + e2)
    comb_ref[0] = jnp.where(lane == i1, w1, 0.0) + jnp.where(lane == i2, w2, 0.0)
    routed = jnp.logical_or(lane == i1, lane == i2)
    earlier = (lax.broadcasted_iota(jnp.int32, (tm, tm), 1)
               < lax.broadcasted_iota(jnp.int32, (tm, tm), 0))
    before = _dot(jnp.where(earlier, 1.0, 0.0).astype(MXU_DTYPE),
                  jnp.where(routed, 1.0, 0.0).astype(MXU_DTYPE))
    slot = jnp.where(routed, before, -1.0)
    slot_ref[0] = slot
    slot_t_ref[0] = slot.T[0:N_EXPERTS, :]
    cnt_ref[0] = jnp.sum(jnp.where(routed, 1, 0), axis=0, keepdims=True)


def _moe_kernel(cnt_ref, h_ref, x_ref, g2_ref, comb_ref, slot_ref, slot_t_ref,
                wg_ref, wu_ref, wd_ref, lg_ref, lb_ref, o_ref, xe_ref, ye_ref, *, n_tiles):
    tm = x_ref.shape[1]
    ch = MOE_CHUNK
    e = pl.program_id(2)
    j = pl.program_id(3)
    last_j = pl.num_programs(3) - 1
    count = cnt_ref[(pl.program_id(0) * n_tiles + pl.program_id(1)) * N_EXPERTS + e]
    n_chunks = (count + ch - 1) // ch

    @pl.when(jnp.logical_and(e == 0, j == 0))
    def _():
        o_ref[0] = jnp.zeros((tm, D_MODEL), F32)

    @pl.when(j == 0)
    def _():
        slot_row = slot_t_ref[0, pl.ds(e, 1), :]
        row = lax.broadcasted_iota(jnp.int32, (ch, tm), 0).astype(F32)

        def gather(c, carry):
            r0 = pl.multiple_of(c * ch, ch)
            pick = jnp.where(slot_row == row + (c * ch).astype(F32), 1.0, 0.0).astype(MXU_DTYPE)
            xe_ref[pl.ds(r0, ch), :] = _dot(pick, h_ref[0]).astype(MXU_DTYPE)
            ye_ref[pl.ds(r0, ch), :] = jnp.zeros((ch, D_MODEL), F32)
            return carry

        lax.fori_loop(0, n_chunks, gather, 0)

    def expert_rows(r0, rows):
        xc = xe_ref[pl.ds(r0, rows), :]
        act = _silu(_dot(xc, wg_ref[0])) * _dot(xc, wu_ref[0])
        ye_ref[pl.ds(r0, rows), :] += _dot(act.astype(MXU_DTYPE), wd_ref[0])

    n_full = count // MOE_ROWS

    def full_piece(c, carry):
        expert_rows(pl.multiple_of(c * MOE_ROWS, MOE_ROWS), MOE_ROWS)
        return carry

    lax.fori_loop(0, n_full, full_piece, 0)
    tail0 = pl.multiple_of(n_full * MOE_ROWS, MOE_ROWS)
    tail_steps = (count - n_full * MOE_ROWS + MOE_ROW_STEP - 1) // MOE_ROW_STEP
    for steps in range(1, MOE_ROWS // MOE_ROW_STEP + 1):
        @pl.when(tail_steps == steps)
        def _(steps=steps):
            expert_rows(tail0, steps * MOE_ROW_STEP)

    @pl.when(j == last_j)
    def _():
        lane = lax.broadcasted_iota(jnp.int32, (tm, LANES), 1)
        mine = lane == e
        slot_col = jnp.sum(jnp.where(mine, slot_ref[0], 0.0), axis=1, keepdims=True)
        comb_col = jnp.sum(jnp.where(mine, comb_ref[0], 0.0), axis=1, keepdims=True)
        col = lax.broadcasted_iota(jnp.int32, (tm, ch), 1).astype(F32)

        def scatter(c, carry):
            r0 = pl.multiple_of(c * ch, ch)
            place = jnp.where(slot_col == col + (c * ch).astype(F32), 1.0, 0.0).astype(MXU_DTYPE)
            o_ref[0] += comb_col * _dot(place, ye_ref[pl.ds(r0, ch), :].astype(MXU_DTYPE))
            return carry

        lax.fori_loop(0, n_chunks, scatter, 0)

    @pl.when(jnp.logical_and(e == pl.num_programs(2) - 1, j == last_j))
    def _():
        o_ref[0] = _resid_ln(x_ref[0], g2_ref[0], o_ref[0], lg_ref[...], lb_ref[...])


def _moe(x, sc, sh, g2, router_w, router_b, w_gu, w_down, ln_g, ln_b, tm=1024, tf=1792):
    bsz, s, d = x.shape
    tm = min(tm, s)
    nt = s // tm
    nf = D_FF_EXPERT // tf
    rw = jnp.pad(router_w, ((0, 0), (0, LANES - N_EXPERTS)))
    rb = jnp.pad(router_b, (0, LANES - N_EXPERTS)).reshape(1, LANES)

    row2 = lambda b, i: (b, i, 0)
    vec2 = lambda b, i: (b, 0, 0)
    h, comb, slot, slot_t, cnt = pl.pallas_call(
        _router_kernel,
        grid=(bsz, nt),
        in_specs=[pl.BlockSpec((1, tm, d), row2),
                  pl.BlockSpec((1, 1, d), vec2), pl.BlockSpec((1, 1, d), vec2),
                  pl.BlockSpec((d, LANES), lambda b, i: (0, 0)),
                  pl.BlockSpec((1, LANES), lambda b, i: (0, 0))],
        out_specs=[pl.BlockSpec((1, tm, d), row2),
                   pl.BlockSpec((1, tm, LANES), row2),
                   pl.BlockSpec((1, tm, LANES), row2),
                   pl.BlockSpec((1, N_EXPERTS, tm), lambda b, i: (b * nt + i, 0, 0)),
                   pl.BlockSpec((1, 1, LANES), lambda b, i: (b * nt + i, 0, 0))],
        out_shape=[jax.ShapeDtypeStruct((bsz, s, d), MXU_DTYPE),
                   jax.ShapeDtypeStruct((bsz, s, LANES), F32),
                   jax.ShapeDtypeStruct((bsz, s, LANES), F32),
                   jax.ShapeDtypeStruct((bsz * nt, N_EXPERTS, tm), F32),
                   jax.ShapeDtypeStruct((bsz * nt, 1, LANES), jnp.int32)],
        compiler_params=_params(("parallel", "parallel")),
        name="moe_router",
    )(x, sc, sh, rw, rb)
    counts = cnt[:, 0, :N_EXPERTS].reshape(bsz * nt * N_EXPERTS)

    row = lambda b, i, e, j, cnt: (b, i, 0)
    vec = lambda b, i, e, j, cnt: (b, 0, 0)
    const2 = lambda b, i, e, j, cnt: (0, 0)
    once = pl.Buffered(1)
    grid_spec = pltpu.PrefetchScalarGridSpec(
        num_scalar_prefetch=1,
        grid=(bsz, nt, N_EXPERTS, nf),
        in_specs=[pl.BlockSpec((1, tm, d), row, pipeline_mode=once),
                  pl.BlockSpec((1, tm, d), row, pipeline_mode=once),
                  pl.BlockSpec((1, 1, d), vec),
                  pl.BlockSpec((1, tm, LANES), row, pipeline_mode=once),
                  pl.BlockSpec((1, tm, LANES), row, pipeline_mode=once),
                  pl.BlockSpec((1, N_EXPERTS, tm), lambda b, i, e, j, cnt: (b * nt + i, 0, 0)),
                  pl.BlockSpec((1, d, tf), lambda b, i, e, j, cnt: (e, 0, j)),
                  pl.BlockSpec((1, d, tf), lambda b, i, e, j, cnt: (e, 0, nf + j)),
                  pl.BlockSpec((1, tf, d), lambda b, i, e, j, cnt: (e, j, 0)),
                  pl.BlockSpec((1, d), const2), pl.BlockSpec((1, d), const2)],
        out_specs=pl.BlockSpec((1, tm, d), row),
        scratch_shapes=[pltpu.VMEM((tm, d), MXU_DTYPE), pltpu.VMEM((tm, d), F32)])
    return pl.pallas_call(
        functools.partial(_moe_kernel, n_tiles=nt),
        grid_spec=grid_spec,
        out_shape=jax.ShapeDtypeStruct((bsz, s, d), F32),
        compiler_params=pltpu.CompilerParams(
            dimension_semantics=("parallel", "parallel", "arbitrary", "arbitrary"),
            vmem_limit_bytes=MOE_VMEM_LIMIT),
        name="moe_experts",
    )(counts, h, x, g2, comb, slot, slot_t, w_gu, w_gu, w_down,
      ln_g.reshape(1, d), ln_b.reshape(1, d))


def _split_w_in(w):
    p = np.cumsum([0, 512, 512, 512, 256, 256, 512, 512, 512, SSD_XBC, SSD_HEADS, 512, 512, 512,
                   N_BRANCH * D_MODEL])
    mq, mk, mv, rq, rk, rv, rg, sz, sxbc, sdt, bq, bk, bv, gates = (
        w[:, p[i]:p[i + 1]] for i in range(14))
    w_a = jnp.concatenate([mq, mk, mv, rq, rk, rv, bq, bk, bv], axis=1).astype(MXU_DTYPE)
    sdt = jnp.pad(sdt, ((0, 0), (0, LANES - SSD_HEADS)))
    w_b = jnp.concatenate([gates, rg, sz, sxbc, sdt], axis=1).astype(MXU_DTYPE)
    return w_a, w_b


def kernel(x, c, w_ada, b_ada, w_in, conv_w, conv_b, dt_bias, a_log, d_skip, ssm_norm_g, ret_gn_g, ret_gn_b, w_br, w_out, ln1_g, ln1_b, ln2_g, ln2_b, ffn_w_gu, ffn_w_down, router_w, router_b, expert_w_gu, expert_w_down):
    bsz = x.shape[0]
    mod = _ada(c, w_ada, b_ada)
    for l in range(DEPTH):
        sh1, sc1, g1, sh2, sc2, g2 = (
            mod[l, :, i * D_MODEL:(i + 1) * D_MODEL].reshape(bsz, 1, D_MODEL) for i in range(6))
        w_a, w_b = _split_w_in(w_in[l])
        proj_a = _inproj(x, sc1, sh1, w_a, MXU_DTYPE, tn=2048)
        proj_b = _inproj(x, sc1, sh1, w_b, F32, tn=896)
        o_a = _moba(proj_a)
        o_b = _retention(proj_a, proj_b, ret_gn_g[l], ret_gn_b[l])
        o_c = _ssd(proj_b, conv_w[l], conv_b[l], dt_bias[l], a_log[l], d_skip[l], ssm_norm_g[l])
        o_d = _stick_breaking(proj_a)
        x = _merge(o_a, o_b, o_c, o_d, proj_b, x, g1, w_br[l].astype(MXU_DTYPE),
                   w_out[l].astype(MXU_DTYPE), ln1_g[l], ln1_b[l])
        if l % 2 == 0:
            x = _ffn(x, sc2, sh2, g2, ffn_w_gu[l // 2].astype(MXU_DTYPE),
                     ffn_w_down[l // 2].astype(MXU_DTYPE), ln2_g[l], ln2_b[l])
        else:
            x = _moe(x, sc2, sh2, g2, router_w[l // 2], router_b[l // 2],
                     expert_w_gu[l // 2].astype(MXU_DTYPE), expert_w_down[l // 2].astype(MXU_DTYPE),
                     ln2_g[l], ln2_b[l])
    return x
```

```python
import functools

import numpy as np
import jax
import jax.numpy as jnp
from jax import lax
from jax.experimental import pallas as pl
from jax.experimental.pallas import tpu as pltpu

F32 = jnp.float32
MXU_DTYPE = jnp.bfloat16

D_MODEL = 1024
DEPTH = 2
MOBA_HEADS = 8
MOBA_BLOCK = 256
MOBA_TOPK = 3
RET_HEADS = 4
RET_DK = 64
RET_DV = 128
RET_CHUNK = 256
SSD_D_INNER = 512
SSD_HEAD_DIM = 64
SSD_HEADS = 8
SSD_GROUPS = 2
SSD_STATE = 128
SSD_CONV = 4
SSD_CHUNK = 256
SSD_XBC = SSD_D_INNER + 2 * SSD_GROUPS * SSD_STATE
SB_HEADS = 8
SB_BLOCK = 256
HEAD_DIM = 64
ATT_HEADS_PER_STEP = 8
ATT_W = ATT_HEADS_PER_STEP * HEAD_DIM
BRANCH_W = 512
N_BRANCH = 4
D_FF = 2816
N_EXPERTS = 8
D_FF_EXPERT = 3584
DEEPNORM_ALPHA = (2.0 * DEPTH) ** 0.25
LN_EPS = 1e-5
NORM_EPS = 1e-6
NEG_INF = -1e30
LOG2E = 1.4426950408889634
LANES = 128
VMEM_LIMIT = 48 * 2 ** 20
MOE_VMEM_LIMIT = 56 * 2 ** 20
MOE_CHUNK = 256
MOE_ROWS = 256
MOE_ROW_STEP = 64

A_MOBA_Q, A_MOBA_K, A_MOBA_V = 0, 512, 1024
A_RET_Q, A_RET_K, A_RET_V = 1536, 1792, 2048
A_SB_Q, A_SB_K, A_SB_V = 2560, 3072, 3584
B_GATES, B_RET_G, B_SSD_Z, B_XBC = 4096, 8192, 8704, 9216
PROJ_WIDTH = 10240


def _params(sem):
    return pltpu.CompilerParams(dimension_semantics=sem, vmem_limit_bytes=VMEM_LIMIT)


def _dot(a, b):
    return jnp.dot(a, b, preferred_element_type=F32)


def _dot_nt(a, b):
    return lax.dot_general(a, b, (((1,), (1,)), ((), ())), preferred_element_type=F32)


def _dot_tn(a, b):
    return lax.dot_general(a, b, (((0,), (0,)), ((), ())), preferred_element_type=F32)


def _split(x):
    hi = x.astype(MXU_DTYPE)
    lo = (x - hi.astype(F32)).astype(MXU_DTYPE)
    return hi, lo


def _dot3(a, b):
    ah, al = _split(a)
    bh, bl = _split(b)
    return _dot(ah, bh) + _dot(al, bh) + _dot(ah, bl)


def _sigmoid(x):
    return 1.0 / (1.0 + jnp.exp(-x))


def _silu(x):
    return x * _sigmoid(x)


def _resid_ln(x, gate, f, ln_g, ln_b):
    y = DEEPNORM_ALPHA * x + gate * f
    mu = jnp.mean(y, axis=-1, keepdims=True)
    d = y - mu
    var = jnp.mean(d * d, axis=-1, keepdims=True)
    return d * lax.rsqrt(var + LN_EPS) * ln_g + ln_b


def _ada_kernel(c_ref, w_ref, b_ref, o_ref):
    c = c_ref[...]
    o_ref[0] = _dot3(_silu(c), w_ref[0]) + b_ref[0]


def _ada(c, w_ada, b_ada):
    depth, d, n = w_ada.shape
    bsz = c.shape[0]
    tn = 1536
    return pl.pallas_call(
        _ada_kernel,
        grid=(depth, n // tn),
        in_specs=[pl.BlockSpec((bsz, d), lambda l, j: (0, 0)),
                  pl.BlockSpec((1, d, tn), lambda l, j: (l, 0, j)),
                  pl.BlockSpec((1, 1, tn), lambda l, j: (l, 0, j))],
        out_specs=pl.BlockSpec((1, bsz, tn), lambda l, j: (l, 0, j)),
        out_shape=jax.ShapeDtypeStruct((depth, bsz, n), F32),
        compiler_params=_params(("arbitrary", "arbitrary")),
        name="ada_mod",
    )(c, w_ada, b_ada.reshape(depth, 1, n))


def _inproj_kernel(x_ref, sc_ref, sh_ref, w_ref, o_ref, h_ref):
    @pl.when(pl.program_id(2) == 0)
    def _():
        h = x_ref[0] * (1.0 + sc_ref[0]) + sh_ref[0]
        h_ref[...] = h.astype(MXU_DTYPE)

    o_ref[0] = _dot(h_ref[...], w_ref[...]).astype(o_ref.dtype)


def _inproj(x, sc, sh, w, out_dtype, tn, tm=1024):
    bsz, s, d = x.shape
    n = w.shape[1]
    tm = min(tm, s)
    return pl.pallas_call(
        _inproj_kernel,
        grid=(bsz, s // tm, n // tn),
        in_specs=[pl.BlockSpec((1, tm, d), lambda b, i, j: (b, i, 0)),
                  pl.BlockSpec((1, 1, d), lambda b, i, j: (b, 0, 0)),
                  pl.BlockSpec((1, 1, d), lambda b, i, j: (b, 0, 0)),
                  pl.BlockSpec((d, tn), lambda b, i, j: (0, j))],
        out_specs=pl.BlockSpec((1, tm, tn), lambda b, i, j: (b, i, j)),
        out_shape=jax.ShapeDtypeStruct((bsz, s, n), out_dtype),
        scratch_shapes=[pltpu.VMEM((tm, d), MXU_DTYPE)],
        compiler_params=_params(("parallel", "parallel", "arbitrary")),
        name="in_proj",
    )(x, sc, sh, w)


def _moba_kernel(q_ref, k_ref, v_ref, sl_ref, o_ref, km_ref, ka_ref, vt_ref, *, nb):
    blk = MOBA_BLOCK
    nh = ATT_HEADS_PER_STEP
    heads = range(nh)
    qi = pl.program_id(2)
    s_len = nb * blk
    nb_pad = -(-nb // 8) * 8
    extra = 2 * HEAD_DIM - HEAD_DIM
    assert nb_pad + 8 <= extra
    cols = [slice(h * HEAD_DIM, (h + 1) * HEAD_DIM) for h in heads]

    @pl.when(qi == 0)
    def _():
        k_all = k_ref[0]
        kk = k_all.astype(F32).reshape(nb, blk, ATT_W)
        km_ref[...] = jnp.zeros_like(km_ref)
        km_ref[0:nb, :] = jnp.sum(kk, axis=1) * (1.0 / blk)
        pos = lax.broadcasted_iota(jnp.int32, (s_len, extra), 0)
        lane = lax.broadcasted_iota(jnp.int32, (s_len, extra), 1)
        one_hot = jnp.where(lane == pos // blk, 1.0, 0.0)
        pos_hi = jnp.where(lane == nb_pad, (pos // blk * blk).astype(F32), 0.0)
        pos_lo = jnp.where(lane == nb_pad + 1, (pos % blk).astype(F32), 0.0)
        tail = (one_hot + pos_hi + pos_lo).astype(MXU_DTYPE)
        for h in heads:
            ka_ref[h] = jnp.concatenate([k_all[:, cols[h]], tail], axis=1)

        ones_rows = jnp.where(lax.broadcasted_iota(jnp.int32, (extra, blk), 0) == 0, 1.0, 0.0)

        def transpose_block(j, carry):
            j0 = pl.multiple_of(j * blk, blk)
            vt = v_ref[0, pl.ds(j0, blk), :].astype(F32).T
            for h in heads:
                vt_ref[j, h] = jnp.concatenate([vt[cols[h], :], ones_rows], axis=0).astype(MXU_DTYPE)
            return carry

        lax.fori_loop(0, nb, transpose_block, 0)

    key_i = lax.broadcasted_iota(jnp.int32, (blk, blk), 0)
    qry_i = lax.broadcasted_iota(jnp.int32, (blk, blk), 1)
    n_iota = lax.broadcasted_iota(jnp.int32, (nb_pad, blk), 0)
    past = n_iota < qi
    k_sel = min(MOBA_TOPK, max(nb - 1, 1))
    q_t = q_ref[0].astype(F32).T
    slope_rows = lax.broadcasted_iota(jnp.int32, (8, blk), 0) < 2

    qa = []
    for h in heads:
        qh_t = q_t[cols[h], :]
        slope = sl_ref[0, :, h * HEAD_DIM:h * HEAD_DIM + 1]
        km_hi, km_lo = _split(km_ref[:, cols[h]])
        qh_mxu = qh_t.astype(MXU_DTYPE)
        gate = (_dot(km_hi, qh_mxu) + _dot(km_lo, qh_mxu))[0:nb_pad, :]
        g = jnp.where(past, gate, NEG_INF)
        g = jnp.where(n_iota < nb, g, -jnp.inf)
        sel = jnp.zeros((nb_pad, blk), F32)
        for _ in range(k_sel):
            mx = jnp.max(g, axis=0, keepdims=True)
            idx = jnp.min(jnp.where(g == mx, n_iota, nb_pad), axis=0, keepdims=True)
            pick = n_iota == idx
            sel = jnp.where(pick, 1.0, sel)
            g = jnp.where(pick, -jnp.inf, g)
        keep = jnp.logical_or(jnp.logical_and(past, sel > 0.5), n_iota == qi)
        bias = jnp.where(keep, 0.0, NEG_INF)
        alibi = jnp.where(slope_rows, slope, 0.0)
        pad = jnp.zeros((extra - nb_pad - 8, blk), F32)
        qa.append(jnp.concatenate([qh_t * HEAD_DIM ** -0.5, bias, alibi, pad],
                                  axis=0).astype(MXU_DTYPE))

    def tiles(js, carry, diag):
        starts = [pl.multiple_of(j * blk, blk) for j in js]
        s = [[_dot(ka_ref[h, pl.ds(j0, blk), :], qa[h]) for h in heads] for j0 in starts]
        m = list(carry[:nh]) if carry is not None else [None] * nh
        acc = list(carry[nh:]) if carry is not None else [None] * nh
        p = {}
        for h in heads:
            sh = [jnp.where(key_i <= qry_i, s[t][h], NEG_INF) if diag else s[t][h]
                  for t in range(len(js))]
            top = jnp.max(sh[0], axis=0, keepdims=True)
            for t in range(1, len(js)):
                top = jnp.maximum(top, jnp.max(sh[t], axis=0, keepdims=True))
            if m[h] is None:
                m[h] = top
            else:
                m_new = jnp.maximum(m[h], top)
                acc[h] = jnp.exp(m[h] - m_new) * acc[h]
                m[h] = m_new
            for t in range(len(js)):
                p[t, h] = jnp.exp(sh[t] - m[h]).astype(MXU_DTYPE)
        for t, j in enumerate(js):
            for h in heads:
                pv = _dot(vt_ref[j, h], p[t, h])
                acc[h] = pv if acc[h] is None else acc[h] + pv
        return tuple(m) + tuple(acc)

    carry0 = tiles([qi], None, True)

    def pair(t, carry):
        return tiles([2 * t, 2 * t + 1], carry, False)

    def single(t, carry):
        return tiles([qi - 1], carry, False)

    res = lax.fori_loop(0, qi // 2, pair, carry0)
    res = lax.fori_loop(0, qi % 2, single, res)
    out_t = [res[nh + h][0:HEAD_DIM, :] / res[nh + h][HEAD_DIM:HEAD_DIM + 1, :] for h in heads]
    o_ref[0] = jnp.concatenate(out_t, axis=0).T.astype(o_ref.dtype)


def _moba(proj_a):
    bsz, s, _ = proj_a.shape
    blk = MOBA_BLOCK
    nb = s // blk
    n_grp = MOBA_HEADS // ATT_HEADS_PER_STEP
    slopes = 2.0 ** (-8.0 * np.arange(1, MOBA_HEADS + 1) / MOBA_HEADS)
    sl = jnp.asarray(np.repeat(slopes, HEAD_DIM).reshape(n_grp, 1, ATT_W), F32)
    qb, kb, vb = A_MOBA_Q // ATT_W, A_MOBA_K // ATT_W, A_MOBA_V // ATT_W
    return pl.pallas_call(
        functools.partial(_moba_kernel, nb=nb),
        grid=(bsz, n_grp, nb),
        in_specs=[pl.BlockSpec((1, blk, ATT_W), lambda b, hg, i: (b, i, qb + hg)),
                  pl.BlockSpec((1, s, ATT_W), lambda b, hg, i: (b, 0, kb + hg)),
                  pl.BlockSpec((1, s, ATT_W), lambda b, hg, i: (b, 0, vb + hg)),
                  pl.BlockSpec((1, 1, ATT_W), lambda b, hg, i: (hg, 0, 0))],
        out_specs=pl.BlockSpec((1, blk, ATT_W), lambda b, hg, i: (b, i, hg)),
        out_shape=jax.ShapeDtypeStruct((bsz, s, MOBA_HEADS * HEAD_DIM), MXU_DTYPE),
        scratch_shapes=[pltpu.VMEM((LANES, ATT_W), F32),
                        pltpu.VMEM((ATT_HEADS_PER_STEP, s, 2 * HEAD_DIM), MXU_DTYPE),
                        pltpu.VMEM((nb, ATT_HEADS_PER_STEP, 2 * HEAD_DIM, blk), MXU_DTYPE)],
        compiler_params=_params(("parallel", "parallel", "arbitrary")),
        name="moba",
    )(proj_a, proj_a, proj_a, sl)


def _sb_kernel(q_ref, k_ref, v_ref, o_ref, vt_ref, *, nb):
    blk = SB_BLOCK
    nh = ATT_HEADS_PER_STEP
    heads = range(nh)
    qi = pl.program_id(2)
    cols = [slice(h * HEAD_DIM, (h + 1) * HEAD_DIM) for h in heads]

    @pl.when(qi == 0)
    def _():
        def transpose_block(j, carry):
            j0 = pl.multiple_of(j * blk, blk)
            vt = v_ref[0, pl.ds(j0, blk), :].astype(F32).T
            for h in heads:
                vt_ref[j, h] = vt[cols[h], :].astype(MXU_DTYPE)
            return carry

        lax.fori_loop(0, nb, transpose_block, 0)

    key_i = lax.broadcasted_iota(jnp.int32, (blk, blk), 0)
    qry_i = lax.broadcasted_iota(jnp.int32, (blk, blk), 1)
    causal = key_i < qry_i
    later = jnp.where(qry_i > key_i, 1.0, 0.0).astype(MXU_DTYPE)
    qs = [(q_ref[0, :, cols[h]].astype(F32) * HEAD_DIM ** -0.5).astype(MXU_DTYPE)
          for h in heads]

    def tiles(js, accs, rests, diag):
        starts = [pl.multiple_of(j * blk, blk) for j in js]
        z = [[_dot_nt(k_ref[0, pl.ds(j0, blk), cols[h]], qs[h]) for h in heads] for j0 in starts]
        log_beta, log_1mb, parts = [], [], []
        for t in range(len(js)):
            for h in heads:
                zz = z[t][h]
                lb = jnp.minimum(zz, 0.0) - jnp.log(1.0 + jnp.exp2(jnp.abs(zz) * (-LOG2E)))
                l1 = lb - zz
                if diag:
                    l1 = jnp.where(causal, l1, 0.0)
                log_beta.append(lb)
                log_1mb.append(l1)
                parts.append(l1.astype(MXU_DTYPE))
        rem = [_dot(later, p) for p in parts]
        w = []
        rests = list(rests)
        for t in range(len(js)):
            for h in heads:
                i = t * nh + h
                wh = jnp.exp(log_beta[i] + rem[i] + rests[h])
                if diag:
                    wh = jnp.where(causal, wh, 0.0)
                w.append(wh.astype(MXU_DTYPE))
                rests[h] = rests[h] + rem[i][0:1, :] + log_1mb[i][0:1, :]
        accs = list(accs)
        for t, j in enumerate(js):
            for h in heads:
                accs[h] = accs[h] + _dot(vt_ref[j, h], w[t * nh + h])
        return accs, rests

    zero_acc = [jnp.zeros((HEAD_DIM, blk), F32)] * nh
    accs, rests = tiles([qi], zero_acc, [jnp.zeros((1, blk), F32)] * nh, True)

    def pair(t, carry):
        j = qi - 1 - 2 * t
        accs, rests = tiles([j, j - 1], carry[:nh], carry[nh:], False)
        return tuple(accs + rests)

    def single(t, carry):
        accs, rests = tiles([0], carry[:nh], carry[nh:], False)
        return tuple(accs + rests)

    res = lax.fori_loop(0, qi // 2, pair, tuple(accs + rests))
    res = lax.fori_loop(0, qi % 2, single, res)
    o_ref[0] = jnp.concatenate(res[:nh], axis=0).T.astype(o_ref.dtype)


def _stick_breaking(proj_a):
    bsz, s, _ = proj_a.shape
    blk = SB_BLOCK
    n_grp = SB_HEADS // ATT_HEADS_PER_STEP
    qb, kb, vb = A_SB_Q // ATT_W, A_SB_K // ATT_W, A_SB_V // ATT_W
    return pl.pallas_call(
        functools.partial(_sb_kernel, nb=s // blk),
        grid=(bsz, n_grp, s // blk),
        in_specs=[pl.BlockSpec((1, blk, ATT_W), lambda b, hg, i: (b, i, qb + hg)),
                  pl.BlockSpec((1, s, ATT_W), lambda b, hg, i: (b, 0, kb + hg)),
                  pl.BlockSpec((1, s, ATT_W), lambda b, hg, i: (b, 0, vb + hg))],
        out_specs=pl.BlockSpec((1, blk, ATT_W), lambda b, hg, i: (b, i, hg)),
        out_shape=jax.ShapeDtypeStruct((bsz, s, SB_HEADS * HEAD_DIM), MXU_DTYPE),
        scratch_shapes=[pltpu.VMEM((s // blk, ATT_HEADS_PER_STEP, HEAD_DIM, blk), MXU_DTYPE)],
        compiler_params=_params(("parallel", "parallel", "arbitrary")),
        name="stick_breaking",
    )(proj_a, proj_a, proj_a)


def _ret_kernel(q_ref, k_ref, v_ref, g_ref, gn_g_ref, gn_b_ref, o_ref, st_ref):
    ch = RET_CHUNK

    @pl.when(pl.program_id(1) == 0)
    def _():
        st_ref[...] = jnp.zeros_like(st_ref)

    i_col = lax.broadcasted_iota(jnp.int32, (ch, 1), 0).astype(F32)
    diff = (lax.broadcasted_iota(jnp.int32, (ch, ch), 0)
            - lax.broadcasted_iota(jnp.int32, (ch, ch), 1)).astype(F32)
    heads = range(RET_HEADS)
    log_g = [float(np.log(1.0 - 2.0 ** (-5.0 - h))) for h in heads]
    qs = [(q_ref[0, :, h * RET_DK:(h + 1) * RET_DK].astype(F32) * RET_DK ** -0.5).astype(MXU_DTYPE)
          for h in heads]
    k = [k_ref[0, :, h * RET_DK:(h + 1) * RET_DK] for h in heads]
    v = [v_ref[0, :, h * RET_DV:(h + 1) * RET_DV] for h in heads]
    prev = [st_ref[h] for h in heads]
    qk = [_dot_nt(qs[h], k[h]) for h in heads]
    o_cross = [_dot(qs[h], prev[h].astype(MXU_DTYPE)) for h in heads]
    kd = [(k[h].astype(F32) * jnp.exp((ch - 1.0 - i_col) * log_g[h])).astype(MXU_DTYPE)
          for h in heads]
    st_new = [_dot_tn(kd[h], v[h]) for h in heads]
    qkd = []
    for h in heads:
        decay = jnp.where(diff >= 0, jnp.exp(jnp.maximum(diff, 0.0) * log_g[h]), 0.0)
        qkd.append((qk[h] * decay).astype(MXU_DTYPE))
    o_intra = [_dot(qkd[h], v[h]) for h in heads]
    for h in heads:
        st_ref[h] = float(np.exp(ch * log_g[h])) * prev[h] + st_new[h]
        o = o_intra[h] + o_cross[h] * jnp.exp((i_col + 1.0) * log_g[h])
        mu = jnp.mean(o, axis=-1, keepdims=True)
        d = o - mu
        var = jnp.mean(d * d, axis=-1, keepdims=True)
        cs = slice(h * RET_DV, (h + 1) * RET_DV)
        on = d * lax.rsqrt(var + NORM_EPS) * gn_g_ref[:, cs] + gn_b_ref[:, cs]
        o_ref[0, :, cs] = (_silu(g_ref[0, :, cs].astype(F32)) * on).astype(o_ref.dtype)


def _retention(proj, gn_g, gn_b):
    bsz, s, _ = proj.shape
    ch = RET_CHUNK
    wq, wv = RET_HEADS * RET_DK, RET_HEADS * RET_DV
    return pl.pallas_call(
        _ret_kernel,
        grid=(bsz, s // ch),
        in_specs=[pl.BlockSpec((1, ch, wq), lambda b, i: (b, i, A_RET_Q // wq)),
                  pl.BlockSpec((1, ch, wq), lambda b, i: (b, i, A_RET_K // wq)),
                  pl.BlockSpec((1, ch, wv), lambda b, i: (b, i, A_RET_V // wv)),
                  pl.BlockSpec((1, ch, wv), lambda b, i: (b, i, B_RET_G // wv)),
                  pl.BlockSpec((1, wv), lambda b, i: (0, 0)),
                  pl.BlockSpec((1, wv), lambda b, i: (0, 0))],
        out_specs=pl.BlockSpec((1, ch, wv), lambda b, i: (b, i, 0)),
        out_shape=jax.ShapeDtypeStruct((bsz, s, wv), MXU_DTYPE),
        scratch_shapes=[pltpu.VMEM((RET_HEADS, RET_DK, RET_DV), F32)],
        compiler_params=_params(("parallel", "arbitrary")),
        name="retention",
    )(proj, proj, proj, proj, gn_g.reshape(1, wv), gn_b.reshape(1, wv))


def _ssd_kernel(z_ref, xbc_ref, dt_ref, cw_ref, cb_ref, dtb_ref, alog_ref, dsk_ref, ng_ref,
                o_ref, tail_ref, st_ref, y_ref):
    ch = SSD_CHUNK
    heads_per_group = SSD_HEADS // SSD_GROUPS
    halo = 8

    @pl.when(pl.program_id(1) == 0)
    def _():
        tail_ref[...] = jnp.zeros_like(tail_ref)
        st_ref[...] = jnp.zeros_like(st_ref)

    xbc = xbc_ref[0].astype(F32)
    xpad = jnp.concatenate([tail_ref[...], xbc], axis=0)
    tail_ref[...] = xbc[ch - halo:, :]
    conv = cb_ref[...]
    for kk in range(SSD_CONV):
        off = halo - (SSD_CONV - 1) + kk
        conv = conv + cw_ref[kk:kk + 1, :] * xpad[off:off + ch, :]
    xc = _silu(conv)
    xs = xc[:, :SSD_D_INNER]
    bm = xc[:, SSD_D_INNER:SSD_D_INNER + SSD_GROUPS * SSD_STATE]
    cm = xc[:, SSD_D_INNER + SSD_GROUPS * SSD_STATE:]

    dtr = dt_ref[0] + dtb_ref[...]
    dt = jnp.maximum(dtr, 0.0) + jnp.log1p(jnp.exp(-jnp.abs(dtr)))
    a = -jnp.exp(alog_ref[...])
    r_iota = lax.broadcasted_iota(jnp.int32, (ch, ch), 0)
    c_iota = lax.broadcasted_iota(jnp.int32, (ch, ch), 1)
    causal = r_iota >= c_iota
    tri = jnp.where(causal, 1.0, 0.0).astype(MXU_DTYPE)
    da = dt * a
    da_hi, da_lo = _split(da)
    cum = _dot(tri, da_hi) + _dot(tri, da_lo)
    cum_t = cum.T
    dt_t = dt.T
    cum_last = cum[ch - 1:ch, :]
    decay_st = jnp.exp(cum_last - cum) * dt
    e_cum = jnp.exp(cum)
    e_last = jnp.exp(cum_last)

    for g in range(SSD_GROUPS):
        bm_g = bm[:, g * SSD_STATE:(g + 1) * SSD_STATE].astype(MXU_DTYPE)
        cm_g = cm[:, g * SSD_STATE:(g + 1) * SSD_STATE].astype(MXU_DTYPE)
        cb = _dot_nt(cm_g, bm_g)
        for r in range(heads_per_group):
            hd = g * heads_per_group + r
            cs = slice(hd * SSD_HEAD_DIM, (hd + 1) * SSD_HEAD_DIM)
            x_h = xs[:, cs]
            seg = cum[:, hd:hd + 1] - cum_t[hd:hd + 1, :]
            lmat = jnp.exp(jnp.where(causal, seg, NEG_INF))
            w = cb * lmat * dt_t[hd:hd + 1, :]
            y = _dot(w.astype(MXU_DTYPE), x_h.astype(MXU_DTYPE))
            prev = st_ref[hd]
            y = y + _dot(cm_g, prev.astype(MXU_DTYPE)) * e_cum[:, hd:hd + 1]
            xd = (x_h * decay_st[:, hd:hd + 1]).astype(MXU_DTYPE)
            st_ref[hd] = e_last[:, hd:hd + 1] * prev + _dot_tn(bm_g, xd)
            y_ref[:, cs] = y + dsk_ref[:, cs] * x_h

    y = y_ref[...] * _silu(z_ref[0].astype(F32))
    gw = SSD_D_INNER // SSD_GROUPS
    for g in range(SSD_GROUPS):
        cs = slice(g * gw, (g + 1) * gw)
        yg = y[:, cs]
        ms = jnp.mean(yg * yg, axis=-1, keepdims=True)
        o_ref[0, :, cs] = (yg * lax.rsqrt(ms + LN_EPS) * ng_ref[:, cs]).astype(o_ref.dtype)


def _ssd(proj, proj_dt, conv_w, conv_b, dt_bias, a_log, d_skip, norm_g):
    bsz, s, _ = proj.shape
    ch = SSD_CHUNK
    pad = LANES - SSD_HEADS
    dtb = jnp.pad(dt_bias, (0, pad)).reshape(1, LANES)
    alog = jnp.pad(a_log, (0, pad)).reshape(1, LANES)
    dsk = jnp.repeat(d_skip, SSD_HEAD_DIM).reshape(1, SSD_D_INNER)
    const = lambda b, i: (0, 0)
    return pl.pallas_call(
        _ssd_kernel,
        grid=(bsz, s // ch),
        in_specs=[pl.BlockSpec((1, ch, SSD_D_INNER), lambda b, i: (b, i, B_SSD_Z // SSD_D_INNER)),
                  pl.BlockSpec((1, ch, SSD_XBC), lambda b, i: (b, i, B_XBC // SSD_XBC)),
                  pl.BlockSpec((1, ch, LANES), lambda b, i: (b, i, 0)),
                  pl.BlockSpec((SSD_CONV, SSD_XBC), const),
                  pl.BlockSpec((1, SSD_XBC), const),
                  pl.BlockSpec((1, LANES), const),
                  pl.BlockSpec((1, LANES), const),
                  pl.BlockSpec((1, SSD_D_INNER), const),
                  pl.BlockSpec((1, SSD_D_INNER), const)],
        out_specs=pl.BlockSpec((1, ch, SSD_D_INNER), lambda b, i: (b, i, 0)),
        out_shape=jax.ShapeDtypeStruct((bsz, s, SSD_D_INNER), MXU_DTYPE),
        scratch_shapes=[pltpu.VMEM((8, SSD_XBC), F32),
                        pltpu.VMEM((SSD_HEADS, SSD_STATE, SSD_HEAD_DIM), F32),
                        pltpu.VMEM((ch, SSD_D_INNER), F32)],
        compiler_params=_params(("parallel", "arbitrary")),
        name="ssd",
    )(proj, proj, proj_dt, conv_w, conv_b.reshape(1, SSD_XBC), dtb, alog, dsk,
      norm_g.reshape(1, SSD_D_INNER))


def _merge_kernel(oa_ref, ob_ref, oc_ref, od_ref, gates_ref, x_ref, g1_ref, wbr_ref, wout_ref,
                  lg_ref, lb_ref, o_ref):
    merged = None
    for n, br in enumerate((oa_ref, ob_ref, oc_ref, od_ref)):
        y = _dot(br[0], wbr_ref[n])
        t = _sigmoid(gates_ref[0, :, n * D_MODEL:(n + 1) * D_MODEL].astype(F32)) * y
        merged = t if merged is None else merged + t
    mix = _dot(merged.astype(MXU_DTYPE), wout_ref[...])
    o_ref[0] = _resid_ln(x_ref[0], g1_ref[0], mix, lg_ref[...], lb_ref[...])


def _merge(o_a, o_b, o_c, o_d, proj, x, g1, w_br, w_out, ln_g, ln_b, tm=512):
    bsz, s, d = x.shape
    tm = min(tm, s)
    gw = N_BRANCH * d
    row = lambda b, i: (b, i, 0)
    const2 = lambda b, i: (0, 0)
    br_spec = pl.BlockSpec((1, tm, BRANCH_W), row)
    return pl.pallas_call(
        _merge_kernel,
        grid=(bsz, s // tm),
        in_specs=[br_spec, br_spec, br_spec, br_spec,
                  pl.BlockSpec((1, tm, gw), lambda b, i: (b, i, B_GATES // gw)),
                  pl.BlockSpec((1, tm, d), row),
                  pl.BlockSpec((1, 1, d), lambda b, i: (b, 0, 0)),
                  pl.BlockSpec((N_BRANCH, BRANCH_W, d), lambda b, i: (0, 0, 0)),
                  pl.BlockSpec((d, d), const2),
                  pl.BlockSpec((1, d), const2),
                  pl.BlockSpec((1, d), const2)],
        out_specs=pl.BlockSpec((1, tm, d), row),
        out_shape=jax.ShapeDtypeStruct((bsz, s, d), F32),
        compiler_params=_params(("parallel", "parallel")),
        name="merge",
    )(o_a, o_b, o_c, o_d, proj, x, g1, w_br, w_out, ln_g.reshape(1, d), ln_b.reshape(1, d))


def _ffn_kernel(x_ref, sc_ref, sh_ref, g2_ref, wg_ref, wu_ref, wd_ref, lg_ref, lb_ref,
                o_ref, h_ref, acc_ref):
    j = pl.program_id(2)

    @pl.when(j == 0)
    def _():
        h = x_ref[0] * (1.0 + sc_ref[0]) + sh_ref[0]
        h_ref[...] = h.astype(MXU_DTYPE)
        acc_ref[...] = jnp.zeros_like(acc_ref)

    h = h_ref[...]
    act = _silu(_dot(h, wg_ref[...])) * _dot(h, wu_ref[...])
    acc_ref[...] += _dot(act.astype(MXU_DTYPE), wd_ref[...])

    @pl.when(j == pl.num_programs(2) - 1)
    def _():
        o_ref[0] = _resid_ln(x_ref[0], g2_ref[0], acc_ref[...], lg_ref[...], lb_ref[...])


def _ffn(x, sc, sh, g2, w_gu, w_down, ln_g, ln_b, tm=512, tf=1408):
    bsz, s, d = x.shape
    tm = min(tm, s)
    nf = D_FF // tf
    row = lambda b, i, j: (b, i, 0)
    vec = lambda b, i, j: (b, 0, 0)
    const2 = lambda b, i, j: (0, 0)
    return pl.pallas_call(
        _ffn_kernel,
        grid=(bsz, s // tm, nf),
        in_specs=[pl.BlockSpec((1, tm, d), row),
                  pl.BlockSpec((1, 1, d), vec), pl.BlockSpec((1, 1, d), vec),
                  pl.BlockSpec((1, 1, d), vec),
                  pl.BlockSpec((d, tf), lambda b, i, j: (0, j)),
                  pl.BlockSpec((d, tf), lambda b, i, j: (0, nf + j)),
                  pl.BlockSpec((tf, d), lambda b, i, j: (j, 0)),
                  pl.BlockSpec((1, d), const2), pl.BlockSpec((1, d), const2)],
        out_specs=pl.BlockSpec((1, tm, d), row),
        out_shape=jax.ShapeDtypeStruct((bsz, s, d), F32),
        scratch_shapes=[pltpu.VMEM((tm, d), MXU_DTYPE), pltpu.VMEM((tm, d), F32)],
        compiler_params=_params(("parallel", "parallel", "arbitrary")),
        name="ffn_dense",
    )(x, sc, sh, g2, w_gu, w_gu, w_down, ln_g.reshape(1, d), ln_b.reshape(1, d))


def _router_kernel(x_ref, sc_ref, sh_ref, rw_ref, rb_ref,
                   h_ref, comb_ref, slot_ref, slot_t_ref, cnt_ref):
    tm = x_ref.shape[1]
    lane = lax.broadcasted_iota(jnp.int32, (tm, LANES), 1)
    h = x_ref[0] * (1.0 + sc_ref[0]) + sh_ref[0]
    h_ref[0] = h.astype(MXU_DTYPE)
    logits = _dot3(h, rw_ref[...]) + rb_ref[...]
    logits = jnp.where(lane < N_EXPERTS, logits, -jnp.inf)
    v1 = jnp.max(logits, axis=1, keepdims=True)
    i1 = jnp.min(jnp.where(logits == v1, lane, LANES), axis=1, keepdims=True)
    rest = jnp.where(lane == i1, -jnp.inf, logits)
    v2 = jnp.max(rest, axis=1, keepdims=True)
    i2 = jnp.min(jnp.where(rest == v2, lane, LANES), axis=1, keepdims=True)
    e2 = jnp.exp(v2 - v1)
    w1 = 1.0 / (1.0 + e2)
    w2 = e2 / (1.0 + e2)
    comb_ref[0] = jnp.where(lane == i1, w1, 0.0) + jnp.where(lane == i2, w2, 0.0)
    routed = jnp.logical_or(lane == i1, lane == i2)
    earlier = (lax.broadcasted_iota(jnp.int32, (tm, tm), 1)
               < lax.broadcasted_iota(jnp.int32, (tm, tm), 0))
    before = _dot(jnp.where(earlier, 1.0, 0.0).astype(MXU_DTYPE),
                  jnp.where(routed, 1.0, 0.0).astype(MXU_DTYPE))
    slot = jnp.where(routed, before, -1.0)
    slot_ref[0] = slot
    slot_t_ref[0] = slot.T[0:N_EXPERTS, :]
    cnt_ref[0] = jnp.sum(jnp.where(routed, 1, 0), axis=0, keepdims=True)


def _moe_kernel(cnt_ref, h_ref, x_ref, g2_ref, comb_ref, slot_ref, slot_t_ref,
                wg_ref, wu_ref, wd_ref, lg_ref, lb_ref, o_ref, xe_ref, ye_ref, *, n_tiles):
    tm = x_ref.shape[1]
    ch = MOE_CHUNK
    e = pl.program_id(2)
    j = pl.program_id(3)
    last_j = pl.num_programs(3) - 1
    count = cnt_ref[(pl.program_id(0) * n_tiles + pl.program_id(1)) * N_EXPERTS + e]
    n_chunks = (count + ch - 1) // ch

    @pl.when(jnp.logical_and(e == 0, j == 0))
    def _():
        o_ref[0] = jnp.zeros((tm, D_MODEL), F32)

    @pl.when(j == 0)
    def _():
        slot_row = slot_t_ref[0, pl.ds(e, 1), :]
        row = lax.broadcasted_iota(jnp.int32, (ch, tm), 0).astype(F32)

        def gather(c, carry):
            r0 = pl.multiple_of(c * ch, ch)
            pick = jnp.where(slot_row == row + (c * ch).astype(F32), 1.0, 0.0).astype(MXU_DTYPE)
            xe_ref[pl.ds(r0, ch), :] = _dot(pick, h_ref[0]).astype(MXU_DTYPE)
            ye_ref[pl.ds(r0, ch), :] = jnp.zeros((ch, D_MODEL), F32)
            return carry

        lax.fori_loop(0, n_chunks, gather, 0)

    def expert_rows(r0, rows):
        xc = xe_ref[pl.ds(r0, rows), :]
        act = _silu(_dot(xc, wg_ref[0])) * _dot(xc, wu_ref[0])
        ye_ref[pl.ds(r0, rows), :] += _dot(act.astype(MXU_DTYPE), wd_ref[0])

    n_full = count // MOE_ROWS

    def full_piece(c, carry):
        expert_rows(pl.multiple_of(c * MOE_ROWS, MOE_ROWS), MOE_ROWS)
        return carry

    lax.fori_loop(0, n_full, full_piece, 0)
    tail0 = pl.multiple_of(n_full * MOE_ROWS, MOE_ROWS)
    tail_steps = (count - n_full * MOE_ROWS + MOE_ROW_STEP - 1) // MOE_ROW_STEP
    for steps in range(1, MOE_ROWS // MOE_ROW_STEP + 1):
        @pl.when(tail_steps == steps)
        def _(steps=steps):
            expert_rows(tail0, steps * MOE_ROW_STEP)

    @pl.when(j == last_j)
    def _():
        lane = lax.broadcasted_iota(jnp.int32, (tm, LANES), 1)
        mine = lane == e
        slot_col = jnp.sum(jnp.where(mine, slot_ref[0], 0.0), axis=1, keepdims=True)
        comb_col = jnp.sum(jnp.where(mine, comb_ref[0], 0.0), axis=1, keepdims=True)
        col = lax.broadcasted_iota(jnp.int32, (tm, ch), 1).astype(F32)

        def scatter(c, carry):
            r0 = pl.multiple_of(c * ch, ch)
            place = jnp.where(slot_col == col + (c * ch).astype(F32), 1.0, 0.0).astype(MXU_DTYPE)
            o_ref[0] += comb_col * _dot(place, ye_ref[pl.ds(r0, ch), :].astype(MXU_DTYPE))
            return carry

        lax.fori_loop(0, n_chunks, scatter, 0)

    @pl.when(jnp.logical_and(e == pl.num_programs(2) - 1, j == last_j))
    def _():
        o_ref[0] = _resid_ln(x_ref[0], g2_ref[0], o_ref[0], lg_ref[...], lb_ref[...])


def _moe(x, sc, sh, g2, router_w, router_b, w_gu, w_down, ln_g, ln_b, tm=1024, tf=1792):
    bsz, s, d = x.shape
    tm = min(tm, s)
    nt = s // tm
    nf = D_FF_EXPERT // tf
    rw = jnp.pad(router_w, ((0, 0), (0, LANES - N_EXPERTS)))
    rb = jnp.pad(router_b, (0, LANES - N_EXPERTS)).reshape(1, LANES)

    row2 = lambda b, i: (b, i, 0)
    vec2 = lambda b, i: (b, 0, 0)
    h, comb, slot, slot_t, cnt = pl.pallas_call(
        _router_kernel,
        grid=(bsz, nt),
        in_specs=[pl.BlockSpec((1, tm, d), row2),
                  pl.BlockSpec((1, 1, d), vec2), pl.BlockSpec((1, 1, d), vec2),
                  pl.BlockSpec((d, LANES), lambda b, i: (0, 0)),
                  pl.BlockSpec((1, LANES), lambda b, i: (0, 0))],
        out_specs=[pl.BlockSpec((1, tm, d), row2),
                   pl.BlockSpec((1, tm, LANES), row2),
                   pl.BlockSpec((1, tm, LANES), row2),
                   pl.BlockSpec((1, N_EXPERTS, tm), lambda b, i: (b * nt + i, 0, 0)),
                   pl.BlockSpec((1, 1, LANES), lambda b, i: (b * nt + i, 0, 0))],
        out_shape=[jax.ShapeDtypeStruct((bsz, s, d), MXU_DTYPE),
                   jax.ShapeDtypeStruct((bsz, s, LANES), F32),
                   jax.ShapeDtypeStruct((bsz, s, LANES), F32),
                   jax.ShapeDtypeStruct((bsz * nt, N_EXPERTS, tm), F32),
                   jax.ShapeDtypeStruct((bsz * nt, 1, LANES), jnp.int32)],
        compiler_params=_params(("parallel", "parallel")),
        name="moe_router",
    )(x, sc, sh, rw, rb)
    counts = cnt[:, 0, :N_EXPERTS].reshape(bsz * nt * N_EXPERTS)

    row = lambda b, i, e, j, cnt: (b, i, 0)
    vec = lambda b, i, e, j, cnt: (b, 0, 0)
    const2 = lambda b, i, e, j, cnt: (0, 0)
    once = pl.Buffered(1)
    grid_spec = pltpu.PrefetchScalarGridSpec(
        num_scalar_prefetch=1,
        grid=(bsz, nt, N_EXPERTS, nf),
        in_specs=[pl.BlockSpec((1, tm, d), row, pipeline_mode=once),
                  pl.BlockSpec((1, tm, d), row, pipeline_mode=once),
                  pl.BlockSpec((1, 1, d), vec),
                  pl.BlockSpec((1, tm, LANES), row, pipeline_mode=once),
                  pl.BlockSpec((1, tm, LANES), row, pipeline_mode=once),
                  pl.BlockSpec((1, N_EXPERTS, tm), lambda b, i, e, j, cnt: (b * nt + i, 0, 0)),
                  pl.BlockSpec((1, d, tf), lambda b, i, e, j, cnt: (e, 0, j)),
                  pl.BlockSpec((1, d, tf), lambda b, i, e, j, cnt: (e, 0, nf + j)),
                  pl.BlockSpec((1, tf, d), lambda b, i, e, j, cnt: (e, j, 0)),
                  pl.BlockSpec((1, d), const2), pl.BlockSpec((1, d), const2)],
        out_specs=pl.BlockSpec((1, tm, d), row),
        scratch_shapes=[pltpu.VMEM((tm, d), MXU_DTYPE), pltpu.VMEM((tm, d), F32)])
    return pl.pallas_call(
        functools.partial(_moe_kernel, n_tiles=nt),
        grid_spec=grid_spec,
        out_shape=jax.ShapeDtypeStruct((bsz, s, d), F32),
        compiler_params=pltpu.CompilerParams(
            dimension_semantics=("parallel", "parallel", "arbitrary", "arbitrary"),
            vmem_limit_bytes=MOE_VMEM_LIMIT),
        name="moe_experts",
    )(counts, h, x, g2, comb, slot, slot_t, w_gu, w_gu, w_down,
      ln_g.reshape(1, d), ln_b.reshape(1, d))


def _split_w_in(w):
    p = np.cumsum([0, 512, 512, 512, 256, 256, 512, 512, 512, SSD_XBC, SSD_HEADS, 512, 512, 512,
                   N_BRANCH * D_MODEL])
    mq, mk, mv, rq, rk, rv, rg, sz, sxbc, sdt, bq, bk, bv, gates = (
        w[:, p[i]:p[i + 1]] for i in range(14))
    w_main = jnp.concatenate([mq, mk, mv, rq, rk, rv, bq, bk, bv, gates, rg, sz, sxbc], axis=1)
    w_dt = jnp.pad(sdt, ((0, 0), (0, LANES - SSD_HEADS)))
    return w_main.astype(MXU_DTYPE), w_dt.astype(MXU_DTYPE)


def kernel(x, c, w_ada, b_ada, w_in, conv_w, conv_b, dt_bias, a_log, d_skip, ssm_norm_g, ret_gn_g, ret_gn_b, w_br, w_out, ln1_g, ln1_b, ln2_g, ln2_b, ffn_w_gu, ffn_w_down, router_w, router_b, expert_w_gu, expert_w_down):
    bsz = x.shape[0]
    mod = _ada(c, w_ada, b_ada)
    for l in range(DEPTH):
        sh1, sc1, g1, sh2, sc2, g2 = (
            mod[l, :, i * D_MODEL:(i + 1) * D_MODEL].reshape(bsz, 1, D_MODEL) for i in range(6))
        w_main, w_dt = _split_w_in(w_in[l])
        proj = _inproj(x, sc1, sh1, w_main, MXU_DTYPE, tn=2048)
        proj_dt = _inproj(x, sc1, sh1, w_dt, F32, tn=LANES)
        o_a = _moba(proj)
        o_b = _retention(proj, ret_gn_g[l], ret_gn_b[l])
        o_c = _ssd(proj, proj_dt, conv_w[l], conv_b[l], dt_bias[l], a_log[l], d_skip[l],
                   ssm_norm_g[l])
        o_d = _stick_breaking(proj)
        x = _merge(o_a, o_b, o_c, o_d, proj, x, g1, w_br[l].astype(MXU_DTYPE),
                   w_out[l].astype(MXU_DTYPE), ln1_g[l], ln1_b[l])
        if l % 2 == 0:
            x = _ffn(x, sc2, sh2, g2, ffn_w_gu[l // 2].astype(MXU_DTYPE),
                     ffn_w_down[l // 2].astype(MXU_DTYPE), ln2_g[l], ln2_b[l])
        else:
            x = _moe(x, sc2, sh2, g2, router_w[l // 2], router_b[l // 2],
                     expert_w_gu[l // 2].astype(MXU_DTYPE), expert_w_down[l // 2].astype(MXU_DTYPE),
                     ln2_g[l], ln2_b[l])
    return x
```

```python
import functools

import numpy as np
import jax
import jax.numpy as jnp
from jax import lax
from jax.experimental import pallas as pl
from jax.experimental.pallas import tpu as pltpu

F32 = jnp.float32
MXU_DTYPE = jnp.bfloat16

D_MODEL = 1024
DEPTH = 2
MOBA_HEADS = 8
MOBA_BLOCK = 256
MOBA_TOPK = 3
RET_HEADS = 4
RET_DK = 64
RET_DV = 128
RET_CHUNK = 256
SSD_D_INNER = 512
SSD_HEAD_DIM = 64
SSD_HEADS = 8
SSD_GROUPS = 2
SSD_STATE = 128
SSD_CONV = 4
SSD_CHUNK = 256
SSD_XBC = SSD_D_INNER + 2 * SSD_GROUPS * SSD_STATE
SSD_HALO = 8
SB_HEADS = 8
SB_BLOCK = 256
HEAD_DIM = 64
ATT_HEADS_PER_STEP = 8
ATT_W = ATT_HEADS_PER_STEP * HEAD_DIM
BRANCH_W = 512
N_BRANCH = 4
D_FF = 2816
N_EXPERTS = 8
D_FF_EXPERT = 3584
DEEPNORM_ALPHA = (2.0 * DEPTH) ** 0.25
LN_EPS = 1e-5
NORM_EPS = 1e-6
NEG_INF = -1e30
LOG2E = 1.4426950408889634
LANES = 128
VMEM_LIMIT = 48 * 2 ** 20
MOE_VMEM_LIMIT = 56 * 2 ** 20
MOE_CHUNK = 256
MOE_ROWS = 256
MOE_ROW_STEP = 64

A_MOBA_Q, A_MOBA_K, A_MOBA_V = 0, 512, 1024
A_RET_Q, A_RET_K, A_RET_V = 1536, 1792, 2048
A_SB_Q, A_SB_K, A_SB_V = 2560, 3072, 3584
B_GATES, B_RET_G, B_SSD_Z, B_XBC = 4096, 8192, 8704, 9216
PROJ_WIDTH = 10240


def _params(sem):
    return pltpu.CompilerParams(dimension_semantics=sem, vmem_limit_bytes=VMEM_LIMIT)


def _dot(a, b):
    return jnp.dot(a, b, preferred_element_type=F32)


def _dot_nt(a, b):
    return lax.dot_general(a, b, (((1,), (1,)), ((), ())), preferred_element_type=F32)


def _dot_tn(a, b):
    return lax.dot_general(a, b, (((0,), (0,)), ((), ())), preferred_element_type=F32)


def _split(x):
    hi = x.astype(MXU_DTYPE)
    lo = (x - hi.astype(F32)).astype(MXU_DTYPE)
    return hi, lo


def _dot3(a, b):
    ah, al = _split(a)
    bh, bl = _split(b)
    return _dot(ah, bh) + _dot(al, bh) + _dot(ah, bl)


def _sigmoid(x):
    return 1.0 / (1.0 + jnp.exp(-x))


def _silu(x):
    return x * _sigmoid(x)


def _resid_ln(x, gate, f, ln_g, ln_b):
    y = DEEPNORM_ALPHA * x + gate * f
    mu = jnp.mean(y, axis=-1, keepdims=True)
    d = y - mu
    var = jnp.mean(d * d, axis=-1, keepdims=True)
    return d * lax.rsqrt(var + LN_EPS) * ln_g + ln_b


def _ada_kernel(c_ref, w_ref, b_ref, o_ref):
    c = c_ref[...]
    o_ref[0] = _dot3(_silu(c), w_ref[0]) + b_ref[0]


def _ada(c, w_ada, b_ada):
    depth, d, n = w_ada.shape
    bsz = c.shape[0]
    tn = 1536
    return pl.pallas_call(
        _ada_kernel,
        grid=(depth, n // tn),
        in_specs=[pl.BlockSpec((bsz, d), lambda l, j: (0, 0)),
                  pl.BlockSpec((1, d, tn), lambda l, j: (l, 0, j)),
                  pl.BlockSpec((1, 1, tn), lambda l, j: (l, 0, j))],
        out_specs=pl.BlockSpec((1, bsz, tn), lambda l, j: (l, 0, j)),
        out_shape=jax.ShapeDtypeStruct((depth, bsz, n), F32),
        compiler_params=_params(("arbitrary", "arbitrary")),
        name="ada_mod",
    )(c, w_ada, b_ada.reshape(depth, 1, n))


def _inproj_kernel(x_ref, sc_ref, sh_ref, w_ref, o_ref, h_ref):
    @pl.when(pl.program_id(2) == 0)
    def _():
        h = x_ref[0] * (1.0 + sc_ref[0]) + sh_ref[0]
        h_ref[...] = h.astype(MXU_DTYPE)

    o_ref[0] = _dot(h_ref[...], w_ref[...]).astype(o_ref.dtype)


def _inproj(x, sc, sh, w, out_dtype, tn, tm=1024):
    bsz, s, d = x.shape
    n = w.shape[1]
    tm = min(tm, s)
    return pl.pallas_call(
        _inproj_kernel,
        grid=(bsz, s // tm, n // tn),
        in_specs=[pl.BlockSpec((1, tm, d), lambda b, i, j: (b, i, 0)),
                  pl.BlockSpec((1, 1, d), lambda b, i, j: (b, 0, 0)),
                  pl.BlockSpec((1, 1, d), lambda b, i, j: (b, 0, 0)),
                  pl.BlockSpec((d, tn), lambda b, i, j: (0, j))],
        out_specs=pl.BlockSpec((1, tm, tn), lambda b, i, j: (b, i, j)),
        out_shape=jax.ShapeDtypeStruct((bsz, s, n), out_dtype),
        scratch_shapes=[pltpu.VMEM((tm, d), MXU_DTYPE)],
        compiler_params=_params(("parallel", "parallel", "arbitrary")),
        name="in_proj",
    )(x, sc, sh, w)


def _moba_kernel(q_ref, k_ref, v_ref, sl_ref, o_ref, km_ref, ka_ref, vt_ref, *, nb):
    blk = MOBA_BLOCK
    nh = ATT_HEADS_PER_STEP
    heads = range(nh)
    qi = pl.program_id(2)
    s_len = nb * blk
    nb_pad = -(-nb // 8) * 8
    extra = 2 * HEAD_DIM - HEAD_DIM
    assert nb_pad + 8 <= extra
    cols = [slice(h * HEAD_DIM, (h + 1) * HEAD_DIM) for h in heads]

    @pl.when(qi == 0)
    def _():
        k_all = k_ref[0]
        kk = k_all.astype(F32).reshape(nb, blk, ATT_W)
        km_ref[...] = jnp.zeros_like(km_ref)
        km_ref[0:nb, :] = jnp.sum(kk, axis=1) * (1.0 / blk)
        pos = lax.broadcasted_iota(jnp.int32, (s_len, extra), 0)
        lane = lax.broadcasted_iota(jnp.int32, (s_len, extra), 1)
        one_hot = jnp.where(lane == pos // blk, 1.0, 0.0)
        pos_hi = jnp.where(lane == nb_pad, (pos // blk * blk).astype(F32), 0.0)
        pos_lo = jnp.where(lane == nb_pad + 1, (pos % blk).astype(F32), 0.0)
        tail = (one_hot + pos_hi + pos_lo).astype(MXU_DTYPE)
        for h in heads:
            ka_ref[h] = jnp.concatenate([k_all[:, cols[h]], tail], axis=1)

        ones_rows = jnp.where(lax.broadcasted_iota(jnp.int32, (extra, blk), 0) == 0, 1.0, 0.0)

        def transpose_block(j, carry):
            j0 = pl.multiple_of(j * blk, blk)
            vt = v_ref[0, pl.ds(j0, blk), :].astype(F32).T
            for h in heads:
                vt_ref[j, h] = jnp.concatenate([vt[cols[h], :], ones_rows], axis=0).astype(MXU_DTYPE)
            return carry

        lax.fori_loop(0, nb, transpose_block, 0)

    key_i = lax.broadcasted_iota(jnp.int32, (blk, blk), 0)
    qry_i = lax.broadcasted_iota(jnp.int32, (blk, blk), 1)
    n_iota = lax.broadcasted_iota(jnp.int32, (nb_pad, blk), 0)
    past = n_iota < qi
    k_sel = min(MOBA_TOPK, max(nb - 1, 1))
    q_t = q_ref[0].astype(F32).T
    slope_rows = lax.broadcasted_iota(jnp.int32, (8, blk), 0) < 2

    qa = []
    for h in heads:
        qh_t = q_t[cols[h], :]
        slope = sl_ref[0, :, h * HEAD_DIM:h * HEAD_DIM + 1]
        km_hi, km_lo = _split(km_ref[:, cols[h]])
        qh_mxu = qh_t.astype(MXU_DTYPE)
        gate = (_dot(km_hi, qh_mxu) + _dot(km_lo, qh_mxu))[0:nb_pad, :]
        g = jnp.where(past, gate, NEG_INF)
        g = jnp.where(n_iota < nb, g, -jnp.inf)
        sel = jnp.zeros((nb_pad, blk), F32)
        for _ in range(k_sel):
            mx = jnp.max(g, axis=0, keepdims=True)
            idx = jnp.min(jnp.where(g == mx, n_iota, nb_pad), axis=0, keepdims=True)
            pick = n_iota == idx
            sel = jnp.where(pick, 1.0, sel)
            g = jnp.where(pick, -jnp.inf, g)
        keep = jnp.logical_or(jnp.logical_and(past, sel > 0.5), n_iota == qi)
        bias = jnp.where(keep, 0.0, NEG_INF)
        alibi = jnp.where(slope_rows, slope, 0.0)
        pad = jnp.zeros((extra - nb_pad - 8, blk), F32)
        qa.append(jnp.concatenate([qh_t * HEAD_DIM ** -0.5, bias, alibi, pad],
                                  axis=0).astype(MXU_DTYPE))

    def tiles(js, carry, diag):
        starts = [pl.multiple_of(j * blk, blk) for j in js]
        s = [[_dot(ka_ref[h, pl.ds(j0, blk), :], qa[h]) for h in heads] for j0 in starts]
        m = list(carry[:nh]) if carry is not None else [None] * nh
        acc = list(carry[nh:]) if carry is not None else [None] * nh
        p = {}
        for h in heads:
            sh = [jnp.where(key_i <= qry_i, s[t][h], NEG_INF) if diag else s[t][h]
                  for t in range(len(js))]
            top = jnp.max(sh[0], axis=0, keepdims=True)
            for t in range(1, len(js)):
                top = jnp.maximum(top, jnp.max(sh[t], axis=0, keepdims=True))
            if m[h] is None:
                m[h] = top
            else:
                m_new = jnp.maximum(m[h], top)
                acc[h] = jnp.exp(m[h] - m_new) * acc[h]
                m[h] = m_new
            for t in range(len(js)):
                p[t, h] = jnp.exp(sh[t] - m[h]).astype(MXU_DTYPE)
        for t, j in enumerate(js):
            for h in heads:
                pv = _dot(vt_ref[j, h], p[t, h])
                acc[h] = pv if acc[h] is None else acc[h] + pv
        return tuple(m) + tuple(acc)

    carry0 = tiles([qi], None, True)

    def pair(t, carry):
        return tiles([2 * t, 2 * t + 1], carry, False)

    def single(t, carry):
        return tiles([qi - 1], carry, False)

    res = lax.fori_loop(0, qi // 2, pair, carry0)
    res = lax.fori_loop(0, qi % 2, single, res)
    out_t = [res[nh + h][0:HEAD_DIM, :] / res[nh + h][HEAD_DIM:HEAD_DIM + 1, :] for h in heads]
    o_ref[0] = jnp.concatenate(out_t, axis=0).T.astype(o_ref.dtype)


def _moba(proj_a):
    bsz, s, _ = proj_a.shape
    blk = MOBA_BLOCK
    nb = s // blk
    n_grp = MOBA_HEADS // ATT_HEADS_PER_STEP
    slopes = 2.0 ** (-8.0 * np.arange(1, MOBA_HEADS + 1) / MOBA_HEADS)
    sl = jnp.asarray(np.repeat(slopes, HEAD_DIM).reshape(n_grp, 1, ATT_W), F32)
    qb, kb, vb = A_MOBA_Q // ATT_W, A_MOBA_K // ATT_W, A_MOBA_V // ATT_W
    return pl.pallas_call(
        functools.partial(_moba_kernel, nb=nb),
        grid=(bsz, n_grp, nb),
        in_specs=[pl.BlockSpec((1, blk, ATT_W), lambda b, hg, i: (b, i, qb + hg)),
                  pl.BlockSpec((1, s, ATT_W), lambda b, hg, i: (b, 0, kb + hg)),
                  pl.BlockSpec((1, s, ATT_W), lambda b, hg, i: (b, 0, vb + hg)),
                  pl.BlockSpec((1, 1, ATT_W), lambda b, hg, i: (hg, 0, 0))],
        out_specs=pl.BlockSpec((1, blk, ATT_W), lambda b, hg, i: (b, i, hg)),
        out_shape=jax.ShapeDtypeStruct((bsz, s, MOBA_HEADS * HEAD_DIM), MXU_DTYPE),
        scratch_shapes=[pltpu.VMEM((LANES, ATT_W), F32),
                        pltpu.VMEM((ATT_HEADS_PER_STEP, s, 2 * HEAD_DIM), MXU_DTYPE),
                        pltpu.VMEM((nb, ATT_HEADS_PER_STEP, 2 * HEAD_DIM, blk), MXU_DTYPE)],
        compiler_params=_params(("parallel", "parallel", "arbitrary")),
        name="moba",
    )(proj_a, proj_a, proj_a, sl)


def _sb_kernel(q_ref, k_ref, v_ref, o_ref, vt_ref, *, nb):
    blk = SB_BLOCK
    nh = ATT_HEADS_PER_STEP
    heads = range(nh)
    qi = pl.program_id(2)
    cols = [slice(h * HEAD_DIM, (h + 1) * HEAD_DIM) for h in heads]

    @pl.when(qi == 0)
    def _():
        def transpose_block(j, carry):
            j0 = pl.multiple_of(j * blk, blk)
            vt = v_ref[0, pl.ds(j0, blk), :].astype(F32).T
            for h in heads:
                vt_ref[j, h] = vt[cols[h], :].astype(MXU_DTYPE)
            return carry

        lax.fori_loop(0, nb, transpose_block, 0)

    key_i = lax.broadcasted_iota(jnp.int32, (blk, blk), 0)
    qry_i = lax.broadcasted_iota(jnp.int32, (blk, blk), 1)
    causal = key_i < qry_i
    later = jnp.where(qry_i > key_i, 1.0, 0.0).astype(MXU_DTYPE)
    qs = [(q_ref[0, :, cols[h]].astype(F32) * HEAD_DIM ** -0.5).astype(MXU_DTYPE)
          for h in heads]

    def tiles(js, accs, rests, diag):
        starts = [pl.multiple_of(j * blk, blk) for j in js]
        z = [[_dot_nt(k_ref[0, pl.ds(j0, blk), cols[h]], qs[h]) for h in heads] for j0 in starts]
        log_beta, log_1mb, parts = [], [], []
        for t in range(len(js)):
            for h in heads:
                zz = z[t][h]
                lb = jnp.minimum(zz, 0.0) - jnp.log(1.0 + jnp.exp2(jnp.abs(zz) * (-LOG2E)))
                l1 = lb - zz
                if diag:
                    l1 = jnp.where(causal, l1, 0.0)
                log_beta.append(lb)
                log_1mb.append(l1)
                parts.append(l1.astype(MXU_DTYPE))
        rem = [_dot(later, p) for p in parts]
        w = []
        rests = list(rests)
        for t in range(len(js)):
            for h in heads:
                i = t * nh + h
                wh = jnp.exp(log_beta[i] + rem[i] + rests[h])
                if diag:
                    wh = jnp.where(causal, wh, 0.0)
                w.append(wh.astype(MXU_DTYPE))
                rests[h] = rests[h] + rem[i][0:1, :] + log_1mb[i][0:1, :]
        accs = list(accs)
        for t, j in enumerate(js):
            for h in heads:
                accs[h] = accs[h] + _dot(vt_ref[j, h], w[t * nh + h])
        return accs, rests

    zero_acc = [jnp.zeros((HEAD_DIM, blk), F32)] * nh
    accs, rests = tiles([qi], zero_acc, [jnp.zeros((1, blk), F32)] * nh, True)

    def pair(t, carry):
        j = qi - 1 - 2 * t
        accs, rests = tiles([j, j - 1], carry[:nh], carry[nh:], False)
        return tuple(accs + rests)

    def single(t, carry):
        accs, rests = tiles([0], carry[:nh], carry[nh:], False)
        return tuple(accs + rests)

    res = lax.fori_loop(0, qi // 2, pair, tuple(accs + rests))
    res = lax.fori_loop(0, qi % 2, single, res)
    o_ref[0] = jnp.concatenate(res[:nh], axis=0).T.astype(o_ref.dtype)


def _stick_breaking(proj_a):
    bsz, s, _ = proj_a.shape
    blk = SB_BLOCK
    n_grp = SB_HEADS // ATT_HEADS_PER_STEP
    qb, kb, vb = A_SB_Q // ATT_W, A_SB_K // ATT_W, A_SB_V // ATT_W
    return pl.pallas_call(
        functools.partial(_sb_kernel, nb=s // blk),
        grid=(bsz, n_grp, s // blk),
        in_specs=[pl.BlockSpec((1, blk, ATT_W), lambda b, hg, i: (b, i, qb + hg)),
                  pl.BlockSpec((1, s, ATT_W), lambda b, hg, i: (b, 0, kb + hg)),
                  pl.BlockSpec((1, s, ATT_W), lambda b, hg, i: (b, 0, vb + hg))],
        out_specs=pl.BlockSpec((1, blk, ATT_W), lambda b, hg, i: (b, i, hg)),
        out_shape=jax.ShapeDtypeStruct((bsz, s, SB_HEADS * HEAD_DIM), MXU_DTYPE),
        scratch_shapes=[pltpu.VMEM((s // blk, ATT_HEADS_PER_STEP, HEAD_DIM, blk), MXU_DTYPE)],
        compiler_params=_params(("parallel", "parallel", "arbitrary")),
        name="stick_breaking",
    )(proj_a, proj_a, proj_a)


def _ret_kernel(q_ref, k_ref, v_ref, g_ref, gn_g_ref, gn_b_ref, o_ref, st_ref):
    ch = RET_CHUNK

    @pl.when(pl.program_id(1) == 0)
    def _():
        st_ref[...] = jnp.zeros_like(st_ref)

    i_col = lax.broadcasted_iota(jnp.int32, (ch, 1), 0).astype(F32)
    diff = (lax.broadcasted_iota(jnp.int32, (ch, ch), 0)
            - lax.broadcasted_iota(jnp.int32, (ch, ch), 1)).astype(F32)
    heads = range(RET_HEADS)
    log_g = [float(np.log(1.0 - 2.0 ** (-5.0 - h))) for h in heads]
    qs = [(q_ref[0, :, h * RET_DK:(h + 1) * RET_DK].astype(F32) * RET_DK ** -0.5).astype(MXU_DTYPE)
          for h in heads]
    k = [k_ref[0, :, h * RET_DK:(h + 1) * RET_DK] for h in heads]
    v = [v_ref[0, :, h * RET_DV:(h + 1) * RET_DV] for h in heads]
    prev = [st_ref[h] for h in heads]
    qk = [_dot_nt(qs[h], k[h]) for h in heads]
    o_cross = [_dot(qs[h], prev[h].astype(MXU_DTYPE)) for h in heads]
    kd = [(k[h].astype(F32) * jnp.exp((ch - 1.0 - i_col) * log_g[h])).astype(MXU_DTYPE)
          for h in heads]
    st_new = [_dot_tn(kd[h], v[h]) for h in heads]
    qkd = []
    for h in heads:
        decay = jnp.where(diff >= 0, jnp.exp(jnp.maximum(diff, 0.0) * log_g[h]), 0.0)
        qkd.append((qk[h] * decay).astype(MXU_DTYPE))
    o_intra = [_dot(qkd[h], v[h]) for h in heads]
    for h in heads:
        st_ref[h] = float(np.exp(ch * log_g[h])) * prev[h] + st_new[h]
        o = o_intra[h] + o_cross[h] * jnp.exp((i_col + 1.0) * log_g[h])
        mu = jnp.mean(o, axis=-1, keepdims=True)
        d = o - mu
        var = jnp.mean(d * d, axis=-1, keepdims=True)
        cs = slice(h * RET_DV, (h + 1) * RET_DV)
        on = d * lax.rsqrt(var + NORM_EPS) * gn_g_ref[:, cs] + gn_b_ref[:, cs]
        o_ref[0, :, cs] = (_silu(g_ref[0, :, cs].astype(F32)) * on).astype(o_ref.dtype)


def _ssd_kernel(z_ref, xbc_ref, dt_ref, cw_ref, cb_ref, dtb_ref, alog_ref, dsk_ref, ng_ref,
                o_ref, tail_ref, st_ref, y_ref):
    ch = SSD_CHUNK
    heads_per_group = SSD_HEADS // SSD_GROUPS
    halo = SSD_HALO

    @pl.when(pl.program_id(1) == 0)
    def _():
        tail_ref[0:halo, :] = jnp.zeros((halo, SSD_XBC), F32)
        st_ref[...] = jnp.zeros_like(st_ref)

    tail_ref[halo:, :] = xbc_ref[0].astype(F32)
    conv = cb_ref[...]
    for kk in range(SSD_CONV):
        off = halo - (SSD_CONV - 1) + kk
        conv = conv + cw_ref[kk:kk + 1, :] * tail_ref[off:off + ch, :]
    tail_ref[0:halo, :] = tail_ref[ch:ch + halo, :]
    xc = _silu(conv)
    xs = xc[:, :SSD_D_INNER]
    bm = xc[:, SSD_D_INNER:SSD_D_INNER + SSD_GROUPS * SSD_STATE]
    cm = xc[:, SSD_D_INNER + SSD_GROUPS * SSD_STATE:]

    dtr = dt_ref[0] + dtb_ref[...]
    dt = jnp.maximum(dtr, 0.0) + jnp.log1p(jnp.exp(-jnp.abs(dtr)))
    a = -jnp.exp(alog_ref[...])
    r_iota = lax.broadcasted_iota(jnp.int32, (ch, ch), 0)
    c_iota = lax.broadcasted_iota(jnp.int32, (ch, ch), 1)
    causal = r_iota >= c_iota
    tri = jnp.where(causal, 1.0, 0.0).astype(MXU_DTYPE)
    da = dt * a
    da_hi, da_lo = _split(da)
    cum = _dot(tri, da_hi) + _dot(tri, da_lo)
    cum_t = cum.T
    dt_t = dt.T
    cum_last = cum[ch - 1:ch, :]
    decay_st = jnp.exp(cum_last - cum) * dt
    e_cum = jnp.exp(cum)
    e_last = jnp.exp(cum_last)

    for g in range(SSD_GROUPS):
        bm_g = bm[:, g * SSD_STATE:(g + 1) * SSD_STATE].astype(MXU_DTYPE)
        cm_g = cm[:, g * SSD_STATE:(g + 1) * SSD_STATE].astype(MXU_DTYPE)
        cb = _dot_nt(cm_g, bm_g)
        for r in range(heads_per_group):
            hd = g * heads_per_group + r
            cs = slice(hd * SSD_HEAD_DIM, (hd + 1) * SSD_HEAD_DIM)
            x_h = xs[:, cs]
            seg = cum[:, hd:hd + 1] - cum_t[hd:hd + 1, :]
            lmat = jnp.exp(jnp.where(causal, seg, NEG_INF))
            w = cb * lmat * dt_t[hd:hd + 1, :]
            y = _dot(w.astype(MXU_DTYPE), x_h.astype(MXU_DTYPE))
            prev = st_ref[hd]
            y = y + _dot(cm_g, prev.astype(MXU_DTYPE)) * e_cum[:, hd:hd + 1]
            xd = (x_h * decay_st[:, hd:hd + 1]).astype(MXU_DTYPE)
            st_ref[hd] = e_last[:, hd:hd + 1] * prev + _dot_tn(bm_g, xd)
            y_ref[:, cs] = y + dsk_ref[:, cs] * x_h

    y = y_ref[...] * _silu(z_ref[0].astype(F32))
    gw = SSD_D_INNER // SSD_GROUPS
    for g in range(SSD_GROUPS):
        cs = slice(g * gw, (g + 1) * gw)
        yg = y[:, cs]
        ms = jnp.mean(yg * yg, axis=-1, keepdims=True)
        o_ref[0, :, cs] = (yg * lax.rsqrt(ms + LN_EPS) * ng_ref[:, cs]).astype(o_ref.dtype)


def _scan_kernel(rq_ref, rk_ref, rv_ref, rg_ref, gn_g_ref, gn_b_ref,
                 z_ref, xbc_ref, dt_ref, cw_ref, cb_ref, dtb_ref, alog_ref, dsk_ref, ng_ref,
                 o_ret_ref, o_ssd_ref, ret_st_ref, tail_ref, ssd_st_ref, y_ref):
    _ret_kernel(rq_ref, rk_ref, rv_ref, rg_ref, gn_g_ref, gn_b_ref, o_ret_ref, ret_st_ref)
    _ssd_kernel(z_ref, xbc_ref, dt_ref, cw_ref, cb_ref, dtb_ref, alog_ref, dsk_ref, ng_ref,
                o_ssd_ref, tail_ref, ssd_st_ref, y_ref)


def _scans(proj, proj_dt, gn_g, gn_b, conv_w, conv_b, dt_bias, a_log, d_skip, norm_g):
    bsz, s, _ = proj.shape
    assert RET_CHUNK == SSD_CHUNK
    ch = SSD_CHUNK
    wq, wv = RET_HEADS * RET_DK, RET_HEADS * RET_DV
    pad = LANES - SSD_HEADS
    dtb = jnp.pad(dt_bias, (0, pad)).reshape(1, LANES)
    alog = jnp.pad(a_log, (0, pad)).reshape(1, LANES)
    dsk = jnp.repeat(d_skip, SSD_HEAD_DIM).reshape(1, SSD_D_INNER)
    const = lambda b, i: (0, 0)
    return pl.pallas_call(
        _scan_kernel,
        grid=(bsz, s // ch),
        in_specs=[pl.BlockSpec((1, ch, wq), lambda b, i: (b, i, A_RET_Q // wq)),
                  pl.BlockSpec((1, ch, wq), lambda b, i: (b, i, A_RET_K // wq)),
                  pl.BlockSpec((1, ch, wv), lambda b, i: (b, i, A_RET_V // wv)),
                  pl.BlockSpec((1, ch, wv), lambda b, i: (b, i, B_RET_G // wv)),
                  pl.BlockSpec((1, wv), const),
                  pl.BlockSpec((1, wv), const),
                  pl.BlockSpec((1, ch, SSD_D_INNER), lambda b, i: (b, i, B_SSD_Z // SSD_D_INNER)),
                  pl.BlockSpec((1, ch, SSD_XBC), lambda b, i: (b, i, B_XBC // SSD_XBC)),
                  pl.BlockSpec((1, ch, LANES), lambda b, i: (b, i, 0)),
                  pl.BlockSpec((SSD_CONV, SSD_XBC), const),
                  pl.BlockSpec((1, SSD_XBC), const),
                  pl.BlockSpec((1, LANES), const),
                  pl.BlockSpec((1, LANES), const),
                  pl.BlockSpec((1, SSD_D_INNER), const),
                  pl.BlockSpec((1, SSD_D_INNER), const)],
        out_specs=[pl.BlockSpec((1, ch, wv), lambda b, i: (b, i, 0)),
                   pl.BlockSpec((1, ch, SSD_D_INNER), lambda b, i: (b, i, 0))],
        out_shape=[jax.ShapeDtypeStruct((bsz, s, wv), MXU_DTYPE),
                   jax.ShapeDtypeStruct((bsz, s, SSD_D_INNER), MXU_DTYPE)],
        scratch_shapes=[pltpu.VMEM((RET_HEADS, RET_DK, RET_DV), F32),
                        pltpu.VMEM((SSD_HALO + ch, SSD_XBC), F32),
                        pltpu.VMEM((SSD_HEADS, SSD_STATE, SSD_HEAD_DIM), F32),
                        pltpu.VMEM((ch, SSD_D_INNER), F32)],
        compiler_params=_params(("parallel", "arbitrary")),
        name="retention_ssd",
    )(proj, proj, proj, proj, gn_g.reshape(1, wv), gn_b.reshape(1, wv),
      proj, proj, proj_dt, conv_w, conv_b.reshape(1, SSD_XBC), dtb, alog, dsk,
      norm_g.reshape(1, SSD_D_INNER))


def _merge_kernel(oa_ref, ob_ref, oc_ref, od_ref, gates_ref, x_ref, g1_ref, wbr_ref, wout_ref,
                  lg_ref, lb_ref, o_ref):
    merged = None
    for n, br in enumerate((oa_ref, ob_ref, oc_ref, od_ref)):
        y = _dot(br[0], wbr_ref[n])
        t = _sigmoid(gates_ref[0, :, n * D_MODEL:(n + 1) * D_MODEL].astype(F32)) * y
        merged = t if merged is None else merged + t
    mix = _dot(merged.astype(MXU_DTYPE), wout_ref[...])
    o_ref[0] = _resid_ln(x_ref[0], g1_ref[0], mix, lg_ref[...], lb_ref[...])


def _merge(o_a, o_b, o_c, o_d, proj, x, g1, w_br, w_out, ln_g, ln_b, tm=512):
    bsz, s, d = x.shape
    tm = min(tm, s)
    gw = N_BRANCH * d
    row = lambda b, i: (b, i, 0)
    const2 = lambda b, i: (0, 0)
    br_spec = pl.BlockSpec((1, tm, BRANCH_W), row)
    return pl.pallas_call(
        _merge_kernel,
        grid=(bsz, s // tm),
        in_specs=[br_spec, br_spec, br_spec, br_spec,
                  pl.BlockSpec((1, tm, gw), lambda b, i: (b, i, B_GATES // gw)),
                  pl.BlockSpec((1, tm, d), row),
                  pl.BlockSpec((1, 1, d), lambda b, i: (b, 0, 0)),
                  pl.BlockSpec((N_BRANCH, BRANCH_W, d), lambda b, i: (0, 0, 0)),
                  pl.BlockSpec((d, d), const2),
                  pl.BlockSpec((1, d), const2),
                  pl.BlockSpec((1, d), const2)],
        out_specs=pl.BlockSpec((1, tm, d), row),
        out_shape=jax.ShapeDtypeStruct((bsz, s, d), F32),
        compiler_params=_params(("parallel", "parallel")),
        name="merge",
    )(o_a, o_b, o_c, o_d, proj, x, g1, w_br, w_out, ln_g.reshape(1, d), ln_b.reshape(1, d))


def _ffn_kernel(x_ref, sc_ref, sh_ref, g2_ref, wg_ref, wu_ref, wd_ref, lg_ref, lb_ref,
                o_ref, h_ref, acc_ref):
    j = pl.program_id(2)

    @pl.when(j == 0)
    def _():
        h = x_ref[0] * (1.0 + sc_ref[0]) + sh_ref[0]
        h_ref[...] = h.astype(MXU_DTYPE)
        acc_ref[...] = jnp.zeros_like(acc_ref)

    h = h_ref[...]
    act = _silu(_dot(h, wg_ref[...])) * _dot(h, wu_ref[...])
    acc_ref[...] += _dot(act.astype(MXU_DTYPE), wd_ref[...])

    @pl.when(j == pl.num_programs(2) - 1)
    def _():
        o_ref[0] = _resid_ln(x_ref[0], g2_ref[0], acc_ref[...], lg_ref[...], lb_ref[...])


def _ffn(x, sc, sh, g2, w_gu, w_down, ln_g, ln_b, tm=512, tf=1408):
    bsz, s, d = x.shape
    tm = min(tm, s)
    nf = D_FF // tf
    row = lambda b, i, j: (b, i, 0)
    vec = lambda b, i, j: (b, 0, 0)
    const2 = lambda b, i, j: (0, 0)
    return pl.pallas_call(
        _ffn_kernel,
        grid=(bsz, s // tm, nf),
        in_specs=[pl.BlockSpec((1, tm, d), row),
                  pl.BlockSpec((1, 1, d), vec), pl.BlockSpec((1, 1, d), vec),
                  pl.BlockSpec((1, 1, d), vec),
                  pl.BlockSpec((d, tf), lambda b, i, j: (0, j)),
                  pl.BlockSpec((d, tf), lambda b, i, j: (0, nf + j)),
                  pl.BlockSpec((tf, d), lambda b, i, j: (j, 0)),
                  pl.BlockSpec((1, d), const2), pl.BlockSpec((1, d), const2)],
        out_specs=pl.BlockSpec((1, tm, d), row),
        out_shape=jax.ShapeDtypeStruct((bsz, s, d), F32),
        scratch_shapes=[pltpu.VMEM((tm, d), MXU_DTYPE), pltpu.VMEM((tm, d), F32)],
        compiler_params=_params(("parallel", "parallel", "arbitrary")),
        name="ffn_dense",
    )(x, sc, sh, g2, w_gu, w_gu, w_down, ln_g.reshape(1, d), ln_b.reshape(1, d))


def _router_kernel(x_ref, sc_ref, sh_ref, rw_ref, rb_ref,
                   h_ref, comb_ref, slot_ref, slot_t_ref, cnt_ref):
    tm = x_ref.shape[1]
    lane = lax.broadcasted_iota(jnp.int32, (tm, LANES), 1)
    h = x_ref[0] * (1.0 + sc_ref[0]) + sh_ref[0]
    h_ref[0] = h.astype(MXU_DTYPE)
    logits = _dot3(h, rw_ref[...]) + rb_ref[...]
    logits = jnp.where(lane < N_EXPERTS, logits, -jnp.inf)
    v1 = jnp.max(logits, axis=1, keepdims=True)
    i1 = jnp.min(jnp.where(logits == v1, lane, LANES), axis=1, keepdims=True)
    rest = jnp.where(lane == i1, -jnp.inf, logits)
    v2 = jnp.max(rest, axis=1, keepdims=True)
    i2 = jnp.min(jnp.where(rest == v2, lane, LANES), axis=1, keepdims=True)
    e2 = jnp.exp(v2 - v1)
    w1 = 1.0 / (1.0 + e2)
    w2 = e2 / (1.0 + e2)
    comb_ref[0] = jnp.where(lane == i1, w1, 0.0) + jnp.where(lane == i2, w2, 0.0)
    routed = jnp.logical_or(lane == i1, lane == i2)
    earlier = (lax.broadcasted_iota(jnp.int32, (tm, tm), 1)
               < lax.broadcasted_iota(jnp.int32, (tm, tm), 0))
    before = _dot(jnp.where(earlier, 1.0, 0.0).astype(MXU_DTYPE),
                  jnp.where(routed, 1.0, 0.0).astype(MXU_DTYPE))
    slot = jnp.where(routed, before, -1.0)
    slot_ref[0] = slot
    slot_t_ref[0] = slot.T[0:N_EXPERTS, :]
    cnt_ref[0] = jnp.sum(jnp.where(routed, 1, 0), axis=0, keepdims=True)


def _moe_kernel(cnt_ref, h_ref, x_ref, g2_ref, comb_ref, slot_ref, slot_t_ref,
                wg_ref, wu_ref, wd_ref, lg_ref, lb_ref, o_ref, xe_ref, ye_ref, *, n_tiles):
    tm = x_ref.shape[1]
    ch = MOE_CHUNK
    e = pl.program_id(2)
    j = pl.program_id(3)
    last_j = pl.num_programs(3) - 1
    count = cnt_ref[(pl.program_id(0) * n_tiles + pl.program_id(1)) * N_EXPERTS + e]
    n_chunks = (count + ch - 1) // ch

    @pl.when(jnp.logical_and(e == 0, j == 0))
    def _():
        o_ref[0] = jnp.zeros((tm, D_MODEL), F32)

    @pl.when(j == 0)
    def _():
        slot_row = slot_t_ref[0, pl.ds(e, 1), :]
        row = lax.broadcasted_iota(jnp.int32, (ch, tm), 0).astype(F32)

        def gather(c, carry):
            r0 = pl.multiple_of(c * ch, ch)
            pick = jnp.where(slot_row == row + (c * ch).astype(F32), 1.0, 0.0).astype(MXU_DTYPE)
            xe_ref[pl.ds(r0, ch), :] = _dot(pick, h_ref[0]).astype(MXU_DTYPE)
            ye_ref[pl.ds(r0, ch), :] = jnp.zeros((ch, D_MODEL), F32)
            return carry

        lax.fori_loop(0, n_chunks, gather, 0)

    def expert_rows(r0, rows):
        xc = xe_ref[pl.ds(r0, rows), :]
        act = _silu(_dot(xc, wg_ref[0])) * _dot(xc, wu_ref[0])
        ye_ref[pl.ds(r0, rows), :] += _dot(act.astype(MXU_DTYPE), wd_ref[0])

    n_full = count // MOE_ROWS

    def full_piece(c, carry):
        expert_rows(pl.multiple_of(c * MOE_ROWS, MOE_ROWS), MOE_ROWS)
        return carry

    lax.fori_loop(0, n_full, full_piece, 0)
    tail0 = pl.multiple_of(n_full * MOE_ROWS, MOE_ROWS)
    tail_steps = (count - n_full * MOE_ROWS + MOE_ROW_STEP - 1) // MOE_ROW_STEP
    for steps in range(1, MOE_ROWS // MOE_ROW_STEP + 1):
        @pl.when(tail_steps == steps)
        def _(steps=steps):
            expert_rows(tail0, steps * MOE_ROW_STEP)

    @pl.when(j == last_j)
    def _():
        lane = lax.broadcasted_iota(jnp.int32, (tm, LANES), 1)
        mine = lane == e
        slot_col = jnp.sum(jnp.where(mine, slot_ref[0], 0.0), axis=1, keepdims=True)
        comb_col = jnp.sum(jnp.where(mine, comb_ref[0], 0.0), axis=1, keepdims=True)
        col = lax.broadcasted_iota(jnp.int32, (tm, ch), 1).astype(F32)

        def scatter(c, carry):
            r0 = pl.multiple_of(c * ch, ch)
            place = jnp.where(slot_col == col + (c * ch).astype(F32), 1.0, 0.0).astype(MXU_DTYPE)
            o_ref[0] += comb_col * _dot(place, ye_ref[pl.ds(r0, ch), :].astype(MXU_DTYPE))
            return carry

        lax.fori_loop(0, n_chunks, scatter, 0)

    @pl.when(jnp.logical_and(e == pl.num_programs(2) - 1, j == last_j))
    def _():
        o_ref[0] = _resid_ln(x_ref[0], g2_ref[0], o_ref[0], lg_ref[...], lb_ref[...])


def _moe(x, sc, sh, g2, router_w, router_b, w_gu, w_down, ln_g, ln_b, tm=1024, tf=1792):
    bsz, s, d = x.shape
    tm = min(tm, s)
    nt = s // tm
    nf = D_FF_EXPERT // tf
    rw = jnp.pad(router_w, ((0, 0), (0, LANES - N_EXPERTS)))
    rb = jnp.pad(router_b, (0, LANES - N_EXPERTS)).reshape(1, LANES)

    row2 = lambda b, i: (b, i, 0)
    vec2 = lambda b, i: (b, 0, 0)
    h, comb, slot, slot_t, cnt = pl.pallas_call(
        _router_kernel,
        grid=(bsz, nt),
        in_specs=[pl.BlockSpec((1, tm, d), row2),
                  pl.BlockSpec((1, 1, d), vec2), pl.BlockSpec((1, 1, d), vec2),
                  pl.BlockSpec((d, LANES), lambda b, i: (0, 0)),
                  pl.BlockSpec((1, LANES), lambda b, i: (0, 0))],
        out_specs=[pl.BlockSpec((1, tm, d), row2),
                   pl.BlockSpec((1, tm, LANES), row2),
                   pl.BlockSpec((1, tm, LANES), row2),
                   pl.BlockSpec((1, N_EXPERTS, tm), lambda b, i: (b * nt + i, 0, 0)),
                   pl.BlockSpec((1, 1, LANES), lambda b, i: (b * nt + i, 0, 0))],
        out_shape=[jax.ShapeDtypeStruct((bsz, s, d), MXU_DTYPE),
                   jax.ShapeDtypeStruct((bsz, s, LANES), F32),
                   jax.ShapeDtypeStruct((bsz, s, LANES), F32),
                   jax.ShapeDtypeStruct((bsz * nt, N_EXPERTS, tm), F32),
                   jax.ShapeDtypeStruct((bsz * nt, 1, LANES), jnp.int32)],
        compiler_params=_params(("parallel", "parallel")),
        name="moe_router",
    )(x, sc, sh, rw, rb)
    counts = cnt[:, 0, :N_EXPERTS].reshape(bsz * nt * N_EXPERTS)

    row = lambda b, i, e, j, cnt: (b, i, 0)
    vec = lambda b, i, e, j, cnt: (b, 0, 0)
    const2 = lambda b, i, e, j, cnt: (0, 0)
    once = pl.Buffered(1)
    grid_spec = pltpu.PrefetchScalarGridSpec(
        num_scalar_prefetch=1,
        grid=(bsz, nt, N_EXPERTS, nf),
        in_specs=[pl.BlockSpec((1, tm, d), row, pipeline_mode=once),
                  pl.BlockSpec((1, tm, d), row, pipeline_mode=once),
                  pl.BlockSpec((1, 1, d), vec),
                  pl.BlockSpec((1, tm, LANES), row, pipeline_mode=once),
                  pl.BlockSpec((1, tm, LANES), row, pipeline_mode=once),
                  pl.BlockSpec((1, N_EXPERTS, tm), lambda b, i, e, j, cnt: (b * nt + i, 0, 0)),
                  pl.BlockSpec((1, d, tf), lambda b, i, e, j, cnt: (e, 0, j)),
                  pl.BlockSpec((1, d, tf), lambda b, i, e, j, cnt: (e, 0, nf + j)),
                  pl.BlockSpec((1, tf, d), lambda b, i, e, j, cnt: (e, j, 0)),
                  pl.BlockSpec((1, d), const2), pl.BlockSpec((1, d), const2)],
        out_specs=pl.BlockSpec((1, tm, d), row),
        scratch_shapes=[pltpu.VMEM((tm, d), MXU_DTYPE), pltpu.VMEM((tm, d), F32)])
    return pl.pallas_call(
        functools.partial(_moe_kernel, n_tiles=nt),
        grid_spec=grid_spec,
        out_shape=jax.ShapeDtypeStruct((bsz, s, d), F32),
        compiler_params=pltpu.CompilerParams(
            dimension_semantics=("parallel", "parallel", "arbitrary", "arbitrary"),
            vmem_limit_bytes=MOE_VMEM_LIMIT),
        name="moe_experts",
    )(counts, h, x, g2, comb, slot, slot_t, w_gu, w_gu, w_down,
      ln_g.reshape(1, d), ln_b.reshape(1, d))


def _split_w_in(w):
    p = np.cumsum([0, 512, 512, 512, 256, 256, 512, 512, 512, SSD_XBC, SSD_HEADS, 512, 512, 512,
                   N_BRANCH * D_MODEL])
    mq, mk, mv, rq, rk, rv, rg, sz, sxbc, sdt, bq, bk, bv, gates = (
        w[:, p[i]:p[i + 1]] for i in range(14))
    w_main = jnp.concatenate([mq, mk, mv, rq, rk, rv, bq, bk, bv, gates, rg, sz, sxbc], axis=1)
    w_dt = jnp.pad(sdt, ((0, 0), (0, LANES - SSD_HEADS)))
    return w_main.astype(MXU_DTYPE), w_dt.astype(MXU_DTYPE)


def kernel(x, c, w_ada, b_ada, w_in, conv_w, conv_b, dt_bias, a_log, d_skip, ssm_norm_g, ret_gn_g, ret_gn_b, w_br, w_out, ln1_g, ln1_b, ln2_g, ln2_b, ffn_w_gu, ffn_w_down, router_w, router_b, expert_w_gu, expert_w_down):
    bsz = x.shape[0]
    mod = _ada(c, w_ada, b_ada)
    for l in range(DEPTH):
        sh1, sc1, g1, sh2, sc2, g2 = (
            mod[l, :, i * D_MODEL:(i + 1) * D_MODEL].reshape(bsz, 1, D_MODEL) for i in range(6))
        w_main, w_dt = _split_w_in(w_in[l])
        proj = _inproj(x, sc1, sh1, w_main, MXU_DTYPE, tn=2048)
        proj_dt = _inproj(x, sc1, sh1, w_dt, F32, tn=LANES)
        o_a = _moba(proj)
        o_b, o_c = _scans(proj, proj_dt, ret_gn_g[l], ret_gn_b[l], conv_w[l], conv_b[l],
                          dt_bias[l], a_log[l], d_skip[l], ssm_norm_g[l])
        o_d = _stick_breaking(proj)
        x = _merge(o_a, o_b, o_c, o_d, proj, x, g1, w_br[l].astype(MXU_DTYPE),
                   w_out[l].astype(MXU_DTYPE), ln1_g[l], ln1_b[l])
        if l % 2 == 0:
            x = _ffn(x, sc2, sh2, g2, ffn_w_gu[l // 2].astype(MXU_DTYPE),
                     ffn_w_down[l // 2].astype(MXU_DTYPE), ln2_g[l], ln2_b[l])
        else:
            x = _moe(x, sc2, sh2, g2, router_w[l // 2], router_b[l // 2],
                     expert_w_gu[l // 2].astype(MXU_DTYPE), expert_w_down[l // 2].astype(MXU_DTYPE),
                     ln2_g[l], ln2_b[l])
    return x
```

```python
import functools

import numpy as np
import jax
import jax.numpy as jnp
from jax import lax
from jax.experimental import pallas as pl
from jax.experimental.pallas import tpu as pltpu

F32 = jnp.float32
MXU_DTYPE = jnp.bfloat16

D_MODEL = 1024
DEPTH = 2
MOBA_HEADS = 8
MOBA_BLOCK = 256
MOBA_TOPK = 3
RET_HEADS = 4
RET_DK = 64
RET_DV = 128
RET_CHUNK = 256
SSD_D_INNER = 512
SSD_HEAD_DIM = 64
SSD_HEADS = 8
SSD_GROUPS = 2
SSD_STATE = 128
SSD_CONV = 4
SSD_CHUNK = 256
SSD_XBC = SSD_D_INNER + 2 * SSD_GROUPS * SSD_STATE
SSD_HALO = 8
SB_HEADS = 8
SB_BLOCK = 256
HEAD_DIM = 64
ATT_HEADS_PER_STEP = 8
ATT_W = ATT_HEADS_PER_STEP * HEAD_DIM
BRANCH_W = 512
N_BRANCH = 4
D_FF = 2816
N_EXPERTS = 8
D_FF_EXPERT = 3584
DEEPNORM_ALPHA = (2.0 * DEPTH) ** 0.25
LN_EPS = 1e-5
NORM_EPS = 1e-6
NEG_INF = -1e30
LOG2E = 1.4426950408889634
LANES = 128
VMEM_LIMIT = 48 * 2 ** 20
MOE_VMEM_LIMIT = 56 * 2 ** 20
MOE_CHUNK = 256
MOE_ROWS = 256
MOE_ROW_STEP = 64

A_MOBA_Q, A_MOBA_K, A_MOBA_V = 0, 512, 1024
A_RET_Q, A_RET_K, A_RET_V = 1536, 1792, 2048
A_SB_Q, A_SB_K, A_SB_V = 2560, 3072, 3584
B_GATES, B_RET_G, B_SSD_Z, B_XBC = 4096, 8192, 8704, 9216
PROJ_WIDTH = 10240


def _params(sem):
    return pltpu.CompilerParams(dimension_semantics=sem, vmem_limit_bytes=VMEM_LIMIT)


def _dot(a, b):
    return jnp.dot(a, b, preferred_element_type=F32)


def _dot_nt(a, b):
    return lax.dot_general(a, b, (((1,), (1,)), ((), ())), preferred_element_type=F32)


def _dot_tn(a, b):
    return lax.dot_general(a, b, (((0,), (0,)), ((), ())), preferred_element_type=F32)


def _split(x):
    hi = x.astype(MXU_DTYPE)
    lo = (x - hi.astype(F32)).astype(MXU_DTYPE)
    return hi, lo


def _dot3(a, b):
    ah, al = _split(a)
    bh, bl = _split(b)
    return _dot(ah, bh) + _dot(al, bh) + _dot(ah, bl)


def _sigmoid(x):
    return 1.0 / (1.0 + jnp.exp(-x))


def _silu(x):
    return x * _sigmoid(x)


def _resid_ln(x, gate, f, ln_g, ln_b):
    y = DEEPNORM_ALPHA * x + gate * f
    mu = jnp.mean(y, axis=-1, keepdims=True)
    d = y - mu
    var = jnp.mean(d * d, axis=-1, keepdims=True)
    return d * lax.rsqrt(var + LN_EPS) * ln_g + ln_b


def _ada_kernel(c_ref, w_ref, b_ref, o_ref):
    c = c_ref[...]
    o_ref[0] = _dot3(_silu(c), w_ref[0]) + b_ref[0]


def _ada(c, w_ada, b_ada):
    depth, d, n = w_ada.shape
    bsz = c.shape[0]
    tn = 1536
    return pl.pallas_call(
        _ada_kernel,
        grid=(depth, n // tn),
        in_specs=[pl.BlockSpec((bsz, d), lambda l, j: (0, 0)),
                  pl.BlockSpec((1, d, tn), lambda l, j: (l, 0, j)),
                  pl.BlockSpec((1, 1, tn), lambda l, j: (l, 0, j))],
        out_specs=pl.BlockSpec((1, bsz, tn), lambda l, j: (l, 0, j)),
        out_shape=jax.ShapeDtypeStruct((depth, bsz, n), F32),
        compiler_params=_params(("arbitrary", "arbitrary")),
        name="ada_mod",
    )(c, w_ada, b_ada.reshape(depth, 1, n))


def _inproj_kernel(x_ref, sc_ref, sh_ref, w_ref, o_ref, h_ref):
    @pl.when(pl.program_id(2) == 0)
    def _():
        h = x_ref[0] * (1.0 + sc_ref[0]) + sh_ref[0]
        h_ref[...] = h.astype(MXU_DTYPE)

    o_ref[0] = _dot(h_ref[...], w_ref[...]).astype(o_ref.dtype)


def _inproj(x, sc, sh, w, out_dtype, tn, tm=1024):
    bsz, s, d = x.shape
    n = w.shape[1]
    tm = min(tm, s)
    return pl.pallas_call(
        _inproj_kernel,
        grid=(bsz, s // tm, n // tn),
        in_specs=[pl.BlockSpec((1, tm, d), lambda b, i, j: (b, i, 0)),
                  pl.BlockSpec((1, 1, d), lambda b, i, j: (b, 0, 0)),
                  pl.BlockSpec((1, 1, d), lambda b, i, j: (b, 0, 0)),
                  pl.BlockSpec((d, tn), lambda b, i, j: (0, j))],
        out_specs=pl.BlockSpec((1, tm, tn), lambda b, i, j: (b, i, j)),
        out_shape=jax.ShapeDtypeStruct((bsz, s, n), out_dtype),
        scratch_shapes=[pltpu.VMEM((tm, d), MXU_DTYPE)],
        compiler_params=_params(("parallel", "parallel", "arbitrary")),
        name="in_proj",
    )(x, sc, sh, w)


def _in_lockstep(*gens):
    results = [None] * len(gens)
    live = list(range(len(gens)))
    while live:
        for i in list(live):
            try:
                next(gens[i])
            except StopIteration as stop:
                results[i] = stop.value
                live.remove(i)
    return results


def _moba_parts(q_ref, k_ref, v_ref, sl_ref, o_ref, km_ref, ka_ref, vt_ref, nb):
    blk = MOBA_BLOCK
    nh = ATT_HEADS_PER_STEP
    heads = range(nh)
    qi = pl.program_id(1)
    s_len = nb * blk
    nb_pad = -(-nb // 8) * 8
    extra = 2 * HEAD_DIM - HEAD_DIM
    assert nb_pad + 8 <= extra
    cols = [slice(h * HEAD_DIM, (h + 1) * HEAD_DIM) for h in heads]

    @pl.when(qi == 0)
    def _():
        k_all = k_ref[0]
        kk = k_all.astype(F32).reshape(nb, blk, ATT_W)
        km_ref[...] = jnp.zeros_like(km_ref)
        km_ref[0:nb, :] = jnp.sum(kk, axis=1) * (1.0 / blk)
        pos = lax.broadcasted_iota(jnp.int32, (s_len, extra), 0)
        lane = lax.broadcasted_iota(jnp.int32, (s_len, extra), 1)
        one_hot = jnp.where(lane == pos // blk, 1.0, 0.0)
        pos_hi = jnp.where(lane == nb_pad, (pos // blk * blk).astype(F32), 0.0)
        pos_lo = jnp.where(lane == nb_pad + 1, (pos % blk).astype(F32), 0.0)
        tail = (one_hot + pos_hi + pos_lo).astype(MXU_DTYPE)
        for h in heads:
            ka_ref[h] = jnp.concatenate([k_all[:, cols[h]], tail], axis=1)

        ones_rows = jnp.where(lax.broadcasted_iota(jnp.int32, (extra, blk), 0) == 0, 1.0, 0.0)

        def transpose_block(j, carry):
            j0 = pl.multiple_of(j * blk, blk)
            vt = v_ref[0, pl.ds(j0, blk), :].astype(F32).T
            for h in heads:
                vt_ref[j, h] = jnp.concatenate([vt[cols[h], :], ones_rows], axis=0).astype(MXU_DTYPE)
            return carry

        lax.fori_loop(0, nb, transpose_block, 0)

    key_i = lax.broadcasted_iota(jnp.int32, (blk, blk), 0)
    qry_i = lax.broadcasted_iota(jnp.int32, (blk, blk), 1)
    n_iota = lax.broadcasted_iota(jnp.int32, (nb_pad, blk), 0)
    past = n_iota < qi
    k_sel = min(MOBA_TOPK, max(nb - 1, 1))
    q_t = q_ref[0].astype(F32).T
    slope_rows = lax.broadcasted_iota(jnp.int32, (8, blk), 0) < 2

    qa = []
    for h in heads:
        qh_t = q_t[cols[h], :]
        slope = sl_ref[0, :, h * HEAD_DIM:h * HEAD_DIM + 1]
        km_hi, km_lo = _split(km_ref[:, cols[h]])
        qh_mxu = qh_t.astype(MXU_DTYPE)
        gate = (_dot(km_hi, qh_mxu) + _dot(km_lo, qh_mxu))[0:nb_pad, :]
        g = jnp.where(past, gate, NEG_INF)
        g = jnp.where(n_iota < nb, g, -jnp.inf)
        sel = jnp.zeros((nb_pad, blk), F32)
        for _ in range(k_sel):
            mx = jnp.max(g, axis=0, keepdims=True)
            idx = jnp.min(jnp.where(g == mx, n_iota, nb_pad), axis=0, keepdims=True)
            pick = n_iota == idx
            sel = jnp.where(pick, 1.0, sel)
            g = jnp.where(pick, -jnp.inf, g)
        keep = jnp.logical_or(jnp.logical_and(past, sel > 0.5), n_iota == qi)
        bias = jnp.where(keep, 0.0, NEG_INF)
        alibi = jnp.where(slope_rows, slope, 0.0)
        pad = jnp.zeros((extra - nb_pad - 8, blk), F32)
        qa.append(jnp.concatenate([qh_t * HEAD_DIM ** -0.5, bias, alibi, pad],
                                  axis=0).astype(MXU_DTYPE))

    def scores(js):
        starts = [pl.multiple_of(j * blk, blk) for j in js]
        return tuple(_dot(ka_ref[h, pl.ds(j0, blk), :], qa[h]) for j0 in starts for h in heads)

    def rest(js, flat_scores, carry, diag):
        s = [flat_scores[t * nh:(t + 1) * nh] for t in range(len(js))]
        m = list(carry[:nh]) if carry is not None else [None] * nh
        acc = list(carry[nh:]) if carry is not None else [None] * nh
        p = {}
        for h in heads:
            sh = [jnp.where(key_i <= qry_i, s[t][h], NEG_INF) if diag else s[t][h]
                  for t in range(len(js))]
            top = jnp.max(sh[0], axis=0, keepdims=True)
            for t in range(1, len(js)):
                top = jnp.maximum(top, jnp.max(sh[t], axis=0, keepdims=True))
            if m[h] is None:
                m[h] = top
            else:
                m_new = jnp.maximum(m[h], top)
                acc[h] = jnp.exp(m[h] - m_new) * acc[h]
                m[h] = m_new
            for t in range(len(js)):
                p[t, h] = jnp.exp(sh[t] - m[h]).astype(MXU_DTYPE)
        yield
        for t, j in enumerate(js):
            for h in heads:
                pv = _dot(vt_ref[j, h], p[t, h])
                acc[h] = pv if acc[h] is None else acc[h] + pv
        return tuple(m) + tuple(acc)

    def finish(carry):
        out_t = [carry[nh + h][0:HEAD_DIM, :] / carry[nh + h][HEAD_DIM:HEAD_DIM + 1, :]
                 for h in heads]
        o_ref[0] = jnp.concatenate(out_t, axis=0).T.astype(o_ref.dtype)

    return scores, rest, None, finish


def _sb_parts(q_ref, k_ref, v_ref, o_ref, vt_ref, nb):
    blk = SB_BLOCK
    nh = ATT_HEADS_PER_STEP
    heads = range(nh)
    qi = pl.program_id(1)
    cols = [slice(h * HEAD_DIM, (h + 1) * HEAD_DIM) for h in heads]

    @pl.when(qi == 0)
    def _():
        def transpose_block(j, carry):
            j0 = pl.multiple_of(j * blk, blk)
            vt = v_ref[0, pl.ds(j0, blk), :].astype(F32).T
            for h in heads:
                vt_ref[j, h] = vt[cols[h], :].astype(MXU_DTYPE)
            return carry

        lax.fori_loop(0, nb, transpose_block, 0)

    key_i = lax.broadcasted_iota(jnp.int32, (blk, blk), 0)
    qry_i = lax.broadcasted_iota(jnp.int32, (blk, blk), 1)
    causal = key_i < qry_i
    later = jnp.where(qry_i > key_i, 1.0, 0.0).astype(MXU_DTYPE)
    qs = [(q_ref[0, :, cols[h]].astype(F32) * HEAD_DIM ** -0.5).astype(MXU_DTYPE)
          for h in heads]

    def scores(js):
        starts = [pl.multiple_of(j * blk, blk) for j in js]
        return tuple(_dot_nt(k_ref[0, pl.ds(j0, blk), cols[h]], qs[h])
                     for j0 in starts for h in heads)

    def rest(js, flat_scores, carry, diag):
        accs, rests = carry[:nh], carry[nh:]
        z = [flat_scores[t * nh:(t + 1) * nh] for t in range(len(js))]
        log_beta, log_1mb, parts = [], [], []
        for t in range(len(js)):
            for h in heads:
                zz = z[t][h]
                lb = jnp.minimum(zz, 0.0) - jnp.log(1.0 + jnp.exp2(jnp.abs(zz) * (-LOG2E)))
                l1 = lb - zz
                if diag:
                    l1 = jnp.where(causal, l1, 0.0)
                log_beta.append(lb)
                log_1mb.append(l1)
                parts.append(l1.astype(MXU_DTYPE))
        yield
        rem = [_dot(later, p) for p in parts]
        yield
        w = []
        rests = list(rests)
        for t in range(len(js)):
            for h in heads:
                i = t * nh + h
                wh = jnp.exp(log_beta[i] + rem[i] + rests[h])
                if diag:
                    wh = jnp.where(causal, wh, 0.0)
                w.append(wh.astype(MXU_DTYPE))
                rests[h] = rests[h] + rem[i][0:1, :] + log_1mb[i][0:1, :]
        yield
        accs = list(accs)
        for t, j in enumerate(js):
            for h in heads:
                accs[h] = accs[h] + _dot(vt_ref[j, h], w[t * nh + h])
        return tuple(accs) + tuple(rests)

    start = tuple([jnp.zeros((HEAD_DIM, blk), F32)] * nh + [jnp.zeros((1, blk), F32)] * nh)

    def finish(carry):
        o_ref[0] = jnp.concatenate(carry[:nh], axis=0).T.astype(o_ref.dtype)

    return scores, rest, start, finish


def _walk_key_blocks(scores, rest, start, finish, z_ref=None, lead=0):
    qi = pl.program_id(1)
    run = lambda gen: _in_lockstep(gen)[0]

    state = run(rest([qi], scores([qi]), start, True))

    def single(t, state):
        return run(rest([qi - 1], scores([qi - 1]), state, False))

    state = lax.fori_loop(0, qi % 2, single, state)
    first = qi - 1 - qi % 2

    if z_ref is None:
        def pair(t, state):
            j = first - 2 * t
            return run(rest([j, j - 1], scores([j, j - 1]), state, False))
    else:
        def look_ahead(t):
            j = jnp.maximum(first - 2 * t, 1)
            for i, z in enumerate(scores([j, j - 1])):
                z_ref[i] = z

        def pair(t, state):
            j = first - 2 * t
            stages = rest([j, j - 1], [z_ref[i] for i in range(z_ref.shape[0])], state, False)
            for _ in range(lead):
                next(stages)
            look_ahead(t + 1)
            return run(stages)

        look_ahead(0)

    finish(lax.fori_loop(0, qi // 2, pair, state))


def _moba_kernel(q_ref, k_ref, v_ref, sl_ref, o_ref, km_ref, ka_ref, vt_ref, z_ref, *, nb):
    _walk_key_blocks(*_moba_parts(q_ref, k_ref, v_ref, sl_ref, o_ref, km_ref, ka_ref, vt_ref, nb),
                     z_ref=z_ref, lead=0)


def _sb_kernel(q_ref, k_ref, v_ref, o_ref, vt_ref, *, nb):
    _walk_key_blocks(*_sb_parts(q_ref, k_ref, v_ref, o_ref, vt_ref, nb))


def _attention_specs(s, q_col, k_col, v_col):
    blk = MOBA_BLOCK
    return ([pl.BlockSpec((1, blk, ATT_W), lambda b, i: (b, i, q_col // ATT_W)),
             pl.BlockSpec((1, s, ATT_W), lambda b, i: (b, 0, k_col // ATT_W)),
             pl.BlockSpec((1, s, ATT_W), lambda b, i: (b, 0, v_col // ATT_W))],
            pl.BlockSpec((1, blk, ATT_W), lambda b, i: (b, i, 0)))


def _moba(proj):
    bsz, s, _ = proj.shape
    assert MOBA_HEADS == ATT_HEADS_PER_STEP
    blk = MOBA_BLOCK
    nb = s // blk
    slopes = 2.0 ** (-8.0 * np.arange(1, MOBA_HEADS + 1) / MOBA_HEADS)
    sl = jnp.asarray(np.repeat(slopes, HEAD_DIM).reshape(1, 1, ATT_W), F32)
    in_specs, out_spec = _attention_specs(s, A_MOBA_Q, A_MOBA_K, A_MOBA_V)
    return pl.pallas_call(
        functools.partial(_moba_kernel, nb=nb),
        grid=(bsz, nb),
        in_specs=in_specs + [pl.BlockSpec((1, 1, ATT_W), lambda b, i: (0, 0, 0))],
        out_specs=out_spec,
        out_shape=jax.ShapeDtypeStruct((bsz, s, ATT_W), MXU_DTYPE),
        scratch_shapes=[pltpu.VMEM((LANES, ATT_W), F32),
                        pltpu.VMEM((ATT_HEADS_PER_STEP, s, 2 * HEAD_DIM), MXU_DTYPE),
                        pltpu.VMEM((nb, ATT_HEADS_PER_STEP, 2 * HEAD_DIM, blk), MXU_DTYPE),
                        pltpu.VMEM((2 * ATT_HEADS_PER_STEP, blk, blk), F32)],
        compiler_params=_params(("parallel", "arbitrary")),
        name="moba",
    )(proj, proj, proj, sl)


def _stick_breaking(proj):
    bsz, s, _ = proj.shape
    assert SB_HEADS == ATT_HEADS_PER_STEP and SB_BLOCK == MOBA_BLOCK
    blk = SB_BLOCK
    nb = s // blk
    in_specs, out_spec = _attention_specs(s, A_SB_Q, A_SB_K, A_SB_V)
    return pl.pallas_call(
        functools.partial(_sb_kernel, nb=nb),
        grid=(bsz, nb),
        in_specs=in_specs,
        out_specs=out_spec,
        out_shape=jax.ShapeDtypeStruct((bsz, s, ATT_W), MXU_DTYPE),
        scratch_shapes=[pltpu.VMEM((nb, ATT_HEADS_PER_STEP, HEAD_DIM, blk), MXU_DTYPE)],
        compiler_params=_params(("parallel", "arbitrary")),
        name="stick_breaking",
    )(proj, proj, proj)


def _ret_kernel(q_ref, k_ref, v_ref, g_ref, gn_g_ref, gn_b_ref, o_ref, st_ref):
    ch = RET_CHUNK

    @pl.when(pl.program_id(1) == 0)
    def _():
        st_ref[...] = jnp.zeros_like(st_ref)

    i_col = lax.broadcasted_iota(jnp.int32, (ch, 1), 0).astype(F32)
    diff = (lax.broadcasted_iota(jnp.int32, (ch, ch), 0)
            - lax.broadcasted_iota(jnp.int32, (ch, ch), 1)).astype(F32)
    heads = range(RET_HEADS)
    log_g = [float(np.log(1.0 - 2.0 ** (-5.0 - h))) for h in heads]
    qs = [(q_ref[0, :, h * RET_DK:(h + 1) * RET_DK].astype(F32) * RET_DK ** -0.5).astype(MXU_DTYPE)
          for h in heads]
    k = [k_ref[0, :, h * RET_DK:(h + 1) * RET_DK] for h in heads]
    v = [v_ref[0, :, h * RET_DV:(h + 1) * RET_DV] for h in heads]
    prev = [st_ref[h] for h in heads]
    qk = [_dot_nt(qs[h], k[h]) for h in heads]
    o_cross = [_dot(qs[h], prev[h].astype(MXU_DTYPE)) for h in heads]
    kd = [(k[h].astype(F32) * jnp.exp((ch - 1.0 - i_col) * log_g[h])).astype(MXU_DTYPE)
          for h in heads]
    st_new = [_dot_tn(kd[h], v[h]) for h in heads]
    qkd = []
    for h in heads:
        decay = jnp.where(diff >= 0, jnp.exp(jnp.maximum(diff, 0.0) * log_g[h]), 0.0)
        qkd.append((qk[h] * decay).astype(MXU_DTYPE))
    o_intra = [_dot(qkd[h], v[h]) for h in heads]
    for h in heads:
        st_ref[h] = float(np.exp(ch * log_g[h])) * prev[h] + st_new[h]
        o = o_intra[h] + o_cross[h] * jnp.exp((i_col + 1.0) * log_g[h])
        mu = jnp.mean(o, axis=-1, keepdims=True)
        d = o - mu
        var = jnp.mean(d * d, axis=-1, keepdims=True)
        cs = slice(h * RET_DV, (h + 1) * RET_DV)
        on = d * lax.rsqrt(var + NORM_EPS) * gn_g_ref[:, cs] + gn_b_ref[:, cs]
        o_ref[0, :, cs] = (_silu(g_ref[0, :, cs].astype(F32)) * on).astype(o_ref.dtype)


def _ssd_kernel(z_ref, xbc_ref, dt_ref, cw_ref, cb_ref, dtb_ref, alog_ref, dsk_ref, ng_ref,
                o_ref, tail_ref, st_ref, y_ref):
    ch = SSD_CHUNK
    heads_per_group = SSD_HEADS // SSD_GROUPS
    halo = SSD_HALO

    @pl.when(pl.program_id(1) == 0)
    def _():
        tail_ref[0:halo, :] = jnp.zeros((halo, SSD_XBC), F32)
        st_ref[...] = jnp.zeros_like(st_ref)

    tail_ref[halo:, :] = xbc_ref[0].astype(F32)
    conv = cb_ref[...]
    for kk in range(SSD_CONV):
        off = halo - (SSD_CONV - 1) + kk
        conv = conv + cw_ref[kk:kk + 1, :] * tail_ref[off:off + ch, :]
    tail_ref[0:halo, :] = tail_ref[ch:ch + halo, :]
    xc = _silu(conv)
    xs = xc[:, :SSD_D_INNER]
    bm = xc[:, SSD_D_INNER:SSD_D_INNER + SSD_GROUPS * SSD_STATE]
    cm = xc[:, SSD_D_INNER + SSD_GROUPS * SSD_STATE:]

    dtr = dt_ref[0] + dtb_ref[...]
    dt = jnp.maximum(dtr, 0.0) + jnp.log1p(jnp.exp(-jnp.abs(dtr)))
    a = -jnp.exp(alog_ref[...])
    r_iota = lax.broadcasted_iota(jnp.int32, (ch, ch), 0)
    c_iota = lax.broadcasted_iota(jnp.int32, (ch, ch), 1)
    causal = r_iota >= c_iota
    tri = jnp.where(causal, 1.0, 0.0).astype(MXU_DTYPE)
    da = dt * a
    da_hi, da_lo = _split(da)
    cum = _dot(tri, da_hi) + _dot(tri, da_lo)
    cum_t = cum.T
    dt_t = dt.T
    cum_last = cum[ch - 1:ch, :]
    decay_st = jnp.exp(cum_last - cum) * dt
    e_cum = jnp.exp(cum)
    e_last = jnp.exp(cum_last)

    for g in range(SSD_GROUPS):
        bm_g = bm[:, g * SSD_STATE:(g + 1) * SSD_STATE].astype(MXU_DTYPE)
        cm_g = cm[:, g * SSD_STATE:(g + 1) * SSD_STATE].astype(MXU_DTYPE)
        cb = _dot_nt(cm_g, bm_g)
        for r in range(heads_per_group):
            hd = g * heads_per_group + r
            cs = slice(hd * SSD_HEAD_DIM, (hd + 1) * SSD_HEAD_DIM)
            x_h = xs[:, cs]
            seg = cum[:, hd:hd + 1] - cum_t[hd:hd + 1, :]
            lmat = jnp.exp(jnp.where(causal, seg, NEG_INF))
            w = cb * lmat * dt_t[hd:hd + 1, :]
            y = _dot(w.astype(MXU_DTYPE), x_h.astype(MXU_DTYPE))
            prev = st_ref[hd]
            y = y + _dot(cm_g, prev.astype(MXU_DTYPE)) * e_cum[:, hd:hd + 1]
            xd = (x_h * decay_st[:, hd:hd + 1]).astype(MXU_DTYPE)
            st_ref[hd] = e_last[:, hd:hd + 1] * prev + _dot_tn(bm_g, xd)
            y_ref[:, cs] = y + dsk_ref[:, cs] * x_h

    y = y_ref[...] * _silu(z_ref[0].astype(F32))
    gw = SSD_D_INNER // SSD_GROUPS
    for g in range(SSD_GROUPS):
        cs = slice(g * gw, (g + 1) * gw)
        yg = y[:, cs]
        ms = jnp.mean(yg * yg, axis=-1, keepdims=True)
        o_ref[0, :, cs] = (yg * lax.rsqrt(ms + LN_EPS) * ng_ref[:, cs]).astype(o_ref.dtype)


def _scan_kernel(rq_ref, rk_ref, rv_ref, rg_ref, gn_g_ref, gn_b_ref,
                 z_ref, xbc_ref, dt_ref, cw_ref, cb_ref, dtb_ref, alog_ref, dsk_ref, ng_ref,
                 o_ret_ref, o_ssd_ref, ret_st_ref, tail_ref, ssd_st_ref, y_ref):
    _ret_kernel(rq_ref, rk_ref, rv_ref, rg_ref, gn_g_ref, gn_b_ref, o_ret_ref, ret_st_ref)
    _ssd_kernel(z_ref, xbc_ref, dt_ref, cw_ref, cb_ref, dtb_ref, alog_ref, dsk_ref, ng_ref,
                o_ssd_ref, tail_ref, ssd_st_ref, y_ref)


def _scans(proj, proj_dt, gn_g, gn_b, conv_w, conv_b, dt_bias, a_log, d_skip, norm_g):
    bsz, s, _ = proj.shape
    assert RET_CHUNK == SSD_CHUNK
    ch = SSD_CHUNK
    wq, wv = RET_HEADS * RET_DK, RET_HEADS * RET_DV
    pad = LANES - SSD_HEADS
    dtb = jnp.pad(dt_bias, (0, pad)).reshape(1, LANES)
    alog = jnp.pad(a_log, (0, pad)).reshape(1, LANES)
    dsk = jnp.repeat(d_skip, SSD_HEAD_DIM).reshape(1, SSD_D_INNER)
    const = lambda b, i: (0, 0)
    return pl.pallas_call(
        _scan_kernel,
        grid=(bsz, s // ch),
        in_specs=[pl.BlockSpec((1, ch, wq), lambda b, i: (b, i, A_RET_Q // wq)),
                  pl.BlockSpec((1, ch, wq), lambda b, i: (b, i, A_RET_K // wq)),
                  pl.BlockSpec((1, ch, wv), lambda b, i: (b, i, A_RET_V // wv)),
                  pl.BlockSpec((1, ch, wv), lambda b, i: (b, i, B_RET_G // wv)),
                  pl.BlockSpec((1, wv), const),
                  pl.BlockSpec((1, wv), const),
                  pl.BlockSpec((1, ch, SSD_D_INNER), lambda b, i: (b, i, B_SSD_Z // SSD_D_INNER)),
                  pl.BlockSpec((1, ch, SSD_XBC), lambda b, i: (b, i, B_XBC // SSD_XBC)),
                  pl.BlockSpec((1, ch, LANES), lambda b, i: (b, i, 0)),
                  pl.BlockSpec((SSD_CONV, SSD_XBC), const),
                  pl.BlockSpec((1, SSD_XBC), const),
                  pl.BlockSpec((1, LANES), const),
                  pl.BlockSpec((1, LANES), const),
                  pl.BlockSpec((1, SSD_D_INNER), const),
                  pl.BlockSpec((1, SSD_D_INNER), const)],
        out_specs=[pl.BlockSpec((1, ch, wv), lambda b, i: (b, i, 0)),
                   pl.BlockSpec((1, ch, SSD_D_INNER), lambda b, i: (b, i, 0))],
        out_shape=[jax.ShapeDtypeStruct((bsz, s, wv), MXU_DTYPE),
                   jax.ShapeDtypeStruct((bsz, s, SSD_D_INNER), MXU_DTYPE)],
        scratch_shapes=[pltpu.VMEM((RET_HEADS, RET_DK, RET_DV), F32),
                        pltpu.VMEM((SSD_HALO + ch, SSD_XBC), F32),
                        pltpu.VMEM((SSD_HEADS, SSD_STATE, SSD_HEAD_DIM), F32),
                        pltpu.VMEM((ch, SSD_D_INNER), F32)],
        compiler_params=_params(("parallel", "arbitrary")),
        name="retention_ssd",
    )(proj, proj, proj, proj, gn_g.reshape(1, wv), gn_b.reshape(1, wv),
      proj, proj, proj_dt, conv_w, conv_b.reshape(1, SSD_XBC), dtb, alog, dsk,
      norm_g.reshape(1, SSD_D_INNER))


def _merge_kernel(oa_ref, ob_ref, oc_ref, od_ref, gates_ref, x_ref, g1_ref, wbr_ref, wout_ref,
                  lg_ref, lb_ref, o_ref):
    merged = None
    for n, br in enumerate((oa_ref, ob_ref, oc_ref, od_ref)):
        y = _dot(br[0], wbr_ref[n])
        t = _sigmoid(gates_ref[0, :, n * D_MODEL:(n + 1) * D_MODEL].astype(F32)) * y
        merged = t if merged is None else merged + t
    mix = _dot(merged.astype(MXU_DTYPE), wout_ref[...])
    o_ref[0] = _resid_ln(x_ref[0], g1_ref[0], mix, lg_ref[...], lb_ref[...])


def _merge(o_a, o_b, o_c, o_d, proj, x, g1, w_br, w_out, ln_g, ln_b, tm=512):
    bsz, s, d = x.shape
    tm = min(tm, s)
    gw = N_BRANCH * d
    row = lambda b, i: (b, i, 0)
    const2 = lambda b, i: (0, 0)
    br_spec = pl.BlockSpec((1, tm, BRANCH_W), row)
    return pl.pallas_call(
        _merge_kernel,
        grid=(bsz, s // tm),
        in_specs=[br_spec, br_spec, br_spec, br_spec,
                  pl.BlockSpec((1, tm, gw), lambda b, i: (b, i, B_GATES // gw)),
                  pl.BlockSpec((1, tm, d), row),
                  pl.BlockSpec((1, 1, d), lambda b, i: (b, 0, 0)),
                  pl.BlockSpec((N_BRANCH, BRANCH_W, d), lambda b, i: (0, 0, 0)),
                  pl.BlockSpec((d, d), const2),
                  pl.BlockSpec((1, d), const2),
                  pl.BlockSpec((1, d), const2)],
        out_specs=pl.BlockSpec((1, tm, d), row),
        out_shape=jax.ShapeDtypeStruct((bsz, s, d), F32),
        compiler_params=_params(("parallel", "parallel")),
        name="merge",
    )(o_a, o_b, o_c, o_d, proj, x, g1, w_br, w_out, ln_g.reshape(1, d), ln_b.reshape(1, d))


def _ffn_kernel(x_ref, sc_ref, sh_ref, g2_ref, wg_ref, wu_ref, wd_ref, lg_ref, lb_ref,
                o_ref, h_ref, acc_ref):
    j = pl.program_id(2)

    @pl.when(j == 0)
    def _():
        h = x_ref[0] * (1.0 + sc_ref[0]) + sh_ref[0]
        h_ref[...] = h.astype(MXU_DTYPE)
        acc_ref[...] = jnp.zeros_like(acc_ref)

    h = h_ref[...]
    act = _silu(_dot(h, wg_ref[...])) * _dot(h, wu_ref[...])
    acc_ref[...] += _dot(act.astype(MXU_DTYPE), wd_ref[...])

    @pl.when(j == pl.num_programs(2) - 1)
    def _():
        o_ref[0] = _resid_ln(x_ref[0], g2_ref[0], acc_ref[...], lg_ref[...], lb_ref[...])


def _ffn(x, sc, sh, g2, w_gu, w_down, ln_g, ln_b, tm=512, tf=1408):
    bsz, s, d = x.shape
    tm = min(tm, s)
    nf = D_FF // tf
    row = lambda b, i, j: (b, i, 0)
    vec = lambda b, i, j: (b, 0, 0)
    const2 = lambda b, i, j: (0, 0)
    return pl.pallas_call(
        _ffn_kernel,
        grid=(bsz, s // tm, nf),
        in_specs=[pl.BlockSpec((1, tm, d), row),
                  pl.BlockSpec((1, 1, d), vec), pl.BlockSpec((1, 1, d), vec),
                  pl.BlockSpec((1, 1, d), vec),
                  pl.BlockSpec((d, tf), lambda b, i, j: (0, j)),
                  pl.BlockSpec((d, tf), lambda b, i, j: (0, nf + j)),
                  pl.BlockSpec((tf, d), lambda b, i, j: (j, 0)),
                  pl.BlockSpec((1, d), const2), pl.BlockSpec((1, d), const2)],
        out_specs=pl.BlockSpec((1, tm, d), row),
        out_shape=jax.ShapeDtypeStruct((bsz, s, d), F32),
        scratch_shapes=[pltpu.VMEM((tm, d), MXU_DTYPE), pltpu.VMEM((tm, d), F32)],
        compiler_params=_params(("parallel", "parallel", "arbitrary")),
        name="ffn_dense",
    )(x, sc, sh, g2, w_gu, w_gu, w_down, ln_g.reshape(1, d), ln_b.reshape(1, d))


def _router_kernel(x_ref, sc_ref, sh_ref, rw_ref, rb_ref,
                   h_ref, comb_ref, slot_ref, slot_t_ref, cnt_ref):
    tm = x_ref.shape[1]
    lane = lax.broadcasted_iota(jnp.int32, (tm, LANES), 1)
    h = x_ref[0] * (1.0 + sc_ref[0]) + sh_ref[0]
    h_ref[0] = h.astype(MXU_DTYPE)
    logits = _dot3(h, rw_ref[...]) + rb_ref[...]
    logits = jnp.where(lane < N_EXPERTS, logits, -jnp.inf)
    v1 = jnp.max(logits, axis=1, keepdims=True)
    i1 = jnp.min(jnp.where(logits == v1, lane, LANES), axis=1, keepdims=True)
    rest = jnp.where(lane == i1, -jnp.inf, logits)
    v2 = jnp.max(rest, axis=1, keepdims=True)
    i2 = jnp.min(jnp.where(rest == v2, lane, LANES), axis=1, keepdims=True)
    e2 = jnp.exp(v2 - v1)
    w1 = 1.0 / (1.0 + e2)
    w2 = e2 / (1.0 + e2)
    comb_ref[0] = jnp.where(lane == i1, w1, 0.0) + jnp.where(lane == i2, w2, 0.0)
    routed = jnp.logical_or(lane == i1, lane == i2)
    earlier = (lax.broadcasted_iota(jnp.int32, (tm, tm), 1)
               < lax.broadcasted_iota(jnp.int32, (tm, tm), 0))
    before = _dot(jnp.where(earlier, 1.0, 0.0).astype(MXU_DTYPE),
                  jnp.where(routed, 1.0, 0.0).astype(MXU_DTYPE))
    slot = jnp.where(routed, before, -1.0)
    slot_ref[0] = slot
    slot_t_ref[0] = slot.T[0:N_EXPERTS, :]
    cnt_ref[0] = jnp.sum(jnp.where(routed, 1, 0), axis=0, keepdims=True)


def _moe_kernel(cnt_ref, h_ref, x_ref, g2_ref, comb_ref, slot_ref, slot_t_ref,
                wg_ref, wu_ref, wd_ref, lg_ref, lb_ref, o_ref, xe_ref, ye_ref, *, n_tiles):
    tm = x_ref.shape[1]
    ch = MOE_CHUNK
    e = pl.program_id(2)
    j = pl.program_id(3)
    last_j = pl.num_programs(3) - 1
    count = cnt_ref[(pl.program_id(0) * n_tiles + pl.program_id(1)) * N_EXPERTS + e]
    n_chunks = (count + ch - 1) // ch

    @pl.when(jnp.logical_and(e == 0, j == 0))
    def _():
        o_ref[0] = jnp.zeros((tm, D_MODEL), F32)

    @pl.when(j == 0)
    def _():
        slot_row = slot_t_ref[0, pl.ds(e, 1), :]
        row = lax.broadcasted_iota(jnp.int32, (ch, tm), 0).astype(F32)

        def gather(c, carry):
            r0 = pl.multiple_of(c * ch, ch)
            pick = jnp.where(slot_row == row + (c * ch).astype(F32), 1.0, 0.0).astype(MXU_DTYPE)
            xe_ref[pl.ds(r0, ch), :] = _dot(pick, h_ref[0]).astype(MXU_DTYPE)
            ye_ref[pl.ds(r0, ch), :] = jnp.zeros((ch, D_MODEL), F32)
            return carry

        lax.fori_loop(0, n_chunks, gather, 0)

    def expert_rows(r0, rows):
        xc = xe_ref[pl.ds(r0, rows), :]
        act = _silu(_dot(xc, wg_ref[0])) * _dot(xc, wu_ref[0])
        ye_ref[pl.ds(r0, rows), :] += _dot(act.astype(MXU_DTYPE), wd_ref[0])

    n_full = count // MOE_ROWS

    def full_piece(c, carry):
        expert_rows(pl.multiple_of(c * MOE_ROWS, MOE_ROWS), MOE_ROWS)
        return carry

    lax.fori_loop(0, n_full, full_piece, 0)
    tail0 = pl.multiple_of(n_full * MOE_ROWS, MOE_ROWS)
    tail_steps = (count - n_full * MOE_ROWS + MOE_ROW_STEP - 1) // MOE_ROW_STEP
    for steps in range(1, MOE_ROWS // MOE_ROW_STEP + 1):
        @pl.when(tail_steps == steps)
        def _(steps=steps):
            expert_rows(tail0, steps * MOE_ROW_STEP)

    @pl.when(j == last_j)
    def _():
        lane = lax.broadcasted_iota(jnp.int32, (tm, LANES), 1)
        mine = lane == e
        slot_col = jnp.sum(jnp.where(mine, slot_ref[0], 0.0), axis=1, keepdims=True)
        comb_col = jnp.sum(jnp.where(mine, comb_ref[0], 0.0), axis=1, keepdims=True)
        col = lax.broadcasted_iota(jnp.int32, (tm, ch), 1).astype(F32)

        def scatter(c, carry):
            r0 = pl.multiple_of(c * ch, ch)
            place = jnp.where(slot_col == col + (c * ch).astype(F32), 1.0, 0.0).astype(MXU_DTYPE)
            o_ref[0] += comb_col * _dot(place, ye_ref[pl.ds(r0, ch), :].astype(MXU_DTYPE))
            return carry

        lax.fori_loop(0, n_chunks, scatter, 0)

    @pl.when(jnp.logical_and(e == pl.num_programs(2) - 1, j == last_j))
    def _():
        o_ref[0] = _resid_ln(x_ref[0], g2_ref[0], o_ref[0], lg_ref[...], lb_ref[...])


def _moe(x, sc, sh, g2, router_w, router_b, w_gu, w_down, ln_g, ln_b, tm=1024, tf=1792):
    bsz, s, d = x.shape
    tm = min(tm, s)
    nt = s // tm
    nf = D_FF_EXPERT // tf
    rw = jnp.pad(router_w, ((0, 0), (0, LANES - N_EXPERTS)))
    rb = jnp.pad(router_b, (0, LANES - N_EXPERTS)).reshape(1, LANES)

    row2 = lambda b, i: (b, i, 0)
    vec2 = lambda b, i: (b, 0, 0)
    h, comb, slot, slot_t, cnt = pl.pallas_call(
        _router_kernel,
        grid=(bsz, nt),
        in_specs=[pl.BlockSpec((1, tm, d), row2),
                  pl.BlockSpec((1, 1, d), vec2), pl.BlockSpec((1, 1, d), vec2),
                  pl.BlockSpec((d, LANES), lambda b, i: (0, 0)),
                  pl.BlockSpec((1, LANES), lambda b, i: (0, 0))],
        out_specs=[pl.BlockSpec((1, tm, d), row2),
                   pl.BlockSpec((1, tm, LANES), row2),
                   pl.BlockSpec((1, tm, LANES), row2),
                   pl.BlockSpec((1, N_EXPERTS, tm), lambda b, i: (b * nt + i, 0, 0)),
                   pl.BlockSpec((1, 1, LANES), lambda b, i: (b * nt + i, 0, 0))],
        out_shape=[jax.ShapeDtypeStruct((bsz, s, d), MXU_DTYPE),
                   jax.ShapeDtypeStruct((bsz, s, LANES), F32),
                   jax.ShapeDtypeStruct((bsz, s, LANES), F32),
                   jax.ShapeDtypeStruct((bsz * nt, N_EXPERTS, tm), F32),
                   jax.ShapeDtypeStruct((bsz * nt, 1, LANES), jnp.int32)],
        compiler_params=_params(("parallel", "parallel")),
        name="moe_router",
    )(x, sc, sh, rw, rb)
    counts = cnt[:, 0, :N_EXPERTS].reshape(bsz * nt * N_EXPERTS)

    row = lambda b, i, e, j, cnt: (b, i, 0)
    vec = lambda b, i, e, j, cnt: (b, 0, 0)
    const2 = lambda b, i, e, j, cnt: (0, 0)
    once = pl.Buffered(1)
    grid_spec = pltpu.PrefetchScalarGridSpec(
        num_scalar_prefetch=1,
        grid=(bsz, nt, N_EXPERTS, nf),
        in_specs=[pl.BlockSpec((1, tm, d), row, pipeline_mode=once),
                  pl.BlockSpec((1, tm, d), row, pipeline_mode=once),
                  pl.BlockSpec((1, 1, d), vec),
                  pl.BlockSpec((1, tm, LANES), row, pipeline_mode=once),
                  pl.BlockSpec((1, tm, LANES), row, pipeline_mode=once),
                  pl.BlockSpec((1, N_EXPERTS, tm), lambda b, i, e, j, cnt: (b * nt + i, 0, 0)),
                  pl.BlockSpec((1, d, tf), lambda b, i, e, j, cnt: (e, 0, j)),
                  pl.BlockSpec((1, d, tf), lambda b, i, e, j, cnt: (e, 0, nf + j)),
                  pl.BlockSpec((1, tf, d), lambda b, i, e, j, cnt: (e, j, 0)),
                  pl.BlockSpec((1, d), const2), pl.BlockSpec((1, d), const2)],
        out_specs=pl.BlockSpec((1, tm, d), row),
        scratch_shapes=[pltpu.VMEM((tm, d), MXU_DTYPE), pltpu.VMEM((tm, d), F32)])
    return pl.pallas_call(
        functools.partial(_moe_kernel, n_tiles=nt),
        grid_spec=grid_spec,
        out_shape=jax.ShapeDtypeStruct((bsz, s, d), F32),
        compiler_params=pltpu.CompilerParams(
            dimension_semantics=("parallel", "parallel", "arbitrary", "arbitrary"),
            vmem_limit_bytes=MOE_VMEM_LIMIT),
        name="moe_experts",
    )(counts, h, x, g2, comb, slot, slot_t, w_gu, w_gu, w_down,
      ln_g.reshape(1, d), ln_b.reshape(1, d))


def _split_w_in(w):
    p = np.cumsum([0, 512, 512, 512, 256, 256, 512, 512, 512, SSD_XBC, SSD_HEADS, 512, 512, 512,
                   N_BRANCH * D_MODEL])
    mq, mk, mv, rq, rk, rv, rg, sz, sxbc, sdt, bq, bk, bv, gates = (
        w[:, p[i]:p[i + 1]] for i in range(14))
    w_main = jnp.concatenate([mq, mk, mv, rq, rk, rv, bq, bk, bv, gates, rg, sz, sxbc], axis=1)
    w_dt = jnp.pad(sdt, ((0, 0), (0, LANES - SSD_HEADS)))
    return w_main.astype(MXU_DTYPE), w_dt.astype(MXU_DTYPE)


def kernel(x, c, w_ada, b_ada, w_in, conv_w, conv_b, dt_bias, a_log, d_skip, ssm_norm_g, ret_gn_g, ret_gn_b, w_br, w_out, ln1_g, ln1_b, ln2_g, ln2_b, ffn_w_gu, ffn_w_down, router_w, router_b, expert_w_gu, expert_w_down):
    bsz = x.shape[0]
    mod = _ada(c, w_ada, b_ada)
    for l in range(DEPTH):
        sh1, sc1, g1, sh2, sc2, g2 = (
            mod[l, :, i * D_MODEL:(i + 1) * D_MODEL].reshape(bsz, 1, D_MODEL) for i in range(6))
        w_main, w_dt = _split_w_in(w_in[l])
        proj = _inproj(x, sc1, sh1, w_main, MXU_DTYPE, tn=2048)
        proj_dt = _inproj(x, sc1, sh1, w_dt, F32, tn=LANES)
        o_a = _moba(proj)
        o_d = _stick_breaking(proj)
        o_b, o_c = _scans(proj, proj_dt, ret_gn_g[l], ret_gn_b[l], conv_w[l], conv_b[l],
                          dt_bias[l], a_log[l], d_skip[l], ssm_norm_g[l])
        x = _merge(o_a, o_b, o_c, o_d, proj, x, g1, w_br[l].astype(MXU_DTYPE),
                   w_out[l].astype(MXU_DTYPE), ln1_g[l], ln1_b[l])
        if l % 2 == 0:
            x = _ffn(x, sc2, sh2, g2, ffn_w_gu[l // 2].astype(MXU_DTYPE),
                     ffn_w_down[l // 2].astype(MXU_DTYPE), ln2_g[l], ln2_b[l])
        else:
            x = _moe(x, sc2, sh2, g2, router_w[l // 2], router_b[l // 2],
                     expert_w_gu[l // 2].astype(MXU_DTYPE), expert_w_down[l // 2].astype(MXU_DTYPE),
                     ln2_g[l], ln2_b[l])
    return x
```

```python
import functools

import numpy as np
import jax
import jax.numpy as jnp
from jax import lax
from jax.experimental import pallas as pl
from jax.experimental.pallas import tpu as pltpu

F32 = jnp.float32
MXU_DTYPE = jnp.bfloat16

D_MODEL = 1024
DEPTH = 2
MOBA_HEADS = 8
MOBA_BLOCK = 256
MOBA_TOPK = 3
RET_HEADS = 4
RET_DK = 64
RET_DV = 128
RET_CHUNK = 256
SSD_D_INNER = 512
SSD_HEAD_DIM = 64
SSD_HEADS = 8
SSD_GROUPS = 2
SSD_STATE = 128
SSD_CONV = 4
SSD_CHUNK = 256
SSD_XBC = SSD_D_INNER + 2 * SSD_GROUPS * SSD_STATE
SSD_HALO = 8
SB_HEADS = 8
SB_BLOCK = 256
HEAD_DIM = 64
ATT_HEADS_PER_STEP = 8
ATT_W = ATT_HEADS_PER_STEP * HEAD_DIM
BRANCH_W = 512
N_BRANCH = 4
D_FF = 2816
N_EXPERTS = 8
D_FF_EXPERT = 3584
DEEPNORM_ALPHA = (2.0 * DEPTH) ** 0.25
LN_EPS = 1e-5
NORM_EPS = 1e-6
NEG_INF = -1e30
LOG2E = 1.4426950408889634
LANES = 128
VMEM_LIMIT = 48 * 2 ** 20
MOE_VMEM_LIMIT = 56 * 2 ** 20
MOE_CHUNK = 256
MOE_ROWS = 256
MOE_ROW_STEP = 64

A_MOBA_Q, A_MOBA_K, A_MOBA_V = 0, 512, 1024
A_RET_Q, A_RET_K, A_RET_V = 1536, 1792, 2048
A_SB_Q, A_SB_K, A_SB_V = 2560, 3072, 3584
B_GATES, B_RET_G, B_SSD_Z, B_XBC = 4096, 8192, 8704, 9216
PROJ_WIDTH = 10240


def _params(sem):
    return pltpu.CompilerParams(dimension_semantics=sem, vmem_limit_bytes=VMEM_LIMIT)


def _dot(a, b):
    return jnp.dot(a, b, preferred_element_type=F32)


def _dot_nt(a, b):
    return lax.dot_general(a, b, (((1,), (1,)), ((), ())), preferred_element_type=F32)


def _dot_tn(a, b):
    return lax.dot_general(a, b, (((0,), (0,)), ((), ())), preferred_element_type=F32)


def _split(x):
    hi = x.astype(MXU_DTYPE)
    lo = (x - hi.astype(F32)).astype(MXU_DTYPE)
    return hi, lo


def _dot3(a, b):
    ah, al = _split(a)
    bh, bl = _split(b)
    return _dot(ah, bh) + _dot(al, bh) + _dot(ah, bl)


def _sigmoid(x):
    return 1.0 / (1.0 + jnp.exp(-x))


def _silu(x):
    return x * _sigmoid(x)


def _resid_ln(x, gate, f, ln_g, ln_b):
    y = DEEPNORM_ALPHA * x + gate * f
    mu = jnp.mean(y, axis=-1, keepdims=True)
    d = y - mu
    var = jnp.mean(d * d, axis=-1, keepdims=True)
    return d * lax.rsqrt(var + LN_EPS) * ln_g + ln_b


def _ada_kernel(c_ref, w_ref, b_ref, o_ref):
    c = c_ref[...]
    o_ref[0] = _dot3(_silu(c), w_ref[0]) + b_ref[0]


def _ada(c, w_ada, b_ada):
    depth, d, n = w_ada.shape
    bsz = c.shape[0]
    tn = 1536
    return pl.pallas_call(
        _ada_kernel,
        grid=(depth, n // tn),
        in_specs=[pl.BlockSpec((bsz, d), lambda l, j: (0, 0)),
                  pl.BlockSpec((1, d, tn), lambda l, j: (l, 0, j)),
                  pl.BlockSpec((1, 1, tn), lambda l, j: (l, 0, j))],
        out_specs=pl.BlockSpec((1, bsz, tn), lambda l, j: (l, 0, j)),
        out_shape=jax.ShapeDtypeStruct((depth, bsz, n), F32),
        compiler_params=_params(("arbitrary", "arbitrary")),
        name="ada_mod",
    )(c, w_ada, b_ada.reshape(depth, 1, n))


def _inproj_kernel(x_ref, sc_ref, sh_ref, w_ref, o_ref, h_ref):
    @pl.when(pl.program_id(2) == 0)
    def _():
        h = x_ref[0] * (1.0 + sc_ref[0]) + sh_ref[0]
        h_ref[...] = h.astype(MXU_DTYPE)

    o_ref[0] = _dot(h_ref[...], w_ref[...]).astype(o_ref.dtype)


def _inproj(x, sc, sh, w, out_dtype, tn, tm=1024):
    bsz, s, d = x.shape
    n = w.shape[1]
    tm = min(tm, s)
    return pl.pallas_call(
        _inproj_kernel,
        grid=(bsz, s // tm, n // tn),
        in_specs=[pl.BlockSpec((1, tm, d), lambda b, i, j: (b, i, 0)),
                  pl.BlockSpec((1, 1, d), lambda b, i, j: (b, 0, 0)),
                  pl.BlockSpec((1, 1, d), lambda b, i, j: (b, 0, 0)),
                  pl.BlockSpec((d, tn), lambda b, i, j: (0, j))],
        out_specs=pl.BlockSpec((1, tm, tn), lambda b, i, j: (b, i, j)),
        out_shape=jax.ShapeDtypeStruct((bsz, s, n), out_dtype),
        scratch_shapes=[pltpu.VMEM((tm, d), MXU_DTYPE)],
        compiler_params=_params(("parallel", "parallel", "arbitrary")),
        name="in_proj",
    )(x, sc, sh, w)


def _in_lockstep(*gens):
    results = [None] * len(gens)
    live = list(range(len(gens)))
    while live:
        for i in list(live):
            try:
                next(gens[i])
            except StopIteration as stop:
                results[i] = stop.value
                live.remove(i)
    return results


def _moba_parts(q_ref, k_ref, v_ref, sl_ref, o_ref, km_ref, ka_ref, vt_ref, nb):
    blk = MOBA_BLOCK
    nh = ATT_HEADS_PER_STEP
    heads = range(nh)
    qi = pl.program_id(1)
    s_len = nb * blk
    nb_pad = -(-nb // 8) * 8
    extra = 2 * HEAD_DIM - HEAD_DIM
    assert nb_pad + 8 <= extra
    cols = [slice(h * HEAD_DIM, (h + 1) * HEAD_DIM) for h in heads]

    @pl.when(qi == 0)
    def _():
        k_all = k_ref[0]
        kk = k_all.astype(F32).reshape(nb, blk, ATT_W)
        km_ref[...] = jnp.zeros_like(km_ref)
        km_ref[0:nb, :] = jnp.sum(kk, axis=1) * (1.0 / blk)
        pos = lax.broadcasted_iota(jnp.int32, (s_len, extra), 0)
        lane = lax.broadcasted_iota(jnp.int32, (s_len, extra), 1)
        one_hot = jnp.where(lane == pos // blk, 1.0, 0.0)
        pos_hi = jnp.where(lane == nb_pad, (pos // blk * blk).astype(F32), 0.0)
        pos_lo = jnp.where(lane == nb_pad + 1, (pos % blk).astype(F32), 0.0)
        tail = (one_hot + pos_hi + pos_lo).astype(MXU_DTYPE)
        for h in heads:
            ka_ref[h] = jnp.concatenate([k_all[:, cols[h]], tail], axis=1)

        ones_rows = jnp.where(lax.broadcasted_iota(jnp.int32, (extra, blk), 0) == 0, 1.0, 0.0)

        def transpose_block(j, carry):
            j0 = pl.multiple_of(j * blk, blk)
            vt = v_ref[0, pl.ds(j0, blk), :].astype(F32).T
            for h in heads:
                vt_ref[j, h] = jnp.concatenate([vt[cols[h], :], ones_rows], axis=0).astype(MXU_DTYPE)
            return carry

        lax.fori_loop(0, nb, transpose_block, 0)

    key_i = lax.broadcasted_iota(jnp.int32, (blk, blk), 0)
    qry_i = lax.broadcasted_iota(jnp.int32, (blk, blk), 1)
    n_iota = lax.broadcasted_iota(jnp.int32, (nb_pad, blk), 0)
    past = n_iota < qi
    k_sel = min(MOBA_TOPK, max(nb - 1, 1))
    q_t = q_ref[0].astype(F32).T
    slope_rows = lax.broadcasted_iota(jnp.int32, (8, blk), 0) < 2

    qa = []
    for h in heads:
        qh_t = q_t[cols[h], :]
        slope = sl_ref[0, :, h * HEAD_DIM:h * HEAD_DIM + 1]
        km_hi, km_lo = _split(km_ref[:, cols[h]])
        qh_mxu = qh_t.astype(MXU_DTYPE)
        gate = (_dot(km_hi, qh_mxu) + _dot(km_lo, qh_mxu))[0:nb_pad, :]
        g = jnp.where(past, gate, NEG_INF)
        g = jnp.where(n_iota < nb, g, -jnp.inf)
        sel = jnp.zeros((nb_pad, blk), F32)
        for _ in range(k_sel):
            mx = jnp.max(g, axis=0, keepdims=True)
            idx = jnp.min(jnp.where(g == mx, n_iota, nb_pad), axis=0, keepdims=True)
            pick = n_iota == idx
            sel = jnp.where(pick, 1.0, sel)
            g = jnp.where(pick, -jnp.inf, g)
        keep = jnp.logical_or(jnp.logical_and(past, sel > 0.5), n_iota == qi)
        bias = jnp.where(keep, 0.0, NEG_INF)
        alibi = jnp.where(slope_rows, slope, 0.0)
        pad = jnp.zeros((extra - nb_pad - 8, blk), F32)
        qa.append(jnp.concatenate([qh_t * HEAD_DIM ** -0.5, bias, alibi, pad],
                                  axis=0).astype(MXU_DTYPE))

    def scores(js):
        starts = [pl.multiple_of(j * blk, blk) for j in js]
        return tuple(_dot(ka_ref[h, pl.ds(j0, blk), :], qa[h]) for j0 in starts for h in heads)

    def rest(js, flat_scores, carry, diag):
        s = [flat_scores[t * nh:(t + 1) * nh] for t in range(len(js))]
        m = list(carry[:nh]) if carry is not None else [None] * nh
        acc = list(carry[nh:]) if carry is not None else [None] * nh
        p = {}
        for h in heads:
            sh = [jnp.where(key_i <= qry_i, s[t][h], NEG_INF) if diag else s[t][h]
                  for t in range(len(js))]
            top = jnp.max(sh[0], axis=0, keepdims=True)
            for t in range(1, len(js)):
                top = jnp.maximum(top, jnp.max(sh[t], axis=0, keepdims=True))
            if m[h] is None:
                m[h] = top
            else:
                m_new = jnp.maximum(m[h], top)
                acc[h] = jnp.exp(m[h] - m_new) * acc[h]
                m[h] = m_new
            for t in range(len(js)):
                p[t, h] = jnp.exp(sh[t] - m[h]).astype(MXU_DTYPE)
        yield
        for t, j in enumerate(js):
            for h in heads:
                pv = _dot(vt_ref[j, h], p[t, h])
                acc[h] = pv if acc[h] is None else acc[h] + pv
        return tuple(m) + tuple(acc)

    def finish(carry):
        out_t = [carry[nh + h][0:HEAD_DIM, :] / carry[nh + h][HEAD_DIM:HEAD_DIM + 1, :]
                 for h in heads]
        o_ref[0] = jnp.concatenate(out_t, axis=0).T.astype(o_ref.dtype)

    return scores, rest, None, finish


def _sb_parts(q_ref, k_ref, v_ref, o_ref, vt_ref, nb):
    blk = SB_BLOCK
    nh = ATT_HEADS_PER_STEP
    heads = range(nh)
    qi = pl.program_id(1)
    cols = [slice(h * HEAD_DIM, (h + 1) * HEAD_DIM) for h in heads]

    @pl.when(qi == 0)
    def _():
        def transpose_block(j, carry):
            j0 = pl.multiple_of(j * blk, blk)
            vt = v_ref[0, pl.ds(j0, blk), :].astype(F32).T
            for h in heads:
                vt_ref[j, h] = vt[cols[h], :].astype(MXU_DTYPE)
            return carry

        lax.fori_loop(0, nb, transpose_block, 0)

    key_i = lax.broadcasted_iota(jnp.int32, (blk, blk), 0)
    qry_i = lax.broadcasted_iota(jnp.int32, (blk, blk), 1)
    causal = key_i < qry_i
    later = jnp.where(qry_i > key_i, 1.0, 0.0).astype(MXU_DTYPE)
    qs = [(q_ref[0, :, cols[h]].astype(F32) * HEAD_DIM ** -0.5).astype(MXU_DTYPE)
          for h in heads]

    def scores(js):
        starts = [pl.multiple_of(j * blk, blk) for j in js]
        return tuple(_dot_nt(k_ref[0, pl.ds(j0, blk), cols[h]], qs[h])
                     for j0 in starts for h in heads)

    def rest(js, flat_scores, carry, diag):
        accs, rests = carry[:nh], carry[nh:]
        z = [flat_scores[t * nh:(t + 1) * nh] for t in range(len(js))]
        log_beta, log_1mb, parts = [], [], []
        for t in range(len(js)):
            for h in heads:
                zz = z[t][h]
                lb = jnp.minimum(zz, 0.0) - jnp.log(1.0 + jnp.exp2(jnp.abs(zz) * (-LOG2E)))
                l1 = lb - zz
                if diag:
                    l1 = jnp.where(causal, l1, 0.0)
                log_beta.append(lb)
                log_1mb.append(l1)
                parts.append(l1.astype(MXU_DTYPE))
        yield
        rem = [_dot(later, p) for p in parts]
        yield
        w = []
        rests = list(rests)
        for t in range(len(js)):
            for h in heads:
                i = t * nh + h
                wh = jnp.exp(log_beta[i] + rem[i] + rests[h])
                if diag:
                    wh = jnp.where(causal, wh, 0.0)
                w.append(wh.astype(MXU_DTYPE))
                rests[h] = rests[h] + rem[i][0:1, :] + log_1mb[i][0:1, :]
        yield
        accs = list(accs)
        for t, j in enumerate(js):
            for h in heads:
                accs[h] = accs[h] + _dot(vt_ref[j, h], w[t * nh + h])
        return tuple(accs) + tuple(rests)

    start = tuple([jnp.zeros((HEAD_DIM, blk), F32)] * nh + [jnp.zeros((1, blk), F32)] * nh)

    def finish(carry):
        o_ref[0] = jnp.concatenate(carry[:nh], axis=0).T.astype(o_ref.dtype)

    return scores, rest, start, finish


def _walk_key_blocks(scores, rest, start, finish, z_ref=None, lead=0):
    qi = pl.program_id(1)
    run = lambda gen: _in_lockstep(gen)[0]

    state = run(rest([qi], scores([qi]), start, True))

    def single(t, state):
        return run(rest([qi - 1], scores([qi - 1]), state, False))

    state = lax.fori_loop(0, qi % 2, single, state)
    first = qi - 1 - qi % 2

    if z_ref is None:
        def pair(t, state):
            j = first - 2 * t
            return run(rest([j, j - 1], scores([j, j - 1]), state, False))
    else:
        def look_ahead(t):
            j = jnp.maximum(first - 2 * t, 1)
            for i, z in enumerate(scores([j, j - 1])):
                z_ref[i] = z

        def pair(t, state):
            j = first - 2 * t
            stages = rest([j, j - 1], [z_ref[i] for i in range(z_ref.shape[0])], state, False)
            for _ in range(lead):
                next(stages)
            look_ahead(t + 1)
            return run(stages)

        look_ahead(0)

    finish(lax.fori_loop(0, qi // 2, pair, state))


def _moba_kernel(q_ref, k_ref, v_ref, sl_ref, o_ref, km_ref, ka_ref, vt_ref, z_ref, *, nb):
    _walk_key_blocks(*_moba_parts(q_ref, k_ref, v_ref, sl_ref, o_ref, km_ref, ka_ref, vt_ref, nb),
                     z_ref=z_ref, lead=0)


def _sb_kernel(q_ref, k_ref, v_ref, o_ref, vt_ref, *, nb):
    _walk_key_blocks(*_sb_parts(q_ref, k_ref, v_ref, o_ref, vt_ref, nb))


def _attention_specs(s, q_col, k_col, v_col):
    blk = MOBA_BLOCK
    return ([pl.BlockSpec((1, blk, ATT_W), lambda b, i: (b, i, q_col // ATT_W)),
             pl.BlockSpec((1, s, ATT_W), lambda b, i: (b, 0, k_col // ATT_W)),
             pl.BlockSpec((1, s, ATT_W), lambda b, i: (b, 0, v_col // ATT_W))],
            pl.BlockSpec((1, blk, ATT_W), lambda b, i: (b, i, 0)))


def _moba(proj):
    bsz, s, _ = proj.shape
    assert MOBA_HEADS == ATT_HEADS_PER_STEP
    blk = MOBA_BLOCK
    nb = s // blk
    slopes = 2.0 ** (-8.0 * np.arange(1, MOBA_HEADS + 1) / MOBA_HEADS)
    sl = jnp.asarray(np.repeat(slopes, HEAD_DIM).reshape(1, 1, ATT_W), F32)
    in_specs, out_spec = _attention_specs(s, A_MOBA_Q, A_MOBA_K, A_MOBA_V)
    return pl.pallas_call(
        functools.partial(_moba_kernel, nb=nb),
        grid=(bsz, nb),
        in_specs=in_specs + [pl.BlockSpec((1, 1, ATT_W), lambda b, i: (0, 0, 0))],
        out_specs=out_spec,
        out_shape=jax.ShapeDtypeStruct((bsz, s, ATT_W), MXU_DTYPE),
        scratch_shapes=[pltpu.VMEM((LANES, ATT_W), F32),
                        pltpu.VMEM((ATT_HEADS_PER_STEP, s, 2 * HEAD_DIM), MXU_DTYPE),
                        pltpu.VMEM((nb, ATT_HEADS_PER_STEP, 2 * HEAD_DIM, blk), MXU_DTYPE),
                        pltpu.VMEM((2 * ATT_HEADS_PER_STEP, blk, blk), F32)],
        compiler_params=_params(("parallel", "arbitrary")),
        name="moba",
    )(proj, proj, proj, sl)


def _stick_breaking(proj):
    bsz, s, _ = proj.shape
    assert SB_HEADS == ATT_HEADS_PER_STEP and SB_BLOCK == MOBA_BLOCK
    blk = SB_BLOCK
    nb = s // blk
    in_specs, out_spec = _attention_specs(s, A_SB_Q, A_SB_K, A_SB_V)
    return pl.pallas_call(
        functools.partial(_sb_kernel, nb=nb),
        grid=(bsz, nb),
        in_specs=in_specs,
        out_specs=out_spec,
        out_shape=jax.ShapeDtypeStruct((bsz, s, ATT_W), MXU_DTYPE),
        scratch_shapes=[pltpu.VMEM((nb, ATT_HEADS_PER_STEP, HEAD_DIM, blk), MXU_DTYPE)],
        compiler_params=_params(("parallel", "arbitrary")),
        name="stick_breaking",
    )(proj, proj, proj)


def _ret_kernel(q_ref, k_ref, v_ref, g_ref, gn_g_ref, gn_b_ref, o_ref, st_ref):
    ch = RET_CHUNK

    @pl.when(pl.program_id(1) == 0)
    def _():
        st_ref[...] = jnp.zeros_like(st_ref)

    i_col = lax.broadcasted_iota(jnp.int32, (ch, 1), 0).astype(F32)
    diff = (lax.broadcasted_iota(jnp.int32, (ch, ch), 0)
            - lax.broadcasted_iota(jnp.int32, (ch, ch), 1)).astype(F32)
    heads = range(RET_HEADS)
    log_g = [float(np.log(1.0 - 2.0 ** (-5.0 - h))) for h in heads]
    qs = [(q_ref[0, :, h * RET_DK:(h + 1) * RET_DK].astype(F32) * RET_DK ** -0.5).astype(MXU_DTYPE)
          for h in heads]
    k = [k_ref[0, :, h * RET_DK:(h + 1) * RET_DK] for h in heads]
    v = [v_ref[0, :, h * RET_DV:(h + 1) * RET_DV] for h in heads]
    prev = [st_ref[h] for h in heads]
    qk = [_dot_nt(qs[h], k[h]) for h in heads]
    o_cross = [_dot(qs[h], prev[h].astype(MXU_DTYPE)) for h in heads]
    kd = [(k[h].astype(F32) * jnp.exp((ch - 1.0 - i_col) * log_g[h])).astype(MXU_DTYPE)
          for h in heads]
    st_new = [_dot_tn(kd[h], v[h]) for h in heads]
    qkd = []
    for h in heads:
        decay = jnp.where(diff >= 0, jnp.exp(jnp.maximum(diff, 0.0) * log_g[h]), 0.0)
        qkd.append((qk[h] * decay).astype(MXU_DTYPE))
    o_intra = [_dot(qkd[h], v[h]) for h in heads]
    for h in heads:
        st_ref[h] = float(np.exp(ch * log_g[h])) * prev[h] + st_new[h]
        o = o_intra[h] + o_cross[h] * jnp.exp((i_col + 1.0) * log_g[h])
        mu = jnp.mean(o, axis=-1, keepdims=True)
        d = o - mu
        var = jnp.mean(d * d, axis=-1, keepdims=True)
        cs = slice(h * RET_DV, (h + 1) * RET_DV)
        on = d * lax.rsqrt(var + NORM_EPS) * gn_g_ref[:, cs] + gn_b_ref[:, cs]
        o_ref[0, :, cs] = (_silu(g_ref[0, :, cs].astype(F32)) * on).astype(o_ref.dtype)


def _ssd_kernel(z_ref, xbc_ref, dt_ref, cw_ref, cb_ref, dtb_ref, alog_ref, dsk_ref, ng_ref,
                o_ref, tail_ref, st_ref, y_ref):
    ch = SSD_CHUNK
    heads_per_group = SSD_HEADS // SSD_GROUPS
    halo = SSD_HALO

    @pl.when(pl.program_id(1) == 0)
    def _():
        tail_ref[0:halo, :] = jnp.zeros((halo, SSD_XBC), F32)
        st_ref[...] = jnp.zeros_like(st_ref)

    tail_ref[halo:, :] = xbc_ref[0].astype(F32)
    conv = cb_ref[...]
    for kk in range(SSD_CONV):
        off = halo - (SSD_CONV - 1) + kk
        conv = conv + cw_ref[kk:kk + 1, :] * tail_ref[off:off + ch, :]
    tail_ref[0:halo, :] = tail_ref[ch:ch + halo, :]
    xc = _silu(conv)
    xs = xc[:, :SSD_D_INNER]
    bm = xc[:, SSD_D_INNER:SSD_D_INNER + SSD_GROUPS * SSD_STATE]
    cm = xc[:, SSD_D_INNER + SSD_GROUPS * SSD_STATE:]

    dtr = dt_ref[0] + dtb_ref[...]
    dt = jnp.maximum(dtr, 0.0) + jnp.log1p(jnp.exp(-jnp.abs(dtr)))
    a = -jnp.exp(alog_ref[...])
    r_iota = lax.broadcasted_iota(jnp.int32, (ch, ch), 0)
    c_iota = lax.broadcasted_iota(jnp.int32, (ch, ch), 1)
    causal = r_iota >= c_iota
    tri = jnp.where(causal, 1.0, 0.0).astype(MXU_DTYPE)
    da = dt * a
    da_hi, da_lo = _split(da)
    cum = _dot(tri, da_hi) + _dot(tri, da_lo)
    cum_t = cum.T
    dt_t = dt.T
    cum_last = cum[ch - 1:ch, :]
    decay_st = jnp.exp(cum_last - cum) * dt
    e_cum = jnp.exp(cum)
    e_last = jnp.exp(cum_last)

    for g in range(SSD_GROUPS):
        bm_g = bm[:, g * SSD_STATE:(g + 1) * SSD_STATE].astype(MXU_DTYPE)
        cm_g = cm[:, g * SSD_STATE:(g + 1) * SSD_STATE].astype(MXU_DTYPE)
        cb = _dot_nt(cm_g, bm_g)
        for r in range(heads_per_group):
            hd = g * heads_per_group + r
            cs = slice(hd * SSD_HEAD_DIM, (hd + 1) * SSD_HEAD_DIM)
            x_h = xs[:, cs]
            seg = cum[:, hd:hd + 1] - cum_t[hd:hd + 1, :]
            lmat = jnp.exp(jnp.where(causal, seg, NEG_INF))
            w = cb * lmat * dt_t[hd:hd + 1, :]
            y = _dot(w.astype(MXU_DTYPE), x_h.astype(MXU_DTYPE))
            prev = st_ref[hd]
            y = y + _dot(cm_g, prev.astype(MXU_DTYPE)) * e_cum[:, hd:hd + 1]
            xd = (x_h * decay_st[:, hd:hd + 1]).astype(MXU_DTYPE)
            st_ref[hd] = e_last[:, hd:hd + 1] * prev + _dot_tn(bm_g, xd)
            y_ref[:, cs] = y + dsk_ref[:, cs] * x_h

    y = y_ref[...] * _silu(z_ref[0].astype(F32))
    gw = SSD_D_INNER // SSD_GROUPS
    for g in range(SSD_GROUPS):
        cs = slice(g * gw, (g + 1) * gw)
        yg = y[:, cs]
        ms = jnp.mean(yg * yg, axis=-1, keepdims=True)
        o_ref[0, :, cs] = (yg * lax.rsqrt(ms + LN_EPS) * ng_ref[:, cs]).astype(o_ref.dtype)


def _scan_kernel(rq_ref, rk_ref, rv_ref, rg_ref, gn_g_ref, gn_b_ref,
                 z_ref, xbc_ref, dt_ref, cw_ref, cb_ref, dtb_ref, alog_ref, dsk_ref, ng_ref,
                 o_ret_ref, o_ssd_ref, ret_st_ref, tail_ref, ssd_st_ref, y_ref):
    _ret_kernel(rq_ref, rk_ref, rv_ref, rg_ref, gn_g_ref, gn_b_ref, o_ret_ref, ret_st_ref)
    _ssd_kernel(z_ref, xbc_ref, dt_ref, cw_ref, cb_ref, dtb_ref, alog_ref, dsk_ref, ng_ref,
                o_ssd_ref, tail_ref, ssd_st_ref, y_ref)


def _scans(proj, proj_dt, gn_g, gn_b, conv_w, conv_b, dt_bias, a_log, d_skip, norm_g):
    bsz, s, _ = proj.shape
    assert RET_CHUNK == SSD_CHUNK
    ch = SSD_CHUNK
    wq, wv = RET_HEADS * RET_DK, RET_HEADS * RET_DV
    pad = LANES - SSD_HEADS
    dtb = jnp.pad(dt_bias, (0, pad)).reshape(1, LANES)
    alog = jnp.pad(a_log, (0, pad)).reshape(1, LANES)
    dsk = jnp.repeat(d_skip, SSD_HEAD_DIM).reshape(1, SSD_D_INNER)
    const = lambda b, i: (0, 0)
    return pl.pallas_call(
        _scan_kernel,
        grid=(bsz, s // ch),
        in_specs=[pl.BlockSpec((1, ch, wq), lambda b, i: (b, i, A_RET_Q // wq)),
                  pl.BlockSpec((1, ch, wq), lambda b, i: (b, i, A_RET_K // wq)),
                  pl.BlockSpec((1, ch, wv), lambda b, i: (b, i, A_RET_V // wv)),
                  pl.BlockSpec((1, ch, wv), lambda b, i: (b, i, B_RET_G // wv)),
                  pl.BlockSpec((1, wv), const),
                  pl.BlockSpec((1, wv), const),
                  pl.BlockSpec((1, ch, SSD_D_INNER), lambda b, i: (b, i, B_SSD_Z // SSD_D_INNER)),
                  pl.BlockSpec((1, ch, SSD_XBC), lambda b, i: (b, i, B_XBC // SSD_XBC)),
                  pl.BlockSpec((1, ch, LANES), lambda b, i: (b, i, 0)),
                  pl.BlockSpec((SSD_CONV, SSD_XBC), const),
                  pl.BlockSpec((1, SSD_XBC), const),
                  pl.BlockSpec((1, LANES), const),
                  pl.BlockSpec((1, LANES), const),
                  pl.BlockSpec((1, SSD_D_INNER), const),
                  pl.BlockSpec((1, SSD_D_INNER), const)],
        out_specs=[pl.BlockSpec((1, ch, wv), lambda b, i: (b, i, 0)),
                   pl.BlockSpec((1, ch, SSD_D_INNER), lambda b, i: (b, i, 0))],
        out_shape=[jax.ShapeDtypeStruct((bsz, s, wv), MXU_DTYPE),
                   jax.ShapeDtypeStruct((bsz, s, SSD_D_INNER), MXU_DTYPE)],
        scratch_shapes=[pltpu.VMEM((RET_HEADS, RET_DK, RET_DV), F32),
                        pltpu.VMEM((SSD_HALO + ch, SSD_XBC), F32),
                        pltpu.VMEM((SSD_HEADS, SSD_STATE, SSD_HEAD_DIM), F32),
                        pltpu.VMEM((ch, SSD_D_INNER), F32)],
        compiler_params=_params(("parallel", "arbitrary")),
        name="retention_ssd",
    )(proj, proj, proj, proj, gn_g.reshape(1, wv), gn_b.reshape(1, wv),
      proj, proj, proj_dt, conv_w, conv_b.reshape(1, SSD_XBC), dtb, alog, dsk,
      norm_g.reshape(1, SSD_D_INNER))


def _merge_kernel(oa_ref, ob_ref, oc_ref, od_ref, gates_ref, x_ref, g1_ref, wbr_ref, wout_ref,
                  lg_ref, lb_ref, o_ref):
    merged = None
    for n, br in enumerate((oa_ref, ob_ref, oc_ref, od_ref)):
        y = _dot(br[0], wbr_ref[n])
        t = _sigmoid(gates_ref[0, :, n * D_MODEL:(n + 1) * D_MODEL].astype(F32)) * y
        merged = t if merged is None else merged + t
    mix = _dot(merged.astype(MXU_DTYPE), wout_ref[...])
    o_ref[0] = _resid_ln(x_ref[0], g1_ref[0], mix, lg_ref[...], lb_ref[...])


def _merge(o_a, o_b, o_c, o_d, proj, x, g1, w_br, w_out, ln_g, ln_b, tm=512):
    bsz, s, d = x.shape
    tm = min(tm, s)
    gw = N_BRANCH * d
    row = lambda b, i: (b, i, 0)
    const2 = lambda b, i: (0, 0)
    br_spec = pl.BlockSpec((1, tm, BRANCH_W), row)
    return pl.pallas_call(
        _merge_kernel,
        grid=(bsz, s // tm),
        in_specs=[br_spec, br_spec, br_spec, br_spec,
                  pl.BlockSpec((1, tm, gw), lambda b, i: (b, i, B_GATES // gw)),
                  pl.BlockSpec((1, tm, d), row),
                  pl.BlockSpec((1, 1, d), lambda b, i: (b, 0, 0)),
                  pl.BlockSpec((N_BRANCH, BRANCH_W, d), lambda b, i: (0, 0, 0)),
                  pl.BlockSpec((d, d), const2),
                  pl.BlockSpec((1, d), const2),
                  pl.BlockSpec((1, d), const2)],
        out_specs=pl.BlockSpec((1, tm, d), row),
        out_shape=jax.ShapeDtypeStruct((bsz, s, d), F32),
        compiler_params=_params(("parallel", "parallel")),
        name="merge",
    )(o_a, o_b, o_c, o_d, proj, x, g1, w_br, w_out, ln_g.reshape(1, d), ln_b.reshape(1, d))


def _ffn_kernel(x_ref, sc_ref, sh_ref, g2_ref, wg_ref, wu_ref, wd_ref, lg_ref, lb_ref,
                o_ref, h_ref, acc_ref):
    j = pl.program_id(2)

    @pl.when(j == 0)
    def _():
        h = x_ref[0] * (1.0 + sc_ref[0]) + sh_ref[0]
        h_ref[...] = h.astype(MXU_DTYPE)
        acc_ref[...] = jnp.zeros_like(acc_ref)

    h = h_ref[...]
    act = _silu(_dot(h, wg_ref[...])) * _dot(h, wu_ref[...])
    acc_ref[...] += _dot(act.astype(MXU_DTYPE), wd_ref[...])

    @pl.when(j == pl.num_programs(2) - 1)
    def _():
        o_ref[0] = _resid_ln(x_ref[0], g2_ref[0], acc_ref[...], lg_ref[...], lb_ref[...])


def _ffn(x, sc, sh, g2, w_gu, w_down, ln_g, ln_b, tm=512, tf=1408):
    bsz, s, d = x.shape
    tm = min(tm, s)
    nf = D_FF // tf
    row = lambda b, i, j: (b, i, 0)
    vec = lambda b, i, j: (b, 0, 0)
    const2 = lambda b, i, j: (0, 0)
    return pl.pallas_call(
        _ffn_kernel,
        grid=(bsz, s // tm, nf),
        in_specs=[pl.BlockSpec((1, tm, d), row),
                  pl.BlockSpec((1, 1, d), vec), pl.BlockSpec((1, 1, d), vec),
                  pl.BlockSpec((1, 1, d), vec),
                  pl.BlockSpec((d, tf), lambda b, i, j: (0, j)),
                  pl.BlockSpec((d, tf), lambda b, i, j: (0, nf + j)),
                  pl.BlockSpec((tf, d), lambda b, i, j: (j, 0)),
                  pl.BlockSpec((1, d), const2), pl.BlockSpec((1, d), const2)],
        out_specs=pl.BlockSpec((1, tm, d), row),
        out_shape=jax.ShapeDtypeStruct((bsz, s, d), F32),
        scratch_shapes=[pltpu.VMEM((tm, d), MXU_DTYPE), pltpu.VMEM((tm, d), F32)],
        compiler_params=_params(("parallel", "parallel", "arbitrary")),
        name="ffn_dense",
    )(x, sc, sh, g2, w_gu, w_gu, w_down, ln_g.reshape(1, d), ln_b.reshape(1, d))


def _router_kernel(x_ref, sc_ref, sh_ref, rw_ref, rb_ref,
                   h_ref, comb_ref, slot_ref, slot_t_ref, cnt_ref):
    tm = x_ref.shape[1]
    lane = lax.broadcasted_iota(jnp.int32, (tm, LANES), 1)
    h = x_ref[0] * (1.0 + sc_ref[0]) + sh_ref[0]
    h_ref[0] = h.astype(MXU_DTYPE)
    logits = _dot3(h, rw_ref[...]) + rb_ref[...]
    logits = jnp.where(lane < N_EXPERTS, logits, -jnp.inf)
    v1 = jnp.max(logits, axis=1, keepdims=True)
    i1 = jnp.min(jnp.where(logits == v1, lane, LANES), axis=1, keepdims=True)
    rest = jnp.where(lane == i1, -jnp.inf, logits)
    v2 = jnp.max(rest, axis=1, keepdims=True)
    i2 = jnp.min(jnp.where(rest == v2, lane, LANES), axis=1, keepdims=True)
    e2 = jnp.exp(v2 - v1)
    w1 = 1.0 / (1.0 + e2)
    w2 = e2 / (1.0 + e2)
    comb_ref[0] = jnp.where(lane == i1, w1, 0.0) + jnp.where(lane == i2, w2, 0.0)
    routed = jnp.logical_or(lane == i1, lane == i2)
    earlier = (lax.broadcasted_iota(jnp.int32, (tm, tm), 1)
               < lax.broadcasted_iota(jnp.int32, (tm, tm), 0))
    before = _dot(jnp.where(earlier, 1.0, 0.0).astype(MXU_DTYPE),
                  jnp.where(routed, 1.0, 0.0).astype(MXU_DTYPE))
    slot = jnp.where(routed, before, -1.0)
    slot_ref[0] = slot
    slot_t_ref[0] = slot.T[0:N_EXPERTS, :]
    cnt_ref[0] = jnp.sum(jnp.where(routed, 1, 0), axis=0, keepdims=True)


def _moe_kernel(cnt_ref, h_ref, x_ref, g2_ref, comb_ref, slot_ref, slot_t_ref,
                wg_ref, wu_ref, wd_ref, lg_ref, lb_ref, o_ref, xe_ref, ye_ref, *, n_tiles, n_sub):
    tr = x_ref.shape[1] // n_sub
    ch = MOE_CHUNK
    e = pl.program_id(2)
    j = pl.program_id(3)
    last_j = pl.num_programs(3) - 1
    first_sub = (pl.program_id(0) * n_tiles + pl.program_id(1)) * n_sub
    counts = [cnt_ref[(first_sub + s) * N_EXPERTS + e] for s in range(n_sub)]
    offsets = [0]
    for s in range(n_sub):
        offsets.append(pl.multiple_of(
            offsets[s] + (counts[s] + MOE_ROW_STEP - 1) // MOE_ROW_STEP * MOE_ROW_STEP,
            MOE_ROW_STEP))
    count = offsets[n_sub - 1] + counts[n_sub - 1]
    n_chunks = [(counts[s] + ch - 1) // ch for s in range(n_sub)]

    @pl.when(jnp.logical_and(e == 0, j == 0))
    def _():
        o_ref[0] = jnp.zeros(o_ref.shape[1:], F32)

    @pl.when(j == 0)
    def _():
        row = lax.broadcasted_iota(jnp.int32, (ch, tr), 0).astype(F32)
        for s in range(n_sub):
            slot_row = slot_t_ref[s, pl.ds(e, 1), :]

            def gather(c, carry, s=s, slot_row=slot_row):
                r0 = pl.multiple_of(offsets[s] + c * ch, MOE_ROW_STEP)
                pick = jnp.where(slot_row == row + (c * ch).astype(F32), 1.0, 0.0)
                xe_ref[pl.ds(r0, ch), :] = _dot(pick.astype(MXU_DTYPE),
                                                h_ref[0, s * tr:(s + 1) * tr, :]).astype(MXU_DTYPE)
                ye_ref[pl.ds(r0, ch), :] = jnp.zeros((ch, D_MODEL), F32)
                return carry

            lax.fori_loop(0, n_chunks[s], gather, 0)

    def expert_rows(r0, rows):
        xc = xe_ref[pl.ds(r0, rows), :]
        act = _silu(_dot(xc, wg_ref[0])) * _dot(xc, wu_ref[0])
        ye_ref[pl.ds(r0, rows), :] += _dot(act.astype(MXU_DTYPE), wd_ref[0])

    n_full = count // MOE_ROWS

    def full_piece(c, carry):
        expert_rows(pl.multiple_of(c * MOE_ROWS, MOE_ROWS), MOE_ROWS)
        return carry

    lax.fori_loop(0, n_full, full_piece, 0)
    tail0 = pl.multiple_of(n_full * MOE_ROWS, MOE_ROWS)
    tail_steps = (count - n_full * MOE_ROWS + MOE_ROW_STEP - 1) // MOE_ROW_STEP
    for steps in range(1, MOE_ROWS // MOE_ROW_STEP + 1):
        @pl.when(tail_steps == steps)
        def _(steps=steps):
            expert_rows(tail0, steps * MOE_ROW_STEP)

    @pl.when(j == last_j)
    def _():
        mine = lax.broadcasted_iota(jnp.int32, (tr, LANES), 1) == e
        col = lax.broadcasted_iota(jnp.int32, (tr, ch), 1).astype(F32)
        for s in range(n_sub):
            rows = slice(s * tr, (s + 1) * tr)
            slot_col = jnp.sum(jnp.where(mine, slot_ref[0, rows, :], 0.0), axis=1, keepdims=True)
            comb_col = jnp.sum(jnp.where(mine, comb_ref[0, rows, :], 0.0), axis=1, keepdims=True)

            def scatter(c, carry, s=s, rows=rows, slot_col=slot_col, comb_col=comb_col):
                r0 = pl.multiple_of(offsets[s] + c * ch, MOE_ROW_STEP)
                place = jnp.where(slot_col == col + (c * ch).astype(F32), 1.0, 0.0)
                o_ref[0, rows, :] += comb_col * _dot(
                    place.astype(MXU_DTYPE), ye_ref[pl.ds(r0, ch), :].astype(MXU_DTYPE))
                return carry

            lax.fori_loop(0, n_chunks[s], scatter, 0)

    @pl.when(jnp.logical_and(e == pl.num_programs(2) - 1, j == last_j))
    def _():
        o_ref[0] = _resid_ln(x_ref[0], g2_ref[0], o_ref[0], lg_ref[...], lb_ref[...])


def _moe(x, sc, sh, g2, router_w, router_b, w_gu, w_down, ln_g, ln_b, tr=1024, n_sub=2, tf=896):
    bsz, s, d = x.shape
    tr = min(tr, s)
    nr = s // tr
    n_sub = min(n_sub, nr)
    tm = n_sub * tr
    nt = s // tm
    nf = D_FF_EXPERT // tf
    rw = jnp.pad(router_w, ((0, 0), (0, LANES - N_EXPERTS)))
    rb = jnp.pad(router_b, (0, LANES - N_EXPERTS)).reshape(1, LANES)

    row2 = lambda b, i: (b, i, 0)
    vec2 = lambda b, i: (b, 0, 0)
    h, comb, slot, slot_t, cnt = pl.pallas_call(
        _router_kernel,
        grid=(bsz, nr),
        in_specs=[pl.BlockSpec((1, tr, d), row2),
                  pl.BlockSpec((1, 1, d), vec2), pl.BlockSpec((1, 1, d), vec2),
                  pl.BlockSpec((d, LANES), lambda b, i: (0, 0)),
                  pl.BlockSpec((1, LANES), lambda b, i: (0, 0))],
        out_specs=[pl.BlockSpec((1, tr, d), row2),
                   pl.BlockSpec((1, tr, LANES), row2),
                   pl.BlockSpec((1, tr, LANES), row2),
                   pl.BlockSpec((1, N_EXPERTS, tr), lambda b, i: (b * nr + i, 0, 0)),
                   pl.BlockSpec((1, 1, LANES), lambda b, i: (b * nr + i, 0, 0))],
        out_shape=[jax.ShapeDtypeStruct((bsz, s, d), MXU_DTYPE),
                   jax.ShapeDtypeStruct((bsz, s, LANES), F32),
                   jax.ShapeDtypeStruct((bsz, s, LANES), F32),
                   jax.ShapeDtypeStruct((bsz * nr, N_EXPERTS, tr), F32),
                   jax.ShapeDtypeStruct((bsz * nr, 1, LANES), jnp.int32)],
        compiler_params=_params(("parallel", "parallel")),
        name="moe_router",
    )(x, sc, sh, rw, rb)
    counts = cnt[:, 0, :N_EXPERTS].reshape(bsz * nr * N_EXPERTS)

    row = lambda b, i, e, j, cnt: (b, i, 0)
    vec = lambda b, i, e, j, cnt: (b, 0, 0)
    const2 = lambda b, i, e, j, cnt: (0, 0)
    once = pl.Buffered(1)
    grid_spec = pltpu.PrefetchScalarGridSpec(
        num_scalar_prefetch=1,
        grid=(bsz, nt, N_EXPERTS, nf),
        in_specs=[pl.BlockSpec((1, tm, d), row, pipeline_mode=once),
                  pl.BlockSpec((1, tm, d), row, pipeline_mode=once),
                  pl.BlockSpec((1, 1, d), vec),
                  pl.BlockSpec((1, tm, LANES), row, pipeline_mode=once),
                  pl.BlockSpec((1, tm, LANES), row, pipeline_mode=once),
                  pl.BlockSpec((n_sub, N_EXPERTS, tr), lambda b, i, e, j, cnt: (b * nt + i, 0, 0)),
                  pl.BlockSpec((1, d, tf), lambda b, i, e, j, cnt: (e, 0, j)),
                  pl.BlockSpec((1, d, tf), lambda b, i, e, j, cnt: (e, 0, nf + j)),
                  pl.BlockSpec((1, tf, d), lambda b, i, e, j, cnt: (e, j, 0)),
                  pl.BlockSpec((1, d), const2), pl.BlockSpec((1, d), const2)],
        out_specs=pl.BlockSpec((1, tm, d), row, pipeline_mode=once),
        scratch_shapes=[pltpu.VMEM((tm, d), MXU_DTYPE), pltpu.VMEM((tm, d), F32)])
    return pl.pallas_call(
        functools.partial(_moe_kernel, n_tiles=nt, n_sub=n_sub),
        grid_spec=grid_spec,
        out_shape=jax.ShapeDtypeStruct((bsz, s, d), F32),
        compiler_params=pltpu.CompilerParams(
            dimension_semantics=("parallel", "parallel", "arbitrary", "arbitrary"),
            vmem_limit_bytes=MOE_VMEM_LIMIT),
        name="moe_experts",
    )(counts, h, x, g2, comb, slot, slot_t, w_gu, w_gu, w_down,
      ln_g.reshape(1, d), ln_b.reshape(1, d))


def _split_w_in(w):
    p = np.cumsum([0, 512, 512, 512, 256, 256, 512, 512, 512, SSD_XBC, SSD_HEADS, 512, 512, 512,
                   N_BRANCH * D_MODEL])
    mq, mk, mv, rq, rk, rv, rg, sz, sxbc, sdt, bq, bk, bv, gates = (
        w[:, p[i]:p[i + 1]] for i in range(14))
    w_main = jnp.concatenate([mq, mk, mv, rq, rk, rv, bq, bk, bv, gates, rg, sz, sxbc], axis=1)
    w_dt = jnp.pad(sdt, ((0, 0), (0, LANES - SSD_HEADS)))
    return w_main.astype(MXU_DTYPE), w_dt.astype(MXU_DTYPE)


def kernel(x, c, w_ada, b_ada, w_in, conv_w, conv_b, dt_bias, a_log, d_skip, ssm_norm_g, ret_gn_g, ret_gn_b, w_br, w_out, ln1_g, ln1_b, ln2_g, ln2_b, ffn_w_gu, ffn_w_down, router_w, router_b, expert_w_gu, expert_w_down):
    bsz = x.shape[0]
    mod = _ada(c, w_ada, b_ada)
    for l in range(DEPTH):
        sh1, sc1, g1, sh2, sc2, g2 = (
            mod[l, :, i * D_MODEL:(i + 1) * D_MODEL].reshape(bsz, 1, D_MODEL) for i in range(6))
        w_main, w_dt = _split_w_in(w_in[l])
        proj = _inproj(x, sc1, sh1, w_main, MXU_DTYPE, tn=2048)
        proj_dt = _inproj(x, sc1, sh1, w_dt, F32, tn=LANES)
        o_a = _moba(proj)
        o_d = _stick_breaking(proj)
        o_b, o_c = _scans(proj, proj_dt, ret_gn_g[l], ret_gn_b[l], conv_w[l], conv_b[l],
                          dt_bias[l], a_log[l], d_skip[l], ssm_norm_g[l])
        x = _merge(o_a, o_b, o_c, o_d, proj, x, g1, w_br[l].astype(MXU_DTYPE),
                   w_out[l].astype(MXU_DTYPE), ln1_g[l], ln1_b[l])
        if l % 2 == 0:
            x = _ffn(x, sc2, sh2, g2, ffn_w_gu[l // 2].astype(MXU_DTYPE),
                     ffn_w_down[l // 2].astype(MXU_DTYPE), ln2_g[l], ln2_b[l])
        else:
            x = _moe(x, sc2, sh2, g2, router_w[l // 2], router_b[l // 2],
                     expert_w_gu[l // 2].astype(MXU_DTYPE), expert_w_down[l // 2].astype(MXU_DTYPE),
                     ln2_g[l], ln2_b[l])
    return x
```

```python
import functools

import numpy as np
import jax
import jax.numpy as jnp
from jax import lax
from jax.experimental import pallas as pl
from jax.experimental.pallas import tpu as pltpu

F32 = jnp.float32
MXU_DTYPE = jnp.bfloat16

D_MODEL = 1024
DEPTH = 2
MOBA_HEADS = 8
MOBA_BLOCK = 256
MOBA_TOPK = 3
RET_HEADS = 4
RET_DK = 64
RET_DV = 128
RET_CHUNK = 256
SSD_D_INNER = 512
SSD_HEAD_DIM = 64
SSD_HEADS = 8
SSD_GROUPS = 2
SSD_STATE = 128
SSD_CONV = 4
SSD_CHUNK = 256
SSD_XBC = SSD_D_INNER + 2 * SSD_GROUPS * SSD_STATE
SSD_HALO = 8
SB_HEADS = 8
SB_BLOCK = 256
HEAD_DIM = 64
ATT_HEADS_PER_STEP = 8
ATT_W = ATT_HEADS_PER_STEP * HEAD_DIM
BRANCH_W = 512
N_BRANCH = 4
D_FF = 2816
N_EXPERTS = 8
D_FF_EXPERT = 3584
DEEPNORM_ALPHA = (2.0 * DEPTH) ** 0.25
LN_EPS = 1e-5
NORM_EPS = 1e-6
NEG_INF = -1e30
LOG2E = 1.4426950408889634
LANES = 128
VMEM_LIMIT = 48 * 2 ** 20
MOE_VMEM_LIMIT = 56 * 2 ** 20
MOE_ROWS = 256
MOE_ROW_STEP = 32
MOE_TAIL_MIN = 128

A_MOBA_Q, A_MOBA_K, A_MOBA_V = 0, 512, 1024
A_RET_Q, A_RET_K, A_RET_V = 1536, 1792, 2048
A_SB_Q, A_SB_K, A_SB_V = 2560, 3072, 3584
B_GATES, B_RET_G, B_SSD_Z, B_XBC = 4096, 8192, 8704, 9216
PROJ_WIDTH = 10240


def _params(sem):
    return pltpu.CompilerParams(dimension_semantics=sem, vmem_limit_bytes=VMEM_LIMIT)


def _dot(a, b):
    return jnp.dot(a, b, preferred_element_type=F32)


def _dot_nt(a, b):
    return lax.dot_general(a, b, (((1,), (1,)), ((), ())), preferred_element_type=F32)


def _dot_tn(a, b):
    return lax.dot_general(a, b, (((0,), (0,)), ((), ())), preferred_element_type=F32)


def _split(x):
    hi = x.astype(MXU_DTYPE)
    lo = (x - hi.astype(F32)).astype(MXU_DTYPE)
    return hi, lo


def _dot3(a, b):
    ah, al = _split(a)
    bh, bl = _split(b)
    return _dot(ah, bh) + _dot(al, bh) + _dot(ah, bl)


def _sigmoid(x):
    return 1.0 / (1.0 + jnp.exp(-x))


def _silu(x):
    return x * _sigmoid(x)


def _resid_ln(x, gate, f, ln_g, ln_b):
    y = DEEPNORM_ALPHA * x + gate * f
    mu = jnp.mean(y, axis=-1, keepdims=True)
    d = y - mu
    var = jnp.mean(d * d, axis=-1, keepdims=True)
    return d * lax.rsqrt(var + LN_EPS) * ln_g + ln_b


def _ada_kernel(c_ref, w_ref, b_ref, o_ref):
    c = c_ref[...]
    o_ref[0] = _dot3(_silu(c), w_ref[0]) + b_ref[0]


def _ada(c, w_ada, b_ada):
    depth, d, n = w_ada.shape
    bsz = c.shape[0]
    tn = 1536
    return pl.pallas_call(
        _ada_kernel,
        grid=(depth, n // tn),
        in_specs=[pl.BlockSpec((bsz, d), lambda l, j: (0, 0)),
                  pl.BlockSpec((1, d, tn), lambda l, j: (l, 0, j)),
                  pl.BlockSpec((1, 1, tn), lambda l, j: (l, 0, j))],
        out_specs=pl.BlockSpec((1, bsz, tn), lambda l, j: (l, 0, j)),
        out_shape=jax.ShapeDtypeStruct((depth, bsz, n), F32),
        compiler_params=_params(("arbitrary", "arbitrary")),
        name="ada_mod",
    )(c, w_ada, b_ada.reshape(depth, 1, n))


def _inproj_kernel(x_ref, sc_ref, sh_ref, w_ref, o_ref, h_ref):
    @pl.when(pl.program_id(2) == 0)
    def _():
        h = x_ref[0] * (1.0 + sc_ref[0]) + sh_ref[0]
        h_ref[...] = h.astype(MXU_DTYPE)

    o_ref[0] = _dot(h_ref[...], w_ref[...]).astype(o_ref.dtype)


def _inproj(x, sc, sh, w, out_dtype, tn, tm=1024):
    bsz, s, d = x.shape
    n = w.shape[1]
    tm = min(tm, s)
    return pl.pallas_call(
        _inproj_kernel,
        grid=(bsz, s // tm, n // tn),
        in_specs=[pl.BlockSpec((1, tm, d), lambda b, i, j: (b, i, 0)),
                  pl.BlockSpec((1, 1, d), lambda b, i, j: (b, 0, 0)),
                  pl.BlockSpec((1, 1, d), lambda b, i, j: (b, 0, 0)),
                  pl.BlockSpec((d, tn), lambda b, i, j: (0, j))],
        out_specs=pl.BlockSpec((1, tm, tn), lambda b, i, j: (b, i, j)),
        out_shape=jax.ShapeDtypeStruct((bsz, s, n), out_dtype),
        scratch_shapes=[pltpu.VMEM((tm, d), MXU_DTYPE)],
        compiler_params=_params(("parallel", "parallel", "arbitrary")),
        name="in_proj",
    )(x, sc, sh, w)


def _in_lockstep(*gens):
    results = [None] * len(gens)
    live = list(range(len(gens)))
    while live:
        for i in list(live):
            try:
                next(gens[i])
            except StopIteration as stop:
                results[i] = stop.value
                live.remove(i)
    return results


def _moba_parts(q_ref, k_ref, v_ref, sl_ref, o_ref, km_ref, ka_ref, vt_ref, nb):
    blk = MOBA_BLOCK
    nh = ATT_HEADS_PER_STEP
    heads = range(nh)
    qi = pl.program_id(1)
    s_len = nb * blk
    nb_pad = -(-nb // 8) * 8
    extra = 2 * HEAD_DIM - HEAD_DIM
    assert nb_pad + 8 <= extra
    cols = [slice(h * HEAD_DIM, (h + 1) * HEAD_DIM) for h in heads]

    @pl.when(qi == 0)
    def _():
        k_all = k_ref[0]
        kk = k_all.astype(F32).reshape(nb, blk, ATT_W)
        km_ref[...] = jnp.zeros_like(km_ref)
        km_ref[0:nb, :] = jnp.sum(kk, axis=1) * (1.0 / blk)
        pos = lax.broadcasted_iota(jnp.int32, (s_len, extra), 0)
        lane = lax.broadcasted_iota(jnp.int32, (s_len, extra), 1)
        one_hot = jnp.where(lane == pos // blk, 1.0, 0.0)
        pos_hi = jnp.where(lane == nb_pad, (pos // blk * blk).astype(F32), 0.0)
        pos_lo = jnp.where(lane == nb_pad + 1, (pos % blk).astype(F32), 0.0)
        tail = (one_hot + pos_hi + pos_lo).astype(MXU_DTYPE)
        for h in heads:
            ka_ref[h] = jnp.concatenate([k_all[:, cols[h]], tail], axis=1)

        ones_rows = jnp.where(lax.broadcasted_iota(jnp.int32, (extra, blk), 0) == 0, 1.0, 0.0)

        def transpose_block(j, carry):
            j0 = pl.multiple_of(j * blk, blk)
            vt = v_ref[0, pl.ds(j0, blk), :].astype(F32).T
            for h in heads:
                vt_ref[j, h] = jnp.concatenate([vt[cols[h], :], ones_rows], axis=0).astype(MXU_DTYPE)
            return carry

        lax.fori_loop(0, nb, transpose_block, 0)

    key_i = lax.broadcasted_iota(jnp.int32, (blk, blk), 0)
    qry_i = lax.broadcasted_iota(jnp.int32, (blk, blk), 1)
    n_iota = lax.broadcasted_iota(jnp.int32, (nb_pad, blk), 0)
    past = n_iota < qi
    k_sel = min(MOBA_TOPK, max(nb - 1, 1))
    q_t = q_ref[0].astype(F32).T
    slope_rows = lax.broadcasted_iota(jnp.int32, (8, blk), 0) < 2

    qa = []
    for h in heads:
        qh_t = q_t[cols[h], :]
        slope = sl_ref[0, :, h * HEAD_DIM:h * HEAD_DIM + 1]
        km_hi, km_lo = _split(km_ref[:, cols[h]])
        qh_mxu = qh_t.astype(MXU_DTYPE)
        gate = (_dot(km_hi, qh_mxu) + _dot(km_lo, qh_mxu))[0:nb_pad, :]
        g = jnp.where(past, gate, NEG_INF)
        g = jnp.where(n_iota < nb, g, -jnp.inf)
        sel = jnp.zeros((nb_pad, blk), F32)
        for _ in range(k_sel):
            mx = jnp.max(g, axis=0, keepdims=True)
            idx = jnp.min(jnp.where(g == mx, n_iota, nb_pad), axis=0, keepdims=True)
            pick = n_iota == idx
            sel = jnp.where(pick, 1.0, sel)
            g = jnp.where(pick, -jnp.inf, g)
        keep = jnp.logical_or(jnp.logical_and(past, sel > 0.5), n_iota == qi)
        bias = jnp.where(keep, 0.0, NEG_INF)
        alibi = jnp.where(slope_rows, slope, 0.0)
        pad = jnp.zeros((extra - nb_pad - 8, blk), F32)
        qa.append(jnp.concatenate([qh_t * HEAD_DIM ** -0.5, bias, alibi, pad],
                                  axis=0).astype(MXU_DTYPE))

    def scores(js):
        starts = [pl.multiple_of(j * blk, blk) for j in js]
        return tuple(_dot(ka_ref[h, pl.ds(j0, blk), :], qa[h]) for j0 in starts for h in heads)

    def rest(js, flat_scores, carry, diag):
        s = [flat_scores[t * nh:(t + 1) * nh] for t in range(len(js))]
        m = list(carry[:nh]) if carry is not None else [None] * nh
        acc = list(carry[nh:]) if carry is not None else [None] * nh
        p = {}
        for h in heads:
            sh = [jnp.where(key_i <= qry_i, s[t][h], NEG_INF) if diag else s[t][h]
                  for t in range(len(js))]
            top = jnp.max(sh[0], axis=0, keepdims=True)
            for t in range(1, len(js)):
                top = jnp.maximum(top, jnp.max(sh[t], axis=0, keepdims=True))
            if m[h] is None:
                m[h] = top
            else:
                m_new = jnp.maximum(m[h], top)
                acc[h] = jnp.exp(m[h] - m_new) * acc[h]
                m[h] = m_new
            for t in range(len(js)):
                p[t, h] = jnp.exp(sh[t] - m[h]).astype(MXU_DTYPE)
        yield
        for t, j in enumerate(js):
            for h in heads:
                pv = _dot(vt_ref[j, h], p[t, h])
                acc[h] = pv if acc[h] is None else acc[h] + pv
        return tuple(m) + tuple(acc)

    def finish(carry):
        out_t = [carry[nh + h][0:HEAD_DIM, :] / carry[nh + h][HEAD_DIM:HEAD_DIM + 1, :]
                 for h in heads]
        o_ref[0] = jnp.concatenate(out_t, axis=0).T.astype(o_ref.dtype)

    return scores, rest, None, finish


def _sb_parts(q_ref, k_ref, v_ref, o_ref, vt_ref, nb):
    blk = SB_BLOCK
    nh = ATT_HEADS_PER_STEP
    heads = range(nh)
    qi = pl.program_id(1)
    cols = [slice(h * HEAD_DIM, (h + 1) * HEAD_DIM) for h in heads]

    @pl.when(qi == 0)
    def _():
        def transpose_block(j, carry):
            j0 = pl.multiple_of(j * blk, blk)
            vt = v_ref[0, pl.ds(j0, blk), :].astype(F32).T
            for h in heads:
                vt_ref[j, h] = vt[cols[h], :].astype(MXU_DTYPE)
            return carry

        lax.fori_loop(0, nb, transpose_block, 0)

    key_i = lax.broadcasted_iota(jnp.int32, (blk, blk), 0)
    qry_i = lax.broadcasted_iota(jnp.int32, (blk, blk), 1)
    causal = key_i < qry_i
    later = jnp.where(qry_i > key_i, 1.0, 0.0).astype(MXU_DTYPE)
    qs = [(q_ref[0, :, cols[h]].astype(F32) * HEAD_DIM ** -0.5).astype(MXU_DTYPE)
          for h in heads]

    def scores(js):
        starts = [pl.multiple_of(j * blk, blk) for j in js]
        return tuple(_dot_nt(k_ref[0, pl.ds(j0, blk), cols[h]], qs[h])
                     for j0 in starts for h in heads)

    def rest(js, flat_scores, carry, diag):
        accs, rests = carry[:nh], carry[nh:]
        z = [flat_scores[t * nh:(t + 1) * nh] for t in range(len(js))]
        log_beta, log_1mb, parts = [], [], []
        for t in range(len(js)):
            for h in heads:
                zz = z[t][h]
                lb = jnp.minimum(zz, 0.0) - jnp.log(1.0 + jnp.exp2(jnp.abs(zz) * (-LOG2E)))
                l1 = lb - zz
                if diag:
                    l1 = jnp.where(causal, l1, 0.0)
                log_beta.append(lb)
                log_1mb.append(l1)
                parts.append(l1.astype(MXU_DTYPE))
        yield
        rem = [_dot(later, p) for p in parts]
        yield
        w = []
        rests = list(rests)
        for t in range(len(js)):
            for h in heads:
                i = t * nh + h
                wh = jnp.exp(log_beta[i] + rem[i] + rests[h])
                if diag:
                    wh = jnp.where(causal, wh, 0.0)
                w.append(wh.astype(MXU_DTYPE))
                rests[h] = rests[h] + rem[i][0:1, :] + log_1mb[i][0:1, :]
        yield
        accs = list(accs)
        for t, j in enumerate(js):
            for h in heads:
                accs[h] = accs[h] + _dot(vt_ref[j, h], w[t * nh + h])
        return tuple(accs) + tuple(rests)

    start = tuple([jnp.zeros((HEAD_DIM, blk), F32)] * nh + [jnp.zeros((1, blk), F32)] * nh)

    def finish(carry):
        o_ref[0] = jnp.concatenate(carry[:nh], axis=0).T.astype(o_ref.dtype)

    return scores, rest, start, finish


def _walk_key_blocks(scores, rest, start, finish, z_ref=None, lead=0):
    qi = pl.program_id(1)
    run = lambda gen: _in_lockstep(gen)[0]

    state = run(rest([qi], scores([qi]), start, True))

    def single(t, state):
        return run(rest([qi - 1], scores([qi - 1]), state, False))

    state = lax.fori_loop(0, qi % 2, single, state)
    first = qi - 1 - qi % 2

    if z_ref is None:
        def pair(t, state):
            j = first - 2 * t
            return run(rest([j, j - 1], scores([j, j - 1]), state, False))
    else:
        def look_ahead(t):
            j = jnp.maximum(first - 2 * t, 1)
            for i, z in enumerate(scores([j, j - 1])):
                z_ref[i] = z

        def pair(t, state):
            j = first - 2 * t
            stages = rest([j, j - 1], [z_ref[i] for i in range(z_ref.shape[0])], state, False)
            for _ in range(lead):
                next(stages)
            look_ahead(t + 1)
            return run(stages)

        look_ahead(0)

    finish(lax.fori_loop(0, qi // 2, pair, state))


def _moba_kernel(q_ref, k_ref, v_ref, sl_ref, o_ref, km_ref, ka_ref, vt_ref, z_ref, *, nb):
    _walk_key_blocks(*_moba_parts(q_ref, k_ref, v_ref, sl_ref, o_ref, km_ref, ka_ref, vt_ref, nb),
                     z_ref=z_ref, lead=0)


def _sb_kernel(q_ref, k_ref, v_ref, o_ref, vt_ref, *, nb):
    _walk_key_blocks(*_sb_parts(q_ref, k_ref, v_ref, o_ref, vt_ref, nb))


def _attention_specs(s, q_col, k_col, v_col):
    blk = MOBA_BLOCK
    return ([pl.BlockSpec((1, blk, ATT_W), lambda b, i: (b, i, q_col // ATT_W)),
             pl.BlockSpec((1, s, ATT_W), lambda b, i: (b, 0, k_col // ATT_W)),
             pl.BlockSpec((1, s, ATT_W), lambda b, i: (b, 0, v_col // ATT_W))],
            pl.BlockSpec((1, blk, ATT_W), lambda b, i: (b, i, 0)))


def _moba(proj):
    bsz, s, _ = proj.shape
    assert MOBA_HEADS == ATT_HEADS_PER_STEP
    blk = MOBA_BLOCK
    nb = s // blk
    slopes = 2.0 ** (-8.0 * np.arange(1, MOBA_HEADS + 1) / MOBA_HEADS)
    sl = jnp.asarray(np.repeat(slopes, HEAD_DIM).reshape(1, 1, ATT_W), F32)
    in_specs, out_spec = _attention_specs(s, A_MOBA_Q, A_MOBA_K, A_MOBA_V)
    return pl.pallas_call(
        functools.partial(_moba_kernel, nb=nb),
        grid=(bsz, nb),
        in_specs=in_specs + [pl.BlockSpec((1, 1, ATT_W), lambda b, i: (0, 0, 0))],
        out_specs=out_spec,
        out_shape=jax.ShapeDtypeStruct((bsz, s, ATT_W), MXU_DTYPE),
        scratch_shapes=[pltpu.VMEM((LANES, ATT_W), F32),
                        pltpu.VMEM((ATT_HEADS_PER_STEP, s, 2 * HEAD_DIM), MXU_DTYPE),
                        pltpu.VMEM((nb, ATT_HEADS_PER_STEP, 2 * HEAD_DIM, blk), MXU_DTYPE),
                        pltpu.VMEM((2 * ATT_HEADS_PER_STEP, blk, blk), F32)],
        compiler_params=_params(("parallel", "arbitrary")),
        name="moba",
    )(proj, proj, proj, sl)


def _stick_breaking(proj):
    bsz, s, _ = proj.shape
    assert SB_HEADS == ATT_HEADS_PER_STEP and SB_BLOCK == MOBA_BLOCK
    blk = SB_BLOCK
    nb = s // blk
    in_specs, out_spec = _attention_specs(s, A_SB_Q, A_SB_K, A_SB_V)
    return pl.pallas_call(
        functools.partial(_sb_kernel, nb=nb),
        grid=(bsz, nb),
        in_specs=in_specs,
        out_specs=out_spec,
        out_shape=jax.ShapeDtypeStruct((bsz, s, ATT_W), MXU_DTYPE),
        scratch_shapes=[pltpu.VMEM((nb, ATT_HEADS_PER_STEP, HEAD_DIM, blk), MXU_DTYPE)],
        compiler_params=_params(("parallel", "arbitrary")),
        name="stick_breaking",
    )(proj, proj, proj)


def _ret_kernel(q_ref, k_ref, v_ref, g_ref, gn_g_ref, gn_b_ref, o_ref, st_ref):
    ch = RET_CHUNK

    @pl.when(pl.program_id(1) == 0)
    def _():
        st_ref[...] = jnp.zeros_like(st_ref)

    i_col = lax.broadcasted_iota(jnp.int32, (ch, 1), 0).astype(F32)
    diff = (lax.broadcasted_iota(jnp.int32, (ch, ch), 0)
            - lax.broadcasted_iota(jnp.int32, (ch, ch), 1)).astype(F32)
    heads = range(RET_HEADS)
    log_g = [float(np.log(1.0 - 2.0 ** (-5.0 - h))) for h in heads]
    qs = [(q_ref[0, :, h * RET_DK:(h + 1) * RET_DK].astype(F32) * RET_DK ** -0.5).astype(MXU_DTYPE)
          for h in heads]
    k = [k_ref[0, :, h * RET_DK:(h + 1) * RET_DK] for h in heads]
    v = [v_ref[0, :, h * RET_DV:(h + 1) * RET_DV] for h in heads]
    prev = [st_ref[h] for h in heads]
    qk = [_dot_nt(qs[h], k[h]) for h in heads]
    o_cross = [_dot(qs[h], prev[h].astype(MXU_DTYPE)) for h in heads]
    kd = [(k[h].astype(F32) * jnp.exp((ch - 1.0 - i_col) * log_g[h])).astype(MXU_DTYPE)
          for h in heads]
    st_new = [_dot_tn(kd[h], v[h]) for h in heads]
    qkd = []
    for h in heads:
        decay = jnp.where(diff >= 0, jnp.exp(jnp.maximum(diff, 0.0) * log_g[h]), 0.0)
        qkd.append((qk[h] * decay).astype(MXU_DTYPE))
    o_intra = [_dot(qkd[h], v[h]) for h in heads]
    for h in heads:
        st_ref[h] = float(np.exp(ch * log_g[h])) * prev[h] + st_new[h]
        o = o_intra[h] + o_cross[h] * jnp.exp((i_col + 1.0) * log_g[h])
        mu = jnp.mean(o, axis=-1, keepdims=True)
        d = o - mu
        var = jnp.mean(d * d, axis=-1, keepdims=True)
        cs = slice(h * RET_DV, (h + 1) * RET_DV)
        on = d * lax.rsqrt(var + NORM_EPS) * gn_g_ref[:, cs] + gn_b_ref[:, cs]
        o_ref[0, :, cs] = (_silu(g_ref[0, :, cs].astype(F32)) * on).astype(o_ref.dtype)


def _ssd_kernel(z_ref, xbc_ref, dt_ref, cw_ref, cb_ref, dtb_ref, alog_ref, dsk_ref, ng_ref,
                o_ref, tail_ref, st_ref, y_ref):
    ch = SSD_CHUNK
    heads_per_group = SSD_HEADS // SSD_GROUPS
    halo = SSD_HALO

    @pl.when(pl.program_id(1) == 0)
    def _():
        tail_ref[0:halo, :] = jnp.zeros((halo, SSD_XBC), F32)
        st_ref[...] = jnp.zeros_like(st_ref)

    tail_ref[halo:, :] = xbc_ref[0].astype(F32)
    conv = cb_ref[...]
    for kk in range(SSD_CONV):
        off = halo - (SSD_CONV - 1) + kk
        conv = conv + cw_ref[kk:kk + 1, :] * tail_ref[off:off + ch, :]
    tail_ref[0:halo, :] = tail_ref[ch:ch + halo, :]
    xc = _silu(conv)
    xs = xc[:, :SSD_D_INNER]
    bm = xc[:, SSD_D_INNER:SSD_D_INNER + SSD_GROUPS * SSD_STATE]
    cm = xc[:, SSD_D_INNER + SSD_GROUPS * SSD_STATE:]

    dtr = dt_ref[0] + dtb_ref[...]
    dt = jnp.maximum(dtr, 0.0) + jnp.log1p(jnp.exp(-jnp.abs(dtr)))
    a = -jnp.exp(alog_ref[...])
    r_iota = lax.broadcasted_iota(jnp.int32, (ch, ch), 0)
    c_iota = lax.broadcasted_iota(jnp.int32, (ch, ch), 1)
    causal = r_iota >= c_iota
    tri = jnp.where(causal, 1.0, 0.0).astype(MXU_DTYPE)
    da = dt * a
    da_hi, da_lo = _split(da)
    cum = _dot(tri, da_hi) + _dot(tri, da_lo)
    cum_t = cum.T
    dt_t = dt.T
    cum_last = cum[ch - 1:ch, :]
    decay_st = jnp.exp(cum_last - cum) * dt
    e_cum = jnp.exp(cum)
    e_last = jnp.exp(cum_last)

    for g in range(SSD_GROUPS):
        bm_g = bm[:, g * SSD_STATE:(g + 1) * SSD_STATE].astype(MXU_DTYPE)
        cm_g = cm[:, g * SSD_STATE:(g + 1) * SSD_STATE].astype(MXU_DTYPE)
        cb = _dot_nt(cm_g, bm_g)
        for r in range(heads_per_group):
            hd = g * heads_per_group + r
            cs = slice(hd * SSD_HEAD_DIM, (hd + 1) * SSD_HEAD_DIM)
            x_h = xs[:, cs]
            seg = cum[:, hd:hd + 1] - cum_t[hd:hd + 1, :]
            lmat = jnp.exp(jnp.where(causal, seg, NEG_INF))
            w = cb * lmat * dt_t[hd:hd + 1, :]
            y = _dot(w.astype(MXU_DTYPE), x_h.astype(MXU_DTYPE))
            prev = st_ref[hd]
            y = y + _dot(cm_g, prev.astype(MXU_DTYPE)) * e_cum[:, hd:hd + 1]
            xd = (x_h * decay_st[:, hd:hd + 1]).astype(MXU_DTYPE)
            st_ref[hd] = e_last[:, hd:hd + 1] * prev + _dot_tn(bm_g, xd)
            y_ref[:, cs] = y + dsk_ref[:, cs] * x_h

    y = y_ref[...] * _silu(z_ref[0].astype(F32))
    gw = SSD_D_INNER // SSD_GROUPS
    for g in range(SSD_GROUPS):
        cs = slice(g * gw, (g + 1) * gw)
        yg = y[:, cs]
        ms = jnp.mean(yg * yg, axis=-1, keepdims=True)
        o_ref[0, :, cs] = (yg * lax.rsqrt(ms + LN_EPS) * ng_ref[:, cs]).astype(o_ref.dtype)


def _scan_kernel(rq_ref, rk_ref, rv_ref, rg_ref, gn_g_ref, gn_b_ref,
                 z_ref, xbc_ref, dt_ref, cw_ref, cb_ref, dtb_ref, alog_ref, dsk_ref, ng_ref,
                 o_ret_ref, o_ssd_ref, ret_st_ref, tail_ref, ssd_st_ref, y_ref):
    _ret_kernel(rq_ref, rk_ref, rv_ref, rg_ref, gn_g_ref, gn_b_ref, o_ret_ref, ret_st_ref)
    _ssd_kernel(z_ref, xbc_ref, dt_ref, cw_ref, cb_ref, dtb_ref, alog_ref, dsk_ref, ng_ref,
                o_ssd_ref, tail_ref, ssd_st_ref, y_ref)


def _scans(proj, proj_dt, gn_g, gn_b, conv_w, conv_b, dt_bias, a_log, d_skip, norm_g):
    bsz, s, _ = proj.shape
    assert RET_CHUNK == SSD_CHUNK
    ch = SSD_CHUNK
    wq, wv = RET_HEADS * RET_DK, RET_HEADS * RET_DV
    pad = LANES - SSD_HEADS
    dtb = jnp.pad(dt_bias, (0, pad)).reshape(1, LANES)
    alog = jnp.pad(a_log, (0, pad)).reshape(1, LANES)
    dsk = jnp.repeat(d_skip, SSD_HEAD_DIM).reshape(1, SSD_D_INNER)
    const = lambda b, i: (0, 0)
    return pl.pallas_call(
        _scan_kernel,
        grid=(bsz, s // ch),
        in_specs=[pl.BlockSpec((1, ch, wq), lambda b, i: (b, i, A_RET_Q // wq)),
                  pl.BlockSpec((1, ch, wq), lambda b, i: (b, i, A_RET_K // wq)),
                  pl.BlockSpec((1, ch, wv), lambda b, i: (b, i, A_RET_V // wv)),
                  pl.BlockSpec((1, ch, wv), lambda b, i: (b, i, B_RET_G // wv)),
                  pl.BlockSpec((1, wv), const),
                  pl.BlockSpec((1, wv), const),
                  pl.BlockSpec((1, ch, SSD_D_INNER), lambda b, i: (b, i, B_SSD_Z // SSD_D_INNER)),
                  pl.BlockSpec((1, ch, SSD_XBC), lambda b, i: (b, i, B_XBC // SSD_XBC)),
                  pl.BlockSpec((1, ch, LANES), lambda b, i: (b, i, 0)),
                  pl.BlockSpec((SSD_CONV, SSD_XBC), const),
                  pl.BlockSpec((1, SSD_XBC), const),
                  pl.BlockSpec((1, LANES), const),
                  pl.BlockSpec((1, LANES), const),
                  pl.BlockSpec((1, SSD_D_INNER), const),
                  pl.BlockSpec((1, SSD_D_INNER), const)],
        out_specs=[pl.BlockSpec((1, ch, wv), lambda b, i: (b, i, 0)),
                   pl.BlockSpec((1, ch, SSD_D_INNER), lambda b, i: (b, i, 0))],
        out_shape=[jax.ShapeDtypeStruct((bsz, s, wv), MXU_DTYPE),
                   jax.ShapeDtypeStruct((bsz, s, SSD_D_INNER), MXU_DTYPE)],
        scratch_shapes=[pltpu.VMEM((RET_HEADS, RET_DK, RET_DV), F32),
                        pltpu.VMEM((SSD_HALO + ch, SSD_XBC), F32),
                        pltpu.VMEM((SSD_HEADS, SSD_STATE, SSD_HEAD_DIM), F32),
                        pltpu.VMEM((ch, SSD_D_INNER), F32)],
        compiler_params=_params(("parallel", "arbitrary")),
        name="retention_ssd",
    )(proj, proj, proj, proj, gn_g.reshape(1, wv), gn_b.reshape(1, wv),
      proj, proj, proj_dt, conv_w, conv_b.reshape(1, SSD_XBC), dtb, alog, dsk,
      norm_g.reshape(1, SSD_D_INNER))


def _merge_kernel(oa_ref, ob_ref, oc_ref, od_ref, gates_ref, x_ref, g1_ref, wbr_ref, wout_ref,
                  lg_ref, lb_ref, o_ref):
    merged = None
    for n, br in enumerate((oa_ref, ob_ref, oc_ref, od_ref)):
        y = _dot(br[0], wbr_ref[n])
        t = _sigmoid(gates_ref[0, :, n * D_MODEL:(n + 1) * D_MODEL].astype(F32)) * y
        merged = t if merged is None else merged + t
    mix = _dot(merged.astype(MXU_DTYPE), wout_ref[...])
    o_ref[0] = _resid_ln(x_ref[0], g1_ref[0], mix, lg_ref[...], lb_ref[...])


def _merge(o_a, o_b, o_c, o_d, proj, x, g1, w_br, w_out, ln_g, ln_b, tm=512):
    bsz, s, d = x.shape
    tm = min(tm, s)
    gw = N_BRANCH * d
    row = lambda b, i: (b, i, 0)
    const2 = lambda b, i: (0, 0)
    br_spec = pl.BlockSpec((1, tm, BRANCH_W), row)
    return pl.pallas_call(
        _merge_kernel,
        grid=(bsz, s // tm),
        in_specs=[br_spec, br_spec, br_spec, br_spec,
                  pl.BlockSpec((1, tm, gw), lambda b, i: (b, i, B_GATES // gw)),
                  pl.BlockSpec((1, tm, d), row),
                  pl.BlockSpec((1, 1, d), lambda b, i: (b, 0, 0)),
                  pl.BlockSpec((N_BRANCH, BRANCH_W, d), lambda b, i: (0, 0, 0)),
                  pl.BlockSpec((d, d), const2),
                  pl.BlockSpec((1, d), const2),
                  pl.BlockSpec((1, d), const2)],
        out_specs=pl.BlockSpec((1, tm, d), row),
        out_shape=jax.ShapeDtypeStruct((bsz, s, d), F32),
        compiler_params=_params(("parallel", "parallel")),
        name="merge",
    )(o_a, o_b, o_c, o_d, proj, x, g1, w_br, w_out, ln_g.reshape(1, d), ln_b.reshape(1, d))


def _ffn_kernel(x_ref, sc_ref, sh_ref, g2_ref, wg_ref, wu_ref, wd_ref, lg_ref, lb_ref,
                o_ref, h_ref, acc_ref):
    j = pl.program_id(2)

    @pl.when(j == 0)
    def _():
        h = x_ref[0] * (1.0 + sc_ref[0]) + sh_ref[0]
        h_ref[...] = h.astype(MXU_DTYPE)
        acc_ref[...] = jnp.zeros_like(acc_ref)

    h = h_ref[...]
    act = _silu(_dot(h, wg_ref[...])) * _dot(h, wu_ref[...])
    acc_ref[...] += _dot(act.astype(MXU_DTYPE), wd_ref[...])

    @pl.when(j == pl.num_programs(2) - 1)
    def _():
        o_ref[0] = _resid_ln(x_ref[0], g2_ref[0], acc_ref[...], lg_ref[...], lb_ref[...])


def _ffn(x, sc, sh, g2, w_gu, w_down, ln_g, ln_b, tm=512, tf=1408):
    bsz, s, d = x.shape
    tm = min(tm, s)
    nf = D_FF // tf
    row = lambda b, i, j: (b, i, 0)
    vec = lambda b, i, j: (b, 0, 0)
    const2 = lambda b, i, j: (0, 0)
    return pl.pallas_call(
        _ffn_kernel,
        grid=(bsz, s // tm, nf),
        in_specs=[pl.BlockSpec((1, tm, d), row),
                  pl.BlockSpec((1, 1, d), vec), pl.BlockSpec((1, 1, d), vec),
                  pl.BlockSpec((1, 1, d), vec),
                  pl.BlockSpec((d, tf), lambda b, i, j: (0, j)),
                  pl.BlockSpec((d, tf), lambda b, i, j: (0, nf + j)),
                  pl.BlockSpec((tf, d), lambda b, i, j: (j, 0)),
                  pl.BlockSpec((1, d), const2), pl.BlockSpec((1, d), const2)],
        out_specs=pl.BlockSpec((1, tm, d), row),
        out_shape=jax.ShapeDtypeStruct((bsz, s, d), F32),
        scratch_shapes=[pltpu.VMEM((tm, d), MXU_DTYPE), pltpu.VMEM((tm, d), F32)],
        compiler_params=_params(("parallel", "parallel", "arbitrary")),
        name="ffn_dense",
    )(x, sc, sh, g2, w_gu, w_gu, w_down, ln_g.reshape(1, d), ln_b.reshape(1, d))


def _router_kernel(x_ref, sc_ref, sh_ref, rw_ref, rb_ref,
                   h_ref, comb_ref, slot_ref, slot_t_ref, cnt_ref):
    tm = x_ref.shape[1]
    lane = lax.broadcasted_iota(jnp.int32, (tm, LANES), 1)
    h = x_ref[0] * (1.0 + sc_ref[0]) + sh_ref[0]
    h_ref[0] = h.astype(MXU_DTYPE)
    logits = _dot3(h, rw_ref[...]) + rb_ref[...]
    logits = jnp.where(lane < N_EXPERTS, logits, -jnp.inf)
    v1 = jnp.max(logits, axis=1, keepdims=True)
    i1 = jnp.min(jnp.where(logits == v1, lane, LANES), axis=1, keepdims=True)
    rest = jnp.where(lane == i1, -jnp.inf, logits)
    v2 = jnp.max(rest, axis=1, keepdims=True)
    i2 = jnp.min(jnp.where(rest == v2, lane, LANES), axis=1, keepdims=True)
    e2 = jnp.exp(v2 - v1)
    w1 = 1.0 / (1.0 + e2)
    w2 = e2 / (1.0 + e2)
    comb_ref[0] = jnp.where(lane == i1, w1, 0.0) + jnp.where(lane == i2, w2, 0.0)
    routed = jnp.logical_or(lane == i1, lane == i2)
    earlier = (lax.broadcasted_iota(jnp.int32, (tm, tm), 1)
               < lax.broadcasted_iota(jnp.int32, (tm, tm), 0))
    before = _dot(jnp.where(earlier, 1.0, 0.0).astype(MXU_DTYPE),
                  jnp.where(routed, 1.0, 0.0).astype(MXU_DTYPE))
    slot = jnp.where(routed, before, -1.0)
    slot_ref[0] = slot
    slot_t_ref[0] = slot.T[0:N_EXPERTS, :]
    cnt_ref[0] = jnp.sum(jnp.where(routed, 1, 0), axis=0, keepdims=True)


def _moe_kernel(cnt_ref, h_ref, x_ref, g2_ref, comb_ref, slot_ref, slot_t_ref,
                wg_ref, wu_ref, wd_ref, lg_ref, lb_ref, o_ref, xe_ref, ye_ref, *, n_tiles):
    tm = x_ref.shape[1]
    e = pl.program_id(2)
    j = pl.program_id(3)
    last_j = pl.num_programs(3) - 1
    count = cnt_ref[(pl.program_id(0) * n_tiles + pl.program_id(1)) * N_EXPERTS + e]
    n_full = jnp.maximum(count - MOE_TAIL_MIN - 1, 0) // MOE_ROWS
    tail0 = pl.multiple_of(n_full * MOE_ROWS, MOE_ROWS)
    tail_steps = (count - n_full * MOE_ROWS + MOE_ROW_STEP - 1) // MOE_ROW_STEP

    def for_pieces(fn):
        def full_piece(c, carry):
            fn(pl.multiple_of(c * MOE_ROWS, MOE_ROWS), MOE_ROWS)
            return carry

        lax.fori_loop(0, n_full, full_piece, 0)
        for steps in range(1, (MOE_ROWS + MOE_TAIL_MIN) // MOE_ROW_STEP + 1):
            @pl.when(tail_steps == steps)
            def _(steps=steps):
                fn(tail0, steps * MOE_ROW_STEP)

    @pl.when(jnp.logical_and(e == 0, j == 0))
    def _():
        o_ref[0] = jnp.zeros((tm, D_MODEL), F32)

    @pl.when(j == 0)
    def _():
        slot_row = slot_t_ref[0, pl.ds(e, 1), :]

        def gather(r0, rows):
            slots = (lax.broadcasted_iota(jnp.int32, (rows, tm), 0) + r0).astype(F32)
            pick = jnp.where(slot_row == slots, 1.0, 0.0).astype(MXU_DTYPE)
            xe_ref[pl.ds(r0, rows), :] = _dot(pick, h_ref[0]).astype(MXU_DTYPE)
            ye_ref[pl.ds(r0, rows), :] = jnp.zeros((rows, D_MODEL), F32)

        for_pieces(gather)

    def expert_rows(r0, rows):
        xc = xe_ref[pl.ds(r0, rows), :]
        act = _silu(_dot(xc, wg_ref[0])) * _dot(xc, wu_ref[0])
        ye_ref[pl.ds(r0, rows), :] += _dot(act.astype(MXU_DTYPE), wd_ref[0])

    for_pieces(expert_rows)

    @pl.when(j == last_j)
    def _():
        lane = lax.broadcasted_iota(jnp.int32, (tm, LANES), 1)
        mine = lane == e
        slot_col = jnp.sum(jnp.where(mine, slot_ref[0], 0.0), axis=1, keepdims=True)
        comb_col = jnp.sum(jnp.where(mine, comb_ref[0], 0.0), axis=1, keepdims=True)

        def scatter(r0, rows):
            slots = (lax.broadcasted_iota(jnp.int32, (tm, rows), 1) + r0).astype(F32)
            place = jnp.where(slot_col == slots, 1.0, 0.0).astype(MXU_DTYPE)
            o_ref[0] += comb_col * _dot(place, ye_ref[pl.ds(r0, rows), :].astype(MXU_DTYPE))

        for_pieces(scatter)

    @pl.when(jnp.logical_and(e == pl.num_programs(2) - 1, j == last_j))
    def _():
        o_ref[0] = _resid_ln(x_ref[0], g2_ref[0], o_ref[0], lg_ref[...], lb_ref[...])


def _moe(x, sc, sh, g2, router_w, router_b, w_gu, w_down, ln_g, ln_b, tm=1024, tf=1792):
    bsz, s, d = x.shape
    tm = min(tm, s)
    nt = s // tm
    nf = D_FF_EXPERT // tf
    rw = jnp.pad(router_w, ((0, 0), (0, LANES - N_EXPERTS)))
    rb = jnp.pad(router_b, (0, LANES - N_EXPERTS)).reshape(1, LANES)

    row2 = lambda b, i: (b, i, 0)
    vec2 = lambda b, i: (b, 0, 0)
    h, comb, slot, slot_t, cnt = pl.pallas_call(
        _router_kernel,
        grid=(bsz, nt),
        in_specs=[pl.BlockSpec((1, tm, d), row2),
                  pl.BlockSpec((1, 1, d), vec2), pl.BlockSpec((1, 1, d), vec2),
                  pl.BlockSpec((d, LANES), lambda b, i: (0, 0)),
                  pl.BlockSpec((1, LANES), lambda b, i: (0, 0))],
        out_specs=[pl.BlockSpec((1, tm, d), row2),
                   pl.BlockSpec((1, tm, LANES), row2),
                   pl.BlockSpec((1, tm, LANES), row2),
                   pl.BlockSpec((1, N_EXPERTS, tm), lambda b, i: (b * nt + i, 0, 0)),
                   pl.BlockSpec((1, 1, LANES), lambda b, i: (b * nt + i, 0, 0))],
        out_shape=[jax.ShapeDtypeStruct((bsz, s, d), MXU_DTYPE),
                   jax.ShapeDtypeStruct((bsz, s, LANES), F32),
                   jax.ShapeDtypeStruct((bsz, s, LANES), F32),
                   jax.ShapeDtypeStruct((bsz * nt, N_EXPERTS, tm), F32),
                   jax.ShapeDtypeStruct((bsz * nt, 1, LANES), jnp.int32)],
        compiler_params=_params(("parallel", "parallel")),
        name="moe_router",
    )(x, sc, sh, rw, rb)
    counts = cnt[:, 0, :N_EXPERTS].reshape(bsz * nt * N_EXPERTS)

    row = lambda b, i, e, j, cnt: (b, i, 0)
    vec = lambda b, i, e, j, cnt: (b, 0, 0)
    const2 = lambda b, i, e, j, cnt: (0, 0)
    once = pl.Buffered(1)
    grid_spec = pltpu.PrefetchScalarGridSpec(
        num_scalar_prefetch=1,
        grid=(bsz, nt, N_EXPERTS, nf),
        in_specs=[pl.BlockSpec((1, tm, d), row, pipeline_mode=once),
                  pl.BlockSpec((1, tm, d), row, pipeline_mode=once),
                  pl.BlockSpec((1, 1, d), vec),
                  pl.BlockSpec((1, tm, LANES), row, pipeline_mode=once),
                  pl.BlockSpec((1, tm, LANES), row, pipeline_mode=once),
                  pl.BlockSpec((1, N_EXPERTS, tm), lambda b, i, e, j, cnt: (b * nt + i, 0, 0)),
                  pl.BlockSpec((1, d, tf), lambda b, i, e, j, cnt: (e, 0, j)),
                  pl.BlockSpec((1, d, tf), lambda b, i, e, j, cnt: (e, 0, nf + j)),
                  pl.BlockSpec((1, tf, d), lambda b, i, e, j, cnt: (e, j, 0)),
                  pl.BlockSpec((1, d), const2), pl.BlockSpec((1, d), const2)],
        out_specs=pl.BlockSpec((1, tm, d), row),
        scratch_shapes=[pltpu.VMEM((tm, d), MXU_DTYPE), pltpu.VMEM((tm, d), F32)])
    return pl.pallas_call(
        functools.partial(_moe_kernel, n_tiles=nt),
        grid_spec=grid_spec,
        out_shape=jax.ShapeDtypeStruct((bsz, s, d), F32),
        compiler_params=pltpu.CompilerParams(
            dimension_semantics=("parallel", "parallel", "arbitrary", "arbitrary"),
            vmem_limit_bytes=MOE_VMEM_LIMIT),
        name="moe_experts",
    )(counts, h, x, g2, comb, slot, slot_t, w_gu, w_gu, w_down,
      ln_g.reshape(1, d), ln_b.reshape(1, d))


def _split_w_in(w):
    p = np.cumsum([0, 512, 512, 512, 256, 256, 512, 512, 512, SSD_XBC, SSD_HEADS, 512, 512, 512,
                   N_BRANCH * D_MODEL])
    mq, mk, mv, rq, rk, rv, rg, sz, sxbc, sdt, bq, bk, bv, gates = (
        w[:, p[i]:p[i + 1]] for i in range(14))
    w_main = jnp.concatenate([mq, mk, mv, rq, rk, rv, bq, bk, bv, gates, rg, sz, sxbc], axis=1)
    w_dt = jnp.pad(sdt, ((0, 0), (0, LANES - SSD_HEADS)))
    return w_main.astype(MXU_DTYPE), w_dt.astype(MXU_DTYPE)


def kernel(x, c, w_ada, b_ada, w_in, conv_w, conv_b, dt_bias, a_log, d_skip, ssm_norm_g, ret_gn_g, ret_gn_b, w_br, w_out, ln1_g, ln1_b, ln2_g, ln2_b, ffn_w_gu, ffn_w_down, router_w, router_b, expert_w_gu, expert_w_down):
    bsz = x.shape[0]
    mod = _ada(c, w_ada, b_ada)
    for l in range(DEPTH):
        sh1, sc1, g1, sh2, sc2, g2 = (
            mod[l, :, i * D_MODEL:(i + 1) * D_MODEL].reshape(bsz, 1, D_MODEL) for i in range(6))
        w_main, w_dt = _split_w_in(w_in[l])
        proj = _inproj(x, sc1, sh1, w_main, MXU_DTYPE, tn=2048)
        proj_dt = _inproj(x, sc1, sh1, w_dt, F32, tn=LANES)
        o_a = _moba(proj)
        o_d = _stick_breaking(proj)
        o_b, o_c = _scans(proj, proj_dt, ret_gn_g[l], ret_gn_b[l], conv_w[l], conv_b[l],
                          dt_bias[l], a_log[l], d_skip[l], ssm_norm_g[l])
        x = _merge(o_a, o_b, o_c, o_d, proj, x, g1, w_br[l].astype(MXU_DTYPE),
                   w_out[l].astype(MXU_DTYPE), ln1_g[l], ln1_b[l])
        if l % 2 == 0:
            x = _ffn(x, sc2, sh2, g2, ffn_w_gu[l // 2].astype(MXU_DTYPE),
                     ffn_w_down[l // 2].astype(MXU_DTYPE), ln2_g[l], ln2_b[l])
        else:
            x = _moe(x, sc2, sh2, g2, router_w[l // 2], router_b[l // 2],
                     expert_w_gu[l // 2].astype(MXU_DTYPE), expert_w_down[l // 2].astype(MXU_DTYPE),
                     ln2_g[l], ln2_b[l])
    return x
```

```python
import functools

import numpy as np
import jax
import jax.numpy as jnp
from jax import lax
from jax.experimental import pallas as pl
from jax.experimental.pallas import tpu as pltpu

F32 = jnp.float32
MXU_DTYPE = jnp.bfloat16

D_MODEL = 1024
DEPTH = 2
MOBA_HEADS = 8
MOBA_BLOCK = 256
MOBA_TOPK = 3
RET_HEADS = 4
RET_DK = 64
RET_DV = 128
RET_CHUNK = 256
SSD_D_INNER = 512
SSD_HEAD_DIM = 64
SSD_HEADS = 8
SSD_GROUPS = 2
SSD_STATE = 128
SSD_CONV = 4
SSD_CHUNK = 256
SSD_XBC = SSD_D_INNER + 2 * SSD_GROUPS * SSD_STATE
SSD_HALO = 8
SB_HEADS = 8
SB_BLOCK = 256
HEAD_DIM = 64
ATT_HEADS_PER_STEP = 8
ATT_W = ATT_HEADS_PER_STEP * HEAD_DIM
BRANCH_W = 512
N_BRANCH = 4
D_FF = 2816
N_EXPERTS = 8
D_FF_EXPERT = 3584
DEEPNORM_ALPHA = (2.0 * DEPTH) ** 0.25
LN_EPS = 1e-5
NORM_EPS = 1e-6
NEG_INF = -1e30
LOG2E = 1.4426950408889634
LANES = 128
VMEM_LIMIT = 48 * 2 ** 20
MOE_VMEM_LIMIT = 56 * 2 ** 20
MOE_ROWS = 256
MOE_ROW_STEP = 64
MOE_TAIL_MIN = 128
MOE_MOVE_STEP = 128

A_MOBA_Q, A_MOBA_K, A_MOBA_V = 0, 512, 1024
A_RET_Q, A_RET_K, A_RET_V = 1536, 1792, 2048
A_SB_Q, A_SB_K, A_SB_V = 2560, 3072, 3584
B_GATES, B_RET_G, B_SSD_Z, B_XBC = 4096, 8192, 8704, 9216
PROJ_WIDTH = 10240


def _params(sem):
    return pltpu.CompilerParams(dimension_semantics=sem, vmem_limit_bytes=VMEM_LIMIT)


def _dot(a, b):
    return jnp.dot(a, b, preferred_element_type=F32)


def _dot_nt(a, b):
    return lax.dot_general(a, b, (((1,), (1,)), ((), ())), preferred_element_type=F32)


def _dot_tn(a, b):
    return lax.dot_general(a, b, (((0,), (0,)), ((), ())), preferred_element_type=F32)


def _split(x):
    hi = x.astype(MXU_DTYPE)
    lo = (x - hi.astype(F32)).astype(MXU_DTYPE)
    return hi, lo


def _dot3(a, b):
    ah, al = _split(a)
    bh, bl = _split(b)
    return _dot(ah, bh) + _dot(al, bh) + _dot(ah, bl)


def _sigmoid(x):
    return 1.0 / (1.0 + jnp.exp(-x))


def _silu(x):
    return x * _sigmoid(x)


def _resid_ln(x, gate, f, ln_g, ln_b):
    y = DEEPNORM_ALPHA * x + gate * f
    mu = jnp.mean(y, axis=-1, keepdims=True)
    d = y - mu
    var = jnp.mean(d * d, axis=-1, keepdims=True)
    return d * lax.rsqrt(var + LN_EPS) * ln_g + ln_b


def _ada_kernel(c_ref, w_ref, b_ref, o_ref):
    c = c_ref[...]
    o_ref[0] = _dot3(_silu(c), w_ref[0]) + b_ref[0]


def _ada(c, w_ada, b_ada):
    depth, d, n = w_ada.shape
    bsz = c.shape[0]
    tn = 1536
    return pl.pallas_call(
        _ada_kernel,
        grid=(depth, n // tn),
        in_specs=[pl.BlockSpec((bsz, d), lambda l, j: (0, 0)),
                  pl.BlockSpec((1, d, tn), lambda l, j: (l, 0, j)),
                  pl.BlockSpec((1, 1, tn), lambda l, j: (l, 0, j))],
        out_specs=pl.BlockSpec((1, bsz, tn), lambda l, j: (l, 0, j)),
        out_shape=jax.ShapeDtypeStruct((depth, bsz, n), F32),
        compiler_params=_params(("arbitrary", "arbitrary")),
        name="ada_mod",
    )(c, w_ada, b_ada.reshape(depth, 1, n))


def _inproj_kernel(x_ref, sc_ref, sh_ref, w_ref, o_ref, h_ref):
    @pl.when(pl.program_id(2) == 0)
    def _():
        h = x_ref[0] * (1.0 + sc_ref[0]) + sh_ref[0]
        h_ref[...] = h.astype(MXU_DTYPE)

    o_ref[0] = _dot(h_ref[...], w_ref[...]).astype(o_ref.dtype)


def _inproj(x, sc, sh, w, out_dtype, tn, tm=1024):
    bsz, s, d = x.shape
    n = w.shape[1]
    tm = min(tm, s)
    return pl.pallas_call(
        _inproj_kernel,
        grid=(bsz, s // tm, n // tn),
        in_specs=[pl.BlockSpec((1, tm, d), lambda b, i, j: (b, i, 0)),
                  pl.BlockSpec((1, 1, d), lambda b, i, j: (b, 0, 0)),
                  pl.BlockSpec((1, 1, d), lambda b, i, j: (b, 0, 0)),
                  pl.BlockSpec((d, tn), lambda b, i, j: (0, j))],
        out_specs=pl.BlockSpec((1, tm, tn), lambda b, i, j: (b, i, j)),
        out_shape=jax.ShapeDtypeStruct((bsz, s, n), out_dtype),
        scratch_shapes=[pltpu.VMEM((tm, d), MXU_DTYPE)],
        compiler_params=_params(("parallel", "parallel", "arbitrary")),
        name="in_proj",
    )(x, sc, sh, w)


def _in_lockstep(*gens):
    results = [None] * len(gens)
    live = list(range(len(gens)))
    while live:
        for i in list(live):
            try:
                next(gens[i])
            except StopIteration as stop:
                results[i] = stop.value
                live.remove(i)
    return results


def _moba_parts(q_ref, k_ref, v_ref, sl_ref, o_ref, km_ref, ka_ref, vt_ref, nb):
    blk = MOBA_BLOCK
    nh = ATT_HEADS_PER_STEP
    heads = range(nh)
    qi = pl.program_id(1)
    s_len = nb * blk
    nb_pad = -(-nb // 8) * 8
    extra = 2 * HEAD_DIM - HEAD_DIM
    assert nb_pad + 8 <= extra
    cols = [slice(h * HEAD_DIM, (h + 1) * HEAD_DIM) for h in heads]

    @pl.when(qi == 0)
    def _():
        k_all = k_ref[0]
        kk = k_all.astype(F32).reshape(nb, blk, ATT_W)
        km_ref[...] = jnp.zeros_like(km_ref)
        km_ref[0:nb, :] = jnp.sum(kk, axis=1) * (1.0 / blk)
        pos = lax.broadcasted_iota(jnp.int32, (s_len, extra), 0)
        lane = lax.broadcasted_iota(jnp.int32, (s_len, extra), 1)
        one_hot = jnp.where(lane == pos // blk, 1.0, 0.0)
        pos_hi = jnp.where(lane == nb_pad, (pos // blk * blk).astype(F32), 0.0)
        pos_lo = jnp.where(lane == nb_pad + 1, (pos % blk).astype(F32), 0.0)
        tail = (one_hot + pos_hi + pos_lo).astype(MXU_DTYPE)
        for h in heads:
            ka_ref[h] = jnp.concatenate([k_all[:, cols[h]], tail], axis=1)

        ones_rows = jnp.where(lax.broadcasted_iota(jnp.int32, (extra, blk), 0) == 0, 1.0, 0.0)

        def transpose_block(j, carry):
            j0 = pl.multiple_of(j * blk, blk)
            vt = v_ref[0, pl.ds(j0, blk), :].astype(F32).T
            for h in heads:
                vt_ref[j, h] = jnp.concatenate([vt[cols[h], :], ones_rows], axis=0).astype(MXU_DTYPE)
            return carry

        lax.fori_loop(0, nb, transpose_block, 0)

    key_i = lax.broadcasted_iota(jnp.int32, (blk, blk), 0)
    qry_i = lax.broadcasted_iota(jnp.int32, (blk, blk), 1)
    n_iota = lax.broadcasted_iota(jnp.int32, (nb_pad, blk), 0)
    past = n_iota < qi
    k_sel = min(MOBA_TOPK, max(nb - 1, 1))
    q_t = q_ref[0].astype(F32).T
    slope_rows = lax.broadcasted_iota(jnp.int32, (8, blk), 0) < 2

    qa = []
    for h in heads:
        qh_t = q_t[cols[h], :]
        slope = sl_ref[0, :, h * HEAD_DIM:h * HEAD_DIM + 1]
        km_hi, km_lo = _split(km_ref[:, cols[h]])
        qh_mxu = qh_t.astype(MXU_DTYPE)
        gate = (_dot(km_hi, qh_mxu) + _dot(km_lo, qh_mxu))[0:nb_pad, :]
        g = jnp.where(past, gate, NEG_INF)
        g = jnp.where(n_iota < nb, g, -jnp.inf)
        sel = jnp.zeros((nb_pad, blk), F32)
        for _ in range(k_sel):
            mx = jnp.max(g, axis=0, keepdims=True)
            idx = jnp.min(jnp.where(g == mx, n_iota, nb_pad), axis=0, keepdims=True)
            pick = n_iota == idx
            sel = jnp.where(pick, 1.0, sel)
            g = jnp.where(pick, -jnp.inf, g)
        keep = jnp.logical_or(jnp.logical_and(past, sel > 0.5), n_iota == qi)
        bias = jnp.where(keep, 0.0, NEG_INF)
        alibi = jnp.where(slope_rows, slope, 0.0)
        pad = jnp.zeros((extra - nb_pad - 8, blk), F32)
        qa.append(jnp.concatenate([qh_t * HEAD_DIM ** -0.5, bias, alibi, pad],
                                  axis=0).astype(MXU_DTYPE))

    def scores(js):
        starts = [pl.multiple_of(j * blk, blk) for j in js]
        return tuple(_dot(ka_ref[h, pl.ds(j0, blk), :], qa[h]) for j0 in starts for h in heads)

    def rest(js, flat_scores, carry, diag):
        s = [flat_scores[t * nh:(t + 1) * nh] for t in range(len(js))]
        m = list(carry[:nh]) if carry is not None else [None] * nh
        acc = list(carry[nh:]) if carry is not None else [None] * nh
        p = {}
        for h in heads:
            sh = [jnp.where(key_i <= qry_i, s[t][h], NEG_INF) if diag else s[t][h]
                  for t in range(len(js))]
            top = jnp.max(sh[0], axis=0, keepdims=True)
            for t in range(1, len(js)):
                top = jnp.maximum(top, jnp.max(sh[t], axis=0, keepdims=True))
            if m[h] is None:
                m[h] = top
            else:
                m_new = jnp.maximum(m[h], top)
                acc[h] = jnp.exp(m[h] - m_new) * acc[h]
                m[h] = m_new
            for t in range(len(js)):
                p[t, h] = jnp.exp(sh[t] - m[h]).astype(MXU_DTYPE)
        yield
        for t, j in enumerate(js):
            for h in heads:
                pv = _dot(vt_ref[j, h], p[t, h])
                acc[h] = pv if acc[h] is None else acc[h] + pv
        return tuple(m) + tuple(acc)

    def finish(carry):
        out_t = [carry[nh + h][0:HEAD_DIM, :] / carry[nh + h][HEAD_DIM:HEAD_DIM + 1, :]
                 for h in heads]
        o_ref[0] = jnp.concatenate(out_t, axis=0).T.astype(o_ref.dtype)

    return scores, rest, None, finish


def _sb_parts(q_ref, k_ref, v_ref, o_ref, vt_ref, nb):
    blk = SB_BLOCK
    nh = ATT_HEADS_PER_STEP
    heads = range(nh)
    qi = pl.program_id(1)
    cols = [slice(h * HEAD_DIM, (h + 1) * HEAD_DIM) for h in heads]

    @pl.when(qi == 0)
    def _():
        def transpose_block(j, carry):
            j0 = pl.multiple_of(j * blk, blk)
            vt = v_ref[0, pl.ds(j0, blk), :].astype(F32).T
            for h in heads:
                vt_ref[j, h] = vt[cols[h], :].astype(MXU_DTYPE)
            return carry

        lax.fori_loop(0, nb, transpose_block, 0)

    key_i = lax.broadcasted_iota(jnp.int32, (blk, blk), 0)
    qry_i = lax.broadcasted_iota(jnp.int32, (blk, blk), 1)
    causal = key_i < qry_i
    later = jnp.where(qry_i > key_i, 1.0, 0.0).astype(MXU_DTYPE)
    qs = [(q_ref[0, :, cols[h]].astype(F32) * HEAD_DIM ** -0.5).astype(MXU_DTYPE)
          for h in heads]

    def scores(js):
        starts = [pl.multiple_of(j * blk, blk) for j in js]
        return tuple(_dot_nt(k_ref[0, pl.ds(j0, blk), cols[h]], qs[h])
                     for j0 in starts for h in heads)

    def rest(js, flat_scores, carry, diag):
        accs, rests = carry[:nh], carry[nh:]
        z = [flat_scores[t * nh:(t + 1) * nh] for t in range(len(js))]
        log_beta, log_1mb, parts = [], [], []
        for t in range(len(js)):
            for h in heads:
                zz = z[t][h]
                lb = jnp.minimum(zz, 0.0) - jnp.log(1.0 + jnp.exp2(jnp.abs(zz) * (-LOG2E)))
                l1 = lb - zz
                if diag:
                    l1 = jnp.where(causal, l1, 0.0)
                log_beta.append(lb)
                log_1mb.append(l1)
                parts.append(l1.astype(MXU_DTYPE))
        yield
        rem = [_dot(later, p) for p in parts]
        yield
        w = []
        rests = list(rests)
        for t in range(len(js)):
            for h in heads:
                i = t * nh + h
                wh = jnp.exp(log_beta[i] + rem[i] + rests[h])
                if diag:
                    wh = jnp.where(causal, wh, 0.0)
                w.append(wh.astype(MXU_DTYPE))
                rests[h] = rests[h] + rem[i][0:1, :] + log_1mb[i][0:1, :]
        yield
        accs = list(accs)
        for t, j in enumerate(js):
            for h in heads:
                accs[h] = accs[h] + _dot(vt_ref[j, h], w[t * nh + h])
        return tuple(accs) + tuple(rests)

    start = tuple([jnp.zeros((HEAD_DIM, blk), F32)] * nh + [jnp.zeros((1, blk), F32)] * nh)

    def finish(carry):
        o_ref[0] = jnp.concatenate(carry[:nh], axis=0).T.astype(o_ref.dtype)

    return scores, rest, start, finish


def _walk_key_blocks(scores, rest, start, finish, z_ref=None, lead=0):
    qi = pl.program_id(1)
    run = lambda gen: _in_lockstep(gen)[0]

    state = run(rest([qi], scores([qi]), start, True))

    def single(t, state):
        return run(rest([qi - 1], scores([qi - 1]), state, False))

    state = lax.fori_loop(0, qi % 2, single, state)
    first = qi - 1 - qi % 2

    if z_ref is None:
        def pair(t, state):
            j = first - 2 * t
            return run(rest([j, j - 1], scores([j, j - 1]), state, False))
    else:
        def look_ahead(t):
            j = jnp.maximum(first - 2 * t, 1)
            for i, z in enumerate(scores([j, j - 1])):
                z_ref[i] = z

        def pair(t, state):
            j = first - 2 * t
            stages = rest([j, j - 1], [z_ref[i] for i in range(z_ref.shape[0])], state, False)
            for _ in range(lead):
                next(stages)
            look_ahead(t + 1)
            return run(stages)

        look_ahead(0)

    finish(lax.fori_loop(0, qi // 2, pair, state))


def _moba_kernel(q_ref, k_ref, v_ref, sl_ref, o_ref, km_ref, ka_ref, vt_ref, z_ref, *, nb):
    _walk_key_blocks(*_moba_parts(q_ref, k_ref, v_ref, sl_ref, o_ref, km_ref, ka_ref, vt_ref, nb),
                     z_ref=z_ref, lead=0)


def _sb_kernel(q_ref, k_ref, v_ref, o_ref, vt_ref, *, nb):
    _walk_key_blocks(*_sb_parts(q_ref, k_ref, v_ref, o_ref, vt_ref, nb))


def _attention_specs(s, q_col, k_col, v_col):
    blk = MOBA_BLOCK
    return ([pl.BlockSpec((1, blk, ATT_W), lambda b, i: (b, i, q_col // ATT_W)),
             pl.BlockSpec((1, s, ATT_W), lambda b, i: (b, 0, k_col // ATT_W)),
             pl.BlockSpec((1, s, ATT_W), lambda b, i: (b, 0, v_col // ATT_W))],
            pl.BlockSpec((1, blk, ATT_W), lambda b, i: (b, i, 0)))


def _moba(proj):
    bsz, s, _ = proj.shape
    assert MOBA_HEADS == ATT_HEADS_PER_STEP
    blk = MOBA_BLOCK
    nb = s // blk
    slopes = 2.0 ** (-8.0 * np.arange(1, MOBA_HEADS + 1) / MOBA_HEADS)
    sl = jnp.asarray(np.repeat(slopes, HEAD_DIM).reshape(1, 1, ATT_W), F32)
    in_specs, out_spec = _attention_specs(s, A_MOBA_Q, A_MOBA_K, A_MOBA_V)
    return pl.pallas_call(
        functools.partial(_moba_kernel, nb=nb),
        grid=(bsz, nb),
        in_specs=in_specs + [pl.BlockSpec((1, 1, ATT_W), lambda b, i: (0, 0, 0))],
        out_specs=out_spec,
        out_shape=jax.ShapeDtypeStruct((bsz, s, ATT_W), MXU_DTYPE),
        scratch_shapes=[pltpu.VMEM((LANES, ATT_W), F32),
                        pltpu.VMEM((ATT_HEADS_PER_STEP, s, 2 * HEAD_DIM), MXU_DTYPE),
                        pltpu.VMEM((nb, ATT_HEADS_PER_STEP, 2 * HEAD_DIM, blk), MXU_DTYPE),
                        pltpu.VMEM((2 * ATT_HEADS_PER_STEP, blk, blk), F32)],
        compiler_params=_params(("parallel", "arbitrary")),
        name="moba",
    )(proj, proj, proj, sl)


def _stick_breaking(proj):
    bsz, s, _ = proj.shape
    assert SB_HEADS == ATT_HEADS_PER_STEP and SB_BLOCK == MOBA_BLOCK
    blk = SB_BLOCK
    nb = s // blk
    in_specs, out_spec = _attention_specs(s, A_SB_Q, A_SB_K, A_SB_V)
    return pl.pallas_call(
        functools.partial(_sb_kernel, nb=nb),
        grid=(bsz, nb),
        in_specs=in_specs,
        out_specs=out_spec,
        out_shape=jax.ShapeDtypeStruct((bsz, s, ATT_W), MXU_DTYPE),
        scratch_shapes=[pltpu.VMEM((nb, ATT_HEADS_PER_STEP, HEAD_DIM, blk), MXU_DTYPE)],
        compiler_params=_params(("parallel", "arbitrary")),
        name="stick_breaking",
    )(proj, proj, proj)


def _ret_kernel(q_ref, k_ref, v_ref, g_ref, gn_g_ref, gn_b_ref, o_ref, st_ref):
    ch = RET_CHUNK

    @pl.when(pl.program_id(1) == 0)
    def _():
        st_ref[...] = jnp.zeros_like(st_ref)

    i_col = lax.broadcasted_iota(jnp.int32, (ch, 1), 0).astype(F32)
    diff = (lax.broadcasted_iota(jnp.int32, (ch, ch), 0)
            - lax.broadcasted_iota(jnp.int32, (ch, ch), 1)).astype(F32)
    heads = range(RET_HEADS)
    log_g = [float(np.log(1.0 - 2.0 ** (-5.0 - h))) for h in heads]
    qs = [(q_ref[0, :, h * RET_DK:(h + 1) * RET_DK].astype(F32) * RET_DK ** -0.5).astype(MXU_DTYPE)
          for h in heads]
    k = [k_ref[0, :, h * RET_DK:(h + 1) * RET_DK] for h in heads]
    v = [v_ref[0, :, h * RET_DV:(h + 1) * RET_DV] for h in heads]
    prev = [st_ref[h] for h in heads]
    qk = [_dot_nt(qs[h], k[h]) for h in heads]
    o_cross = [_dot(qs[h], prev[h].astype(MXU_DTYPE)) for h in heads]
    kd = [(k[h].astype(F32) * jnp.exp((ch - 1.0 - i_col) * log_g[h])).astype(MXU_DTYPE)
          for h in heads]
    st_new = [_dot_tn(kd[h], v[h]) for h in heads]
    qkd = []
    for h in heads:
        decay = jnp.where(diff >= 0, jnp.exp(jnp.maximum(diff, 0.0) * log_g[h]), 0.0)
        qkd.append((qk[h] * decay).astype(MXU_DTYPE))
    o_intra = [_dot(qkd[h], v[h]) for h in heads]
    for h in heads:
        st_ref[h] = float(np.exp(ch * log_g[h])) * prev[h] + st_new[h]
        o = o_intra[h] + o_cross[h] * jnp.exp((i_col + 1.0) * log_g[h])
        mu = jnp.mean(o, axis=-1, keepdims=True)
        d = o - mu
        var = jnp.mean(d * d, axis=-1, keepdims=True)
        cs = slice(h * RET_DV, (h + 1) * RET_DV)
        on = d * lax.rsqrt(var + NORM_EPS) * gn_g_ref[:, cs] + gn_b_ref[:, cs]
        o_ref[0, :, cs] = (_silu(g_ref[0, :, cs].astype(F32)) * on).astype(o_ref.dtype)


def _ssd_kernel(z_ref, xbc_ref, dt_ref, cw_ref, cb_ref, dtb_ref, alog_ref, dsk_ref, ng_ref,
                o_ref, tail_ref, st_ref, y_ref):
    ch = SSD_CHUNK
    heads_per_group = SSD_HEADS // SSD_GROUPS
    halo = SSD_HALO

    @pl.when(pl.program_id(1) == 0)
    def _():
        tail_ref[0:halo, :] = jnp.zeros((halo, SSD_XBC), F32)
        st_ref[...] = jnp.zeros_like(st_ref)

    tail_ref[halo:, :] = xbc_ref[0].astype(F32)
    conv = cb_ref[...]
    for kk in range(SSD_CONV):
        off = halo - (SSD_CONV - 1) + kk
        conv = conv + cw_ref[kk:kk + 1, :] * tail_ref[off:off + ch, :]
    tail_ref[0:halo, :] = tail_ref[ch:ch + halo, :]
    xc = _silu(conv)
    xs = xc[:, :SSD_D_INNER]
    bm = xc[:, SSD_D_INNER:SSD_D_INNER + SSD_GROUPS * SSD_STATE]
    cm = xc[:, SSD_D_INNER + SSD_GROUPS * SSD_STATE:]

    dtr = dt_ref[0] + dtb_ref[...]
    dt = jnp.maximum(dtr, 0.0) + jnp.log1p(jnp.exp(-jnp.abs(dtr)))
    a = -jnp.exp(alog_ref[...])
    r_iota = lax.broadcasted_iota(jnp.int32, (ch, ch), 0)
    c_iota = lax.broadcasted_iota(jnp.int32, (ch, ch), 1)
    causal = r_iota >= c_iota
    tri = jnp.where(causal, 1.0, 0.0).astype(MXU_DTYPE)
    da = dt * a
    da_hi, da_lo = _split(da)
    cum = _dot(tri, da_hi) + _dot(tri, da_lo)
    cum_t = cum.T
    dt_t = dt.T
    cum_last = cum[ch - 1:ch, :]
    decay_st = jnp.exp(cum_last - cum) * dt
    e_cum = jnp.exp(cum)
    e_last = jnp.exp(cum_last)

    for g in range(SSD_GROUPS):
        bm_g = bm[:, g * SSD_STATE:(g + 1) * SSD_STATE].astype(MXU_DTYPE)
        cm_g = cm[:, g * SSD_STATE:(g + 1) * SSD_STATE].astype(MXU_DTYPE)
        cb = _dot_nt(cm_g, bm_g)
        for r in range(heads_per_group):
            hd = g * heads_per_group + r
            cs = slice(hd * SSD_HEAD_DIM, (hd + 1) * SSD_HEAD_DIM)
            x_h = xs[:, cs]
            seg = cum[:, hd:hd + 1] - cum_t[hd:hd + 1, :]
            lmat = jnp.exp(jnp.where(causal, seg, NEG_INF))
            w = cb * lmat * dt_t[hd:hd + 1, :]
            y = _dot(w.astype(MXU_DTYPE), x_h.astype(MXU_DTYPE))
            prev = st_ref[hd]
            y = y + _dot(cm_g, prev.astype(MXU_DTYPE)) * e_cum[:, hd:hd + 1]
            xd = (x_h * decay_st[:, hd:hd + 1]).astype(MXU_DTYPE)
            st_ref[hd] = e_last[:, hd:hd + 1] * prev + _dot_tn(bm_g, xd)
            y_ref[:, cs] = y + dsk_ref[:, cs] * x_h

    y = y_ref[...] * _silu(z_ref[0].astype(F32))
    gw = SSD_D_INNER // SSD_GROUPS
    for g in range(SSD_GROUPS):
        cs = slice(g * gw, (g + 1) * gw)
        yg = y[:, cs]
        ms = jnp.mean(yg * yg, axis=-1, keepdims=True)
        o_ref[0, :, cs] = (yg * lax.rsqrt(ms + LN_EPS) * ng_ref[:, cs]).astype(o_ref.dtype)


def _scan_kernel(rq_ref, rk_ref, rv_ref, rg_ref, gn_g_ref, gn_b_ref,
                 z_ref, xbc_ref, dt_ref, cw_ref, cb_ref, dtb_ref, alog_ref, dsk_ref, ng_ref,
                 o_ret_ref, o_ssd_ref, ret_st_ref, tail_ref, ssd_st_ref, y_ref):
    _ret_kernel(rq_ref, rk_ref, rv_ref, rg_ref, gn_g_ref, gn_b_ref, o_ret_ref, ret_st_ref)
    _ssd_kernel(z_ref, xbc_ref, dt_ref, cw_ref, cb_ref, dtb_ref, alog_ref, dsk_ref, ng_ref,
                o_ssd_ref, tail_ref, ssd_st_ref, y_ref)


def _scans(proj, proj_dt, gn_g, gn_b, conv_w, conv_b, dt_bias, a_log, d_skip, norm_g):
    bsz, s, _ = proj.shape
    assert RET_CHUNK == SSD_CHUNK
    ch = SSD_CHUNK
    wq, wv = RET_HEADS * RET_DK, RET_HEADS * RET_DV
    pad = LANES - SSD_HEADS
    dtb = jnp.pad(dt_bias, (0, pad)).reshape(1, LANES)
    alog = jnp.pad(a_log, (0, pad)).reshape(1, LANES)
    dsk = jnp.repeat(d_skip, SSD_HEAD_DIM).reshape(1, SSD_D_INNER)
    const = lambda b, i: (0, 0)
    return pl.pallas_call(
        _scan_kernel,
        grid=(bsz, s // ch),
        in_specs=[pl.BlockSpec((1, ch, wq), lambda b, i: (b, i, A_RET_Q // wq)),
                  pl.BlockSpec((1, ch, wq), lambda b, i: (b, i, A_RET_K // wq)),
                  pl.BlockSpec((1, ch, wv), lambda b, i: (b, i, A_RET_V // wv)),
                  pl.BlockSpec((1, ch, wv), lambda b, i: (b, i, B_RET_G // wv)),
                  pl.BlockSpec((1, wv), const),
                  pl.BlockSpec((1, wv), const),
                  pl.BlockSpec((1, ch, SSD_D_INNER), lambda b, i: (b, i, B_SSD_Z // SSD_D_INNER)),
                  pl.BlockSpec((1, ch, SSD_XBC), lambda b, i: (b, i, B_XBC // SSD_XBC)),
                  pl.BlockSpec((1, ch, LANES), lambda b, i: (b, i, 0)),
                  pl.BlockSpec((SSD_CONV, SSD_XBC), const),
                  pl.BlockSpec((1, SSD_XBC), const),
                  pl.BlockSpec((1, LANES), const),
                  pl.BlockSpec((1, LANES), const),
                  pl.BlockSpec((1, SSD_D_INNER), const),
                  pl.BlockSpec((1, SSD_D_INNER), const)],
        out_specs=[pl.BlockSpec((1, ch, wv), lambda b, i: (b, i, 0)),
                   pl.BlockSpec((1, ch, SSD_D_INNER), lambda b, i: (b, i, 0))],
        out_shape=[jax.ShapeDtypeStruct((bsz, s, wv), MXU_DTYPE),
                   jax.ShapeDtypeStruct((bsz, s, SSD_D_INNER), MXU_DTYPE)],
        scratch_shapes=[pltpu.VMEM((RET_HEADS, RET_DK, RET_DV), F32),
                        pltpu.VMEM((SSD_HALO + ch, SSD_XBC), F32),
                        pltpu.VMEM((SSD_HEADS, SSD_STATE, SSD_HEAD_DIM), F32),
                        pltpu.VMEM((ch, SSD_D_INNER), F32)],
        compiler_params=_params(("parallel", "arbitrary")),
        name="retention_ssd",
    )(proj, proj, proj, proj, gn_g.reshape(1, wv), gn_b.reshape(1, wv),
      proj, proj, proj_dt, conv_w, conv_b.reshape(1, SSD_XBC), dtb, alog, dsk,
      norm_g.reshape(1, SSD_D_INNER))


def _merge_kernel(oa_ref, ob_ref, oc_ref, od_ref, gates_ref, x_ref, g1_ref, wbr_ref, wout_ref,
                  lg_ref, lb_ref, o_ref):
    merged = None
    for n, br in enumerate((oa_ref, ob_ref, oc_ref, od_ref)):
        y = _dot(br[0], wbr_ref[n])
        t = _sigmoid(gates_ref[0, :, n * D_MODEL:(n + 1) * D_MODEL].astype(F32)) * y
        merged = t if merged is None else merged + t
    mix = _dot(merged.astype(MXU_DTYPE), wout_ref[...])
    o_ref[0] = _resid_ln(x_ref[0], g1_ref[0], mix, lg_ref[...], lb_ref[...])


def _merge(o_a, o_b, o_c, o_d, proj, x, g1, w_br, w_out, ln_g, ln_b, tm=512):
    bsz, s, d = x.shape
    tm = min(tm, s)
    gw = N_BRANCH * d
    row = lambda b, i: (b, i, 0)
    const2 = lambda b, i: (0, 0)
    br_spec = pl.BlockSpec((1, tm, BRANCH_W), row)
    return pl.pallas_call(
        _merge_kernel,
        grid=(bsz, s // tm),
        in_specs=[br_spec, br_spec, br_spec, br_spec,
                  pl.BlockSpec((1, tm, gw), lambda b, i: (b, i, B_GATES // gw)),
                  pl.BlockSpec((1, tm, d), row),
                  pl.BlockSpec((1, 1, d), lambda b, i: (b, 0, 0)),
                  pl.BlockSpec((N_BRANCH, BRANCH_W, d), lambda b, i: (0, 0, 0)),
                  pl.BlockSpec((d, d), const2),
                  pl.BlockSpec((1, d), const2),
                  pl.BlockSpec((1, d), const2)],
        out_specs=pl.BlockSpec((1, tm, d), row),
        out_shape=jax.ShapeDtypeStruct((bsz, s, d), F32),
        compiler_params=_params(("parallel", "parallel")),
        name="merge",
    )(o_a, o_b, o_c, o_d, proj, x, g1, w_br, w_out, ln_g.reshape(1, d), ln_b.reshape(1, d))


def _ffn_kernel(x_ref, sc_ref, sh_ref, g2_ref, wg_ref, wu_ref, wd_ref, lg_ref, lb_ref,
                o_ref, h_ref, acc_ref):
    j = pl.program_id(2)

    @pl.when(j == 0)
    def _():
        h = x_ref[0] * (1.0 + sc_ref[0]) + sh_ref[0]
        h_ref[...] = h.astype(MXU_DTYPE)
        acc_ref[...] = jnp.zeros_like(acc_ref)

    h = h_ref[...]
    act = _silu(_dot(h, wg_ref[...])) * _dot(h, wu_ref[...])
    acc_ref[...] += _dot(act.astype(MXU_DTYPE), wd_ref[...])

    @pl.when(j == pl.num_programs(2) - 1)
    def _():
        o_ref[0] = _resid_ln(x_ref[0], g2_ref[0], acc_ref[...], lg_ref[...], lb_ref[...])


def _ffn(x, sc, sh, g2, w_gu, w_down, ln_g, ln_b, tm=512, tf=1408):
    bsz, s, d = x.shape
    tm = min(tm, s)
    nf = D_FF // tf
    row = lambda b, i, j: (b, i, 0)
    vec = lambda b, i, j: (b, 0, 0)
    const2 = lambda b, i, j: (0, 0)
    return pl.pallas_call(
        _ffn_kernel,
        grid=(bsz, s // tm, nf),
        in_specs=[pl.BlockSpec((1, tm, d), row),
                  pl.BlockSpec((1, 1, d), vec), pl.BlockSpec((1, 1, d), vec),
                  pl.BlockSpec((1, 1, d), vec),
                  pl.BlockSpec((d, tf), lambda b, i, j: (0, j)),
                  pl.BlockSpec((d, tf), lambda b, i, j: (0, nf + j)),
                  pl.BlockSpec((tf, d), lambda b, i, j: (j, 0)),
                  pl.BlockSpec((1, d), const2), pl.BlockSpec((1, d), const2)],
        out_specs=pl.BlockSpec((1, tm, d), row),
        out_shape=jax.ShapeDtypeStruct((bsz, s, d), F32),
        scratch_shapes=[pltpu.VMEM((tm, d), MXU_DTYPE), pltpu.VMEM((tm, d), F32)],
        compiler_params=_params(("parallel", "parallel", "arbitrary")),
        name="ffn_dense",
    )(x, sc, sh, g2, w_gu, w_gu, w_down, ln_g.reshape(1, d), ln_b.reshape(1, d))


def _router_kernel(x_ref, sc_ref, sh_ref, rw_ref, rb_ref,
                   h_ref, comb_ref, slot_ref, slot_t_ref, cnt_ref):
    tm = x_ref.shape[1]
    lane = lax.broadcasted_iota(jnp.int32, (tm, LANES), 1)
    h = x_ref[0] * (1.0 + sc_ref[0]) + sh_ref[0]
    h_ref[0] = h.astype(MXU_DTYPE)
    logits = _dot3(h, rw_ref[...]) + rb_ref[...]
    logits = jnp.where(lane < N_EXPERTS, logits, -jnp.inf)
    v1 = jnp.max(logits, axis=1, keepdims=True)
    i1 = jnp.min(jnp.where(logits == v1, lane, LANES), axis=1, keepdims=True)
    rest = jnp.where(lane == i1, -jnp.inf, logits)
    v2 = jnp.max(rest, axis=1, keepdims=True)
    i2 = jnp.min(jnp.where(rest == v2, lane, LANES), axis=1, keepdims=True)
    e2 = jnp.exp(v2 - v1)
    w1 = 1.0 / (1.0 + e2)
    w2 = e2 / (1.0 + e2)
    comb_ref[0] = jnp.where(lane == i1, w1, 0.0) + jnp.where(lane == i2, w2, 0.0)
    routed = jnp.logical_or(lane == i1, lane == i2)
    earlier = (lax.broadcasted_iota(jnp.int32, (tm, tm), 1)
               < lax.broadcasted_iota(jnp.int32, (tm, tm), 0))
    before = _dot(jnp.where(earlier, 1.0, 0.0).astype(MXU_DTYPE),
                  jnp.where(routed, 1.0, 0.0).astype(MXU_DTYPE))
    slot = jnp.where(routed, before, -1.0)
    slot_ref[0] = slot
    slot_t_ref[0] = slot.T[0:N_EXPERTS, :]
    cnt_ref[0] = jnp.sum(jnp.where(routed, 1, 0), axis=0, keepdims=True)


def _moe_kernel(cnt_ref, h_ref, x_ref, g2_ref, comb_ref, slot_ref, slot_t_ref,
                wg_ref, wu_ref, wd_ref, lg_ref, lb_ref, o_ref, xe_ref, ye_ref, *, n_tiles):
    tm = x_ref.shape[1]
    e = pl.program_id(2)
    j = pl.program_id(3)
    last_j = pl.num_programs(3) - 1
    count = cnt_ref[(pl.program_id(0) * n_tiles + pl.program_id(1)) * N_EXPERTS + e]
    n_full = jnp.maximum(count - MOE_TAIL_MIN - 1, 0) // MOE_ROWS
    tail0 = pl.multiple_of(n_full * MOE_ROWS, MOE_ROWS)
    tail = count - n_full * MOE_ROWS

    def for_pieces(fn, step=MOE_ROW_STEP):
        def full_piece(c, carry):
            fn(pl.multiple_of(c * MOE_ROWS, MOE_ROWS), MOE_ROWS)
            return carry

        lax.fori_loop(0, n_full, full_piece, 0)
        tail_steps = (tail + step - 1) // step
        for steps in range(1, (MOE_ROWS + MOE_TAIL_MIN) // step + 1):
            @pl.when(tail_steps == steps)
            def _(steps=steps):
                fn(tail0, steps * step)

    @pl.when(jnp.logical_and(e == 0, j == 0))
    def _():
        o_ref[0] = jnp.zeros((tm, D_MODEL), F32)

    @pl.when(j == 0)
    def _():
        slot_row = slot_t_ref[0, pl.ds(e, 1), :]

        def gather(r0, rows):
            slots = (lax.broadcasted_iota(jnp.int32, (rows, tm), 0) + r0).astype(F32)
            pick = jnp.where(slot_row == slots, 1.0, 0.0).astype(MXU_DTYPE)
            xe_ref[pl.ds(r0, rows), :] = _dot(pick, h_ref[0]).astype(MXU_DTYPE)
            ye_ref[pl.ds(r0, rows), :] = jnp.zeros((rows, D_MODEL), F32)

        for_pieces(gather, MOE_MOVE_STEP)

    def expert_rows(r0, rows):
        xc = xe_ref[pl.ds(r0, rows), :]
        act = _silu(_dot(xc, wg_ref[0])) * _dot(xc, wu_ref[0])
        ye_ref[pl.ds(r0, rows), :] += _dot(act.astype(MXU_DTYPE), wd_ref[0])

    for_pieces(expert_rows)

    @pl.when(j == last_j)
    def _():
        lane = lax.broadcasted_iota(jnp.int32, (tm, LANES), 1)
        mine = lane == e
        slot_col = jnp.sum(jnp.where(mine, slot_ref[0], 0.0), axis=1, keepdims=True)
        comb_col = jnp.sum(jnp.where(mine, comb_ref[0], 0.0), axis=1, keepdims=True)

        def scatter(r0, rows):
            slots = (lax.broadcasted_iota(jnp.int32, (tm, rows), 1) + r0).astype(F32)
            place = jnp.where(slot_col == slots, 1.0, 0.0).astype(MXU_DTYPE)
            o_ref[0] += comb_col * _dot(place, ye_ref[pl.ds(r0, rows), :].astype(MXU_DTYPE))

        for_pieces(scatter, MOE_MOVE_STEP)

    @pl.when(jnp.logical_and(e == pl.num_programs(2) - 1, j == last_j))
    def _():
        o_ref[0] = _resid_ln(x_ref[0], g2_ref[0], o_ref[0], lg_ref[...], lb_ref[...])


def _moe(x, sc, sh, g2, router_w, router_b, w_gu, w_down, ln_g, ln_b, tm=1024, tf=1792):
    bsz, s, d = x.shape
    tm = min(tm, s)
    nt = s // tm
    nf = D_FF_EXPERT // tf
    rw = jnp.pad(router_w, ((0, 0), (0, LANES - N_EXPERTS)))
    rb = jnp.pad(router_b, (0, LANES - N_EXPERTS)).reshape(1, LANES)

    row2 = lambda b, i: (b, i, 0)
    vec2 = lambda b, i: (b, 0, 0)
    h, comb, slot, slot_t, cnt = pl.pallas_call(
        _router_kernel,
        grid=(bsz, nt),
        in_specs=[pl.BlockSpec((1, tm, d), row2),
                  pl.BlockSpec((1, 1, d), vec2), pl.BlockSpec((1, 1, d), vec2),
                  pl.BlockSpec((d, LANES), lambda b, i: (0, 0)),
                  pl.BlockSpec((1, LANES), lambda b, i: (0, 0))],
        out_specs=[pl.BlockSpec((1, tm, d), row2),
                   pl.BlockSpec((1, tm, LANES), row2),
                   pl.BlockSpec((1, tm, LANES), row2),
                   pl.BlockSpec((1, N_EXPERTS, tm), lambda b, i: (b * nt + i, 0, 0)),
                   pl.BlockSpec((1, 1, LANES), lambda b, i: (b * nt + i, 0, 0))],
        out_shape=[jax.ShapeDtypeStruct((bsz, s, d), MXU_DTYPE),
                   jax.ShapeDtypeStruct((bsz, s, LANES), F32),
                   jax.ShapeDtypeStruct((bsz, s, LANES), F32),
                   jax.ShapeDtypeStruct((bsz * nt, N_EXPERTS, tm), F32),
                   jax.ShapeDtypeStruct((bsz * nt, 1, LANES), jnp.int32)],
        compiler_params=_params(("parallel", "parallel")),
        name="moe_router",
    )(x, sc, sh, rw, rb)
    counts = cnt[:, 0, :N_EXPERTS].reshape(bsz * nt * N_EXPERTS)

    row = lambda b, i, e, j, cnt: (b, i, 0)
    vec = lambda b, i, e, j, cnt: (b, 0, 0)
    const2 = lambda b, i, e, j, cnt: (0, 0)
    once = pl.Buffered(1)
    grid_spec = pltpu.PrefetchScalarGridSpec(
        num_scalar_prefetch=1,
        grid=(bsz, nt, N_EXPERTS, nf),
        in_specs=[pl.BlockSpec((1, tm, d), row, pipeline_mode=once),
                  pl.BlockSpec((1, tm, d), row, pipeline_mode=once),
                  pl.BlockSpec((1, 1, d), vec),
                  pl.BlockSpec((1, tm, LANES), row, pipeline_mode=once),
                  pl.BlockSpec((1, tm, LANES), row, pipeline_mode=once),
                  pl.BlockSpec((1, N_EXPERTS, tm), lambda b, i, e, j, cnt: (b * nt + i, 0, 0)),
                  pl.BlockSpec((1, d, tf), lambda b, i, e, j, cnt: (e, 0, j)),
                  pl.BlockSpec((1, d, tf), lambda b, i, e, j, cnt: (e, 0, nf + j)),
                  pl.BlockSpec((1, tf, d), lambda b, i, e, j, cnt: (e, j, 0)),
                  pl.BlockSpec((1, d), const2), pl.BlockSpec((1, d), const2)],
        out_specs=pl.BlockSpec((1, tm, d), row),
        scratch_shapes=[pltpu.VMEM((tm, d), MXU_DTYPE), pltpu.VMEM((tm, d), F32)])
    return pl.pallas_call(
        functools.partial(_moe_kernel, n_tiles=nt),
        grid_spec=grid_spec,
        out_shape=jax.ShapeDtypeStruct((bsz, s, d), F32),
        compiler_params=pltpu.CompilerParams(
            dimension_semantics=("parallel", "parallel", "arbitrary", "arbitrary"),
            vmem_limit_bytes=MOE_VMEM_LIMIT),
        name="moe_experts",
    )(counts, h, x, g2, comb, slot, slot_t, w_gu, w_gu, w_down,
      ln_g.reshape(1, d), ln_b.reshape(1, d))


def _split_w_in(w):
    p = np.cumsum([0, 512, 512, 512, 256, 256, 512, 512, 512, SSD_XBC, SSD_HEADS, 512, 512, 512,
                   N_BRANCH * D_MODEL])
    mq, mk, mv, rq, rk, rv, rg, sz, sxbc, sdt, bq, bk, bv, gates = (
        w[:, p[i]:p[i + 1]] for i in range(14))
    w_main = jnp.concatenate([mq, mk, mv, rq, rk, rv, bq, bk, bv, gates, rg, sz, sxbc], axis=1)
    w_dt = jnp.pad(sdt, ((0, 0), (0, LANES - SSD_HEADS)))
    return w_main.astype(MXU_DTYPE), w_dt.astype(MXU_DTYPE)


def kernel(x, c, w_ada, b_ada, w_in, conv_w, conv_b, dt_bias, a_log, d_skip, ssm_norm_g, ret_gn_g, ret_gn_b, w_br, w_out, ln1_g, ln1_b, ln2_g, ln2_b, ffn_w_gu, ffn_w_down, router_w, router_b, expert_w_gu, expert_w_down):
    bsz = x.shape[0]
    mod = _ada(c, w_ada, b_ada)
    for l in range(DEPTH):
        sh1, sc1, g1, sh2, sc2, g2 = (
            mod[l, :, i * D_MODEL:(i + 1) * D_MODEL].reshape(bsz, 1, D_MODEL) for i in range(6))
        w_main, w_dt = _split_w_in(w_in[l])
        proj = _inproj(x, sc1, sh1, w_main, MXU_DTYPE, tn=2048)
        proj_dt = _inproj(x, sc1, sh1, w_dt, F32, tn=LANES)
        o_a = _moba(proj)
        o_d = _stick_breaking(proj)
        o_b, o_c = _scans(proj, proj_dt, ret_gn_g[l], ret_gn_b[l], conv_w[l], conv_b[l],
                          dt_bias[l], a_log[l], d_skip[l], ssm_norm_g[l])
        x = _merge(o_a, o_b, o_c, o_d, proj, x, g1, w_br[l].astype(MXU_DTYPE),
                   w_out[l].astype(MXU_DTYPE), ln1_g[l], ln1_b[l])
        if l % 2 == 0:
            x = _ffn(x, sc2, sh2, g2, ffn_w_gu[l // 2].astype(MXU_DTYPE),
                     ffn_w_down[l // 2].astype(MXU_DTYPE), ln2_g[l], ln2_b[l])
        else:
            x = _moe(x, sc2, sh2, g2, router_w[l // 2], router_b[l // 2],
                     expert_w_gu[l // 2].astype(MXU_DTYPE), expert_w_down[l // 2].astype(MXU_DTYPE),
                     ln2_g[l], ln2_b[l])
    return x
```

```python
import functools

import numpy as np
import jax
import jax.numpy as jnp
from jax import lax
from jax.experimental import pallas as pl
from jax.experimental.pallas import tpu as pltpu

F32 = jnp.float32
MXU_DTYPE = jnp.bfloat16

D_MODEL = 1024
DEPTH = 2
MOBA_HEADS = 8
MOBA_BLOCK = 256
MOBA_TOPK = 3
RET_HEADS = 4
RET_DK = 64
RET_DV = 128
RET_CHUNK = 256
SSD_D_INNER = 512
SSD_HEAD_DIM = 64
SSD_HEADS = 8
SSD_GROUPS = 2
SSD_STATE = 128
SSD_CONV = 4
SSD_CHUNK = 256
SSD_XBC = SSD_D_INNER + 2 * SSD_GROUPS * SSD_STATE
SSD_HALO = 8
SB_HEADS = 8
SB_BLOCK = 256
HEAD_DIM = 64
ATT_HEADS_PER_STEP = 8
ATT_W = ATT_HEADS_PER_STEP * HEAD_DIM
BRANCH_W = 512
N_BRANCH = 4
D_FF = 2816
N_EXPERTS = 8
D_FF_EXPERT = 3584
DEEPNORM_ALPHA = (2.0 * DEPTH) ** 0.25
LN_EPS = 1e-5
NORM_EPS = 1e-6
NEG_INF = -1e30
LOG2E = 1.4426950408889634
LANES = 128
VMEM_LIMIT = 48 * 2 ** 20
MOE_VMEM_LIMIT = 56 * 2 ** 20
MOE_ROWS = 256
MOE_ROW_STEP = 64
MOE_TAIL_MIN = 128

A_MOBA_Q, A_MOBA_K, A_MOBA_V = 0, 512, 1024
A_RET_Q, A_RET_K, A_RET_V = 1536, 1792, 2048
A_SB_Q, A_SB_K, A_SB_V = 2560, 3072, 3584
B_GATES, B_RET_G, B_SSD_Z, B_XBC = 4096, 8192, 8704, 9216
PROJ_WIDTH = 10240


def _params(sem):
    return pltpu.CompilerParams(dimension_semantics=sem, vmem_limit_bytes=VMEM_LIMIT)


def _dot(a, b):
    return jnp.dot(a, b, preferred_element_type=F32)


def _dot_nt(a, b):
    return lax.dot_general(a, b, (((1,), (1,)), ((), ())), preferred_element_type=F32)


def _dot_tn(a, b):
    return lax.dot_general(a, b, (((0,), (0,)), ((), ())), preferred_element_type=F32)


def _split(x):
    hi = x.astype(MXU_DTYPE)
    lo = (x - hi.astype(F32)).astype(MXU_DTYPE)
    return hi, lo


def _dot3(a, b):
    ah, al = _split(a)
    bh, bl = _split(b)
    return _dot(ah, bh) + _dot(al, bh) + _dot(ah, bl)


def _sigmoid(x):
    return 1.0 / (1.0 + jnp.exp(-x))


def _silu(x):
    return x * _sigmoid(x)


def _resid_ln(x, gate, f, ln_g, ln_b):
    y = DEEPNORM_ALPHA * x + gate * f
    mu = jnp.mean(y, axis=-1, keepdims=True)
    d = y - mu
    var = jnp.mean(d * d, axis=-1, keepdims=True)
    return d * lax.rsqrt(var + LN_EPS) * ln_g + ln_b


def _ada_kernel(c_ref, w_ref, b_ref, o_ref):
    c = c_ref[...]
    o_ref[0] = _dot3(_silu(c), w_ref[0]) + b_ref[0]


def _ada(c, w_ada, b_ada):
    depth, d, n = w_ada.shape
    bsz = c.shape[0]
    tn = 1536
    return pl.pallas_call(
        _ada_kernel,
        grid=(depth, n // tn),
        in_specs=[pl.BlockSpec((bsz, d), lambda l, j: (0, 0)),
                  pl.BlockSpec((1, d, tn), lambda l, j: (l, 0, j)),
                  pl.BlockSpec((1, 1, tn), lambda l, j: (l, 0, j))],
        out_specs=pl.BlockSpec((1, bsz, tn), lambda l, j: (l, 0, j)),
        out_shape=jax.ShapeDtypeStruct((depth, bsz, n), F32),
        compiler_params=_params(("arbitrary", "arbitrary")),
        name="ada_mod",
    )(c, w_ada, b_ada.reshape(depth, 1, n))


def _inproj_kernel(x_ref, sc_ref, sh_ref, w_ref, wdt_ref, o_ref, odt_ref, h_ref):
    @pl.when(pl.program_id(2) == 0)
    def _():
        h = x_ref[0] * (1.0 + sc_ref[0]) + sh_ref[0]
        h_ref[...] = h.astype(MXU_DTYPE)
        odt_ref[0] = _dot(h_ref[...], wdt_ref[...])

    o_ref[0] = _dot(h_ref[...], w_ref[...]).astype(o_ref.dtype)


def _inproj(x, sc, sh, w, w_dt, tn, tm=1024):
    bsz, s, d = x.shape
    n = w.shape[1]
    n_dt = w_dt.shape[1]
    tm = min(tm, s)
    return pl.pallas_call(
        _inproj_kernel,
        grid=(bsz, s // tm, n // tn),
        in_specs=[pl.BlockSpec((1, tm, d), lambda b, i, j: (b, i, 0)),
                  pl.BlockSpec((1, 1, d), lambda b, i, j: (b, 0, 0)),
                  pl.BlockSpec((1, 1, d), lambda b, i, j: (b, 0, 0)),
                  pl.BlockSpec((d, tn), lambda b, i, j: (0, j)),
                  pl.BlockSpec((d, n_dt), lambda b, i, j: (0, 0))],
        out_specs=[pl.BlockSpec((1, tm, tn), lambda b, i, j: (b, i, j)),
                   pl.BlockSpec((1, tm, n_dt), lambda b, i, j: (b, i, 0))],
        out_shape=[jax.ShapeDtypeStruct((bsz, s, n), MXU_DTYPE),
                   jax.ShapeDtypeStruct((bsz, s, n_dt), F32)],
        scratch_shapes=[pltpu.VMEM((tm, d), MXU_DTYPE)],
        compiler_params=_params(("parallel", "parallel", "arbitrary")),
        name="in_proj",
    )(x, sc, sh, w, w_dt)


def _moba_parts(q_ref, k_ref, v_ref, sl_ref, o_ref, km_ref, ka_ref, vt_ref, nb):
    blk = MOBA_BLOCK
    nh = ATT_HEADS_PER_STEP
    heads = range(nh)
    qi = pl.program_id(1)
    s_len = nb * blk
    nb_pad = -(-nb // 8) * 8
    extra = HEAD_DIM
    assert nb_pad + 8 <= extra
    cols = [slice(h * HEAD_DIM, (h + 1) * HEAD_DIM) for h in heads]

    @pl.when(qi == 0)
    def _():
        k_all = k_ref[0]
        kk = k_all.astype(F32).reshape(nb, blk, ATT_W)
        km_ref[...] = jnp.zeros_like(km_ref)
        km_ref[0:nb, :] = jnp.sum(kk, axis=1) * (1.0 / blk)
        pos = lax.broadcasted_iota(jnp.int32, (s_len, extra), 0)
        lane = lax.broadcasted_iota(jnp.int32, (s_len, extra), 1)
        one_hot = jnp.where(lane == pos // blk, 1.0, 0.0)
        pos_hi = jnp.where(lane == nb_pad, (pos // blk * blk).astype(F32), 0.0)
        pos_lo = jnp.where(lane == nb_pad + 1, (pos % blk).astype(F32), 0.0)
        tail = (one_hot + pos_hi + pos_lo).astype(MXU_DTYPE)
        for h in heads:
            ka_ref[h] = jnp.concatenate([k_all[:, cols[h]], tail], axis=1)

        ones_rows = jnp.where(lax.broadcasted_iota(jnp.int32, (extra, blk), 0) == 0, 1.0, 0.0)

        def transpose_block(j, carry):
            j0 = pl.multiple_of(j * blk, blk)
            vt = v_ref[0, pl.ds(j0, blk), :].astype(F32).T
            for h in heads:
                vt_ref[j, h] = jnp.concatenate([vt[cols[h], :], ones_rows], axis=0).astype(MXU_DTYPE)
            return carry

        lax.fori_loop(0, nb, transpose_block, 0)

    key_i = lax.broadcasted_iota(jnp.int32, (blk, blk), 0)
    qry_i = lax.broadcasted_iota(jnp.int32, (blk, blk), 1)
    n_iota = lax.broadcasted_iota(jnp.int32, (nb_pad, blk), 0)
    past = n_iota < qi
    k_sel = min(MOBA_TOPK, max(nb - 1, 1))
    q_t = q_ref[0].astype(F32).T
    slope_rows = lax.broadcasted_iota(jnp.int32, (8, blk), 0) < 2

    qa = []
    for h in heads:
        qh_t = q_t[cols[h], :]
        slope = sl_ref[0, :, h * HEAD_DIM:h * HEAD_DIM + 1]
        km_hi, km_lo = _split(km_ref[:, cols[h]])
        qh_mxu = qh_t.astype(MXU_DTYPE)
        gate = (_dot(km_hi, qh_mxu) + _dot(km_lo, qh_mxu))[0:nb_pad, :]
        g = jnp.where(past, gate, NEG_INF)
        g = jnp.where(n_iota < nb, g, -jnp.inf)
        sel = jnp.zeros((nb_pad, blk), F32)
        for _ in range(k_sel):
            mx = jnp.max(g, axis=0, keepdims=True)
            idx = jnp.min(jnp.where(g == mx, n_iota, nb_pad), axis=0, keepdims=True)
            pick = n_iota == idx
            sel = jnp.where(pick, 1.0, sel)
            g = jnp.where(pick, -jnp.inf, g)
        keep = jnp.logical_or(jnp.logical_and(past, sel > 0.5), n_iota == qi)
        bias = jnp.where(keep, 0.0, NEG_INF)
        alibi = jnp.where(slope_rows, slope, 0.0)
        pad = jnp.zeros((extra - nb_pad - 8, blk), F32)
        qa.append(jnp.concatenate([qh_t * HEAD_DIM ** -0.5, bias, alibi, pad],
                                  axis=0).astype(MXU_DTYPE))

    def scores(js):
        starts = [pl.multiple_of(j * blk, blk) for j in js]
        return tuple(_dot(ka_ref[h, pl.ds(j0, blk), :], qa[h]) for j0 in starts for h in heads)

    def rest(js, flat_scores, carry, diag):
        s = [flat_scores[t * nh:(t + 1) * nh] for t in range(len(js))]
        m = list(carry[:nh]) if carry is not None else [None] * nh
        acc = list(carry[nh:]) if carry is not None else [None] * nh
        p = {}
        for h in heads:
            sh = [jnp.where(key_i <= qry_i, s[t][h], NEG_INF) if diag else s[t][h]
                  for t in range(len(js))]
            top = jnp.max(sh[0], axis=0, keepdims=True)
            for t in range(1, len(js)):
                top = jnp.maximum(top, jnp.max(sh[t], axis=0, keepdims=True))
            if m[h] is None:
                m[h] = top
            else:
                m_new = jnp.maximum(m[h], top)
                acc[h] = jnp.exp(m[h] - m_new) * acc[h]
                m[h] = m_new
            for t in range(len(js)):
                p[t, h] = jnp.exp(sh[t] - m[h]).astype(MXU_DTYPE)
        for t, j in enumerate(js):
            for h in heads:
                pv = _dot(vt_ref[j, h], p[t, h])
                acc[h] = pv if acc[h] is None else acc[h] + pv
        return tuple(m) + tuple(acc)

    def finish(carry):
        out_t = [carry[nh + h][0:HEAD_DIM, :] / carry[nh + h][HEAD_DIM:HEAD_DIM + 1, :]
                 for h in heads]
        o_ref[0] = jnp.concatenate(out_t, axis=0).T.astype(o_ref.dtype)

    return scores, rest, None, finish


def _sb_parts(q_ref, k_ref, v_ref, o_ref, vt_ref, nb):
    blk = SB_BLOCK
    nh = ATT_HEADS_PER_STEP
    heads = range(nh)
    qi = pl.program_id(1)
    cols = [slice(h * HEAD_DIM, (h + 1) * HEAD_DIM) for h in heads]

    @pl.when(qi == 0)
    def _():
        def transpose_block(j, carry):
            j0 = pl.multiple_of(j * blk, blk)
            vt = v_ref[0, pl.ds(j0, blk), :].astype(F32).T
            for h in heads:
                vt_ref[j, h] = vt[cols[h], :].astype(MXU_DTYPE)
            return carry

        lax.fori_loop(0, nb, transpose_block, 0)

    key_i = lax.broadcasted_iota(jnp.int32, (blk, blk), 0)
    qry_i = lax.broadcasted_iota(jnp.int32, (blk, blk), 1)
    causal = key_i < qry_i
    later = jnp.where(qry_i > key_i, 1.0, 0.0).astype(MXU_DTYPE)
    qs = [(q_ref[0, :, cols[h]].astype(F32) * HEAD_DIM ** -0.5).astype(MXU_DTYPE)
          for h in heads]

    def scores(js):
        starts = [pl.multiple_of(j * blk, blk) for j in js]
        return tuple(_dot_nt(k_ref[0, pl.ds(j0, blk), cols[h]], qs[h])
                     for j0 in starts for h in heads)

    def rest(js, flat_scores, carry, diag):
        accs, rests = carry[:nh], carry[nh:]
        z = [flat_scores[t * nh:(t + 1) * nh] for t in range(len(js))]
        log_beta, log_1mb, parts = [], [], []
        for t in range(len(js)):
            for h in heads:
                zz = z[t][h]
                lb = jnp.minimum(zz, 0.0) - jnp.log(1.0 + jnp.exp2(jnp.abs(zz) * (-LOG2E)))
                l1 = lb - zz
                if diag:
                    l1 = jnp.where(causal, l1, 0.0)
                log_beta.append(lb)
                log_1mb.append(l1)
                parts.append(l1.astype(MXU_DTYPE))
        rem = [_dot(later, p) for p in parts]
        w = []
        rests = list(rests)
        for t in range(len(js)):
            for h in heads:
                i = t * nh + h
                wh = jnp.exp(log_beta[i] + rem[i] + rests[h])
                if diag:
                    wh = jnp.where(causal, wh, 0.0)
                w.append(wh.astype(MXU_DTYPE))
                rests[h] = rests[h] + rem[i][0:1, :] + log_1mb[i][0:1, :]
        accs = list(accs)
        for t, j in enumerate(js):
            for h in heads:
                accs[h] = accs[h] + _dot(vt_ref[j, h], w[t * nh + h])
        return tuple(accs) + tuple(rests)

    start = tuple([jnp.zeros((HEAD_DIM, blk), F32)] * nh + [jnp.zeros((1, blk), F32)] * nh)

    def finish(carry):
        o_ref[0] = jnp.concatenate(carry[:nh], axis=0).T.astype(o_ref.dtype)

    return scores, rest, start, finish


def _walk_key_blocks(scores, rest, start, finish, z_ref=None):
    qi = pl.program_id(1)

    state = rest([qi], scores([qi]), start, True)

    def single(t, state):
        return rest([qi - 1], scores([qi - 1]), state, False)

    state = lax.fori_loop(0, qi % 2, single, state)
    first = qi - 1 - qi % 2

    if z_ref is None:
        def pair(t, state):
            j = first - 2 * t
            return rest([j, j - 1], scores([j, j - 1]), state, False)
    else:
        def look_ahead(t):
            j = jnp.maximum(first - 2 * t, 1)
            for i, z in enumerate(scores([j, j - 1])):
                z_ref[i] = z

        def pair(t, state):
            j = first - 2 * t
            z = [z_ref[i] for i in range(z_ref.shape[0])]
            look_ahead(t + 1)
            return rest([j, j - 1], z, state, False)

        look_ahead(0)

    finish(lax.fori_loop(0, qi // 2, pair, state))


def _moba_kernel(q_ref, k_ref, v_ref, sl_ref, o_ref, km_ref, ka_ref, vt_ref, z_ref, *, nb):
    _walk_key_blocks(*_moba_parts(q_ref, k_ref, v_ref, sl_ref, o_ref, km_ref, ka_ref, vt_ref, nb),
                     z_ref=z_ref)


def _sb_kernel(q_ref, k_ref, v_ref, o_ref, vt_ref, *, nb):
    _walk_key_blocks(*_sb_parts(q_ref, k_ref, v_ref, o_ref, vt_ref, nb))


def _attention_specs(s, q_col, k_col, v_col):
    blk = MOBA_BLOCK
    return ([pl.BlockSpec((1, blk, ATT_W), lambda b, i: (b, i, q_col // ATT_W)),
             pl.BlockSpec((1, s, ATT_W), lambda b, i: (b, 0, k_col // ATT_W)),
             pl.BlockSpec((1, s, ATT_W), lambda b, i: (b, 0, v_col // ATT_W))],
            pl.BlockSpec((1, blk, ATT_W), lambda b, i: (b, i, 0)))


def _moba(proj):
    bsz, s, _ = proj.shape
    assert MOBA_HEADS == ATT_HEADS_PER_STEP
    blk = MOBA_BLOCK
    nb = s // blk
    slopes = 2.0 ** (-8.0 * np.arange(1, MOBA_HEADS + 1) / MOBA_HEADS)
    sl = jnp.asarray(np.repeat(slopes, HEAD_DIM).reshape(1, 1, ATT_W), F32)
    in_specs, out_spec = _attention_specs(s, A_MOBA_Q, A_MOBA_K, A_MOBA_V)
    return pl.pallas_call(
        functools.partial(_moba_kernel, nb=nb),
        grid=(bsz, nb),
        in_specs=in_specs + [pl.BlockSpec((1, 1, ATT_W), lambda b, i: (0, 0, 0))],
        out_specs=out_spec,
        out_shape=jax.ShapeDtypeStruct((bsz, s, ATT_W), MXU_DTYPE),
        scratch_shapes=[pltpu.VMEM((LANES, ATT_W), F32),
                        pltpu.VMEM((ATT_HEADS_PER_STEP, s, 2 * HEAD_DIM), MXU_DTYPE),
                        pltpu.VMEM((nb, ATT_HEADS_PER_STEP, 2 * HEAD_DIM, blk), MXU_DTYPE),
                        pltpu.VMEM((2 * ATT_HEADS_PER_STEP, blk, blk), F32)],
        compiler_params=_params(("parallel", "arbitrary")),
        name="moba",
    )(proj, proj, proj, sl)


def _stick_breaking(proj):
    bsz, s, _ = proj.shape
    assert SB_HEADS == ATT_HEADS_PER_STEP and SB_BLOCK == MOBA_BLOCK
    blk = SB_BLOCK
    nb = s // blk
    in_specs, out_spec = _attention_specs(s, A_SB_Q, A_SB_K, A_SB_V)
    return pl.pallas_call(
        functools.partial(_sb_kernel, nb=nb),
        grid=(bsz, nb),
        in_specs=in_specs,
        out_specs=out_spec,
        out_shape=jax.ShapeDtypeStruct((bsz, s, ATT_W), MXU_DTYPE),
        scratch_shapes=[pltpu.VMEM((nb, ATT_HEADS_PER_STEP, HEAD_DIM, blk), MXU_DTYPE)],
        compiler_params=_params(("parallel", "arbitrary")),
        name="stick_breaking",
    )(proj, proj, proj)


def _ret_kernel(q_ref, k_ref, v_ref, g_ref, gn_g_ref, gn_b_ref, o_ref, st_ref):
    ch = RET_CHUNK

    @pl.when(pl.program_id(1) == 0)
    def _():
        st_ref[...] = jnp.zeros_like(st_ref)

    i_col = lax.broadcasted_iota(jnp.int32, (ch, 1), 0).astype(F32)
    diff = (lax.broadcasted_iota(jnp.int32, (ch, ch), 0)
            - lax.broadcasted_iota(jnp.int32, (ch, ch), 1)).astype(F32)
    heads = range(RET_HEADS)
    log_g = [float(np.log(1.0 - 2.0 ** (-5.0 - h))) for h in heads]
    qs = [(q_ref[0, :, h * RET_DK:(h + 1) * RET_DK].astype(F32) * RET_DK ** -0.5).astype(MXU_DTYPE)
          for h in heads]
    k = [k_ref[0, :, h * RET_DK:(h + 1) * RET_DK] for h in heads]
    v = [v_ref[0, :, h * RET_DV:(h + 1) * RET_DV] for h in heads]
    prev = [st_ref[h] for h in heads]
    qk = [_dot_nt(qs[h], k[h]) for h in heads]
    o_cross = [_dot(qs[h], prev[h].astype(MXU_DTYPE)) for h in heads]
    kd = [(k[h].astype(F32) * jnp.exp((ch - 1.0 - i_col) * log_g[h])).astype(MXU_DTYPE)
          for h in heads]
    st_new = [_dot_tn(kd[h], v[h]) for h in heads]
    qkd = []
    for h in heads:
        decay = jnp.where(diff >= 0, jnp.exp(jnp.maximum(diff, 0.0) * log_g[h]), 0.0)
        qkd.append((qk[h] * decay).astype(MXU_DTYPE))
    o_intra = [_dot(qkd[h], v[h]) for h in heads]
    for h in heads:
        st_ref[h] = float(np.exp(ch * log_g[h])) * prev[h] + st_new[h]
        o = o_intra[h] + o_cross[h] * jnp.exp((i_col + 1.0) * log_g[h])
        mu = jnp.mean(o, axis=-1, keepdims=True)
        d = o - mu
        var = jnp.mean(d * d, axis=-1, keepdims=True)
        cs = slice(h * RET_DV, (h + 1) * RET_DV)
        on = d * lax.rsqrt(var + NORM_EPS) * gn_g_ref[:, cs] + gn_b_ref[:, cs]
        o_ref[0, :, cs] = (_silu(g_ref[0, :, cs].astype(F32)) * on).astype(o_ref.dtype)


def _ssd_kernel(z_ref, xbc_ref, dt_ref, cw_ref, cb_ref, dtb_ref, alog_ref, dsk_ref, ng_ref,
                o_ref, tail_ref, st_ref, y_ref):
    ch = SSD_CHUNK
    heads_per_group = SSD_HEADS // SSD_GROUPS
    halo = SSD_HALO

    @pl.when(pl.program_id(1) == 0)
    def _():
        tail_ref[0:halo, :] = jnp.zeros((halo, SSD_XBC), F32)
        st_ref[...] = jnp.zeros_like(st_ref)

    tail_ref[halo:, :] = xbc_ref[0].astype(F32)
    conv = cb_ref[...]
    for kk in range(SSD_CONV):
        off = halo - (SSD_CONV - 1) + kk
        conv = conv + cw_ref[kk:kk + 1, :] * tail_ref[off:off + ch, :]
    tail_ref[0:halo, :] = tail_ref[ch:ch + halo, :]
    xc = _silu(conv)
    xs = xc[:, :SSD_D_INNER]
    bm = xc[:, SSD_D_INNER:SSD_D_INNER + SSD_GROUPS * SSD_STATE]
    cm = xc[:, SSD_D_INNER + SSD_GROUPS * SSD_STATE:]

    dtr = dt_ref[0] + dtb_ref[...]
    dt = jnp.maximum(dtr, 0.0) + jnp.log1p(jnp.exp(-jnp.abs(dtr)))
    a = -jnp.exp(alog_ref[...])
    r_iota = lax.broadcasted_iota(jnp.int32, (ch, ch), 0)
    c_iota = lax.broadcasted_iota(jnp.int32, (ch, ch), 1)
    causal = r_iota >= c_iota
    tri = jnp.where(causal, 1.0, 0.0).astype(MXU_DTYPE)
    da = dt * a
    da_hi, da_lo = _split(da)
    cum = _dot(tri, da_hi) + _dot(tri, da_lo)
    cum_t = cum.T
    dt_t = dt.T
    cum_last = cum[ch - 1:ch, :]
    decay_st = jnp.exp(cum_last - cum) * dt
    e_cum = jnp.exp(cum)
    e_last = jnp.exp(cum_last)

    for g in range(SSD_GROUPS):
        bm_g = bm[:, g * SSD_STATE:(g + 1) * SSD_STATE].astype(MXU_DTYPE)
        cm_g = cm[:, g * SSD_STATE:(g + 1) * SSD_STATE].astype(MXU_DTYPE)
        cb = _dot_nt(cm_g, bm_g)
        for r in range(heads_per_group):
            hd = g * heads_per_group + r
            cs = slice(hd * SSD_HEAD_DIM, (hd + 1) * SSD_HEAD_DIM)
            x_h = xs[:, cs]
            seg = cum[:, hd:hd + 1] - cum_t[hd:hd + 1, :]
            lmat = jnp.exp(jnp.where(causal, seg, NEG_INF))
            w = cb * lmat * dt_t[hd:hd + 1, :]
            y = _dot(w.astype(MXU_DTYPE), x_h.astype(MXU_DTYPE))
            prev = st_ref[hd]
            y = y + _dot(cm_g, prev.astype(MXU_DTYPE)) * e_cum[:, hd:hd + 1]
            xd = (x_h * decay_st[:, hd:hd + 1]).astype(MXU_DTYPE)
            st_ref[hd] = e_last[:, hd:hd + 1] * prev + _dot_tn(bm_g, xd)
            y_ref[:, cs] = y + dsk_ref[:, cs] * x_h

    y = y_ref[...] * _silu(z_ref[0].astype(F32))
    gw = SSD_D_INNER // SSD_GROUPS
    for g in range(SSD_GROUPS):
        cs = slice(g * gw, (g + 1) * gw)
        yg = y[:, cs]
        ms = jnp.mean(yg * yg, axis=-1, keepdims=True)
        o_ref[0, :, cs] = (yg * lax.rsqrt(ms + LN_EPS) * ng_ref[:, cs]).astype(o_ref.dtype)


def _scan_kernel(rq_ref, rk_ref, rv_ref, rg_ref, gn_g_ref, gn_b_ref,
                 z_ref, xbc_ref, dt_ref, cw_ref, cb_ref, dtb_ref, alog_ref, dsk_ref, ng_ref,
                 o_ret_ref, o_ssd_ref, ret_st_ref, tail_ref, ssd_st_ref, y_ref):
    _ret_kernel(rq_ref, rk_ref, rv_ref, rg_ref, gn_g_ref, gn_b_ref, o_ret_ref, ret_st_ref)
    _ssd_kernel(z_ref, xbc_ref, dt_ref, cw_ref, cb_ref, dtb_ref, alog_ref, dsk_ref, ng_ref,
                o_ssd_ref, tail_ref, ssd_st_ref, y_ref)


def _scans(proj, proj_dt, gn_g, gn_b, conv_w, conv_b, dt_bias, a_log, d_skip, norm_g):
    bsz, s, _ = proj.shape
    assert RET_CHUNK == SSD_CHUNK
    ch = SSD_CHUNK
    wq, wv = RET_HEADS * RET_DK, RET_HEADS * RET_DV
    pad = LANES - SSD_HEADS
    dtb = jnp.pad(dt_bias, (0, pad)).reshape(1, LANES)
    alog = jnp.pad(a_log, (0, pad)).reshape(1, LANES)
    dsk = jnp.repeat(d_skip, SSD_HEAD_DIM).reshape(1, SSD_D_INNER)
    const = lambda b, i: (0, 0)
    return pl.pallas_call(
        _scan_kernel,
        grid=(bsz, s // ch),
        in_specs=[pl.BlockSpec((1, ch, wq), lambda b, i: (b, i, A_RET_Q // wq)),
                  pl.BlockSpec((1, ch, wq), lambda b, i: (b, i, A_RET_K // wq)),
                  pl.BlockSpec((1, ch, wv), lambda b, i: (b, i, A_RET_V // wv)),
                  pl.BlockSpec((1, ch, wv), lambda b, i: (b, i, B_RET_G // wv)),
                  pl.BlockSpec((1, wv), const),
                  pl.BlockSpec((1, wv), const),
                  pl.BlockSpec((1, ch, SSD_D_INNER), lambda b, i: (b, i, B_SSD_Z // SSD_D_INNER)),
                  pl.BlockSpec((1, ch, SSD_XBC), lambda b, i: (b, i, B_XBC // SSD_XBC)),
                  pl.BlockSpec((1, ch, LANES), lambda b, i: (b, i, 0)),
                  pl.BlockSpec((SSD_CONV, SSD_XBC), const),
                  pl.BlockSpec((1, SSD_XBC), const),
                  pl.BlockSpec((1, LANES), const),
                  pl.BlockSpec((1, LANES), const),
                  pl.BlockSpec((1, SSD_D_INNER), const),
                  pl.BlockSpec((1, SSD_D_INNER), const)],
        out_specs=[pl.BlockSpec((1, ch, wv), lambda b, i: (b, i, 0)),
                   pl.BlockSpec((1, ch, SSD_D_INNER), lambda b, i: (b, i, 0))],
        out_shape=[jax.ShapeDtypeStruct((bsz, s, wv), MXU_DTYPE),
                   jax.ShapeDtypeStruct((bsz, s, SSD_D_INNER), MXU_DTYPE)],
        scratch_shapes=[pltpu.VMEM((RET_HEADS, RET_DK, RET_DV), F32),
                        pltpu.VMEM((SSD_HALO + ch, SSD_XBC), F32),
                        pltpu.VMEM((SSD_HEADS, SSD_STATE, SSD_HEAD_DIM), F32),
                        pltpu.VMEM((ch, SSD_D_INNER), F32)],
        compiler_params=_params(("parallel", "arbitrary")),
        name="retention_ssd",
    )(proj, proj, proj, proj, gn_g.reshape(1, wv), gn_b.reshape(1, wv),
      proj, proj, proj_dt, conv_w, conv_b.reshape(1, SSD_XBC), dtb, alog, dsk,
      norm_g.reshape(1, SSD_D_INNER))


def _merge_kernel(oa_ref, ob_ref, oc_ref, od_ref, gates_ref, x_ref, g1_ref, wbr_ref, wout_ref,
                  lg_ref, lb_ref, o_ref):
    merged = None
    for n, br in enumerate((oa_ref, ob_ref, oc_ref, od_ref)):
        y = _dot(br[0], wbr_ref[n])
        t = _sigmoid(gates_ref[0, :, n * D_MODEL:(n + 1) * D_MODEL].astype(F32)) * y
        merged = t if merged is None else merged + t
    mix = _dot(merged.astype(MXU_DTYPE), wout_ref[...])
    o_ref[0] = _resid_ln(x_ref[0], g1_ref[0], mix, lg_ref[...], lb_ref[...])


def _merge(o_a, o_b, o_c, o_d, proj, x, g1, w_br, w_out, ln_g, ln_b, tm=512):
    bsz, s, d = x.shape
    tm = min(tm, s)
    gw = N_BRANCH * d
    row = lambda b, i: (b, i, 0)
    const2 = lambda b, i: (0, 0)
    br_spec = pl.BlockSpec((1, tm, BRANCH_W), row)
    return pl.pallas_call(
        _merge_kernel,
        grid=(bsz, s // tm),
        in_specs=[br_spec, br_spec, br_spec, br_spec,
                  pl.BlockSpec((1, tm, gw), lambda b, i: (b, i, B_GATES // gw)),
                  pl.BlockSpec((1, tm, d), row),
                  pl.BlockSpec((1, 1, d), lambda b, i: (b, 0, 0)),
                  pl.BlockSpec((N_BRANCH, BRANCH_W, d), lambda b, i: (0, 0, 0)),
                  pl.BlockSpec((d, d), const2),
                  pl.BlockSpec((1, d), const2),
                  pl.BlockSpec((1, d), const2)],
        out_specs=pl.BlockSpec((1, tm, d), row),
        out_shape=jax.ShapeDtypeStruct((bsz, s, d), F32),
        compiler_params=_params(("parallel", "parallel")),
        name="merge",
    )(o_a, o_b, o_c, o_d, proj, x, g1, w_br, w_out, ln_g.reshape(1, d), ln_b.reshape(1, d))


def _ffn_kernel(x_ref, sc_ref, sh_ref, g2_ref, wg_ref, wu_ref, wd_ref, lg_ref, lb_ref,
                o_ref, h_ref, acc_ref):
    j = pl.program_id(2)

    @pl.when(j == 0)
    def _():
        h = x_ref[0] * (1.0 + sc_ref[0]) + sh_ref[0]
        h_ref[...] = h.astype(MXU_DTYPE)
        acc_ref[...] = jnp.zeros_like(acc_ref)

    h = h_ref[...]
    act = _silu(_dot(h, wg_ref[...])) * _dot(h, wu_ref[...])
    acc_ref[...] += _dot(act.astype(MXU_DTYPE), wd_ref[...])

    @pl.when(j == pl.num_programs(2) - 1)
    def _():
        o_ref[0] = _resid_ln(x_ref[0], g2_ref[0], acc_ref[...], lg_ref[...], lb_ref[...])


def _ffn(x, sc, sh, g2, w_gu, w_down, ln_g, ln_b, tm=512, tf=1408):
    bsz, s, d = x.shape
    tm = min(tm, s)
    nf = D_FF // tf
    row = lambda b, i, j: (b, i, 0)
    vec = lambda b, i, j: (b, 0, 0)
    const2 = lambda b, i, j: (0, 0)
    return pl.pallas_call(
        _ffn_kernel,
        grid=(bsz, s // tm, nf),
        in_specs=[pl.BlockSpec((1, tm, d), row),
                  pl.BlockSpec((1, 1, d), vec), pl.BlockSpec((1, 1, d), vec),
                  pl.BlockSpec((1, 1, d), vec),
                  pl.BlockSpec((d, tf), lambda b, i, j: (0, j)),
                  pl.BlockSpec((d, tf), lambda b, i, j: (0, nf + j)),
                  pl.BlockSpec((tf, d), lambda b, i, j: (j, 0)),
                  pl.BlockSpec((1, d), const2), pl.BlockSpec((1, d), const2)],
        out_specs=pl.BlockSpec((1, tm, d), row),
        out_shape=jax.ShapeDtypeStruct((bsz, s, d), F32),
        scratch_shapes=[pltpu.VMEM((tm, d), MXU_DTYPE), pltpu.VMEM((tm, d), F32)],
        compiler_params=_params(("parallel", "parallel", "arbitrary")),
        name="ffn_dense",
    )(x, sc, sh, g2, w_gu, w_gu, w_down, ln_g.reshape(1, d), ln_b.reshape(1, d))


def _router_kernel(x_ref, sc_ref, sh_ref, rw_ref, rb_ref,
                   h_ref, comb_ref, slot_ref, slot_t_ref, cnt_ref):
    tm = x_ref.shape[1]
    lane = lax.broadcasted_iota(jnp.int32, (tm, LANES), 1)
    h = x_ref[0] * (1.0 + sc_ref[0]) + sh_ref[0]
    h_ref[0] = h.astype(MXU_DTYPE)
    logits = _dot3(h, rw_ref[...]) + rb_ref[...]
    logits = jnp.where(lane < N_EXPERTS, logits, -jnp.inf)
    v1 = jnp.max(logits, axis=1, keepdims=True)
    i1 = jnp.min(jnp.where(logits == v1, lane, LANES), axis=1, keepdims=True)
    rest = jnp.where(lane == i1, -jnp.inf, logits)
    v2 = jnp.max(rest, axis=1, keepdims=True)
    i2 = jnp.min(jnp.where(rest == v2, lane, LANES), axis=1, keepdims=True)
    e2 = jnp.exp(v2 - v1)
    w1 = 1.0 / (1.0 + e2)
    w2 = e2 / (1.0 + e2)
    comb_ref[0] = jnp.where(lane == i1, w1, 0.0) + jnp.where(lane == i2, w2, 0.0)
    routed = jnp.logical_or(lane == i1, lane == i2)
    earlier = (lax.broadcasted_iota(jnp.int32, (tm, tm), 1)
               < lax.broadcasted_iota(jnp.int32, (tm, tm), 0))
    before = _dot(jnp.where(earlier, 1.0, 0.0).astype(MXU_DTYPE),
                  jnp.where(routed, 1.0, 0.0).astype(MXU_DTYPE))
    slot = jnp.where(routed, before, -1.0)
    slot_ref[0] = slot
    slot_t_ref[0] = slot.T[0:N_EXPERTS, :]
    cnt_ref[0] = jnp.sum(jnp.where(routed, 1, 0), axis=0, keepdims=True)


def _moe_kernel(cnt_ref, h_ref, x_ref, g2_ref, comb_ref, slot_ref, slot_t_ref,
                wg_ref, wu_ref, wd_ref, lg_ref, lb_ref, o_ref, xe_ref, ye_ref, *, n_tiles):
    tm = x_ref.shape[1]
    e = pl.program_id(2)
    j = pl.program_id(3)
    last_j = pl.num_programs(3) - 1
    count = cnt_ref[(pl.program_id(0) * n_tiles + pl.program_id(1)) * N_EXPERTS + e]
    n_full = jnp.maximum(count - MOE_TAIL_MIN - 1, 0) // MOE_ROWS
    tail0 = pl.multiple_of(n_full * MOE_ROWS, MOE_ROWS)
    tail_steps = (count - n_full * MOE_ROWS + MOE_ROW_STEP - 1) // MOE_ROW_STEP

    def for_pieces(fn):
        def full_piece(c, carry):
            fn(pl.multiple_of(c * MOE_ROWS, MOE_ROWS), MOE_ROWS)
            return carry

        lax.fori_loop(0, n_full, full_piece, 0)
        for steps in range(1, (MOE_ROWS + MOE_TAIL_MIN) // MOE_ROW_STEP + 1):
            @pl.when(tail_steps == steps)
            def _(steps=steps):
                fn(tail0, steps * MOE_ROW_STEP)

    @pl.when(jnp.logical_and(e == 0, j == 0))
    def _():
        o_ref[0] = jnp.zeros((tm, D_MODEL), F32)

    @pl.when(j == 0)
    def _():
        slot_row = slot_t_ref[0, pl.ds(e, 1), :]

        def gather(r0, rows):
            slots = (lax.broadcasted_iota(jnp.int32, (rows, tm), 0) + r0).astype(F32)
            pick = jnp.where(slot_row == slots, 1.0, 0.0).astype(MXU_DTYPE)
            xe_ref[pl.ds(r0, rows), :] = _dot(pick, h_ref[0]).astype(MXU_DTYPE)
            ye_ref[pl.ds(r0, rows), :] = jnp.zeros((rows, D_MODEL), F32)

        for_pieces(gather)

    def expert_rows(r0, rows):
        xc = xe_ref[pl.ds(r0, rows), :]
        act = _silu(_dot(xc, wg_ref[0])) * _dot(xc, wu_ref[0])
        ye_ref[pl.ds(r0, rows), :] += _dot(act.astype(MXU_DTYPE), wd_ref[0])

    for_pieces(expert_rows)

    @pl.when(j == last_j)
    def _():
        lane = lax.broadcasted_iota(jnp.int32, (tm, LANES), 1)
        mine = lane == e
        slot_col = jnp.sum(jnp.where(mine, slot_ref[0], 0.0), axis=1, keepdims=True)
        comb_col = jnp.sum(jnp.where(mine, comb_ref[0], 0.0), axis=1, keepdims=True)

        def scatter(r0, rows):
            slots = (lax.broadcasted_iota(jnp.int32, (tm, rows), 1) + r0).astype(F32)
            place = jnp.where(slot_col == slots, 1.0, 0.0).astype(MXU_DTYPE)
            o_ref[0] += comb_col * _dot(place, ye_ref[pl.ds(r0, rows), :].astype(MXU_DTYPE))

        for_pieces(scatter)

    @pl.when(jnp.logical_and(e == pl.num_programs(2) - 1, j == last_j))
    def _():
        o_ref[0] = _resid_ln(x_ref[0], g2_ref[0], o_ref[0], lg_ref[...], lb_ref[...])


def _moe(x, sc, sh, g2, router_w, router_b, w_gu, w_down, ln_g, ln_b, tm=1024, tf=1792):
    bsz, s, d = x.shape
    tm = min(tm, s)
    nt = s // tm
    nf = D_FF_EXPERT // tf
    rw = jnp.pad(router_w, ((0, 0), (0, LANES - N_EXPERTS)))
    rb = jnp.pad(router_b, (0, LANES - N_EXPERTS)).reshape(1, LANES)

    row2 = lambda b, i: (b, i, 0)
    vec2 = lambda b, i: (b, 0, 0)
    h, comb, slot, slot_t, cnt = pl.pallas_call(
        _router_kernel,
        grid=(bsz, nt),
        in_specs=[pl.BlockSpec((1, tm, d), row2),
                  pl.BlockSpec((1, 1, d), vec2), pl.BlockSpec((1, 1, d), vec2),
                  pl.BlockSpec((d, LANES), lambda b, i: (0, 0)),
                  pl.BlockSpec((1, LANES), lambda b, i: (0, 0))],
        out_specs=[pl.BlockSpec((1, tm, d), row2),
                   pl.BlockSpec((1, tm, LANES), row2),
                   pl.BlockSpec((1, tm, LANES), row2),
                   pl.BlockSpec((1, N_EXPERTS, tm), lambda b, i: (b * nt + i, 0, 0)),
                   pl.BlockSpec((1, 1, LANES), lambda b, i: (b * nt + i, 0, 0))],
        out_shape=[jax.ShapeDtypeStruct((bsz, s, d), MXU_DTYPE),
                   jax.ShapeDtypeStruct((bsz, s, LANES), F32),
                   jax.ShapeDtypeStruct((bsz, s, LANES), F32),
                   jax.ShapeDtypeStruct((bsz * nt, N_EXPERTS, tm), F32),
                   jax.ShapeDtypeStruct((bsz * nt, 1, LANES), jnp.int32)],
        compiler_params=_params(("parallel", "parallel")),
        name="moe_router",
    )(x, sc, sh, rw, rb)
    counts = cnt[:, 0, :N_EXPERTS].reshape(bsz * nt * N_EXPERTS)

    row = lambda b, i, e, j, cnt: (b, i, 0)
    vec = lambda b, i, e, j, cnt: (b, 0, 0)
    const2 = lambda b, i, e, j, cnt: (0, 0)
    once = pl.Buffered(1)
    grid_spec = pltpu.PrefetchScalarGridSpec(
        num_scalar_prefetch=1,
        grid=(bsz, nt, N_EXPERTS, nf),
        in_specs=[pl.BlockSpec((1, tm, d), row, pipeline_mode=once),
                  pl.BlockSpec((1, tm, d), row, pipeline_mode=once),
                  pl.BlockSpec((1, 1, d), vec),
                  pl.BlockSpec((1, tm, LANES), row, pipeline_mode=once),
                  pl.BlockSpec((1, tm, LANES), row, pipeline_mode=once),
                  pl.BlockSpec((1, N_EXPERTS, tm), lambda b, i, e, j, cnt: (b * nt + i, 0, 0)),
                  pl.BlockSpec((1, d, tf), lambda b, i, e, j, cnt: (e, 0, j)),
                  pl.BlockSpec((1, d, tf), lambda b, i, e, j, cnt: (e, 0, nf + j)),
                  pl.BlockSpec((1, tf, d), lambda b, i, e, j, cnt: (e, j, 0)),
                  pl.BlockSpec((1, d), const2), pl.BlockSpec((1, d), const2)],
        out_specs=pl.BlockSpec((1, tm, d), row),
        scratch_shapes=[pltpu.VMEM((tm, d), MXU_DTYPE), pltpu.VMEM((tm, d), F32)])
    return pl.pallas_call(
        functools.partial(_moe_kernel, n_tiles=nt),
        grid_spec=grid_spec,
        out_shape=jax.ShapeDtypeStruct((bsz, s, d), F32),
        compiler_params=pltpu.CompilerParams(
            dimension_semantics=("parallel", "parallel", "arbitrary", "arbitrary"),
            vmem_limit_bytes=MOE_VMEM_LIMIT),
        name="moe_experts",
    )(counts, h, x, g2, comb, slot, slot_t, w_gu, w_gu, w_down,
      ln_g.reshape(1, d), ln_b.reshape(1, d))


def _split_w_in(w):
    p = np.cumsum([0, 512, 512, 512, 256, 256, 512, 512, 512, SSD_XBC, SSD_HEADS, 512, 512, 512,
                   N_BRANCH * D_MODEL])
    mq, mk, mv, rq, rk, rv, rg, sz, sxbc, sdt, bq, bk, bv, gates = (
        w[:, p[i]:p[i + 1]] for i in range(14))
    w_main = jnp.concatenate([mq, mk, mv, rq, rk, rv, bq, bk, bv, gates, rg, sz, sxbc], axis=1)
    w_dt = jnp.pad(sdt, ((0, 0), (0, LANES - SSD_HEADS)))
    assert w_main.shape[1] == PROJ_WIDTH
    return w_main.astype(MXU_DTYPE), w_dt.astype(MXU_DTYPE)


def kernel(x, c, w_ada, b_ada, w_in, conv_w, conv_b, dt_bias, a_log, d_skip, ssm_norm_g, ret_gn_g, ret_gn_b, w_br, w_out, ln1_g, ln1_b, ln2_g, ln2_b, ffn_w_gu, ffn_w_down, router_w, router_b, expert_w_gu, expert_w_down):
    bsz = x.shape[0]
    mod = _ada(c, w_ada, b_ada)
    for l in range(DEPTH):
        sh1, sc1, g1, sh2, sc2, g2 = (
            mod[l, :, i * D_MODEL:(i + 1) * D_MODEL].reshape(bsz, 1, D_MODEL) for i in range(6))
        w_main, w_dt = _split_w_in(w_in[l])
        proj, proj_dt = _inproj(x, sc1, sh1, w_main, w_dt, tn=2560)
        o_a = _moba(proj)
        o_d = _stick_breaking(proj)
        o_b, o_c = _scans(proj, proj_dt, ret_gn_g[l], ret_gn_b[l], conv_w[l], conv_b[l],
                          dt_bias[l], a_log[l], d_skip[l], ssm_norm_g[l])
        x = _merge(o_a, o_b, o_c, o_d, proj, x, g1, w_br[l].astype(MXU_DTYPE),
                   w_out[l].astype(MXU_DTYPE), ln1_g[l], ln1_b[l])
        if l % 2 == 0:
            x = _ffn(x, sc2, sh2, g2, ffn_w_gu[l // 2].astype(MXU_DTYPE),
                     ffn_w_down[l // 2].astype(MXU_DTYPE), ln2_g[l], ln2_b[l])
        else:
            x = _moe(x, sc2, sh2, g2, router_w[l // 2], router_b[l // 2],
                     expert_w_gu[l // 2].astype(MXU_DTYPE), expert_w_down[l // 2].astype(MXU_DTYPE),
                     ln2_g[l], ln2_b[l])
    return x
```

```python
import functools

import numpy as np
import jax
import jax.numpy as jnp
from jax import lax
from jax.experimental import pallas as pl
from jax.experimental.pallas import tpu as pltpu

F32 = jnp.float32
MXU_DTYPE = jnp.bfloat16

D_MODEL = 1024
DEPTH = 2
MOBA_HEADS = 8
MOBA_BLOCK = 256
MOBA_TOPK = 3
RET_HEADS = 4
RET_DK = 64
RET_DV = 128
RET_CHUNK = 256
SSD_D_INNER = 512
SSD_HEAD_DIM = 64
SSD_HEADS = 8
SSD_GROUPS = 2
SSD_STATE = 128
SSD_CONV = 4
SSD_CHUNK = 256
SSD_XBC = SSD_D_INNER + 2 * SSD_GROUPS * SSD_STATE
SSD_HALO = 8
SB_HEADS = 8
SB_BLOCK = 256
HEAD_DIM = 64
ATT_HEADS_PER_STEP = 8
ATT_W = ATT_HEADS_PER_STEP * HEAD_DIM
ATT_Q_BLOCKS_PER_STEP = 2
BRANCH_W = 512
N_BRANCH = 4
D_FF = 2816
N_EXPERTS = 8
D_FF_EXPERT = 3584
DEEPNORM_ALPHA = (2.0 * DEPTH) ** 0.25
LN_EPS = 1e-5
NORM_EPS = 1e-6
NEG_INF = -1e30
LOG2E = 1.4426950408889634
LANES = 128
VMEM_LIMIT = 48 * 2 ** 20
MOE_VMEM_LIMIT = 56 * 2 ** 20
MOE_ROWS = 256
MOE_ROW_STEP = 64
MOE_TAIL_MIN = 128

A_MOBA_Q, A_MOBA_K, A_MOBA_V = 0, 512, 1024
A_RET_Q, A_RET_K, A_RET_V = 1536, 1792, 2048
A_SB_Q, A_SB_K, A_SB_V = 2560, 3072, 3584
B_GATES, B_RET_G, B_SSD_Z, B_XBC = 4096, 8192, 8704, 9216
PROJ_WIDTH = 10240


def _params(sem):
    return pltpu.CompilerParams(dimension_semantics=sem, vmem_limit_bytes=VMEM_LIMIT)


def _dot(a, b):
    return jnp.dot(a, b, preferred_element_type=F32)


def _dot_nt(a, b):
    return lax.dot_general(a, b, (((1,), (1,)), ((), ())), preferred_element_type=F32)


def _dot_tn(a, b):
    return lax.dot_general(a, b, (((0,), (0,)), ((), ())), preferred_element_type=F32)


def _split(x):
    hi = x.astype(MXU_DTYPE)
    lo = (x - hi.astype(F32)).astype(MXU_DTYPE)
    return hi, lo


def _dot3(a, b):
    ah, al = _split(a)
    bh, bl = _split(b)
    return _dot(ah, bh) + _dot(al, bh) + _dot(ah, bl)


def _sigmoid(x):
    return 1.0 / (1.0 + jnp.exp(-x))


def _silu(x):
    return x * _sigmoid(x)


def _resid_ln(x, gate, f, ln_g, ln_b):
    y = DEEPNORM_ALPHA * x + gate * f
    mu = jnp.mean(y, axis=-1, keepdims=True)
    d = y - mu
    var = jnp.mean(d * d, axis=-1, keepdims=True)
    return d * lax.rsqrt(var + LN_EPS) * ln_g + ln_b


def _ada_kernel(c_ref, w_ref, b_ref, o_ref):
    c = c_ref[...]
    o_ref[0] = _dot3(_silu(c), w_ref[0]) + b_ref[0]


def _ada(c, w_ada, b_ada):
    depth, d, n = w_ada.shape
    bsz = c.shape[0]
    tn = 1536
    return pl.pallas_call(
        _ada_kernel,
        grid=(depth, n // tn),
        in_specs=[pl.BlockSpec((bsz, d), lambda l, j: (0, 0)),
                  pl.BlockSpec((1, d, tn), lambda l, j: (l, 0, j)),
                  pl.BlockSpec((1, 1, tn), lambda l, j: (l, 0, j))],
        out_specs=pl.BlockSpec((1, bsz, tn), lambda l, j: (l, 0, j)),
        out_shape=jax.ShapeDtypeStruct((depth, bsz, n), F32),
        compiler_params=_params(("arbitrary", "arbitrary")),
        name="ada_mod",
    )(c, w_ada, b_ada.reshape(depth, 1, n))


def _inproj_kernel(x_ref, sc_ref, sh_ref, w_ref, wdt_ref, o_ref, odt_ref, h_ref):
    @pl.when(pl.program_id(2) == 0)
    def _():
        h = x_ref[0] * (1.0 + sc_ref[0]) + sh_ref[0]
        h_ref[...] = h.astype(MXU_DTYPE)
        odt_ref[0] = _dot(h_ref[...], wdt_ref[...])

    o_ref[0] = _dot(h_ref[...], w_ref[...]).astype(o_ref.dtype)


def _inproj(x, sc, sh, w, w_dt, tn, tm=1024):
    bsz, s, d = x.shape
    n = w.shape[1]
    n_dt = w_dt.shape[1]
    tm = min(tm, s)
    return pl.pallas_call(
        _inproj_kernel,
        grid=(bsz, s // tm, n // tn),
        in_specs=[pl.BlockSpec((1, tm, d), lambda b, i, j: (b, i, 0)),
                  pl.BlockSpec((1, 1, d), lambda b, i, j: (b, 0, 0)),
                  pl.BlockSpec((1, 1, d), lambda b, i, j: (b, 0, 0)),
                  pl.BlockSpec((d, tn), lambda b, i, j: (0, j)),
                  pl.BlockSpec((d, n_dt), lambda b, i, j: (0, 0))],
        out_specs=[pl.BlockSpec((1, tm, tn), lambda b, i, j: (b, i, j)),
                   pl.BlockSpec((1, tm, n_dt), lambda b, i, j: (b, i, 0))],
        out_shape=[jax.ShapeDtypeStruct((bsz, s, n), MXU_DTYPE),
                   jax.ShapeDtypeStruct((bsz, s, n_dt), F32)],
        scratch_shapes=[pltpu.VMEM((tm, d), MXU_DTYPE)],
        compiler_params=_params(("parallel", "parallel", "arbitrary")),
        name="in_proj",
    )(x, sc, sh, w, w_dt)


def _moba_parts(qi, rows, q_ref, k_ref, v_ref, sl_ref, o_ref, km_ref, ka_ref, vt_ref, nb):
    blk = MOBA_BLOCK
    nh = ATT_HEADS_PER_STEP
    heads = range(nh)
    s_len = nb * blk
    nb_pad = -(-nb // 8) * 8
    extra = HEAD_DIM
    assert nb_pad + 8 <= extra
    cols = [slice(h * HEAD_DIM, (h + 1) * HEAD_DIM) for h in heads]

    @pl.when(qi == 0)
    def _():
        k_all = k_ref[0]
        kk = k_all.astype(F32).reshape(nb, blk, ATT_W)
        km_ref[...] = jnp.zeros_like(km_ref)
        km_ref[0:nb, :] = jnp.sum(kk, axis=1) * (1.0 / blk)
        pos = lax.broadcasted_iota(jnp.int32, (s_len, extra), 0)
        lane = lax.broadcasted_iota(jnp.int32, (s_len, extra), 1)
        one_hot = jnp.where(lane == pos // blk, 1.0, 0.0)
        pos_hi = jnp.where(lane == nb_pad, (pos // blk * blk).astype(F32), 0.0)
        pos_lo = jnp.where(lane == nb_pad + 1, (pos % blk).astype(F32), 0.0)
        tail = (one_hot + pos_hi + pos_lo).astype(MXU_DTYPE)
        for h in heads:
            ka_ref[h] = jnp.concatenate([k_all[:, cols[h]], tail], axis=1)

        ones_rows = jnp.where(lax.broadcasted_iota(jnp.int32, (extra, blk), 0) == 0, 1.0, 0.0)

        def transpose_block(j, carry):
            j0 = pl.multiple_of(j * blk, blk)
            vt = v_ref[0, pl.ds(j0, blk), :].astype(F32).T
            for h in heads:
                vt_ref[j, h] = jnp.concatenate([vt[cols[h], :], ones_rows], axis=0).astype(MXU_DTYPE)
            return carry

        lax.fori_loop(0, nb, transpose_block, 0)

    key_i = lax.broadcasted_iota(jnp.int32, (blk, blk), 0)
    qry_i = lax.broadcasted_iota(jnp.int32, (blk, blk), 1)
    n_iota = lax.broadcasted_iota(jnp.int32, (nb_pad, blk), 0)
    past = n_iota < qi
    k_sel = min(MOBA_TOPK, max(nb - 1, 1))
    q_t = q_ref[0, rows, :].astype(F32).T
    slope_rows = lax.broadcasted_iota(jnp.int32, (8, blk), 0) < 2

    qa = []
    for h in heads:
        qh_t = q_t[cols[h], :]
        slope = sl_ref[0, :, h * HEAD_DIM:h * HEAD_DIM + 1]
        km_hi, km_lo = _split(km_ref[:, cols[h]])
        qh_mxu = qh_t.astype(MXU_DTYPE)
        gate = (_dot(km_hi, qh_mxu) + _dot(km_lo, qh_mxu))[0:nb_pad, :]
        g = jnp.where(past, gate, NEG_INF)
        g = jnp.where(n_iota < nb, g, -jnp.inf)
        sel = jnp.zeros((nb_pad, blk), F32)
        for _ in range(k_sel):
            mx = jnp.max(g, axis=0, keepdims=True)
            idx = jnp.min(jnp.where(g == mx, n_iota, nb_pad), axis=0, keepdims=True)
            pick = n_iota == idx
            sel = jnp.where(pick, 1.0, sel)
            g = jnp.where(pick, -jnp.inf, g)
        keep = jnp.logical_or(jnp.logical_and(past, sel > 0.5), n_iota == qi)
        bias = jnp.where(keep, 0.0, NEG_INF)
        alibi = jnp.where(slope_rows, slope, 0.0)
        pad = jnp.zeros((extra - nb_pad - 8, blk), F32)
        qa.append(jnp.concatenate([qh_t * HEAD_DIM ** -0.5, bias, alibi, pad],
                                  axis=0).astype(MXU_DTYPE))

    def scores(js):
        starts = [pl.multiple_of(j * blk, blk) for j in js]
        return tuple(_dot(ka_ref[h, pl.ds(j0, blk), :], qa[h]) for j0 in starts for h in heads)

    def rest(js, flat_scores, carry, diag):
        s = [flat_scores[t * nh:(t + 1) * nh] for t in range(len(js))]
        m = list(carry[:nh]) if carry is not None else [None] * nh
        acc = list(carry[nh:]) if carry is not None else [None] * nh
        p = {}
        for h in heads:
            sh = [jnp.where(key_i <= qry_i, s[t][h], NEG_INF) if diag else s[t][h]
                  for t in range(len(js))]
            top = jnp.max(sh[0], axis=0, keepdims=True)
            for t in range(1, len(js)):
                top = jnp.maximum(top, jnp.max(sh[t], axis=0, keepdims=True))
            if m[h] is None:
                m[h] = top
            else:
                m_new = jnp.maximum(m[h], top)
                acc[h] = jnp.exp(m[h] - m_new) * acc[h]
                m[h] = m_new
            for t in range(len(js)):
                p[t, h] = jnp.exp(sh[t] - m[h]).astype(MXU_DTYPE)
        for t, j in enumerate(js):
            for h in heads:
                pv = _dot(vt_ref[j, h], p[t, h])
                acc[h] = pv if acc[h] is None else acc[h] + pv
        return tuple(m) + tuple(acc)

    def finish(carry):
        out_t = [carry[nh + h][0:HEAD_DIM, :] / carry[nh + h][HEAD_DIM:HEAD_DIM + 1, :]
                 for h in heads]
        o_ref[0, rows, :] = jnp.concatenate(out_t, axis=0).T.astype(o_ref.dtype)

    return scores, rest, None, finish


def _sb_parts(qi, rows, q_ref, k_ref, v_ref, o_ref, vt_ref, nb):
    blk = SB_BLOCK
    nh = ATT_HEADS_PER_STEP
    heads = range(nh)
    cols =[slice(h * HEAD_DIM, (h + 1) * HEAD_DIM) for h in heads]

    @pl.when(qi == 0)
    def _():
        def transpose_block(j, carry):
            j0 = pl.multiple_of(j * blk, blk)
            vt = v_ref[0, pl.ds(j0, blk), :].astype(F32).T
            for h in heads:
                vt_ref[j, h] = vt[cols[h], :].astype(MXU_DTYPE)
            return carry

        lax.fori_loop(0, nb, transpose_block, 0)

    key_i = lax.broadcasted_iota(jnp.int32, (blk, blk), 0)
    qry_i = lax.broadcasted_iota(jnp.int32, (blk, blk), 1)
    causal = key_i < qry_i
    later = jnp.where(qry_i > key_i, 1.0, 0.0).astype(MXU_DTYPE)
    qs = [(q_ref[0, rows, cols[h]].astype(F32) * HEAD_DIM ** -0.5).astype(MXU_DTYPE)
          for h in heads]

    def scores(js):
        starts = [pl.multiple_of(j * blk, blk) for j in js]
        return tuple(_dot_nt(k_ref[0, pl.ds(j0, blk), cols[h]], qs[h])
                     for j0 in starts for h in heads)

    def rest(js, flat_scores, carry, diag):
        accs, rests = carry[:nh], carry[nh:]
        z = [flat_scores[t * nh:(t + 1) * nh] for t in range(len(js))]
        log_beta, log_1mb, parts = [], [], []
        for t in range(len(js)):
            for h in heads:
                zz = z[t][h]
                lb = jnp.minimum(zz, 0.0) - jnp.log(1.0 + jnp.exp2(jnp.abs(zz) * (-LOG2E)))
                l1 = lb - zz
                if diag:
                    l1 = jnp.where(causal, l1, 0.0)
                log_beta.append(lb)
                log_1mb.append(l1)
                parts.append(l1.astype(MXU_DTYPE))
        rem = [_dot(later, p) for p in parts]
        w = []
        rests = list(rests)
        for t in range(len(js)):
            for h in heads:
                i = t * nh + h
                wh = jnp.exp(log_beta[i] + rem[i] + rests[h])
                if diag:
                    wh = jnp.where(causal, wh, 0.0)
                w.append(wh.astype(MXU_DTYPE))
                rests[h] = rests[h] + rem[i][0:1, :] + log_1mb[i][0:1, :]
        accs = list(accs)
        for t, j in enumerate(js):
            for h in heads:
                accs[h] = accs[h] + _dot(vt_ref[j, h], w[t * nh + h])
        return tuple(accs) + tuple(rests)

    start = tuple([jnp.zeros((HEAD_DIM, blk), F32)] * nh + [jnp.zeros((1, blk), F32)] * nh)

    def finish(carry):
        o_ref[0, rows, :] = jnp.concatenate(carry[:nh], axis=0).T.astype(o_ref.dtype)

    return scores, rest, start, finish


def _walk_key_blocks(qi, scores, rest, start, finish, z_ref=None):
    state = rest([qi], scores([qi]), start, True)

    def single(t, state):
        return rest([qi - 1], scores([qi - 1]), state, False)

    state = lax.fori_loop(0, qi % 2, single, state)
    first = qi - 1 - qi % 2

    if z_ref is None:
        def pair(t, state):
            j = first - 2 * t
            return rest([j, j - 1], scores([j, j - 1]), state, False)
    else:
        def look_ahead(t):
            j = jnp.maximum(first - 2 * t, 1)
            for i, z in enumerate(scores([j, j - 1])):
                z_ref[i] = z

        def pair(t, state):
            j = first - 2 * t
            z = [z_ref[i] for i in range(z_ref.shape[0])]
            look_ahead(t + 1)
            return rest([j, j - 1], z, state, False)

        look_ahead(0)

    finish(lax.fori_loop(0, qi // 2, pair, state))


def _for_query_blocks(fn):
    def one(u, carry):
        fn(pl.program_id(1) * ATT_Q_BLOCKS_PER_STEP + u,
           pl.ds(pl.multiple_of(u * MOBA_BLOCK, MOBA_BLOCK), MOBA_BLOCK))
        return carry

    lax.fori_loop(0, ATT_Q_BLOCKS_PER_STEP, one, 0)


def _moba_kernel(q_ref, k_ref, v_ref, sl_ref, o_ref, km_ref, ka_ref, vt_ref, z_ref, *, nb):
    _for_query_blocks(lambda qi, rows: _walk_key_blocks(
        qi, *_moba_parts(qi, rows, q_ref, k_ref, v_ref, sl_ref, o_ref, km_ref, ka_ref, vt_ref, nb),
        z_ref=z_ref))


def _sb_kernel(q_ref, k_ref, v_ref, o_ref, vt_ref, *, nb):
    _for_query_blocks(lambda qi, rows: _walk_key_blocks(
        qi, *_sb_parts(qi, rows, q_ref, k_ref, v_ref, o_ref, vt_ref, nb)))


def _attention_specs(s, q_col, k_col, v_col):
    rows = ATT_Q_BLOCKS_PER_STEP * MOBA_BLOCK
    return ([pl.BlockSpec((1, rows, ATT_W), lambda b, i: (b, i, q_col // ATT_W)),
             pl.BlockSpec((1, s, ATT_W), lambda b, i: (b, 0, k_col // ATT_W)),
             pl.BlockSpec((1, s, ATT_W), lambda b, i: (b, 0, v_col // ATT_W))],
            pl.BlockSpec((1, rows, ATT_W), lambda b, i: (b, i, 0)))


def _moba(proj):
    bsz, s, _ = proj.shape
    assert MOBA_HEADS == ATT_HEADS_PER_STEP
    blk = MOBA_BLOCK
    nb = s // blk
    slopes = 2.0 ** (-8.0 * np.arange(1, MOBA_HEADS + 1) / MOBA_HEADS)
    sl = jnp.asarray(np.repeat(slopes, HEAD_DIM).reshape(1, 1, ATT_W), F32)
    in_specs, out_spec = _attention_specs(s, A_MOBA_Q, A_MOBA_K, A_MOBA_V)
    return pl.pallas_call(
        functools.partial(_moba_kernel, nb=nb),
        grid=(bsz, nb // ATT_Q_BLOCKS_PER_STEP),
        in_specs=in_specs + [pl.BlockSpec((1, 1, ATT_W), lambda b, i: (0, 0, 0))],
        out_specs=out_spec,
        out_shape=jax.ShapeDtypeStruct((bsz, s, ATT_W), MXU_DTYPE),
        scratch_shapes=[pltpu.VMEM((LANES, ATT_W), F32),
                        pltpu.VMEM((ATT_HEADS_PER_STEP, s, 2 * HEAD_DIM), MXU_DTYPE),
                        pltpu.VMEM((nb, ATT_HEADS_PER_STEP, 2 * HEAD_DIM, blk), MXU_DTYPE),
                        pltpu.VMEM((2 * ATT_HEADS_PER_STEP, blk, blk), F32)],
        compiler_params=_params(("parallel", "arbitrary")),
        name="moba",
    )(proj, proj, proj, sl)


def _stick_breaking(proj):
    bsz, s, _ = proj.shape
    assert SB_HEADS == ATT_HEADS_PER_STEP and SB_BLOCK == MOBA_BLOCK
    blk = SB_BLOCK
    nb = s // blk
    in_specs, out_spec = _attention_specs(s, A_SB_Q, A_SB_K, A_SB_V)
    return pl.pallas_call(
        functools.partial(_sb_kernel, nb=nb),
        grid=(bsz, nb // ATT_Q_BLOCKS_PER_STEP),
        in_specs=in_specs,
        out_specs=out_spec,
        out_shape=jax.ShapeDtypeStruct((bsz, s, ATT_W), MXU_DTYPE),
        scratch_shapes=[pltpu.VMEM((nb, ATT_HEADS_PER_STEP, HEAD_DIM, blk), MXU_DTYPE)],
        compiler_params=_params(("parallel", "arbitrary")),
        name="stick_breaking",
    )(proj, proj, proj)


def _ret_kernel(q_ref, k_ref, v_ref, g_ref, gn_g_ref, gn_b_ref, o_ref, st_ref):
    ch = RET_CHUNK

    @pl.when(pl.program_id(1) == 0)
    def _():
        st_ref[...] = jnp.zeros_like(st_ref)

    i_col = lax.broadcasted_iota(jnp.int32, (ch, 1), 0).astype(F32)
    diff = (lax.broadcasted_iota(jnp.int32, (ch, ch), 0)
            - lax.broadcasted_iota(jnp.int32, (ch, ch), 1)).astype(F32)
    heads = range(RET_HEADS)
    log_g = [float(np.log(1.0 - 2.0 ** (-5.0 - h))) for h in heads]
    qs = [(q_ref[0, :, h * RET_DK:(h + 1) * RET_DK].astype(F32) * RET_DK ** -0.5).astype(MXU_DTYPE)
          for h in heads]
    k = [k_ref[0, :, h * RET_DK:(h + 1) * RET_DK] for h in heads]
    v = [v_ref[0, :, h * RET_DV:(h + 1) * RET_DV] for h in heads]
    prev = [st_ref[h] for h in heads]
    qk = [_dot_nt(qs[h], k[h]) for h in heads]
    o_cross = [_dot(qs[h], prev[h].astype(MXU_DTYPE)) for h in heads]
    kd = [(k[h].astype(F32) * jnp.exp((ch - 1.0 - i_col) * log_g[h])).astype(MXU_DTYPE)
          for h in heads]
    st_new = [_dot_tn(kd[h], v[h]) for h in heads]
    qkd = []
    for h in heads:
        decay = jnp.where(diff >= 0, jnp.exp(jnp.maximum(diff, 0.0) * log_g[h]), 0.0)
        qkd.append((qk[h] * decay).astype(MXU_DTYPE))
    o_intra = [_dot(qkd[h], v[h]) for h in heads]
    for h in heads:
        st_ref[h] = float(np.exp(ch * log_g[h])) * prev[h] + st_new[h]
        o = o_intra[h] + o_cross[h] * jnp.exp((i_col + 1.0) * log_g[h])
        mu = jnp.mean(o, axis=-1, keepdims=True)
        d = o - mu
        var = jnp.mean(d * d, axis=-1, keepdims=True)
        cs = slice(h * RET_DV, (h + 1) * RET_DV)
        on = d * lax.rsqrt(var + NORM_EPS) * gn_g_ref[:, cs] + gn_b_ref[:, cs]
        o_ref[0, :, cs] = (_silu(g_ref[0, :, cs].astype(F32)) * on).astype(o_ref.dtype)


def _ssd_kernel(z_ref, xbc_ref, dt_ref, cw_ref, cb_ref, dtb_ref, alog_ref, dsk_ref, ng_ref,
                o_ref, tail_ref, st_ref, y_ref):
    ch = SSD_CHUNK
    heads_per_group = SSD_HEADS // SSD_GROUPS
    halo = SSD_HALO

    @pl.when(pl.program_id(1) == 0)
    def _():
        tail_ref[0:halo, :] = jnp.zeros((halo, SSD_XBC), F32)
        st_ref[...] = jnp.zeros_like(st_ref)

    tail_ref[halo:, :] = xbc_ref[0].astype(F32)
    conv = cb_ref[...]
    for kk in range(SSD_CONV):
        off = halo - (SSD_CONV - 1) + kk
        conv = conv + cw_ref[kk:kk + 1, :] * tail_ref[off:off + ch, :]
    tail_ref[0:halo, :] = tail_ref[ch:ch + halo, :]
    xc = _silu(conv)
    xs = xc[:, :SSD_D_INNER]
    bm = xc[:, SSD_D_INNER:SSD_D_INNER + SSD_GROUPS * SSD_STATE]
    cm = xc[:, SSD_D_INNER + SSD_GROUPS * SSD_STATE:]

    dtr = dt_ref[0] + dtb_ref[...]
    dt = jnp.maximum(dtr, 0.0) + jnp.log1p(jnp.exp(-jnp.abs(dtr)))
    a = -jnp.exp(alog_ref[...])
    r_iota = lax.broadcasted_iota(jnp.int32, (ch, ch), 0)
    c_iota = lax.broadcasted_iota(jnp.int32, (ch, ch), 1)
    causal = r_iota >= c_iota
    tri = jnp.where(causal, 1.0, 0.0).astype(MXU_DTYPE)
    da = dt * a
    da_hi, da_lo = _split(da)
    cum = _dot(tri, da_hi) + _dot(tri, da_lo)
    cum_t = cum.T
    dt_t = dt.T
    cum_last = cum[ch - 1:ch, :]
    decay_st = jnp.exp(cum_last - cum) * dt
    e_cum = jnp.exp(cum)
    e_last = jnp.exp(cum_last)

    for g in range(SSD_GROUPS):
        bm_g = bm[:, g * SSD_STATE:(g + 1) * SSD_STATE].astype(MXU_DTYPE)
        cm_g = cm[:, g * SSD_STATE:(g + 1) * SSD_STATE].astype(MXU_DTYPE)
        cb = _dot_nt(cm_g, bm_g)
        for r in range(heads_per_group):
            hd = g * heads_per_group + r
            cs = slice(hd * SSD_HEAD_DIM, (hd + 1) * SSD_HEAD_DIM)
            x_h = xs[:, cs]
            seg = cum[:, hd:hd + 1] - cum_t[hd:hd + 1, :]
            lmat = jnp.exp(jnp.where(causal, seg, NEG_INF))
            w = cb * lmat * dt_t[hd:hd + 1, :]
            y = _dot(w.astype(MXU_DTYPE), x_h.astype(MXU_DTYPE))
            prev = st_ref[hd]
            y = y + _dot(cm_g, prev.astype(MXU_DTYPE)) * e_cum[:, hd:hd + 1]
            xd = (x_h * decay_st[:, hd:hd + 1]).astype(MXU_DTYPE)
            st_ref[hd] = e_last[:, hd:hd + 1] * prev + _dot_tn(bm_g, xd)
            y_ref[:, cs] = y + dsk_ref[:, cs] * x_h

    y = y_ref[...] * _silu(z_ref[0].astype(F32))
    gw = SSD_D_INNER // SSD_GROUPS
    for g in range(SSD_GROUPS):
        cs = slice(g * gw, (g + 1) * gw)
        yg = y[:, cs]
        ms = jnp.mean(yg * yg, axis=-1, keepdims=True)
        o_ref[0, :, cs] = (yg * lax.rsqrt(ms + LN_EPS) * ng_ref[:, cs]).astype(o_ref.dtype)


def _scan_kernel(rq_ref, rk_ref, rv_ref, rg_ref, gn_g_ref, gn_b_ref,
                 z_ref, xbc_ref, dt_ref, cw_ref, cb_ref, dtb_ref, alog_ref, dsk_ref, ng_ref,
                 o_ret_ref, o_ssd_ref, ret_st_ref, tail_ref, ssd_st_ref, y_ref):
    _ret_kernel(rq_ref, rk_ref, rv_ref, rg_ref, gn_g_ref, gn_b_ref, o_ret_ref, ret_st_ref)
    _ssd_kernel(z_ref, xbc_ref, dt_ref, cw_ref, cb_ref, dtb_ref, alog_ref, dsk_ref, ng_ref,
                o_ssd_ref, tail_ref, ssd_st_ref, y_ref)


def _scans(proj, proj_dt, gn_g, gn_b, conv_w, conv_b, dt_bias, a_log, d_skip, norm_g):
    bsz, s, _ = proj.shape
    assert RET_CHUNK == SSD_CHUNK
    ch = SSD_CHUNK
    wq, wv = RET_HEADS * RET_DK, RET_HEADS * RET_DV
    pad = LANES - SSD_HEADS
    dtb = jnp.pad(dt_bias, (0, pad)).reshape(1, LANES)
    alog = jnp.pad(a_log, (0, pad)).reshape(1, LANES)
    dsk = jnp.repeat(d_skip, SSD_HEAD_DIM).reshape(1, SSD_D_INNER)
    const = lambda b, i: (0, 0)
    return pl.pallas_call(
        _scan_kernel,
        grid=(bsz, s // ch),
        in_specs=[pl.BlockSpec((1, ch, wq), lambda b, i: (b, i, A_RET_Q // wq)),
                  pl.BlockSpec((1, ch, wq), lambda b, i: (b, i, A_RET_K // wq)),
                  pl.BlockSpec((1, ch, wv), lambda b, i: (b, i, A_RET_V // wv)),
                  pl.BlockSpec((1, ch, wv), lambda b, i: (b, i, B_RET_G // wv)),
                  pl.BlockSpec((1, wv), const),
                  pl.BlockSpec((1, wv), const),
                  pl.BlockSpec((1, ch, SSD_D_INNER), lambda b, i: (b, i, B_SSD_Z // SSD_D_INNER)),
                  pl.BlockSpec((1, ch, SSD_XBC), lambda b, i: (b, i, B_XBC // SSD_XBC)),
                  pl.BlockSpec((1, ch, LANES), lambda b, i: (b, i, 0)),
                  pl.BlockSpec((SSD_CONV, SSD_XBC), const),
                  pl.BlockSpec((1, SSD_XBC), const),
                  pl.BlockSpec((1, LANES), const),
                  pl.BlockSpec((1, LANES), const),
                  pl.BlockSpec((1, SSD_D_INNER), const),
                  pl.BlockSpec((1, SSD_D_INNER), const)],
        out_specs=[pl.BlockSpec((1, ch, wv), lambda b, i: (b, i, 0)),
                   pl.BlockSpec((1, ch, SSD_D_INNER), lambda b, i: (b, i, 0))],
        out_shape=[jax.ShapeDtypeStruct((bsz, s, wv), MXU_DTYPE),
                   jax.ShapeDtypeStruct((bsz, s, SSD_D_INNER), MXU_DTYPE)],
        scratch_shapes=[pltpu.VMEM((RET_HEADS, RET_DK, RET_DV), F32),
                        pltpu.VMEM((SSD_HALO + ch, SSD_XBC), F32),
                        pltpu.VMEM((SSD_HEADS, SSD_STATE, SSD_HEAD_DIM), F32),
                        pltpu.VMEM((ch, SSD_D_INNER), F32)],
        compiler_params=_params(("parallel", "arbitrary")),
        name="retention_ssd",
    )(proj, proj, proj, proj, gn_g.reshape(1, wv), gn_b.reshape(1, wv),
      proj, proj, proj_dt, conv_w, conv_b.reshape(1, SSD_XBC), dtb, alog, dsk,
      norm_g.reshape(1, SSD_D_INNER))


def _merge_kernel(oa_ref, ob_ref, oc_ref, od_ref, gates_ref, x_ref, g1_ref, wbr_ref, wout_ref,
                  lg_ref, lb_ref, o_ref):
    merged = None
    for n, br in enumerate((oa_ref, ob_ref, oc_ref, od_ref)):
        y = _dot(br[0], wbr_ref[n])
        t = _sigmoid(gates_ref[0, :, n * D_MODEL:(n + 1) * D_MODEL].astype(F32)) * y
        merged = t if merged is None else merged + t
    mix = _dot(merged.astype(MXU_DTYPE), wout_ref[...])
    o_ref[0] = _resid_ln(x_ref[0], g1_ref[0], mix, lg_ref[...], lb_ref[...])


def _merge(o_a, o_b, o_c, o_d, proj, x, g1, w_br, w_out, ln_g, ln_b, tm=512):
    bsz, s, d = x.shape
    tm = min(tm, s)
    gw = N_BRANCH * d
    row = lambda b, i: (b, i, 0)
    const2 = lambda b, i: (0, 0)
    br_spec = pl.BlockSpec((1, tm, BRANCH_W), row)
    return pl.pallas_call(
        _merge_kernel,
        grid=(bsz, s // tm),
        in_specs=[br_spec, br_spec, br_spec, br_spec,
                  pl.BlockSpec((1, tm, gw), lambda b, i: (b, i, B_GATES // gw)),
                  pl.BlockSpec((1, tm, d), row),
                  pl.BlockSpec((1, 1, d), lambda b, i: (b, 0, 0)),
                  pl.BlockSpec((N_BRANCH, BRANCH_W, d), lambda b, i: (0, 0, 0)),
                  pl.BlockSpec((d, d), const2),
                  pl.BlockSpec((1, d), const2),
                  pl.BlockSpec((1, d), const2)],
        out_specs=pl.BlockSpec((1, tm, d), row),
        out_shape=jax.ShapeDtypeStruct((bsz, s, d), F32),
        compiler_params=_params(("parallel", "parallel")),
        name="merge",
    )(o_a, o_b, o_c, o_d, proj, x, g1, w_br, w_out, ln_g.reshape(1, d), ln_b.reshape(1, d))


def _ffn_kernel(x_ref, sc_ref, sh_ref, g2_ref, wg_ref, wu_ref, wd_ref, lg_ref, lb_ref,
                o_ref, h_ref, acc_ref):
    j = pl.program_id(2)

    @pl.when(j == 0)
    def _():
        h = x_ref[0] * (1.0 + sc_ref[0]) + sh_ref[0]
        h_ref[...] = h.astype(MXU_DTYPE)
        acc_ref[...] = jnp.zeros_like(acc_ref)

    h = h_ref[...]
    act = _silu(_dot(h, wg_ref[...])) * _dot(h, wu_ref[...])
    acc_ref[...] += _dot(act.astype(MXU_DTYPE), wd_ref[...])

    @pl.when(j == pl.num_programs(2) - 1)
    def _():
        o_ref[0] = _resid_ln(x_ref[0], g2_ref[0], acc_ref[...], lg_ref[...], lb_ref[...])


def _ffn(x, sc, sh, g2, w_gu, w_down, ln_g, ln_b, tm=512, tf=1408):
    bsz, s, d = x.shape
    tm = min(tm, s)
    nf = D_FF // tf
    row = lambda b, i, j: (b, i, 0)
    vec = lambda b, i, j: (b, 0, 0)
    const2 = lambda b, i, j: (0, 0)
    return pl.pallas_call(
        _ffn_kernel,
        grid=(bsz, s // tm, nf),
        in_specs=[pl.BlockSpec((1, tm, d), row),
                  pl.BlockSpec((1, 1, d), vec), pl.BlockSpec((1, 1, d), vec),
                  pl.BlockSpec((1, 1, d), vec),
                  pl.BlockSpec((d, tf), lambda b, i, j: (0, j)),
                  pl.BlockSpec((d, tf), lambda b, i, j: (0, nf + j)),
                  pl.BlockSpec((tf, d), lambda b, i, j: (j, 0)),
                  pl.BlockSpec((1, d), const2), pl.BlockSpec((1, d), const2)],
        out_specs=pl.BlockSpec((1, tm, d), row),
        out_shape=jax.ShapeDtypeStruct((bsz, s, d), F32),
        scratch_shapes=[pltpu.VMEM((tm, d), MXU_DTYPE), pltpu.VMEM((tm, d), F32)],
        compiler_params=_params(("parallel", "parallel", "arbitrary")),
        name="ffn_dense",
    )(x, sc, sh, g2, w_gu, w_gu, w_down, ln_g.reshape(1, d), ln_b.reshape(1, d))


def _router_kernel(x_ref, sc_ref, sh_ref, rw_ref, rb_ref,
                   h_ref, comb_ref, slot_ref, slot_t_ref, cnt_ref):
    tm = x_ref.shape[1]
    lane = lax.broadcasted_iota(jnp.int32, (tm, LANES), 1)
    h = x_ref[0] * (1.0 + sc_ref[0]) + sh_ref[0]
    h_ref[0] = h.astype(MXU_DTYPE)
    logits = _dot3(h, rw_ref[...]) + rb_ref[...]
    logits = jnp.where(lane < N_EXPERTS, logits, -jnp.inf)
    v1 = jnp.max(logits, axis=1, keepdims=True)
    i1 = jnp.min(jnp.where(logits == v1, lane, LANES), axis=1, keepdims=True)
    rest = jnp.where(lane == i1, -jnp.inf, logits)
    v2 = jnp.max(rest, axis=1, keepdims=True)
    i2 = jnp.min(jnp.where(rest == v2, lane, LANES), axis=1, keepdims=True)
    e2 = jnp.exp(v2 - v1)
    w1 = 1.0 / (1.0 + e2)
    w2 = e2 / (1.0 + e2)
    comb_ref[0] = jnp.where(lane == i1, w1, 0.0) + jnp.where(lane == i2, w2, 0.0)
    routed = jnp.logical_or(lane == i1, lane == i2)
    earlier = (lax.broadcasted_iota(jnp.int32, (tm, tm), 1)
               < lax.broadcasted_iota(jnp.int32, (tm, tm), 0))
    before = _dot(jnp.where(earlier, 1.0, 0.0).astype(MXU_DTYPE),
                  jnp.where(routed, 1.0, 0.0).astype(MXU_DTYPE))
    slot = jnp.where(routed, before, -1.0)
    slot_ref[0] = slot
    slot_t_ref[0] = slot.T[0:N_EXPERTS, :]
    cnt_ref[0] = jnp.sum(jnp.where(routed, 1, 0), axis=0, keepdims=True)


def _moe_kernel(cnt_ref, h_ref, x_ref, g2_ref, comb_ref, slot_ref, slot_t_ref,
                wg_ref, wu_ref, wd_ref, lg_ref, lb_ref, o_ref, xe_ref, ye_ref, *, n_tiles):
    tm = x_ref.shape[1]
    e = pl.program_id(2)
    j = pl.program_id(3)
    last_j = pl.num_programs(3) - 1
    count = cnt_ref[(pl.program_id(0) * n_tiles + pl.program_id(1)) * N_EXPERTS + e]
    n_full = jnp.maximum(count - MOE_TAIL_MIN - 1, 0) // MOE_ROWS
    tail0 = pl.multiple_of(n_full * MOE_ROWS, MOE_ROWS)
    tail_steps = (count - n_full * MOE_ROWS + MOE_ROW_STEP - 1) // MOE_ROW_STEP

    def for_pieces(fn):
        def full_piece(c, carry):
            fn(pl.multiple_of(c * MOE_ROWS, MOE_ROWS), MOE_ROWS)
            return carry

        lax.fori_loop(0, n_full, full_piece, 0)
        for steps in range(1, (MOE_ROWS + MOE_TAIL_MIN) // MOE_ROW_STEP + 1):
            @pl.when(tail_steps == steps)
            def _(steps=steps):
                fn(tail0, steps * MOE_ROW_STEP)

    @pl.when(jnp.logical_and(e == 0, j == 0))
    def _():
        o_ref[0] = jnp.zeros((tm, D_MODEL), F32)

    @pl.when(j == 0)
    def _():
        slot_row = slot_t_ref[0, pl.ds(e, 1), :]

        def gather(r0, rows):
            slots = (lax.broadcasted_iota(jnp.int32, (rows, tm), 0) + r0).astype(F32)
            pick = jnp.where(slot_row == slots, 1.0, 0.0).astype(MXU_DTYPE)
            xe_ref[pl.ds(r0, rows), :] = _dot(pick, h_ref[0]).astype(MXU_DTYPE)
            ye_ref[pl.ds(r0, rows), :] = jnp.zeros((rows, D_MODEL), F32)

        for_pieces(gather)

    def expert_rows(r0, rows):
        xc = xe_ref[pl.ds(r0, rows), :]
        act = _silu(_dot(xc, wg_ref[0])) * _dot(xc, wu_ref[0])
        ye_ref[pl.ds(r0, rows), :] += _dot(act.astype(MXU_DTYPE), wd_ref[0])

    for_pieces(expert_rows)

    @pl.when(j == last_j)
    def _():
        lane = lax.broadcasted_iota(jnp.int32, (tm, LANES), 1)
        mine = lane == e
        slot_col = jnp.sum(jnp.where(mine, slot_ref[0], 0.0), axis=1, keepdims=True)
        comb_col = jnp.sum(jnp.where(mine, comb_ref[0], 0.0), axis=1, keepdims=True)

        def scatter(r0, rows):
            slots = (lax.broadcasted_iota(jnp.int32, (tm, rows), 1) + r0).astype(F32)
            place = jnp.where(slot_col == slots, 1.0, 0.0).astype(MXU_DTYPE)
            o_ref[0] += comb_col * _dot(place, ye_ref[pl.ds(r0, rows), :].astype(MXU_DTYPE))

        for_pieces(scatter)

    @pl.when(jnp.logical_and(e == pl.num_programs(2) - 1, j == last_j))
    def _():
        o_ref[0] = _resid_ln(x_ref[0], g2_ref[0], o_ref[0], lg_ref[...], lb_ref[...])


def _moe(x, sc, sh, g2, router_w, router_b, w_gu, w_down, ln_g, ln_b, tm=1024, tf=1792):
    bsz, s, d = x.shape
    tm = min(tm, s)
    nt = s // tm
    nf = D_FF_EXPERT // tf
    rw = jnp.pad(router_w, ((0, 0), (0, LANES - N_EXPERTS)))
    rb = jnp.pad(router_b, (0, LANES - N_EXPERTS)).reshape(1, LANES)

    row2 = lambda b, i: (b, i, 0)
    vec2 = lambda b, i: (b, 0, 0)
    h, comb, slot, slot_t, cnt = pl.pallas_call(
        _router_kernel,
        grid=(bsz, nt),
        in_specs=[pl.BlockSpec((1, tm, d), row2),
                  pl.BlockSpec((1, 1, d), vec2), pl.BlockSpec((1, 1, d), vec2),
                  pl.BlockSpec((d, LANES), lambda b, i: (0, 0)),
                  pl.BlockSpec((1, LANES), lambda b, i: (0, 0))],
        out_specs=[pl.BlockSpec((1, tm, d), row2),
                   pl.BlockSpec((1, tm, LANES), row2),
                   pl.BlockSpec((1, tm, LANES), row2),
                   pl.BlockSpec((1, N_EXPERTS, tm), lambda b, i: (b * nt + i, 0, 0)),
                   pl.BlockSpec((1, 1, LANES), lambda b, i: (b * nt + i, 0, 0))],
        out_shape=[jax.ShapeDtypeStruct((bsz, s, d), MXU_DTYPE),
                   jax.ShapeDtypeStruct((bsz, s, LANES), F32),
                   jax.ShapeDtypeStruct((bsz, s, LANES), F32),
                   jax.ShapeDtypeStruct((bsz * nt, N_EXPERTS, tm), F32),
                   jax.ShapeDtypeStruct((bsz * nt, 1, LANES), jnp.int32)],
        compiler_params=_params(("parallel", "parallel")),
        name="moe_router",
    )(x, sc, sh, rw, rb)
    counts = cnt[:, 0, :N_EXPERTS].reshape(bsz * nt * N_EXPERTS)

    row = lambda b, i, e, j, cnt: (b, i, 0)
    vec = lambda b, i, e, j, cnt: (b, 0, 0)
    const2 = lambda b, i, e, j, cnt: (0, 0)
    once = pl.Buffered(1)
    grid_spec = pltpu.PrefetchScalarGridSpec(
        num_scalar_prefetch=1,
        grid=(bsz, nt, N_EXPERTS, nf),
        in_specs=[pl.BlockSpec((1, tm, d), row, pipeline_mode=once),
                  pl.BlockSpec((1, tm, d), row, pipeline_mode=once),
                  pl.BlockSpec((1, 1, d), vec),
                  pl.BlockSpec((1, tm, LANES), row, pipeline_mode=once),
                  pl.BlockSpec((1, tm, LANES), row, pipeline_mode=once),
                  pl.BlockSpec((1, N_EXPERTS, tm), lambda b, i, e, j, cnt: (b * nt + i, 0, 0)),
                  pl.BlockSpec((1, d, tf), lambda b, i, e, j, cnt: (e, 0, j)),
                  pl.BlockSpec((1, d, tf), lambda b, i, e, j, cnt: (e, 0, nf + j)),
                  pl.BlockSpec((1, tf, d), lambda b, i, e, j, cnt: (e, j, 0)),
                  pl.BlockSpec((1, d), const2), pl.BlockSpec((1, d), const2)],
        out_specs=pl.BlockSpec((1, tm, d), row),
        scratch_shapes=[pltpu.VMEM((tm, d), MXU_DTYPE), pltpu.VMEM((tm, d), F32)])
    return pl.pallas_call(
        functools.partial(_moe_kernel, n_tiles=nt),
        grid_spec=grid_spec,
        out_shape=jax.ShapeDtypeStruct((bsz, s, d), F32),
        compiler_params=pltpu.CompilerParams(
            dimension_semantics=("parallel", "parallel", "arbitrary", "arbitrary"),
            vmem_limit_bytes=MOE_VMEM_LIMIT),
        name="moe_experts",
    )(counts, h, x, g2, comb, slot, slot_t, w_gu, w_gu, w_down,
      ln_g.reshape(1, d), ln_b.reshape(1, d))


def _split_w_in(w):
    p = np.cumsum([0, 512, 512, 512, 256, 256, 512, 512, 512, SSD_XBC, SSD_HEADS, 512, 512, 512,
                   N_BRANCH * D_MODEL])
    mq, mk, mv, rq, rk, rv, rg, sz, sxbc, sdt, bq, bk, bv, gates = (
        w[:, p[i]:p[i + 1]] for i in range(14))
    w_main = jnp.concatenate([mq, mk, mv, rq, rk, rv, bq, bk, bv, gates, rg, sz, sxbc], axis=1)
    w_dt = jnp.pad(sdt, ((0, 0), (0, LANES - SSD_HEADS)))
    assert w_main.shape[1] == PROJ_WIDTH
    return w_main.astype(MXU_DTYPE), w_dt.astype(MXU_DTYPE)


def kernel(x, c, w_ada, b_ada, w_in, conv_w, conv_b, dt_bias, a_log, d_skip, ssm_norm_g, ret_gn_g, ret_gn_b, w_br, w_out, ln1_g, ln1_b, ln2_g, ln2_b, ffn_w_gu, ffn_w_down, router_w, router_b, expert_w_gu, expert_w_down):
    bsz = x.shape[0]
    mod = _ada(c, w_ada, b_ada)
    for l in range(DEPTH):
        sh1, sc1, g1, sh2, sc2, g2 = (
            mod[l, :, i * D_MODEL:(i + 1) * D_MODEL].reshape(bsz, 1, D_MODEL) for i in range(6))
        w_main, w_dt = _split_w_in(w_in[l])
        proj, proj_dt = _inproj(x, sc1, sh1, w_main, w_dt, tn=2560)
        o_a = _moba(proj)
        o_d = _stick_breaking(proj)
        o_b, o_c = _scans(proj, proj_dt, ret_gn_g[l], ret_gn_b[l], conv_w[l], conv_b[l],
                          dt_bias[l], a_log[l], d_skip[l], ssm_norm_g[l])
        x = _merge(o_a, o_b, o_c, o_d, proj, x, g1, w_br[l].astype(MXU_DTYPE),
                   w_out[l].astype(MXU_DTYPE), ln1_g[l], ln1_b[l])
        if l % 2 == 0:
            x = _ffn(x, sc2, sh2, g2, ffn_w_gu[l // 2].astype(MXU_DTYPE),
                     ffn_w_down[l // 2].astype(MXU_DTYPE), ln2_g[l], ln2_b[l])
        else:
            x = _moe(x, sc2, sh2, g2, router_w[l // 2], router_b[l // 2],
                     expert_w_gu[l // 2].astype(MXU_DTYPE), expert_w_down[l // 2].astype(MXU_DTYPE),
                     ln2_g[l], ln2_b[l])
    return x
```

```python
import functools

import numpy as np
import jax
import jax.numpy as jnp
from jax import lax
from jax.experimental import pallas as pl
from jax.experimental.pallas import tpu as pltpu

F32 = jnp.float32
MXU_DTYPE = jnp.bfloat16

D_MODEL = 1024
DEPTH = 2
MOBA_HEADS = 8
MOBA_BLOCK = 256
MOBA_TOPK = 3
RET_HEADS = 4
RET_DK = 64
RET_DV = 128
RET_CHUNK = 256
SSD_D_INNER = 512
SSD_HEAD_DIM = 64
SSD_HEADS = 8
SSD_GROUPS = 2
SSD_STATE = 128
SSD_CONV = 4
SSD_CHUNK = 256
SSD_XBC = SSD_D_INNER + 2 * SSD_GROUPS * SSD_STATE
SSD_HALO = 8
SB_HEADS = 8
SB_BLOCK = 256
HEAD_DIM = 64
ATT_HEADS_PER_STEP = 8
ATT_W = ATT_HEADS_PER_STEP * HEAD_DIM
ATT_Q_BLOCKS_PER_STEP = 2
BRANCH_W = 512
N_BRANCH = 4
D_FF = 2816
N_EXPERTS = 8
D_FF_EXPERT = 3584
DEEPNORM_ALPHA = (2.0 * DEPTH) ** 0.25
LN_EPS = 1e-5
NORM_EPS = 1e-6
NEG_INF = -1e30
LOG2E = 1.4426950408889634
LANES = 128
VMEM_LIMIT = 48 * 2 ** 20
MOE_VMEM_LIMIT = 56 * 2 ** 20
MOE_ROWS = 256
MOE_ROW_STEP = 64
MOE_TAIL_MIN = 128

A_MOBA_Q, A_MOBA_K, A_MOBA_V = 0, 512, 1024
A_RET_Q, A_RET_K, A_RET_V = 1536, 1792, 2048
A_SB_Q, A_SB_K, A_SB_V = 2560, 3072, 3584
B_GATES, B_RET_G, B_SSD_Z, B_XBC = 4096, 8192, 8704, 9216
PROJ_WIDTH = 10240


def _params(sem):
    return pltpu.CompilerParams(dimension_semantics=sem, vmem_limit_bytes=VMEM_LIMIT)


def _dot(a, b):
    return jnp.dot(a, b, preferred_element_type=F32)


def _dot_nt(a, b):
    return lax.dot_general(a, b, (((1,), (1,)), ((), ())), preferred_element_type=F32)


def _dot_tn(a, b):
    return lax.dot_general(a, b, (((0,), (0,)), ((), ())), preferred_element_type=F32)


def _split(x):
    hi = x.astype(MXU_DTYPE)
    lo = (x - hi.astype(F32)).astype(MXU_DTYPE)
    return hi, lo


def _dot3(a, b):
    ah, al = _split(a)
    bh, bl = _split(b)
    return _dot(ah, bh) + _dot(al, bh) + _dot(ah, bl)


def _sigmoid(x):
    return 1.0 / (1.0 + jnp.exp(-x))


def _silu(x):
    return x * _sigmoid(x)


def _resid_ln(x, gate, f, ln_g, ln_b):
    y = DEEPNORM_ALPHA * x + gate * f
    mu = jnp.mean(y, axis=-1, keepdims=True)
    d = y - mu
    var = jnp.mean(d * d, axis=-1, keepdims=True)
    return d * lax.rsqrt(var + LN_EPS) * ln_g + ln_b


def _ada_kernel(c_ref, w_ref, b_ref, o_ref):
    c = c_ref[...]
    o_ref[0] = _dot3(_silu(c), w_ref[0]) + b_ref[0]


def _ada(c, w_ada, b_ada):
    depth, d, n = w_ada.shape
    bsz = c.shape[0]
    tn = 1536
    return pl.pallas_call(
        _ada_kernel,
        grid=(depth, n // tn),
        in_specs=[pl.BlockSpec((bsz, d), lambda l, j: (0, 0)),
                  pl.BlockSpec((1, d, tn), lambda l, j: (l, 0, j)),
                  pl.BlockSpec((1, 1, tn), lambda l, j: (l, 0, j))],
        out_specs=pl.BlockSpec((1, bsz, tn), lambda l, j: (l, 0, j)),
        out_shape=jax.ShapeDtypeStruct((depth, bsz, n), F32),
        compiler_params=_params(("arbitrary", "arbitrary")),
        name="ada_mod",
    )(c, w_ada, b_ada.reshape(depth, 1, n))


def _inproj_kernel(x_ref, sc_ref, sh_ref, w_ref, wdt_ref, o_ref, odt_ref, h_ref):
    @pl.when(pl.program_id(2) == 0)
    def _():
        h = x_ref[0] * (1.0 + sc_ref[0]) + sh_ref[0]
        h_ref[...] = h.astype(MXU_DTYPE)
        odt_ref[0] = _dot(h_ref[...], wdt_ref[...])

    o_ref[0] = _dot(h_ref[...], w_ref[...]).astype(o_ref.dtype)


def _inproj(x, sc, sh, w, w_dt, tn, tm=1024):
    bsz, s, d = x.shape
    n = w.shape[1]
    n_dt = w_dt.shape[1]
    tm = min(tm, s)
    return pl.pallas_call(
        _inproj_kernel,
        grid=(bsz, s // tm, n // tn),
        in_specs=[pl.BlockSpec((1, tm, d), lambda b, i, j: (b, i, 0)),
                  pl.BlockSpec((1, 1, d), lambda b, i, j: (b, 0, 0)),
                  pl.BlockSpec((1, 1, d), lambda b, i, j: (b, 0, 0)),
                  pl.BlockSpec((d, tn), lambda b, i, j: (0, j)),
                  pl.BlockSpec((d, n_dt), lambda b, i, j: (0, 0))],
        out_specs=[pl.BlockSpec((1, tm, tn), lambda b, i, j: (b, i, j)),
                   pl.BlockSpec((1, tm, n_dt), lambda b, i, j: (b, i, 0))],
        out_shape=[jax.ShapeDtypeStruct((bsz, s, n), MXU_DTYPE),
                   jax.ShapeDtypeStruct((bsz, s, n_dt), F32)],
        scratch_shapes=[pltpu.VMEM((tm, d), MXU_DTYPE)],
        compiler_params=_params(("parallel", "parallel", "arbitrary")),
        name="in_proj",
    )(x, sc, sh, w, w_dt)


def _moba_parts(qi, rows, q_ref, k_ref, v_ref, sl_ref, o_ref, km_ref, ka_ref, vt_ref, nb):
    blk = MOBA_BLOCK
    nh = ATT_HEADS_PER_STEP
    heads = range(nh)
    s_len = nb * blk
    nb_pad = -(-nb // 8) * 8
    extra = HEAD_DIM
    assert nb_pad + 8 <= extra
    cols = [slice(h * HEAD_DIM, (h + 1) * HEAD_DIM) for h in heads]

    @pl.when(qi == 0)
    def _():
        k_all = k_ref[0]
        kk = k_all.astype(F32).reshape(nb, blk, ATT_W)
        km_ref[...] = jnp.zeros_like(km_ref)
        km_ref[0:nb, :] = jnp.sum(kk, axis=1) * (1.0 / blk)
        pos = lax.broadcasted_iota(jnp.int32, (s_len, extra), 0)
        lane = lax.broadcasted_iota(jnp.int32, (s_len, extra), 1)
        one_hot = jnp.where(lane == pos // blk, 1.0, 0.0)
        pos_hi = jnp.where(lane == nb_pad, (pos // blk * blk).astype(F32), 0.0)
        pos_lo = jnp.where(lane == nb_pad + 1, (pos % blk).astype(F32), 0.0)
        tail = (one_hot + pos_hi + pos_lo).astype(MXU_DTYPE)
        for h in heads:
            ka_ref[h] = jnp.concatenate([k_all[:, cols[h]], tail], axis=1)

        ones_rows = jnp.where(lax.broadcasted_iota(jnp.int32, (extra, blk), 0) == 0, 1.0, 0.0)

        def transpose_block(j, carry):
            j0 = pl.multiple_of(j * blk, blk)
            vt = v_ref[0, pl.ds(j0, blk), :].astype(F32).T
            for h in heads:
                vt_ref[j, h] = jnp.concatenate([vt[cols[h], :], ones_rows], axis=0).astype(MXU_DTYPE)
            return carry

        lax.fori_loop(0, nb, transpose_block, 0)

    key_i = lax.broadcasted_iota(jnp.int32, (blk, blk), 0)
    qry_i = lax.broadcasted_iota(jnp.int32, (blk, blk), 1)
    n_iota = lax.broadcasted_iota(jnp.int32, (nb_pad, blk), 0)
    past = n_iota < qi
    k_sel = min(MOBA_TOPK, max(nb - 1, 1))
    q_t = q_ref[0, rows, :].astype(F32).T
    slope_rows = lax.broadcasted_iota(jnp.int32, (8, blk), 0) < 2

    qa = []
    for h in heads:
        qh_t = q_t[cols[h], :]
        slope = sl_ref[0, :, h * HEAD_DIM:h * HEAD_DIM + 1]
        km_hi, km_lo = _split(km_ref[:, cols[h]])
        qh_mxu = qh_t.astype(MXU_DTYPE)
        gate = (_dot(km_hi, qh_mxu) + _dot(km_lo, qh_mxu))[0:nb_pad, :]
        g = jnp.where(past, gate, NEG_INF)
        g = jnp.where(n_iota < nb, g, -jnp.inf)
        sel = jnp.zeros((nb_pad, blk), F32)
        for _ in range(k_sel):
            mx = jnp.max(g, axis=0, keepdims=True)
            idx = jnp.min(jnp.where(g == mx, n_iota, nb_pad), axis=0, keepdims=True)
            pick = n_iota == idx
            sel = jnp.where(pick, 1.0, sel)
            g = jnp.where(pick, -jnp.inf, g)
        keep = jnp.logical_or(jnp.logical_and(past, sel > 0.5), n_iota == qi)
        bias = jnp.where(keep, 0.0, NEG_INF)
        alibi = jnp.where(slope_rows, slope, 0.0)
        pad = jnp.zeros((extra - nb_pad - 8, blk), F32)
        qa.append(jnp.concatenate([qh_t * HEAD_DIM ** -0.5, bias, alibi, pad],
                                  axis=0).astype(MXU_DTYPE))

    def scores(js):
        starts = [pl.multiple_of(j * blk, blk) for j in js]
        return tuple(_dot(ka_ref[h, pl.ds(j0, blk), :], qa[h]) for j0 in starts for h in heads)

    def rest(js, flat_scores, carry, diag):
        s = [flat_scores[t * nh:(t + 1) * nh] for t in range(len(js))]
        m = list(carry[:nh]) if carry is not None else [None] * nh
        acc = list(carry[nh:]) if carry is not None else [None] * nh
        p = {}
        for h in heads:
            sh = [jnp.where(key_i <= qry_i, s[t][h], NEG_INF) if diag else s[t][h]
                  for t in range(len(js))]
            top = jnp.max(sh[0], axis=0, keepdims=True)
            for t in range(1, len(js)):
                top = jnp.maximum(top, jnp.max(sh[t], axis=0, keepdims=True))
            if m[h] is None:
                m[h] = top
            else:
                m_new = jnp.maximum(m[h], top)
                acc[h] = jnp.exp(m[h] - m_new) * acc[h]
                m[h] = m_new
            for t in range(len(js)):
                p[t, h] = jnp.exp(sh[t] - m[h]).astype(MXU_DTYPE)
        for t, j in enumerate(js):
            for h in heads:
                pv = _dot(vt_ref[j, h], p[t, h])
                acc[h] = pv if acc[h] is None else acc[h] + pv
        return tuple(m) + tuple(acc)

    def finish(carry):
        out_t = [carry[nh + h][0:HEAD_DIM, :] / carry[nh + h][HEAD_DIM:HEAD_DIM + 1, :]
                 for h in heads]
        o_ref[0, rows, :] = jnp.concatenate(out_t, axis=0).T.astype(o_ref.dtype)

    return scores, rest, None, finish


def _sb_parts(qi, rows, q_ref, k_ref, v_ref, o_ref, vt_ref, nb):
    blk = SB_BLOCK
    nh = ATT_HEADS_PER_STEP
    heads = range(nh)
    cols =[slice(h * HEAD_DIM, (h + 1) * HEAD_DIM) for h in heads]

    @pl.when(qi == 0)
    def _():
        def transpose_block(j, carry):
            j0 = pl.multiple_of(j * blk, blk)
            vt = v_ref[0, pl.ds(j0, blk), :].astype(F32).T
            for h in heads:
                vt_ref[j, h] = vt[cols[h], :].astype(MXU_DTYPE)
            return carry

        lax.fori_loop(0, nb, transpose_block, 0)

    key_i = lax.broadcasted_iota(jnp.int32, (blk, blk), 0)
    qry_i = lax.broadcasted_iota(jnp.int32, (blk, blk), 1)
    causal = key_i < qry_i
    later = jnp.where(qry_i > key_i, 1.0, 0.0).astype(MXU_DTYPE)
    qs = [(q_ref[0, rows, cols[h]].astype(F32) * HEAD_DIM ** -0.5).astype(MXU_DTYPE)
          for h in heads]

    def scores(js):
        starts = [pl.multiple_of(j * blk, blk) for j in js]
        return tuple(_dot_nt(k_ref[0, pl.ds(j0, blk), cols[h]], qs[h])
                     for j0 in starts for h in heads)

    def rest(js, flat_scores, carry, diag):
        accs, rests = carry[:nh], carry[nh:]
        z = [flat_scores[t * nh:(t + 1) * nh] for t in range(len(js))]
        log_beta, log_1mb, parts = [], [], []
        for t in range(len(js)):
            for h in heads:
                zz = z[t][h]
                lb = jnp.minimum(zz, 0.0) - jnp.log(1.0 + jnp.exp2(jnp.abs(zz) * (-LOG2E)))
                l1 = lb - zz
                if diag:
                    l1 = jnp.where(causal, l1, 0.0)
                log_beta.append(lb)
                log_1mb.append(l1)
                parts.append(l1.astype(MXU_DTYPE))
        rem = [_dot(later, p) for p in parts]
        w = []
        rests = list(rests)
        for t in range(len(js)):
            for h in heads:
                i = t * nh + h
                wh = jnp.exp(log_beta[i] + rem[i] + rests[h])
                if diag:
                    wh = jnp.where(causal, wh, 0.0)
                w.append(wh.astype(MXU_DTYPE))
                rests[h] = rests[h] + rem[i][0:1, :] + log_1mb[i][0:1, :]
        accs = list(accs)
        for t, j in enumerate(js):
            for h in heads:
                accs[h] = accs[h] + _dot(vt_ref[j, h], w[t * nh + h])
        return tuple(accs) + tuple(rests)

    start = tuple([jnp.zeros((HEAD_DIM, blk), F32)] * nh + [jnp.zeros((1, blk), F32)] * nh)

    def finish(carry):
        o_ref[0, rows, :] = jnp.concatenate(carry[:nh], axis=0).T.astype(o_ref.dtype)

    return scores, rest, start, finish


def _walk_key_blocks(qi, scores, rest, start, finish, z_ref=None):
    state = rest([qi], scores([qi]), start, True)

    def single(t, state):
        return rest([qi - 1], scores([qi - 1]), state, False)

    state = lax.fori_loop(0, qi % 2, single, state)
    first = qi - 1 - qi % 2

    if z_ref is None:
        def pair(t, state):
            j = first - 2 * t
            return rest([j, j - 1], scores([j, j - 1]), state, False)
    else:
        def look_ahead(t):
            j = jnp.maximum(first - 2 * t, 1)
            for i, z in enumerate(scores([j, j - 1])):
                z_ref[i] = z

        def pair(t, state):
            j = first - 2 * t
            z = [z_ref[i] for i in range(z_ref.shape[0])]
            look_ahead(t + 1)
            return rest([j, j - 1], z, state, False)

        look_ahead(0)

    finish(lax.fori_loop(0, qi // 2, pair, state))


def _for_query_blocks(fn):
    def one(u, carry):
        fn(pl.program_id(1) * ATT_Q_BLOCKS_PER_STEP + u,
           pl.ds(pl.multiple_of(u * MOBA_BLOCK, MOBA_BLOCK), MOBA_BLOCK))
        return carry

    lax.fori_loop(0, ATT_Q_BLOCKS_PER_STEP, one, 0)


def _moba_kernel(q_ref, k_ref, v_ref, sl_ref, o_ref, km_ref, ka_ref, vt_ref, z_ref, *, nb):
    _for_query_blocks(lambda qi, rows: _walk_key_blocks(
        qi, *_moba_parts(qi, rows, q_ref, k_ref, v_ref, sl_ref, o_ref, km_ref, ka_ref, vt_ref, nb),
        z_ref=z_ref))


def _sb_kernel(q_ref, k_ref, v_ref, o_ref, vt_ref, *, nb):
    _for_query_blocks(lambda qi, rows: _walk_key_blocks(
        qi, *_sb_parts(qi, rows, q_ref, k_ref, v_ref, o_ref, vt_ref, nb)))


def _attention_specs(s, q_col, k_col, v_col):
    rows = ATT_Q_BLOCKS_PER_STEP * MOBA_BLOCK
    return ([pl.BlockSpec((1, rows, ATT_W), lambda b, i: (b, i, q_col // ATT_W)),
             pl.BlockSpec((1, s, ATT_W), lambda b, i: (b, 0, k_col // ATT_W)),
             pl.BlockSpec((1, s, ATT_W), lambda b, i: (b, 0, v_col // ATT_W))],
            pl.BlockSpec((1, rows, ATT_W), lambda b, i: (b, i, 0)))


def _moba(proj):
    bsz, s, _ = proj.shape
    assert MOBA_HEADS == ATT_HEADS_PER_STEP
    blk = MOBA_BLOCK
    nb = s // blk
    slopes = 2.0 ** (-8.0 * np.arange(1, MOBA_HEADS + 1) / MOBA_HEADS)
    sl = jnp.asarray(np.repeat(slopes, HEAD_DIM).reshape(1, 1, ATT_W), F32)
    in_specs, out_spec = _attention_specs(s, A_MOBA_Q, A_MOBA_K, A_MOBA_V)
    return pl.pallas_call(
        functools.partial(_moba_kernel, nb=nb),
        grid=(bsz, nb // ATT_Q_BLOCKS_PER_STEP),
        in_specs=in_specs + [pl.BlockSpec((1, 1, ATT_W), lambda b, i: (0, 0, 0))],
        out_specs=out_spec,
        out_shape=jax.ShapeDtypeStruct((bsz, s, ATT_W), MXU_DTYPE),
        scratch_shapes=[pltpu.VMEM((LANES, ATT_W), F32),
                        pltpu.VMEM((ATT_HEADS_PER_STEP, s, 2 * HEAD_DIM), MXU_DTYPE),
                        pltpu.VMEM((nb, ATT_HEADS_PER_STEP, 2 * HEAD_DIM, blk), MXU_DTYPE),
                        pltpu.VMEM((2 * ATT_HEADS_PER_STEP, blk, blk), F32)],
        compiler_params=_params(("parallel", "arbitrary")),
        name="moba",
    )(proj, proj, proj, sl)


def _stick_breaking(proj):
    bsz, s, _ = proj.shape
    assert SB_HEADS == ATT_HEADS_PER_STEP and SB_BLOCK == MOBA_BLOCK
    blk = SB_BLOCK
    nb = s // blk
    in_specs, out_spec = _attention_specs(s, A_SB_Q, A_SB_K, A_SB_V)
    return pl.pallas_call(
        functools.partial(_sb_kernel, nb=nb),
        grid=(bsz, nb // ATT_Q_BLOCKS_PER_STEP),
        in_specs=in_specs,
        out_specs=out_spec,
        out_shape=jax.ShapeDtypeStruct((bsz, s, ATT_W), MXU_DTYPE),
        scratch_shapes=[pltpu.VMEM((nb, ATT_HEADS_PER_STEP, HEAD_DIM, blk), MXU_DTYPE)],
        compiler_params=_params(("parallel", "arbitrary")),
        name="stick_breaking",
    )(proj, proj, proj)


def _ret_kernel(q_ref, k_ref, v_ref, g_ref, gn_g_ref, gn_b_ref, o_ref, st_ref):
    ch = RET_CHUNK

    @pl.when(pl.program_id(1) == 0)
    def _():
        st_ref[...] = jnp.zeros_like(st_ref)

    i_col = lax.broadcasted_iota(jnp.int32, (ch, 1), 0).astype(F32)
    diff = (lax.broadcasted_iota(jnp.int32, (ch, ch), 0)
            - lax.broadcasted_iota(jnp.int32, (ch, ch), 1)).astype(F32)
    heads = range(RET_HEADS)
    log_g = [float(np.log(1.0 - 2.0 ** (-5.0 - h))) for h in heads]
    qs = [(q_ref[0, :, h * RET_DK:(h + 1) * RET_DK].astype(F32) * RET_DK ** -0.5).astype(MXU_DTYPE)
          for h in heads]
    k = [k_ref[0, :, h * RET_DK:(h + 1) * RET_DK] for h in heads]
    v = [v_ref[0, :, h * RET_DV:(h + 1) * RET_DV] for h in heads]
    prev = [st_ref[h] for h in heads]
    qk = [_dot_nt(qs[h], k[h]) for h in heads]
    o_cross = [_dot(qs[h], prev[h].astype(MXU_DTYPE)) for h in heads]
    kd = [(k[h].astype(F32) * jnp.exp((ch - 1.0 - i_col) * log_g[h])).astype(MXU_DTYPE)
          for h in heads]
    st_new = [_dot_tn(kd[h], v[h]) for h in heads]
    qkd = []
    for h in heads:
        decay = jnp.where(diff >= 0, jnp.exp(jnp.maximum(diff, 0.0) * log_g[h]), 0.0)
        qkd.append((qk[h] * decay).astype(MXU_DTYPE))
    o_intra = [_dot(qkd[h], v[h]) for h in heads]
    for h in heads:
        st_ref[h] = float(np.exp(ch * log_g[h])) * prev[h] + st_new[h]
        o = o_intra[h] + o_cross[h] * jnp.exp((i_col + 1.0) * log_g[h])
        mu = jnp.mean(o, axis=-1, keepdims=True)
        d = o - mu
        var = jnp.mean(d * d, axis=-1, keepdims=True)
        cs = slice(h * RET_DV, (h + 1) * RET_DV)
        on = d * lax.rsqrt(var + NORM_EPS) * gn_g_ref[:, cs] + gn_b_ref[:, cs]
        o_ref[0, :, cs] = (_silu(g_ref[0, :, cs].astype(F32)) * on).astype(o_ref.dtype)


def _ssd_kernel(z_ref, xbc_ref, dt_ref, cw_ref, cb_ref, dtb_ref, alog_ref, dsk_ref, ng_ref,
                o_ref, tail_ref, st_ref, y_ref):
    ch = SSD_CHUNK
    heads_per_group = SSD_HEADS // SSD_GROUPS
    halo = SSD_HALO

    @pl.when(pl.program_id(1) == 0)
    def _():
        tail_ref[0:halo, :] = jnp.zeros((halo, SSD_XBC), F32)
        st_ref[...] = jnp.zeros_like(st_ref)

    tail_ref[halo:, :] = xbc_ref[0].astype(F32)
    conv = cb_ref[...]
    for kk in range(SSD_CONV):
        off = halo - (SSD_CONV - 1) + kk
        conv = conv + cw_ref[kk:kk + 1, :] * tail_ref[off:off + ch, :]
    tail_ref[0:halo, :] = tail_ref[ch:ch + halo, :]
    xc = _silu(conv)
    xs = xc[:, :SSD_D_INNER]
    bm = xc[:, SSD_D_INNER:SSD_D_INNER + SSD_GROUPS * SSD_STATE]
    cm = xc[:, SSD_D_INNER + SSD_GROUPS * SSD_STATE:]

    dtr = dt_ref[0] + dtb_ref[...]
    dt = jnp.maximum(dtr, 0.0) + jnp.log1p(jnp.exp(-jnp.abs(dtr)))
    a = -jnp.exp(alog_ref[...])
    r_iota = lax.broadcasted_iota(jnp.int32, (ch, ch), 0)
    c_iota = lax.broadcasted_iota(jnp.int32, (ch, ch), 1)
    causal = r_iota >= c_iota
    tri = jnp.where(causal, 1.0, 0.0).astype(MXU_DTYPE)
    da = dt * a
    da_hi, da_lo = _split(da)
    cum = _dot(tri, da_hi) + _dot(tri, da_lo)
    cum_t = cum.T
    dt_t = dt.T
    cum_last = cum[ch - 1:ch, :]
    decay_st = jnp.exp(cum_last - cum) * dt
    e_cum = jnp.exp(cum)
    e_last = jnp.exp(cum_last)

    for g in range(SSD_GROUPS):
        bm_g = bm[:, g * SSD_STATE:(g + 1) * SSD_STATE].astype(MXU_DTYPE)
        cm_g = cm[:, g * SSD_STATE:(g + 1) * SSD_STATE].astype(MXU_DTYPE)
        cb = _dot_nt(cm_g, bm_g)
        for r in range(heads_per_group):
            hd = g * heads_per_group + r
            cs = slice(hd * SSD_HEAD_DIM, (hd + 1) * SSD_HEAD_DIM)
            x_h = xs[:, cs]
            seg = cum[:, hd:hd + 1] - cum_t[hd:hd + 1, :]
            lmat = jnp.exp(jnp.where(causal, seg, NEG_INF))
            w = cb * lmat * dt_t[hd:hd + 1, :]
            y = _dot(w.astype(MXU_DTYPE), x_h.astype(MXU_DTYPE))
            prev = st_ref[hd]
            y = y + _dot(cm_g, prev.astype(MXU_DTYPE)) * e_cum[:, hd:hd + 1]
            xd = (x_h * decay_st[:, hd:hd + 1]).astype(MXU_DTYPE)
            st_ref[hd] = e_last[:, hd:hd + 1] * prev + _dot_tn(bm_g, xd)
            y_ref[:, cs] = y + dsk_ref[:, cs] * x_h

    y = y_ref[...] * _silu(z_ref[0].astype(F32))
    gw = SSD_D_INNER // SSD_GROUPS
    for g in range(SSD_GROUPS):
        cs = slice(g * gw, (g + 1) * gw)
        yg = y[:, cs]
        ms = jnp.mean(yg * yg, axis=-1, keepdims=True)
        o_ref[0, :, cs] = (yg * lax.rsqrt(ms + LN_EPS) * ng_ref[:, cs]).astype(o_ref.dtype)


def _scan_kernel(rq_ref, rk_ref, rv_ref, rg_ref, gn_g_ref, gn_b_ref,
                 z_ref, xbc_ref, dt_ref, cw_ref, cb_ref, dtb_ref, alog_ref, dsk_ref, ng_ref,
                 o_ret_ref, o_ssd_ref, ret_st_ref, tail_ref, ssd_st_ref, y_ref):
    _ret_kernel(rq_ref, rk_ref, rv_ref, rg_ref, gn_g_ref, gn_b_ref, o_ret_ref, ret_st_ref)
    _ssd_kernel(z_ref, xbc_ref, dt_ref, cw_ref, cb_ref, dtb_ref, alog_ref, dsk_ref, ng_ref,
                o_ssd_ref, tail_ref, ssd_st_ref, y_ref)


def _scans(proj, proj_dt, gn_g, gn_b, conv_w, conv_b, dt_bias, a_log, d_skip, norm_g):
    bsz, s, _ = proj.shape
    assert RET_CHUNK == SSD_CHUNK
    ch = SSD_CHUNK
    wq, wv = RET_HEADS * RET_DK, RET_HEADS * RET_DV
    pad = LANES - SSD_HEADS
    dtb = jnp.pad(dt_bias, (0, pad)).reshape(1, LANES)
    alog = jnp.pad(a_log, (0, pad)).reshape(1, LANES)
    dsk = jnp.repeat(d_skip, SSD_HEAD_DIM).reshape(1, SSD_D_INNER)
    const = lambda b, i: (0, 0)
    return pl.pallas_call(
        _scan_kernel,
        grid=(bsz, s // ch),
        in_specs=[pl.BlockSpec((1, ch, wq), lambda b, i: (b, i, A_RET_Q // wq)),
                  pl.BlockSpec((1, ch, wq), lambda b, i: (b, i, A_RET_K // wq)),
                  pl.BlockSpec((1, ch, wv), lambda b, i: (b, i, A_RET_V // wv)),
                  pl.BlockSpec((1, ch, wv), lambda b, i: (b, i, B_RET_G // wv)),
                  pl.BlockSpec((1, wv), const),
                  pl.BlockSpec((1, wv), const),
                  pl.BlockSpec((1, ch, SSD_D_INNER), lambda b, i: (b, i, B_SSD_Z // SSD_D_INNER)),
                  pl.BlockSpec((1, ch, SSD_XBC), lambda b, i: (b, i, B_XBC // SSD_XBC)),
                  pl.BlockSpec((1, ch, LANES), lambda b, i: (b, i, 0)),
                  pl.BlockSpec((SSD_CONV, SSD_XBC), const),
                  pl.BlockSpec((1, SSD_XBC), const),
                  pl.BlockSpec((1, LANES), const),
                  pl.BlockSpec((1, LANES), const),
                  pl.BlockSpec((1, SSD_D_INNER), const),
                  pl.BlockSpec((1, SSD_D_INNER), const)],
        out_specs=[pl.BlockSpec((1, ch, wv), lambda b, i: (b, i, 0)),
                   pl.BlockSpec((1, ch, SSD_D_INNER), lambda b, i: (b, i, 0))],
        out_shape=[jax.ShapeDtypeStruct((bsz, s, wv), MXU_DTYPE),
                   jax.ShapeDtypeStruct((bsz, s, SSD_D_INNER), MXU_DTYPE)],
        scratch_shapes=[pltpu.VMEM((RET_HEADS, RET_DK, RET_DV), F32),
                        pltpu.VMEM((SSD_HALO + ch, SSD_XBC), F32),
                        pltpu.VMEM((SSD_HEADS, SSD_STATE, SSD_HEAD_DIM), F32),
                        pltpu.VMEM((ch, SSD_D_INNER), F32)],
        compiler_params=_params(("parallel", "arbitrary")),
        name="retention_ssd",
    )(proj, proj, proj, proj, gn_g.reshape(1, wv), gn_b.reshape(1, wv),
      proj, proj, proj_dt, conv_w, conv_b.reshape(1, SSD_XBC), dtb, alog, dsk,
      norm_g.reshape(1, SSD_D_INNER))


def _merge_kernel(oa_ref, ob_ref, oc_ref, od_ref, gates_ref, x_ref, g1_ref, wbr_ref, wout_ref,
                  lg_ref, lb_ref, o_ref):
    merged = None
    for n, br in enumerate((oa_ref, ob_ref, oc_ref, od_ref)):
        y = _dot(br[0], wbr_ref[n])
        t = _sigmoid(gates_ref[0, :, n * D_MODEL:(n + 1) * D_MODEL].astype(F32)) * y
        merged = t if merged is None else merged + t
    mix = _dot(merged.astype(MXU_DTYPE), wout_ref[...])
    o_ref[0] = _resid_ln(x_ref[0], g1_ref[0], mix, lg_ref[...], lb_ref[...])


def _merge(o_a, o_b, o_c, o_d, proj, x, g1, w_br, w_out, ln_g, ln_b, tm=512):
    bsz, s, d = x.shape
    tm = min(tm, s)
    gw = N_BRANCH * d
    row = lambda b, i: (b, i, 0)
    const2 = lambda b, i: (0, 0)
    br_spec = pl.BlockSpec((1, tm, BRANCH_W), row)
    return pl.pallas_call(
        _merge_kernel,
        grid=(bsz, s // tm),
        in_specs=[br_spec, br_spec, br_spec, br_spec,
                  pl.BlockSpec((1, tm, gw), lambda b, i: (b, i, B_GATES // gw)),
                  pl.BlockSpec((1, tm, d), row),
                  pl.BlockSpec((1, 1, d), lambda b, i: (b, 0, 0)),
                  pl.BlockSpec((N_BRANCH, BRANCH_W, d), lambda b, i: (0, 0, 0)),
                  pl.BlockSpec((d, d), const2),
                  pl.BlockSpec((1, d), const2),
                  pl.BlockSpec((1, d), const2)],
        out_specs=pl.BlockSpec((1, tm, d), row),
        out_shape=jax.ShapeDtypeStruct((bsz, s, d), F32),
        compiler_params=_params(("parallel", "parallel")),
        name="merge",
    )(o_a, o_b, o_c, o_d, proj, x, g1, w_br, w_out, ln_g.reshape(1, d), ln_b.reshape(1, d))


def _ffn_kernel(x_ref, sc_ref, sh_ref, g2_ref, wg_ref, wu_ref, wd_ref, lg_ref, lb_ref, o_ref):
    x = x_ref[0]
    h = (x * (1.0 + sc_ref[0]) + sh_ref[0]).astype(MXU_DTYPE)
    act = _silu(_dot(h, wg_ref[...])) * _dot(h, wu_ref[...])
    f = _dot(act.astype(MXU_DTYPE), wd_ref[...])
    o_ref[0] = _resid_ln(x, g2_ref[0], f, lg_ref[...], lb_ref[...])


def _ffn(x, sc, sh, g2, w_gu, w_down, ln_g, ln_b, tm=512):
    bsz, s, d = x.shape
    tm = min(tm, s)
    row = lambda b, i: (b, i, 0)
    vec = lambda b, i: (b, 0, 0)
    once = pl.Buffered(1)
    return pl.pallas_call(
        _ffn_kernel,
        grid=(bsz, s // tm),
        in_specs=[pl.BlockSpec((1, tm, d), row),
                  pl.BlockSpec((1, 1, d), vec), pl.BlockSpec((1, 1, d), vec),
                  pl.BlockSpec((1, 1, d), vec),
                  pl.BlockSpec((d, D_FF), lambda b, i: (0, 0), pipeline_mode=once),
                  pl.BlockSpec((d, D_FF), lambda b, i: (0, 1), pipeline_mode=once),
                  pl.BlockSpec((D_FF, d), lambda b, i: (0, 0), pipeline_mode=once),
                  pl.BlockSpec((1, d), lambda b, i: (0, 0)),
                  pl.BlockSpec((1, d), lambda b, i: (0, 0))],
        out_specs=pl.BlockSpec((1, tm, d), row),
        out_shape=jax.ShapeDtypeStruct((bsz, s, d), F32),
        compiler_params=_params(("parallel", "parallel")),
        name="ffn_dense",
    )(x, sc, sh, g2, w_gu, w_gu, w_down, ln_g.reshape(1, d), ln_b.reshape(1, d))


def _router_kernel(x_ref, sc_ref, sh_ref, rw_ref, rb_ref,
                   h_ref, comb_ref, slot_ref, slot_t_ref, cnt_ref):
    tm = x_ref.shape[1]
    lane = lax.broadcasted_iota(jnp.int32, (tm, LANES), 1)
    h = x_ref[0] * (1.0 + sc_ref[0]) + sh_ref[0]
    h_ref[0] = h.astype(MXU_DTYPE)
    logits = _dot3(h, rw_ref[...]) + rb_ref[...]
    logits = jnp.where(lane < N_EXPERTS, logits, -jnp.inf)
    v1 = jnp.max(logits, axis=1, keepdims=True)
    i1 = jnp.min(jnp.where(logits == v1, lane, LANES), axis=1, keepdims=True)
    rest = jnp.where(lane == i1, -jnp.inf, logits)
    v2 = jnp.max(rest, axis=1, keepdims=True)
    i2 = jnp.min(jnp.where(rest == v2, lane, LANES), axis=1, keepdims=True)
    e2 = jnp.exp(v2 - v1)
    w1 = 1.0 / (1.0 + e2)
    w2 = e2 / (1.0 + e2)
    comb_ref[0] = jnp.where(lane == i1, w1, 0.0) + jnp.where(lane == i2, w2, 0.0)
    routed = jnp.logical_or(lane == i1, lane == i2)
    earlier = (lax.broadcasted_iota(jnp.int32, (tm, tm), 1)
               < lax.broadcasted_iota(jnp.int32, (tm, tm), 0))
    before = _dot(jnp.where(earlier, 1.0, 0.0).astype(MXU_DTYPE),
                  jnp.where(routed, 1.0, 0.0).astype(MXU_DTYPE))
    slot = jnp.where(routed, before, -1.0)
    slot_ref[0] = slot
    slot_t_ref[0] = slot.T[0:N_EXPERTS, :]
    cnt_ref[0] = jnp.sum(jnp.where(routed, 1, 0), axis=0, keepdims=True)


def _moe_kernel(cnt_ref, h_ref, x_ref, g2_ref, comb_ref, slot_ref, slot_t_ref,
                wg_ref, wu_ref, wd_ref, lg_ref, lb_ref, o_ref, xe_ref, ye_ref, *, n_tiles):
    tm = x_ref.shape[1]
    e = pl.program_id(2)
    j = pl.program_id(3)
    last_j = pl.num_programs(3) - 1
    count = cnt_ref[(pl.program_id(0) * n_tiles + pl.program_id(1)) * N_EXPERTS + e]
    n_full = jnp.maximum(count - MOE_TAIL_MIN - 1, 0) // MOE_ROWS
    tail0 = pl.multiple_of(n_full * MOE_ROWS, MOE_ROWS)
    tail_steps = (count - n_full * MOE_ROWS + MOE_ROW_STEP - 1) // MOE_ROW_STEP

    def for_pieces(fn):
        def full_piece(c, carry):
            fn(pl.multiple_of(c * MOE_ROWS, MOE_ROWS), MOE_ROWS)
            return carry

        lax.fori_loop(0, n_full, full_piece, 0)
        for steps in range(1, (MOE_ROWS + MOE_TAIL_MIN) // MOE_ROW_STEP + 1):
            @pl.when(tail_steps == steps)
            def _(steps=steps):
                fn(tail0, steps * MOE_ROW_STEP)

    @pl.when(jnp.logical_and(e == 0, j == 0))
    def _():
        o_ref[0] = jnp.zeros((tm, D_MODEL), F32)

    @pl.when(j == 0)
    def _():
        slot_row = slot_t_ref[0, pl.ds(e, 1), :]

        def gather(r0, rows):
            slots = (lax.broadcasted_iota(jnp.int32, (rows, tm), 0) + r0).astype(F32)
            pick = jnp.where(slot_row == slots, 1.0, 0.0).astype(MXU_DTYPE)
            xe_ref[pl.ds(r0, rows), :] = _dot(pick, h_ref[0]).astype(MXU_DTYPE)
            ye_ref[pl.ds(r0, rows), :] = jnp.zeros((rows, D_MODEL), F32)

        for_pieces(gather)

    def expert_rows(r0, rows):
        xc = xe_ref[pl.ds(r0, rows), :]
        act = _silu(_dot(xc, wg_ref[0])) * _dot(xc, wu_ref[0])
        ye_ref[pl.ds(r0, rows), :] += _dot(act.astype(MXU_DTYPE), wd_ref[0])

    for_pieces(expert_rows)

    @pl.when(j == last_j)
    def _():
        lane = lax.broadcasted_iota(jnp.int32, (tm, LANES), 1)
        mine = lane == e
        slot_col = jnp.sum(jnp.where(mine, slot_ref[0], 0.0), axis=1, keepdims=True)
        comb_col = jnp.sum(jnp.where(mine, comb_ref[0], 0.0), axis=1, keepdims=True)

        def scatter(r0, rows):
            slots = (lax.broadcasted_iota(jnp.int32, (tm, rows), 1) + r0).astype(F32)
            place = jnp.where(slot_col == slots, 1.0, 0.0).astype(MXU_DTYPE)
            o_ref[0] += comb_col * _dot(place, ye_ref[pl.ds(r0, rows), :].astype(MXU_DTYPE))

        for_pieces(scatter)

    @pl.when(jnp.logical_and(e == pl.num_programs(2) - 1, j == last_j))
    def _():
        o_ref[0] = _resid_ln(x_ref[0], g2_ref[0], o_ref[0], lg_ref[...], lb_ref[...])


def _moe(x, sc, sh, g2, router_w, router_b, w_gu, w_down, ln_g, ln_b, tm=1024, tf=1792):
    bsz, s, d = x.shape
    tm = min(tm, s)
    nt = s // tm
    nf = D_FF_EXPERT // tf
    rw = jnp.pad(router_w, ((0, 0), (0, LANES - N_EXPERTS)))
    rb = jnp.pad(router_b, (0, LANES - N_EXPERTS)).reshape(1, LANES)

    row2 = lambda b, i: (b, i, 0)
    vec2 = lambda b, i: (b, 0, 0)
    h, comb, slot, slot_t, cnt = pl.pallas_call(
        _router_kernel,
        grid=(bsz, nt),
        in_specs=[pl.BlockSpec((1, tm, d), row2),
                  pl.BlockSpec((1, 1, d), vec2), pl.BlockSpec((1, 1, d), vec2),
                  pl.BlockSpec((d, LANES), lambda b, i: (0, 0)),
                  pl.BlockSpec((1, LANES), lambda b, i: (0, 0))],
        out_specs=[pl.BlockSpec((1, tm, d), row2),
                   pl.BlockSpec((1, tm, LANES), row2),
                   pl.BlockSpec((1, tm, LANES), row2),
                   pl.BlockSpec((1, N_EXPERTS, tm), lambda b, i: (b * nt + i, 0, 0)),
                   pl.BlockSpec((1, 1, LANES), lambda b, i: (b * nt + i, 0, 0))],
        out_shape=[jax.ShapeDtypeStruct((bsz, s, d), MXU_DTYPE),
                   jax.ShapeDtypeStruct((bsz, s, LANES), F32),
                   jax.ShapeDtypeStruct((bsz, s, LANES), F32),
                   jax.ShapeDtypeStruct((bsz * nt, N_EXPERTS, tm), F32),
                   jax.ShapeDtypeStruct((bsz * nt, 1, LANES), jnp.int32)],
        compiler_params=_params(("parallel", "parallel")),
        name="moe_router",
    )(x, sc, sh, rw, rb)
    counts = cnt[:, 0, :N_EXPERTS].reshape(bsz * nt * N_EXPERTS)

    row = lambda b, i, e, j, cnt: (b, i, 0)
    vec = lambda b, i, e, j, cnt: (b, 0, 0)
    const2 = lambda b, i, e, j, cnt: (0, 0)
    once = pl.Buffered(1)
    grid_spec = pltpu.PrefetchScalarGridSpec(
        num_scalar_prefetch=1,
        grid=(bsz, nt, N_EXPERTS, nf),
        in_specs=[pl.BlockSpec((1, tm, d), row, pipeline_mode=once),
                  pl.BlockSpec((1, tm, d), row, pipeline_mode=once),
                  pl.BlockSpec((1, 1, d), vec),
                  pl.BlockSpec((1, tm, LANES), row, pipeline_mode=once),
                  pl.BlockSpec((1, tm, LANES), row, pipeline_mode=once),
                  pl.BlockSpec((1, N_EXPERTS, tm), lambda b, i, e, j, cnt: (b * nt + i, 0, 0)),
                  pl.BlockSpec((1, d, tf), lambda b, i, e, j, cnt: (e, 0, j)),
                  pl.BlockSpec((1, d, tf), lambda b, i, e, j, cnt: (e, 0, nf + j)),
                  pl.BlockSpec((1, tf, d), lambda b, i, e, j, cnt: (e, j, 0)),
                  pl.BlockSpec((1, d), const2), pl.BlockSpec((1, d), const2)],
        out_specs=pl.BlockSpec((1, tm, d), row),
        scratch_shapes=[pltpu.VMEM((tm, d), MXU_DTYPE), pltpu.VMEM((tm, d), F32)])
    return pl.pallas_call(
        functools.partial(_moe_kernel, n_tiles=nt),
        grid_spec=grid_spec,
        out_shape=jax.ShapeDtypeStruct((bsz, s, d), F32),
        compiler_params=pltpu.CompilerParams(
            dimension_semantics=("parallel", "parallel", "arbitrary", "arbitrary"),
            vmem_limit_bytes=MOE_VMEM_LIMIT),
        name="moe_experts",
    )(counts, h, x, g2, comb, slot, slot_t, w_gu, w_gu, w_down,
      ln_g.reshape(1, d), ln_b.reshape(1, d))


def _split_w_in(w):
    p = np.cumsum([0, 512, 512, 512, 256, 256, 512, 512, 512, SSD_XBC, SSD_HEADS, 512, 512, 512,
                   N_BRANCH * D_MODEL])
    mq, mk, mv, rq, rk, rv, rg, sz, sxbc, sdt, bq, bk, bv, gates = (
        w[:, p[i]:p[i + 1]] for i in range(14))
    w_main = jnp.concatenate([mq, mk, mv, rq, rk, rv, bq, bk, bv, gates, rg, sz, sxbc], axis=1)
    w_dt = jnp.pad(sdt, ((0, 0), (0, LANES - SSD_HEADS)))
    assert w_main.shape[1] == PROJ_WIDTH
    return w_main.astype(MXU_DTYPE), w_dt.astype(MXU_DTYPE)


def kernel(x, c, w_ada, b_ada, w_in, conv_w, conv_b, dt_bias, a_log, d_skip, ssm_norm_g, ret_gn_g, ret_gn_b, w_br, w_out, ln1_g, ln1_b, ln2_g, ln2_b, ffn_w_gu, ffn_w_down, router_w, router_b, expert_w_gu, expert_w_down):
    bsz = x.shape[0]
    mod = _ada(c, w_ada, b_ada)
    for l in range(DEPTH):
        sh1, sc1, g1, sh2, sc2, g2 = (
            mod[l, :, i * D_MODEL:(i + 1) * D_MODEL].reshape(bsz, 1, D_MODEL) for i in range(6))
        w_main, w_dt = _split_w_in(w_in[l])
        proj, proj_dt = _inproj(x, sc1, sh1, w_main, w_dt, tn=2560)
        o_a = _moba(proj)
        o_d = _stick_breaking(proj)
        o_b, o_c = _scans(proj, proj_dt, ret_gn_g[l], ret_gn_b[l], conv_w[l], conv_b[l],
                          dt_bias[l], a_log[l], d_skip[l], ssm_norm_g[l])
        x = _merge(o_a, o_b, o_c, o_d, proj, x, g1, w_br[l].astype(MXU_DTYPE),
                   w_out[l].astype(MXU_DTYPE), ln1_g[l], ln1_b[l])
        if l % 2 == 0:
            x = _ffn(x, sc2, sh2, g2, ffn_w_gu[l // 2].astype(MXU_DTYPE),
                     ffn_w_down[l // 2].astype(MXU_DTYPE), ln2_g[l], ln2_b[l])
        else:
            x = _moe(x, sc2, sh2, g2, router_w[l // 2], router_b[l // 2],
                     expert_w_gu[l // 2].astype(MXU_DTYPE), expert_w_down[l // 2].astype(MXU_DTYPE),
                     ln2_g[l], ln2_b[l])
    return x
```

```python
import functools

import numpy as np
import jax
import jax.numpy as jnp
from jax import lax
from jax.experimental import pallas as pl
from jax.experimental.pallas import tpu as pltpu

F32 = jnp.float32
MXU_DTYPE = jnp.bfloat16

D_MODEL = 1024
DEPTH = 2
MOBA_HEADS = 8
MOBA_BLOCK = 256
MOBA_TOPK = 3
RET_HEADS = 4
RET_DK = 64
RET_DV = 128
RET_CHUNK = 256
SSD_D_INNER = 512
SSD_HEAD_DIM = 64
SSD_HEADS = 8
SSD_GROUPS = 2
SSD_STATE = 128
SSD_CONV = 4
SSD_CHUNK = 256
SSD_XBC = SSD_D_INNER + 2 * SSD_GROUPS * SSD_STATE
SSD_HALO = 8
SB_HEADS = 8
SB_BLOCK = 256
HEAD_DIM = 64
ATT_HEADS_PER_STEP = 8
ATT_W = ATT_HEADS_PER_STEP * HEAD_DIM
ATT_Q_BLOCKS_PER_STEP = 2
BRANCH_W = 512
N_BRANCH = 4
D_FF = 2816
N_EXPERTS = 8
D_FF_EXPERT = 3584
DEEPNORM_ALPHA = (2.0 * DEPTH) ** 0.25
LN_EPS = 1e-5
NORM_EPS = 1e-6
NEG_INF = -1e30
LOG2E = 1.4426950408889634
LANES = 128
VMEM_LIMIT = 48 * 2 ** 20
MOE_VMEM_LIMIT = 56 * 2 ** 20
MOE_ROWS = 256
MOE_ROW_STEP = 64
MOE_TAIL_MIN = 128

A_MOBA_Q, A_MOBA_K, A_MOBA_V = 0, 512, 1024
A_RET_Q, A_RET_K, A_RET_V = 1536, 1792, 2048
A_SB_Q, A_SB_K, A_SB_V = 2560, 3072, 3584
B_GATES, B_RET_G, B_SSD_Z, B_XBC = 4096, 8192, 8704, 9216
PROJ_WIDTH = 10240


def _params(sem):
    return pltpu.CompilerParams(dimension_semantics=sem, vmem_limit_bytes=VMEM_LIMIT)


def _dot(a, b):
    return jnp.dot(a, b, preferred_element_type=F32)


def _dot_nt(a, b):
    return lax.dot_general(a, b, (((1,), (1,)), ((), ())), preferred_element_type=F32)


def _dot_tn(a, b):
    return lax.dot_general(a, b, (((0,), (0,)), ((), ())), preferred_element_type=F32)


def _split(x):
    hi = x.astype(MXU_DTYPE)
    lo = (x - hi.astype(F32)).astype(MXU_DTYPE)
    return hi, lo


def _dot3(a, b):
    ah, al = _split(a)
    bh, bl = _split(b)
    return _dot(ah, bh) + _dot(al, bh) + _dot(ah, bl)


def _sigmoid(x):
    return 1.0 / (1.0 + jnp.exp(-x))


def _silu(x):
    return x * _sigmoid(x)


def _resid_ln(x, gate, f, ln_g, ln_b):
    y = DEEPNORM_ALPHA * x + gate * f
    mu = jnp.mean(y, axis=-1, keepdims=True)
    d = y - mu
    var = jnp.mean(d * d, axis=-1, keepdims=True)
    return d * lax.rsqrt(var + LN_EPS) * ln_g + ln_b


def _ada_kernel(c_ref, w_ref, b_ref, o_ref):
    c = c_ref[...]
    o_ref[0] = _dot3(_silu(c), w_ref[0]) + b_ref[0]


def _ada(c, w_ada, b_ada):
    depth, d, n = w_ada.shape
    bsz = c.shape[0]
    tn = 1536
    return pl.pallas_call(
        _ada_kernel,
        grid=(depth, n // tn),
        in_specs=[pl.BlockSpec((bsz, d), lambda l, j: (0, 0)),
                  pl.BlockSpec((1, d, tn), lambda l, j: (l, 0, j)),
                  pl.BlockSpec((1, 1, tn), lambda l, j: (l, 0, j))],
        out_specs=pl.BlockSpec((1, bsz, tn), lambda l, j: (l, 0, j)),
        out_shape=jax.ShapeDtypeStruct((depth, bsz, n), F32),
        compiler_params=_params(("arbitrary", "arbitrary")),
        name="ada_mod",
    )(c, w_ada, b_ada.reshape(depth, 1, n))


def _inproj_kernel(x_ref, sc_ref, sh_ref, w_ref, wdt_ref, o_ref, odt_ref, *, tn):
    h = (x_ref[0] * (1.0 + sc_ref[0]) + sh_ref[0]).astype(MXU_DTYPE)
    odt_ref[0] = _dot(h, wdt_ref[...])
    for c in range(w_ref.shape[1] // tn):
        cols = slice(c * tn, (c + 1) * tn)
        o_ref[0, :, cols] = _dot(h, w_ref[:, cols]).astype(o_ref.dtype)


def _inproj(x, sc, sh, w, w_dt, tn, tm=512):
    bsz, s, d = x.shape
    n = w.shape[1]
    n_dt = w_dt.shape[1]
    tm = min(tm, s)
    once = pl.Buffered(1)
    return pl.pallas_call(
        functools.partial(_inproj_kernel, tn=tn),
        grid=(bsz, s // tm),
        in_specs=[pl.BlockSpec((1, tm, d), lambda b, i: (b, i, 0)),
                  pl.BlockSpec((1, 1, d), lambda b, i: (b, 0, 0)),
                  pl.BlockSpec((1, 1, d), lambda b, i: (b, 0, 0)),
                  pl.BlockSpec((d, n), lambda b, i: (0, 0), pipeline_mode=once),
                  pl.BlockSpec((d, n_dt), lambda b, i: (0, 0), pipeline_mode=once)],
        out_specs=[pl.BlockSpec((1, tm, n), lambda b, i: (b, i, 0)),
                   pl.BlockSpec((1, tm, n_dt), lambda b, i: (b, i, 0))],
        out_shape=[jax.ShapeDtypeStruct((bsz, s, n), MXU_DTYPE),
                   jax.ShapeDtypeStruct((bsz, s, n_dt), F32)],
        compiler_params=pltpu.CompilerParams(dimension_semantics=("parallel", "parallel"),
                                             vmem_limit_bytes=MOE_VMEM_LIMIT),
        name="in_proj",
    )(x, sc, sh, w, w_dt)


def _moba_parts(qi, rows, q_ref, k_ref, v_ref, sl_ref, o_ref, km_ref, ka_ref, vt_ref, nb):
    blk = MOBA_BLOCK
    nh = ATT_HEADS_PER_STEP
    heads = range(nh)
    s_len = nb * blk
    nb_pad = -(-nb // 8) * 8
    extra = HEAD_DIM
    assert nb_pad + 8 <= extra
    cols = [slice(h * HEAD_DIM, (h + 1) * HEAD_DIM) for h in heads]

    @pl.when(qi == 0)
    def _():
        k_all = k_ref[0]
        kk = k_all.astype(F32).reshape(nb, blk, ATT_W)
        km_ref[...] = jnp.zeros_like(km_ref)
        km_ref[0:nb, :] = jnp.sum(kk, axis=1) * (1.0 / blk)
        pos = lax.broadcasted_iota(jnp.int32, (s_len, extra), 0)
        lane = lax.broadcasted_iota(jnp.int32, (s_len, extra), 1)
        one_hot = jnp.where(lane == pos // blk, 1.0, 0.0)
        pos_hi = jnp.where(lane == nb_pad, (pos // blk * blk).astype(F32), 0.0)
        pos_lo = jnp.where(lane == nb_pad + 1, (pos % blk).astype(F32), 0.0)
        tail = (one_hot + pos_hi + pos_lo).astype(MXU_DTYPE)
        for h in heads:
            ka_ref[h] = jnp.concatenate([k_all[:, cols[h]], tail], axis=1)

        ones_rows = jnp.where(lax.broadcasted_iota(jnp.int32, (extra, blk), 0) == 0, 1.0, 0.0)

        def transpose_block(j, carry):
            j0 = pl.multiple_of(j * blk, blk)
            vt = v_ref[0, pl.ds(j0, blk), :].astype(F32).T
            for h in heads:
                vt_ref[j, h] = jnp.concatenate([vt[cols[h], :], ones_rows], axis=0).astype(MXU_DTYPE)
            return carry

        lax.fori_loop(0, nb, transpose_block, 0)

    key_i = lax.broadcasted_iota(jnp.int32, (blk, blk), 0)
    qry_i = lax.broadcasted_iota(jnp.int32, (blk, blk), 1)
    n_iota = lax.broadcasted_iota(jnp.int32, (nb_pad, blk), 0)
    past = n_iota < qi
    k_sel = min(MOBA_TOPK, max(nb - 1, 1))
    q_t = q_ref[0, rows, :].astype(F32).T
    slope_rows = lax.broadcasted_iota(jnp.int32, (8, blk), 0) < 2

    qa = []
    for h in heads:
        qh_t = q_t[cols[h], :]
        slope = sl_ref[0, :, h * HEAD_DIM:h * HEAD_DIM + 1]
        km_hi, km_lo = _split(km_ref[:, cols[h]])
        qh_mxu = qh_t.astype(MXU_DTYPE)
        gate = (_dot(km_hi, qh_mxu) + _dot(km_lo, qh_mxu))[0:nb_pad, :]
        g = jnp.where(past, gate, NEG_INF)
        g = jnp.where(n_iota < nb, g, -jnp.inf)
        sel = jnp.zeros((nb_pad, blk), F32)
        for _ in range(k_sel):
            mx = jnp.max(g, axis=0, keepdims=True)
            idx = jnp.min(jnp.where(g == mx, n_iota, nb_pad), axis=0, keepdims=True)
            pick = n_iota == idx
            sel = jnp.where(pick, 1.0, sel)
            g = jnp.where(pick, -jnp.inf, g)
        keep = jnp.logical_or(jnp.logical_and(past, sel > 0.5), n_iota == qi)
        bias = jnp.where(keep, 0.0, NEG_INF)
        alibi = jnp.where(slope_rows, slope, 0.0)
        pad = jnp.zeros((extra - nb_pad - 8, blk), F32)
        qa.append(jnp.concatenate([qh_t * HEAD_DIM ** -0.5, bias, alibi, pad],
                                  axis=0).astype(MXU_DTYPE))

    def scores(js):
        starts = [pl.multiple_of(j * blk, blk) for j in js]
        return tuple(_dot(ka_ref[h, pl.ds(j0, blk), :], qa[h]) for j0 in starts for h in heads)

    def rest(js, flat_scores, carry, diag):
        s = [flat_scores[t * nh:(t + 1) * nh] for t in range(len(js))]
        m = list(carry[:nh]) if carry is not None else [None] * nh
        acc = list(carry[nh:]) if carry is not None else [None] * nh
        p = {}
        for h in heads:
            sh = [jnp.where(key_i <= qry_i, s[t][h], NEG_INF) if diag else s[t][h]
                  for t in range(len(js))]
            top = jnp.max(sh[0], axis=0, keepdims=True)
            for t in range(1, len(js)):
                top = jnp.maximum(top, jnp.max(sh[t], axis=0, keepdims=True))
            if m[h] is None:
                m[h] = top
            else:
                m_new = jnp.maximum(m[h], top)
                acc[h] = jnp.exp(m[h] - m_new) * acc[h]
                m[h] = m_new
            for t in range(len(js)):
                p[t, h] = jnp.exp(sh[t] - m[h]).astype(MXU_DTYPE)
        for t, j in enumerate(js):
            for h in heads:
                pv = _dot(vt_ref[j, h], p[t, h])
                acc[h] = pv if acc[h] is None else acc[h] + pv
        return tuple(m) + tuple(acc)

    def finish(carry):
        out_t = [carry[nh + h][0:HEAD_DIM, :] / carry[nh + h][HEAD_DIM:HEAD_DIM + 1, :]
                 for h in heads]
        o_ref[0, rows, :] = jnp.concatenate(out_t, axis=0).T.astype(o_ref.dtype)

    return scores, rest, None, finish


def _sb_parts(qi, rows, q_ref, k_ref, v_ref, o_ref, vt_ref, nb):
    blk = SB_BLOCK
    nh = ATT_HEADS_PER_STEP
    heads = range(nh)
    cols =[slice(h * HEAD_DIM, (h + 1) * HEAD_DIM) for h in heads]

    @pl.when(qi == 0)
    def _():
        def transpose_block(j, carry):
            j0 = pl.multiple_of(j * blk, blk)
            vt = v_ref[0, pl.ds(j0, blk), :].astype(F32).T
            for h in heads:
                vt_ref[j, h] = vt[cols[h], :].astype(MXU_DTYPE)
            return carry

        lax.fori_loop(0, nb, transpose_block, 0)

    key_i = lax.broadcasted_iota(jnp.int32, (blk, blk), 0)
    qry_i = lax.broadcasted_iota(jnp.int32, (blk, blk), 1)
    causal = key_i < qry_i
    later = jnp.where(qry_i > key_i, 1.0, 0.0).astype(MXU_DTYPE)
    qs = [(q_ref[0, rows, cols[h]].astype(F32) * HEAD_DIM ** -0.5).astype(MXU_DTYPE)
          for h in heads]

    def scores(js):
        starts = [pl.multiple_of(j * blk, blk) for j in js]
        return tuple(_dot_nt(k_ref[0, pl.ds(j0, blk), cols[h]], qs[h])
                     for j0 in starts for h in heads)

    def rest(js, flat_scores, carry, diag):
        accs, rests = carry[:nh], carry[nh:]
        z = [flat_scores[t * nh:(t + 1) * nh] for t in range(len(js))]
        log_beta, log_1mb, parts = [], [], []
        for t in range(len(js)):
            for h in heads:
                zz = z[t][h]
                lb = jnp.minimum(zz, 0.0) - jnp.log(1.0 + jnp.exp2(jnp.abs(zz) * (-LOG2E)))
                l1 = lb - zz
                if diag:
                    l1 = jnp.where(causal, l1, 0.0)
                log_beta.append(lb)
                log_1mb.append(l1)
                parts.append(l1.astype(MXU_DTYPE))
        rem = [_dot(later, p) for p in parts]
        w = []
        rests = list(rests)
        for t in range(len(js)):
            for h in heads:
                i = t * nh + h
                wh = jnp.exp(log_beta[i] + rem[i] + rests[h])
                if diag:
                    wh = jnp.where(causal, wh, 0.0)
                w.append(wh.astype(MXU_DTYPE))
                rests[h] = rests[h] + rem[i][0:1, :] + log_1mb[i][0:1, :]
        accs = list(accs)
        for t, j in enumerate(js):
            for h in heads:
                accs[h] = accs[h] + _dot(vt_ref[j, h], w[t * nh + h])
        return tuple(accs) + tuple(rests)

    start = tuple([jnp.zeros((HEAD_DIM, blk), F32)] * nh + [jnp.zeros((1, blk), F32)] * nh)

    def finish(carry):
        o_ref[0, rows, :] = jnp.concatenate(carry[:nh], axis=0).T.astype(o_ref.dtype)

    return scores, rest, start, finish


def _walk_key_blocks(qi, scores, rest, start, finish, z_ref=None):
    state = rest([qi], scores([qi]), start, True)

    def single(t, state):
        return rest([qi - 1], scores([qi - 1]), state, False)

    state = lax.fori_loop(0, qi % 2, single, state)
    first = qi - 1 - qi % 2

    if z_ref is None:
        def pair(t, state):
            j = first - 2 * t
            return rest([j, j - 1], scores([j, j - 1]), state, False)
    else:
        def look_ahead(t):
            j = jnp.maximum(first - 2 * t, 1)
            for i, z in enumerate(scores([j, j - 1])):
                z_ref[i] = z

        def pair(t, state):
            j = first - 2 * t
            z = [z_ref[i] for i in range(z_ref.shape[0])]
            look_ahead(t + 1)
            return rest([j, j - 1], z, state, False)

        look_ahead(0)

    finish(lax.fori_loop(0, qi // 2, pair, state))


def _for_query_blocks(fn):
    def one(u, carry):
        fn(pl.program_id(1) * ATT_Q_BLOCKS_PER_STEP + u,
           pl.ds(pl.multiple_of(u * MOBA_BLOCK, MOBA_BLOCK), MOBA_BLOCK))
        return carry

    lax.fori_loop(0, ATT_Q_BLOCKS_PER_STEP, one, 0)


def _moba_kernel(q_ref, k_ref, v_ref, sl_ref, o_ref, km_ref, ka_ref, vt_ref, z_ref, *, nb):
    _for_query_blocks(lambda qi, rows: _walk_key_blocks(
        qi, *_moba_parts(qi, rows, q_ref, k_ref, v_ref, sl_ref, o_ref, km_ref, ka_ref, vt_ref, nb),
        z_ref=z_ref))


def _sb_kernel(q_ref, k_ref, v_ref, o_ref, vt_ref, *, nb):
    _for_query_blocks(lambda qi, rows: _walk_key_blocks(
        qi, *_sb_parts(qi, rows, q_ref, k_ref, v_ref, o_ref, vt_ref, nb)))


def _attention_specs(s, q_col, k_col, v_col):
    rows = ATT_Q_BLOCKS_PER_STEP * MOBA_BLOCK
    return ([pl.BlockSpec((1, rows, ATT_W), lambda b, i: (b, i, q_col // ATT_W)),
             pl.BlockSpec((1, s, ATT_W), lambda b, i: (b, 0, k_col // ATT_W)),
             pl.BlockSpec((1, s, ATT_W), lambda b, i: (b, 0, v_col // ATT_W))],
            pl.BlockSpec((1, rows, ATT_W), lambda b, i: (b, i, 0)))


def _moba(proj):
    bsz, s, _ = proj.shape
    assert MOBA_HEADS == ATT_HEADS_PER_STEP
    blk = MOBA_BLOCK
    nb = s // blk
    slopes = 2.0 ** (-8.0 * np.arange(1, MOBA_HEADS + 1) / MOBA_HEADS)
    sl = jnp.asarray(np.repeat(slopes, HEAD_DIM).reshape(1, 1, ATT_W), F32)
    in_specs, out_spec = _attention_specs(s, A_MOBA_Q, A_MOBA_K, A_MOBA_V)
    return pl.pallas_call(
        functools.partial(_moba_kernel, nb=nb),
        grid=(bsz, nb // ATT_Q_BLOCKS_PER_STEP),
        in_specs=in_specs + [pl.BlockSpec((1, 1, ATT_W), lambda b, i: (0, 0, 0))],
        out_specs=out_spec,
        out_shape=jax.ShapeDtypeStruct((bsz, s, ATT_W), MXU_DTYPE),
        scratch_shapes=[pltpu.VMEM((LANES, ATT_W), F32),
                        pltpu.VMEM((ATT_HEADS_PER_STEP, s, 2 * HEAD_DIM), MXU_DTYPE),
                        pltpu.VMEM((nb, ATT_HEADS_PER_STEP, 2 * HEAD_DIM, blk), MXU_DTYPE),
                        pltpu.VMEM((2 * ATT_HEADS_PER_STEP, blk, blk), F32)],
        compiler_params=_params(("parallel", "arbitrary")),
        name="moba",
    )(proj, proj, proj, sl)


def _stick_breaking(proj):
    bsz, s, _ = proj.shape
    assert SB_HEADS == ATT_HEADS_PER_STEP and SB_BLOCK == MOBA_BLOCK
    blk = SB_BLOCK
    nb = s // blk
    in_specs, out_spec = _attention_specs(s, A_SB_Q, A_SB_K, A_SB_V)
    return pl.pallas_call(
        functools.partial(_sb_kernel, nb=nb),
        grid=(bsz, nb // ATT_Q_BLOCKS_PER_STEP),
        in_specs=in_specs,
        out_specs=out_spec,
        out_shape=jax.ShapeDtypeStruct((bsz, s, ATT_W), MXU_DTYPE),
        scratch_shapes=[pltpu.VMEM((nb, ATT_HEADS_PER_STEP, HEAD_DIM, blk), MXU_DTYPE)],
        compiler_params=_params(("parallel", "arbitrary")),
        name="stick_breaking",
    )(proj, proj, proj)


def _ret_kernel(q_ref, k_ref, v_ref, g_ref, gn_g_ref, gn_b_ref, o_ref, st_ref):
    ch = RET_CHUNK

    @pl.when(pl.program_id(1) == 0)
    def _():
        st_ref[...] = jnp.zeros_like(st_ref)

    i_col = lax.broadcasted_iota(jnp.int32, (ch, 1), 0).astype(F32)
    diff = (lax.broadcasted_iota(jnp.int32, (ch, ch), 0)
            - lax.broadcasted_iota(jnp.int32, (ch, ch), 1)).astype(F32)
    heads = range(RET_HEADS)
    log_g = [float(np.log(1.0 - 2.0 ** (-5.0 - h))) for h in heads]
    qs = [(q_ref[0, :, h * RET_DK:(h + 1) * RET_DK].astype(F32) * RET_DK ** -0.5).astype(MXU_DTYPE)
          for h in heads]
    k = [k_ref[0, :, h * RET_DK:(h + 1) * RET_DK] for h in heads]
    v = [v_ref[0, :, h * RET_DV:(h + 1) * RET_DV] for h in heads]
    prev = [st_ref[h] for h in heads]
    qk = [_dot_nt(qs[h], k[h]) for h in heads]
    o_cross = [_dot(qs[h], prev[h].astype(MXU_DTYPE)) for h in heads]
    kd = [(k[h].astype(F32) * jnp.exp((ch - 1.0 - i_col) * log_g[h])).astype(MXU_DTYPE)
          for h in heads]
    st_new = [_dot_tn(kd[h], v[h]) for h in heads]
    qkd = []
    for h in heads:
        decay = jnp.where(diff >= 0, jnp.exp(jnp.maximum(diff, 0.0) * log_g[h]), 0.0)
        qkd.append((qk[h] * decay).astype(MXU_DTYPE))
    o_intra = [_dot(qkd[h], v[h]) for h in heads]
    for h in heads:
        st_ref[h] = float(np.exp(ch * log_g[h])) * prev[h] + st_new[h]
        o = o_intra[h] + o_cross[h] * jnp.exp((i_col + 1.0) * log_g[h])
        mu = jnp.mean(o, axis=-1, keepdims=True)
        d = o - mu
        var = jnp.mean(d * d, axis=-1, keepdims=True)
        cs = slice(h * RET_DV, (h + 1) * RET_DV)
        on = d * lax.rsqrt(var + NORM_EPS) * gn_g_ref[:, cs] + gn_b_ref[:, cs]
        o_ref[0, :, cs] = (_silu(g_ref[0, :, cs].astype(F32)) * on).astype(o_ref.dtype)


def _ssd_kernel(z_ref, xbc_ref, dt_ref, cw_ref, cb_ref, dtb_ref, alog_ref, dsk_ref, ng_ref,
                o_ref, tail_ref, st_ref, y_ref):
    ch = SSD_CHUNK
    heads_per_group = SSD_HEADS // SSD_GROUPS
    halo = SSD_HALO

    @pl.when(pl.program_id(1) == 0)
    def _():
        tail_ref[0:halo, :] = jnp.zeros((halo, SSD_XBC), F32)
        st_ref[...] = jnp.zeros_like(st_ref)

    tail_ref[halo:, :] = xbc_ref[0].astype(F32)
    conv = cb_ref[...]
    for kk in range(SSD_CONV):
        off = halo - (SSD_CONV - 1) + kk
        conv = conv + cw_ref[kk:kk + 1, :] * tail_ref[off:off + ch, :]
    tail_ref[0:halo, :] = tail_ref[ch:ch + halo, :]
    xc = _silu(conv)
    xs = xc[:, :SSD_D_INNER]
    bm = xc[:, SSD_D_INNER:SSD_D_INNER + SSD_GROUPS * SSD_STATE]
    cm = xc[:, SSD_D_INNER + SSD_GROUPS * SSD_STATE:]

    dtr = dt_ref[0] + dtb_ref[...]
    dt = jnp.maximum(dtr, 0.0) + jnp.log1p(jnp.exp(-jnp.abs(dtr)))
    a = -jnp.exp(alog_ref[...])
    r_iota = lax.broadcasted_iota(jnp.int32, (ch, ch), 0)
    c_iota = lax.broadcasted_iota(jnp.int32, (ch, ch), 1)
    causal = r_iota >= c_iota
    tri = jnp.where(causal, 1.0, 0.0).astype(MXU_DTYPE)
    da = dt * a
    da_hi, da_lo = _split(da)
    cum = _dot(tri, da_hi) + _dot(tri, da_lo)
    cum_t = cum.T
    dt_t = dt.T
    cum_last = cum[ch - 1:ch, :]
    decay_st = jnp.exp(cum_last - cum) * dt
    e_cum = jnp.exp(cum)
    e_last = jnp.exp(cum_last)

    for g in range(SSD_GROUPS):
        bm_g = bm[:, g * SSD_STATE:(g + 1) * SSD_STATE].astype(MXU_DTYPE)
        cm_g = cm[:, g * SSD_STATE:(g + 1) * SSD_STATE].astype(MXU_DTYPE)
        cb = _dot_nt(cm_g, bm_g)
        for r in range(heads_per_group):
            hd = g * heads_per_group + r
            cs = slice(hd * SSD_HEAD_DIM, (hd + 1) * SSD_HEAD_DIM)
            x_h = xs[:, cs]
            seg = cum[:, hd:hd + 1] - cum_t[hd:hd + 1, :]
            lmat = jnp.exp(jnp.where(causal, seg, NEG_INF))
            w = cb * lmat * dt_t[hd:hd + 1, :]
            y = _dot(w.astype(MXU_DTYPE), x_h.astype(MXU_DTYPE))
            prev = st_ref[hd]
            y = y + _dot(cm_g, prev.astype(MXU_DTYPE)) * e_cum[:, hd:hd + 1]
            xd = (x_h * decay_st[:, hd:hd + 1]).astype(MXU_DTYPE)
            st_ref[hd] = e_last[:, hd:hd + 1] * prev + _dot_tn(bm_g, xd)
            y_ref[:, cs] = y + dsk_ref[:, cs] * x_h

    y = y_ref[...] * _silu(z_ref[0].astype(F32))
    gw = SSD_D_INNER // SSD_GROUPS
    for g in range(SSD_GROUPS):
        cs = slice(g * gw, (g + 1) * gw)
        yg = y[:, cs]
        ms = jnp.mean(yg * yg, axis=-1, keepdims=True)
        o_ref[0, :, cs] = (yg * lax.rsqrt(ms + LN_EPS) * ng_ref[:, cs]).astype(o_ref.dtype)


def _scan_kernel(rq_ref, rk_ref, rv_ref, rg_ref, gn_g_ref, gn_b_ref,
                 z_ref, xbc_ref, dt_ref, cw_ref, cb_ref, dtb_ref, alog_ref, dsk_ref, ng_ref,
                 o_ret_ref, o_ssd_ref, ret_st_ref, tail_ref, ssd_st_ref, y_ref):
    _ret_kernel(rq_ref, rk_ref, rv_ref, rg_ref, gn_g_ref, gn_b_ref, o_ret_ref, ret_st_ref)
    _ssd_kernel(z_ref, xbc_ref, dt_ref, cw_ref, cb_ref, dtb_ref, alog_ref, dsk_ref, ng_ref,
                o_ssd_ref, tail_ref, ssd_st_ref, y_ref)


def _scans(proj, proj_dt, gn_g, gn_b, conv_w, conv_b, dt_bias, a_log, d_skip, norm_g):
    bsz, s, _ = proj.shape
    assert RET_CHUNK == SSD_CHUNK
    ch = SSD_CHUNK
    wq, wv = RET_HEADS * RET_DK, RET_HEADS * RET_DV
    pad = LANES - SSD_HEADS
    dtb = jnp.pad(dt_bias, (0, pad)).reshape(1, LANES)
    alog = jnp.pad(a_log, (0, pad)).reshape(1, LANES)
    dsk = jnp.repeat(d_skip, SSD_HEAD_DIM).reshape(1, SSD_D_INNER)
    const = lambda b, i: (0, 0)
    return pl.pallas_call(
        _scan_kernel,
        grid=(bsz, s // ch),
        in_specs=[pl.BlockSpec((1, ch, wq), lambda b, i: (b, i, A_RET_Q // wq)),
                  pl.BlockSpec((1, ch, wq), lambda b, i: (b, i, A_RET_K // wq)),
                  pl.BlockSpec((1, ch, wv), lambda b, i: (b, i, A_RET_V // wv)),
                  pl.BlockSpec((1, ch, wv), lambda b, i: (b, i, B_RET_G // wv)),
                  pl.BlockSpec((1, wv), const),
                  pl.BlockSpec((1, wv), const),
                  pl.BlockSpec((1, ch, SSD_D_INNER), lambda b, i: (b, i, B_SSD_Z // SSD_D_INNER)),
                  pl.BlockSpec((1, ch, SSD_XBC), lambda b, i: (b, i, B_XBC // SSD_XBC)),
                  pl.BlockSpec((1, ch, LANES), lambda b, i: (b, i, 0)),
                  pl.BlockSpec((SSD_CONV, SSD_XBC), const),
                  pl.BlockSpec((1, SSD_XBC), const),
                  pl.BlockSpec((1, LANES), const),
                  pl.BlockSpec((1, LANES), const),
                  pl.BlockSpec((1, SSD_D_INNER), const),
                  pl.BlockSpec((1, SSD_D_INNER), const)],
        out_specs=[pl.BlockSpec((1, ch, wv), lambda b, i: (b, i, 0)),
                   pl.BlockSpec((1, ch, SSD_D_INNER), lambda b, i: (b, i, 0))],
        out_shape=[jax.ShapeDtypeStruct((bsz, s, wv), MXU_DTYPE),
                   jax.ShapeDtypeStruct((bsz, s, SSD_D_INNER), MXU_DTYPE)],
        scratch_shapes=[pltpu.VMEM((RET_HEADS, RET_DK, RET_DV), F32),
                        pltpu.VMEM((SSD_HALO + ch, SSD_XBC), F32),
                        pltpu.VMEM((SSD_HEADS, SSD_STATE, SSD_HEAD_DIM), F32),
                        pltpu.VMEM((ch, SSD_D_INNER), F32)],
        compiler_params=_params(("parallel", "arbitrary")),
        name="retention_ssd",
    )(proj, proj, proj, proj, gn_g.reshape(1, wv), gn_b.reshape(1, wv),
      proj, proj, proj_dt, conv_w, conv_b.reshape(1, SSD_XBC), dtb, alog, dsk,
      norm_g.reshape(1, SSD_D_INNER))


def _merge_kernel(oa_ref, ob_ref, oc_ref, od_ref, gates_ref, x_ref, g1_ref, wbr_ref, wout_ref,
                  lg_ref, lb_ref, o_ref):
    merged = None
    for n, br in enumerate((oa_ref, ob_ref, oc_ref, od_ref)):
        y = _dot(br[0], wbr_ref[n])
        t = _sigmoid(gates_ref[0, :, n * D_MODEL:(n + 1) * D_MODEL].astype(F32)) * y
        merged = t if merged is None else merged + t
    mix = _dot(merged.astype(MXU_DTYPE), wout_ref[...])
    o_ref[0] = _resid_ln(x_ref[0], g1_ref[0], mix, lg_ref[...], lb_ref[...])


def _merge(o_a, o_b, o_c, o_d, proj, x, g1, w_br, w_out, ln_g, ln_b, tm=512):
    bsz, s, d = x.shape
    tm = min(tm, s)
    gw = N_BRANCH * d
    row = lambda b, i: (b, i, 0)
    const2 = lambda b, i: (0, 0)
    br_spec = pl.BlockSpec((1, tm, BRANCH_W), row)
    return pl.pallas_call(
        _merge_kernel,
        grid=(bsz, s // tm),
        in_specs=[br_spec, br_spec, br_spec, br_spec,
                  pl.BlockSpec((1, tm, gw), lambda b, i: (b, i, B_GATES // gw)),
                  pl.BlockSpec((1, tm, d), row),
                  pl.BlockSpec((1, 1, d), lambda b, i: (b, 0, 0)),
                  pl.BlockSpec((N_BRANCH, BRANCH_W, d), lambda b, i: (0, 0, 0)),
                  pl.BlockSpec((d, d), const2),
                  pl.BlockSpec((1, d), const2),
                  pl.BlockSpec((1, d), const2)],
        out_specs=pl.BlockSpec((1, tm, d), row),
        out_shape=jax.ShapeDtypeStruct((bsz, s, d), F32),
        compiler_params=_params(("parallel", "parallel")),
        name="merge",
    )(o_a, o_b, o_c, o_d, proj, x, g1, w_br, w_out, ln_g.reshape(1, d), ln_b.reshape(1, d))


def _ffn_kernel(x_ref, sc_ref, sh_ref, g2_ref, wg_ref, wu_ref, wd_ref, lg_ref, lb_ref, o_ref):
    x = x_ref[0]
    h = (x * (1.0 + sc_ref[0]) + sh_ref[0]).astype(MXU_DTYPE)
    act = _silu(_dot(h, wg_ref[...])) * _dot(h, wu_ref[...])
    f = _dot(act.astype(MXU_DTYPE), wd_ref[...])
    o_ref[0] = _resid_ln(x, g2_ref[0], f, lg_ref[...], lb_ref[...])


def _ffn(x, sc, sh, g2, w_gu, w_down, ln_g, ln_b, tm=512):
    bsz, s, d = x.shape
    tm = min(tm, s)
    row = lambda b, i: (b, i, 0)
    vec = lambda b, i: (b, 0, 0)
    once = pl.Buffered(1)
    return pl.pallas_call(
        _ffn_kernel,
        grid=(bsz, s // tm),
        in_specs=[pl.BlockSpec((1, tm, d), row),
                  pl.BlockSpec((1, 1, d), vec), pl.BlockSpec((1, 1, d), vec),
                  pl.BlockSpec((1, 1, d), vec),
                  pl.BlockSpec((d, D_FF), lambda b, i: (0, 0), pipeline_mode=once),
                  pl.BlockSpec((d, D_FF), lambda b, i: (0, 1), pipeline_mode=once),
                  pl.BlockSpec((D_FF, d), lambda b, i: (0, 0), pipeline_mode=once),
                  pl.BlockSpec((1, d), lambda b, i: (0, 0)),
                  pl.BlockSpec((1, d), lambda b, i: (0, 0))],
        out_specs=pl.BlockSpec((1, tm, d), row),
        out_shape=jax.ShapeDtypeStruct((bsz, s, d), F32),
        compiler_params=_params(("parallel", "parallel")),
        name="ffn_dense",
    )(x, sc, sh, g2, w_gu, w_gu, w_down, ln_g.reshape(1, d), ln_b.reshape(1, d))


def _router_kernel(x_ref, sc_ref, sh_ref, rw_ref, rb_ref,
                   h_ref, comb_ref, slot_ref, slot_t_ref, cnt_ref):
    tm = x_ref.shape[1]
    lane = lax.broadcasted_iota(jnp.int32, (tm, LANES), 1)
    h = x_ref[0] * (1.0 + sc_ref[0]) + sh_ref[0]
    h_ref[0] = h.astype(MXU_DTYPE)
    logits = _dot3(h, rw_ref[...]) + rb_ref[...]
    logits = jnp.where(lane < N_EXPERTS, logits, -jnp.inf)
    v1 = jnp.max(logits, axis=1, keepdims=True)
    i1 = jnp.min(jnp.where(logits == v1, lane, LANES), axis=1, keepdims=True)
    rest = jnp.where(lane == i1, -jnp.inf, logits)
    v2 = jnp.max(rest, axis=1, keepdims=True)
    i2 = jnp.min(jnp.where(rest == v2, lane, LANES), axis=1, keepdims=True)
    e2 = jnp.exp(v2 - v1)
    w1 = 1.0 / (1.0 + e2)
    w2 = e2 / (1.0 + e2)
    comb_ref[0] = jnp.where(lane == i1, w1, 0.0) + jnp.where(lane == i2, w2, 0.0)
    routed = jnp.logical_or(lane == i1, lane == i2)
    earlier = (lax.broadcasted_iota(jnp.int32, (tm, tm), 1)
               < lax.broadcasted_iota(jnp.int32, (tm, tm), 0))
    before = _dot(jnp.where(earlier, 1.0, 0.0).astype(MXU_DTYPE),
                  jnp.where(routed, 1.0, 0.0).astype(MXU_DTYPE))
    slot = jnp.where(routed, before, -1.0)
    slot_ref[0] = slot
    slot_t_ref[0] = slot.T[0:N_EXPERTS, :]
    cnt_ref[0] = jnp.sum(jnp.where(routed, 1, 0), axis=0, keepdims=True)


def _moe_kernel(cnt_ref, h_ref, x_ref, g2_ref, comb_ref, slot_ref, slot_t_ref,
                wg_ref, wu_ref, wd_ref, lg_ref, lb_ref, o_ref, xe_ref, ye_ref, *, n_tiles):
    tm = x_ref.shape[1]
    e = pl.program_id(2)
    j = pl.program_id(3)
    last_j = pl.num_programs(3) - 1
    count = cnt_ref[(pl.program_id(0) * n_tiles + pl.program_id(1)) * N_EXPERTS + e]
    n_full = jnp.maximum(count - MOE_TAIL_MIN - 1, 0) // MOE_ROWS
    tail0 = pl.multiple_of(n_full * MOE_ROWS, MOE_ROWS)
    tail_steps = (count - n_full * MOE_ROWS + MOE_ROW_STEP - 1) // MOE_ROW_STEP

    def for_pieces(fn):
        def full_piece(c, carry):
            fn(pl.multiple_of(c * MOE_ROWS, MOE_ROWS), MOE_ROWS)
            return carry

        lax.fori_loop(0, n_full, full_piece, 0)
        for steps in range(1, (MOE_ROWS + MOE_TAIL_MIN) // MOE_ROW_STEP + 1):
            @pl.when(tail_steps == steps)
            def _(steps=steps):
                fn(tail0, steps * MOE_ROW_STEP)

    @pl.when(jnp.logical_and(e == 0, j == 0))
    def _():
        o_ref[0] = jnp.zeros((tm, D_MODEL), F32)

    @pl.when(j == 0)
    def _():
        slot_row = slot_t_ref[0, pl.ds(e, 1), :]

        def gather(r0, rows):
            slots = (lax.broadcasted_iota(jnp.int32, (rows, tm), 0) + r0).astype(F32)
            pick = jnp.where(slot_row == slots, 1.0, 0.0).astype(MXU_DTYPE)
            xe_ref[pl.ds(r0, rows), :] = _dot(pick, h_ref[0]).astype(MXU_DTYPE)
            ye_ref[pl.ds(r0, rows), :] = jnp.zeros((rows, D_MODEL), F32)

        for_pieces(gather)

    def expert_rows(r0, rows):
        xc = xe_ref[pl.ds(r0, rows), :]
        act = _silu(_dot(xc, wg_ref[0])) * _dot(xc, wu_ref[0])
        ye_ref[pl.ds(r0, rows), :] += _dot(act.astype(MXU_DTYPE), wd_ref[0])

    for_pieces(expert_rows)

    @pl.when(j == last_j)
    def _():
        lane = lax.broadcasted_iota(jnp.int32, (tm, LANES), 1)
        mine = lane == e
        slot_col = jnp.sum(jnp.where(mine, slot_ref[0], 0.0), axis=1, keepdims=True)
        comb_col = jnp.sum(jnp.where(mine, comb_ref[0], 0.0), axis=1, keepdims=True)

        def scatter(r0, rows):
            slots = (lax.broadcasted_iota(jnp.int32, (tm, rows), 1) + r0).astype(F32)
            place = jnp.where(slot_col == slots, 1.0, 0.0).astype(MXU_DTYPE)
            o_ref[0] += comb_col * _dot(place, ye_ref[pl.ds(r0, rows), :].astype(MXU_DTYPE))

        for_pieces(scatter)

    @pl.when(jnp.logical_and(e == pl.num_programs(2) - 1, j == last_j))
    def _():
        o_ref[0] = _resid_ln(x_ref[0], g2_ref[0], o_ref[0], lg_ref[...], lb_ref[...])


def _moe(x, sc, sh, g2, router_w, router_b, w_gu, w_down, ln_g, ln_b, tm=1024, tf=1792):
    bsz, s, d = x.shape
    tm = min(tm, s)
    nt = s // tm
    nf = D_FF_EXPERT // tf
    rw = jnp.pad(router_w, ((0, 0), (0, LANES - N_EXPERTS)))
    rb = jnp.pad(router_b, (0, LANES - N_EXPERTS)).reshape(1, LANES)

    row2 = lambda b, i: (b, i, 0)
    vec2 = lambda b, i: (b, 0, 0)
    h, comb, slot, slot_t, cnt = pl.pallas_call(
        _router_kernel,
        grid=(bsz, nt),
        in_specs=[pl.BlockSpec((1, tm, d), row2),
                  pl.BlockSpec((1, 1, d), vec2), pl.BlockSpec((1, 1, d), vec2),
                  pl.BlockSpec((d, LANES), lambda b, i: (0, 0)),
                  pl.BlockSpec((1, LANES), lambda b, i: (0, 0))],
        out_specs=[pl.BlockSpec((1, tm, d), row2),
                   pl.BlockSpec((1, tm, LANES), row2),
                   pl.BlockSpec((1, tm, LANES), row2),
                   pl.BlockSpec((1, N_EXPERTS, tm), lambda b, i: (b * nt + i, 0, 0)),
                   pl.BlockSpec((1, 1, LANES), lambda b, i: (b * nt + i, 0, 0))],
        out_shape=[jax.ShapeDtypeStruct((bsz, s, d), MXU_DTYPE),
                   jax.ShapeDtypeStruct((bsz, s, LANES), F32),
                   jax.ShapeDtypeStruct((bsz, s, LANES), F32),
                   jax.ShapeDtypeStruct((bsz * nt, N_EXPERTS, tm), F32),
                   jax.ShapeDtypeStruct((bsz * nt, 1, LANES), jnp.int32)],
        compiler_params=_params(("parallel", "parallel")),
        name="moe_router",
    )(x, sc, sh, rw, rb)
    counts = cnt[:, 0, :N_EXPERTS].reshape(bsz * nt * N_EXPERTS)

    row = lambda b, i, e, j, cnt: (b, i, 0)
    vec = lambda b, i, e, j, cnt: (b, 0, 0)
    const2 = lambda b, i, e, j, cnt: (0, 0)
    once = pl.Buffered(1)
    grid_spec = pltpu.PrefetchScalarGridSpec(
        num_scalar_prefetch=1,
        grid=(bsz, nt, N_EXPERTS, nf),
        in_specs=[pl.BlockSpec((1, tm, d), row, pipeline_mode=once),
                  pl.BlockSpec((1, tm, d), row, pipeline_mode=once),
                  pl.BlockSpec((1, 1, d), vec),
                  pl.BlockSpec((1, tm, LANES), row, pipeline_mode=once),
                  pl.BlockSpec((1, tm, LANES), row, pipeline_mode=once),
                  pl.BlockSpec((1, N_EXPERTS, tm), lambda b, i, e, j, cnt: (b * nt + i, 0, 0)),
                  pl.BlockSpec((1, d, tf), lambda b, i, e, j, cnt: (e, 0, j)),
                  pl.BlockSpec((1, d, tf), lambda b, i, e, j, cnt: (e, 0, nf + j)),
                  pl.BlockSpec((1, tf, d), lambda b, i, e, j, cnt: (e, j, 0)),
                  pl.BlockSpec((1, d), const2), pl.BlockSpec((1, d), const2)],
        out_specs=pl.BlockSpec((1, tm, d), row),
        scratch_shapes=[pltpu.VMEM((tm, d), MXU_DTYPE), pltpu.VMEM((tm, d), F32)])
    return pl.pallas_call(
        functools.partial(_moe_kernel, n_tiles=nt),
        grid_spec=grid_spec,
        out_shape=jax.ShapeDtypeStruct((bsz, s, d), F32),
        compiler_params=pltpu.CompilerParams(
            dimension_semantics=("parallel", "parallel", "arbitrary", "arbitrary"),
            vmem_limit_bytes=MOE_VMEM_LIMIT),
        name="moe_experts",
    )(counts, h, x, g2, comb, slot, slot_t, w_gu, w_gu, w_down,
      ln_g.reshape(1, d), ln_b.reshape(1, d))


def _split_w_in(w):
    p = np.cumsum([0, 512, 512, 512, 256, 256, 512, 512, 512, SSD_XBC, SSD_HEADS, 512, 512, 512,
                   N_BRANCH * D_MODEL])
    mq, mk, mv, rq, rk, rv, rg, sz, sxbc, sdt, bq, bk, bv, gates = (
        w[:, p[i]:p[i + 1]] for i in range(14))
    w_main = jnp.concatenate([mq, mk, mv, rq, rk, rv, bq, bk, bv, gates, rg, sz, sxbc], axis=1)
    w_dt = jnp.pad(sdt, ((0, 0), (0, LANES - SSD_HEADS)))
    assert w_main.shape[1] == PROJ_WIDTH
    return w_main.astype(MXU_DTYPE), w_dt.astype(MXU_DTYPE)


def kernel(x, c, w_ada, b_ada, w_in, conv_w, conv_b, dt_bias, a_log, d_skip, ssm_norm_g, ret_gn_g, ret_gn_b, w_br, w_out, ln1_g, ln1_b, ln2_g, ln2_b, ffn_w_gu, ffn_w_down, router_w, router_b, expert_w_gu, expert_w_down):
    bsz = x.shape[0]
    mod = _ada(c, w_ada, b_ada)
    for l in range(DEPTH):
        sh1, sc1, g1, sh2, sc2, g2 = (
            mod[l, :, i * D_MODEL:(i + 1) * D_MODEL].reshape(bsz, 1, D_MODEL) for i in range(6))
        w_main, w_dt = _split_w_in(w_in[l])
        proj, proj_dt = _inproj(x, sc1, sh1, w_main, w_dt, tn=2560)
        o_a = _moba(proj)
        o_d = _stick_breaking(proj)
        o_b, o_c = _scans(proj, proj_dt, ret_gn_g[l], ret_gn_b[l], conv_w[l], conv_b[l],
                          dt_bias[l], a_log[l], d_skip[l], ssm_norm_g[l])
        x = _merge(o_a, o_b, o_c, o_d, proj, x, g1, w_br[l].astype(MXU_DTYPE),
                   w_out[l].astype(MXU_DTYPE), ln1_g[l], ln1_b[l])
        if l % 2 == 0:
            x = _ffn(x, sc2, sh2, g2, ffn_w_gu[l // 2].astype(MXU_DTYPE),
                     ffn_w_down[l // 2].astype(MXU_DTYPE), ln2_g[l], ln2_b[l])
        else:
            x = _moe(x, sc2, sh2, g2, router_w[l // 2], router_b[l // 2],
                     expert_w_gu[l // 2].astype(MXU_DTYPE), expert_w_down[l // 2].astype(MXU_DTYPE),
                     ln2_g[l], ln2_b[l])
    return x
```
